```python
import math
import jax
import jax.numpy as jnp
from jax import lax
import numpy as np

D_MODEL = 1024
BATCH = 8
SEQ = 4096
DEPTH = 4

N_MIXERS = 3
N_POOL_LAYERS = len(range(0, DEPTH, N_MIXERS))
N_SC_LAYERS = len(range(1, DEPTH, N_MIXERS))
N_CF_LAYERS = len(range(2, DEPTH, N_MIXERS))

POOL_WINDOWS = (2, 4, 8, 16)
N_POOL_GROUPS = len(POOL_WINDOWS)
POOL_GROUP_DIM = D_MODEL // N_POOL_GROUPS
MAX_POOL = max(POOL_WINDOWS)

SHORT_CONV_WIDTH = 3
CONFORMER_KERNEL = 31

N_EXPERTS = 32
TOP_K = 4
EXPERT_DIM = D_MODEL
SWIGLU_LIMIT = 7.0
SWIGLU_ALPHA = 1.702
EXPERT_BLOCK = 128

DEEPNORM_ALPHA = (2.0 * DEPTH) ** 0.25
DEEPNORM_BETA = (8.0 * DEPTH) ** -0.25
LN_EPS = 1e-5

kernel_name = "hybrid_pool_shortconv_conformer_moe_deepnorm"


def layer_norm(x, g, b):
    xf = x.astype(jnp.float32)
    mu = jnp.mean(xf, axis=-1, keepdims=True)
    xc = xf - mu
    var = jnp.mean(xc * xc, axis=-1, keepdims=True)
    y = xc * lax.rsqrt(var + LN_EPS) * g.astype(jnp.float32) + b.astype(jnp.float32)
    return y.astype(x.dtype)


def causal_depthwise_conv(u, w):
    k = w.shape[0]
    return lax.conv_general_dilated(
        u, w[:, None, :].astype(u.dtype), window_strides=(1,), padding=[(k - 1, 0)],
        dimension_numbers=("NWC", "WIO", "NWC"), feature_group_count=u.shape[-1])


def pool_mixer(x, w_groups, scale):
    seq = x.shape[1]
    xf = x.astype(jnp.float32)
    csum = jnp.pad(jnp.cumsum(xf, axis=1), ((0, 0), (MAX_POOL, 0), (0, 0)))
    outs = []
    for g, win_len in enumerate(POOL_WINDOWS):
        sl = slice(g * POOL_GROUP_DIM, (g + 1) * POOL_GROUP_DIM)
        window_sum = csum[:, MAX_POOL:, sl] - csum[:, MAX_POOL - win_len:MAX_POOL - win_len + seq, sl]
        count = jnp.minimum(jnp.arange(1, seq + 1, dtype=jnp.float32), float(win_len))
        diff = window_sum / count[None, :, None] - xf[..., sl]
        outs.append(jnp.einsum("bsc,ce->bse", diff.astype(x.dtype), w_groups[g]))
    return jnp.concatenate(outs, axis=-1) * scale


def short_conv_mixer(x, w_in, conv_w, w_out):
    proj = jnp.einsum("bsd,de->bse", x, w_in)
    gate_b, gate_c, h = jnp.split(proj, 3, axis=-1)
    u = causal_depthwise_conv(gate_c * h, conv_w)
    return jnp.einsum("bsd,de->bse", gate_b * u, w_out)


def conformer_conv_mixer(x, w_in, b_in, dw_w, dw_b, ln_g, ln_b, w_out, b_out):
    a, gate = jnp.split(jnp.einsum("bsd,de->bse", x, w_in) + b_in, 2, axis=-1)
    u = a * jax.nn.sigmoid(gate)
    u = causal_depthwise_conv(u, dw_w) + dw_b
    u = jax.nn.silu(layer_norm(u, ln_g, ln_b))
    return jnp.einsum("bsd,de->bse", u, w_out) + b_out


def expert_ffn(xb, w_gu, b_gu, w_dn, b_dn):
    h = xb @ w_gu + b_gu
    g = jnp.minimum(h[..., 0::2], SWIGLU_LIMIT)
    up = jnp.clip(h[..., 1::2], -SWIGLU_LIMIT, SWIGLU_LIMIT)
    glu = g * jax.nn.sigmoid(SWIGLU_ALPHA * g)
    return ((up + 1.0) * glu) @ w_dn + b_dn


def moe_ffn(x, router_w, router_b, w_gu, b_gu, w_dn, b_dn):
    bsz, seq, d = x.shape
    n_tok = bsz * seq
    x2d = x.reshape(n_tok, d)
    logits = jnp.dot(x2d.astype(jnp.float32), router_w.astype(jnp.float32)) + router_b.astype(jnp.float32)
    top_vals, top_idx = lax.top_k(logits, TOP_K)
    gates = jax.nn.softmax(top_vals, axis=-1)

    n_assign = n_tok * TOP_K
    flat_e = top_idx.reshape(n_assign).astype(jnp.int32)
    flat_tok = jnp.arange(n_assign, dtype=jnp.int32) // TOP_K
    order = jnp.argsort(flat_e)
    sorted_e = flat_e[order]
    sorted_tok = flat_tok[order]

    counts = jnp.bincount(flat_e, length=N_EXPERTS).astype(jnp.int32)
    padded_counts = ((counts + EXPERT_BLOCK - 1) // EXPERT_BLOCK) * EXPERT_BLOCK
    padded_end = jnp.cumsum(padded_counts)
    padded_start = padded_end - padded_counts
    group_start = jnp.cumsum(counts) - counts
    rank = jnp.arange(n_assign, dtype=jnp.int32) - group_start[sorted_e]
    dest = padded_start[sorted_e] + rank

    n_blocks = -(-n_assign // EXPERT_BLOCK) + N_EXPERTS
    n_pad = n_blocks * EXPERT_BLOCK
    row_tok = jnp.zeros((n_pad,), jnp.int32).at[dest].set(sorted_tok)
    block_start = jnp.arange(n_blocks, dtype=jnp.int32) * EXPERT_BLOCK
    block_expert = jnp.minimum(jnp.searchsorted(padded_end, block_start, side="right"),
                               N_EXPERTS - 1).astype(jnp.int32)

    xs = x2d[row_tok].reshape(n_blocks, EXPERT_BLOCK, d)

    def run_block(args):
        xb, e = args
        return expert_ffn(xb, w_gu[e], b_gu[e], w_dn[e], b_dn[e])

    ys = lax.map(run_block, (xs, block_expert)).reshape(n_pad, d)
    assign_row = jnp.zeros((n_assign,), jnp.int32).at[order].set(dest)
    y = ys[assign_row].reshape(n_tok, TOP_K, d)
    out = jnp.einsum("tk,tkd->td", gates.astype(y.dtype), y)
    return out.reshape(bsz, seq, d)


def setup_inputs(seed: int = 0) -> dict:
    key = jax.random.key(seed)
    keys = iter(jax.random.split(key, 32))

    def nrm(shape, scale):
        return scale * jax.random.normal(next(keys), shape, jnp.float32)

    d, f = D_MODEL, EXPERT_DIM
    return {
        "x": nrm((BATCH, SEQ, d), 1.0),
        "pool_w": nrm((N_POOL_LAYERS, N_POOL_GROUPS, POOL_GROUP_DIM, POOL_GROUP_DIM),
                      DEEPNORM_BETA * POOL_GROUP_DIM ** -0.5),
        "pool_scale": 1.0 + nrm((N_POOL_LAYERS, d), 0.1),
        "sc_w_in": nrm((N_SC_LAYERS, d, 3 * d), d ** -0.5),
        "sc_conv_w": nrm((N_SC_LAYERS, SHORT_CONV_WIDTH, d), SHORT_CONV_WIDTH ** -0.5),
        "sc_w_out": nrm((N_SC_LAYERS, d, d), DEEPNORM_BETA * d ** -0.5),
        "cf_w_in": nrm((N_CF_LAYERS, d, 2 * d), d ** -0.5),
        "cf_b_in": nrm((N_CF_LAYERS, 2 * d), 0.02),
        "cf_dw_w": nrm((N_CF_LAYERS, CONFORMER_KERNEL, d), CONFORMER_KERNEL ** -0.5),
        "cf_dw_b": nrm((N_CF_LAYERS, d), 0.02),
        "cf_ln_g": 1.0 + nrm((N_CF_LAYERS, d), 0.02),
        "cf_ln_b": nrm((N_CF_LAYERS, d), 0.02),
        "cf_w_out": nrm((N_CF_LAYERS, d, d), DEEPNORM_BETA * d ** -0.5),
        "cf_b_out": nrm((N_CF_LAYERS, d), 0.02),
        "mix_ln_g": 1.0 + nrm((DEPTH, d), 0.02),
        "mix_ln_b": nrm((DEPTH, d), 0.02),
        "router_w": nrm((DEPTH, d, N_EXPERTS), d ** -0.5),
        "router_b": nrm((DEPTH, N_EXPERTS), 0.01),
        "moe_w_gu": nrm((DEPTH, N_EXPERTS, d, 2 * f), d ** -0.5),
        "moe_b_gu": nrm((DEPTH, N_EXPERTS, 2 * f), 0.02),
        "moe_w_dn": nrm((DEPTH, N_EXPERTS, f, d), DEEPNORM_BETA * f ** -0.5),
        "moe_b_dn": nrm((DEPTH, N_EXPERTS, d), 0.02),
        "ffn_ln_g": 1.0 + nrm((DEPTH, d), 0.02),
        "ffn_ln_b": nrm((DEPTH, d), 0.02),
    }


def reference(x, pool_w, pool_scale, sc_w_in, sc_conv_w, sc_w_out,
              cf_w_in, cf_b_in, cf_dw_w, cf_dw_b, cf_ln_g, cf_ln_b, cf_w_out, cf_b_out,
              mix_ln_g, mix_ln_b, router_w, router_b, moe_w_gu, moe_b_gu, moe_w_dn, moe_b_dn,
              ffn_ln_g, ffn_ln_b):
    ia, ib, ic = 0, 0, 0
    for layer in range(DEPTH):
        kind = layer % N_MIXERS
        if kind == 0:
            h = pool_mixer(x, pool_w[ia], pool_scale[ia])
            ia += 1
        elif kind == 1:
            h = short_conv_mixer(x, sc_w_in[ib], sc_conv_w[ib], sc_w_out[ib])
            ib += 1
        else:
            h = conformer_conv_mixer(x, cf_w_in[ic], cf_b_in[ic], cf_dw_w[ic], cf_dw_b[ic],
                                     cf_ln_g[ic], cf_ln_b[ic], cf_w_out[ic], cf_b_out[ic])
            ic += 1
        x = layer_norm(DEEPNORM_ALPHA * x + h, mix_ln_g[layer], mix_ln_b[layer])
        h = moe_ffn(x, router_w[layer], router_b[layer], moe_w_gu[layer], moe_b_gu[layer],
                    moe_w_dn[layer], moe_b_dn[layer])
        x = layer_norm(DEEPNORM_ALPHA * x + h, ffn_ln_g[layer], ffn_ln_b[layer])
    return x
```

```python
import functools

import jax
import jax.numpy as jnp
from jax import lax
from jax.experimental import pallas as pl
from jax.experimental.pallas import tpu as pltpu

LANES = 128
TOP_K = 4
POOL_WINDOWS = (2, 4, 8, 16)
POOL_HALO = 16
SHORT_CONV_HALO = 8
CONFORMER_HALO = 32
SWIGLU_LIMIT = 7.0
SWIGLU_ALPHA = 1.702
LN_EPS = 1e-5
TOKEN_BLOCK = 512
EXPERT_BLOCK_ROWS = 256
COMBINE_BLOCK = 256
DISPATCH_CHUNK = 2048
VMEM_LIMIT_BYTES = 56 * 1024 * 1024

_F32 = jnp.float32
_BF16 = jnp.bfloat16


def _layer_norm(z, g, b):
    mu = jnp.mean(z, axis=-1, keepdims=True)
    zc = z - mu
    var = jnp.mean(zc * zc, axis=-1, keepdims=True)
    return zc * lax.rsqrt(var + LN_EPS) * g + b


def _store_rows(row_ref, val):
    rows, d = val.shape
    rs = d // LANES
    for j in range(rs):
        row_ref[pl.ds(j, rows, stride=rs), :] = val[:, j * LANES:(j + 1) * LANES]


def _load_rows(row_ref, rows, rs, lead=None):
    if lead is None:
        return [row_ref[pl.ds(j, rows, stride=rs), :] for j in range(rs)]
    return [row_ref[lead, pl.ds(j, rows, stride=rs), :] for j in range(rs)]


def _post_norm_and_route(z, g_ref, b_ref, rwt_ref, rb_ref, first,
                         x1r_ref, mi_ref, mg_ref, cnt_ref, carry_ref):
    n_tok = z.shape[0]
    n_exp = rwt_ref.shape[0]

    @pl.when(first)
    def _():
        carry_ref[...] = jnp.zeros_like(carry_ref)

    x1 = _layer_norm(z, g_ref[...], b_ref[...])
    _store_rows(x1r_ref, x1)

    logits = lax.dot_general(rwt_ref[...], x1, (((1,), (1,)), ((), ())),
                             precision=lax.Precision.HIGHEST,
                             preferred_element_type=_F32) + rb_ref[...]
    eidx = lax.broadcasted_iota(jnp.int32, logits.shape, 0)
    work = logits
    chosen = jnp.zeros(logits.shape, jnp.bool_)
    vals, idxs = [], []
    for _ in range(TOP_K):
        m = jnp.max(work, axis=0, keepdims=True)
        sel = jnp.min(jnp.where(work == m, eidx, n_exp), axis=0, keepdims=True)
        hit = eidx == sel
        vals.append(m)
        idxs.append(sel)
        chosen = jnp.logical_or(chosen, hit)
        work = jnp.where(hit, -jnp.inf, work)
    exps = [jnp.exp(v - vals[0]) for v in vals]
    denom = exps[0] + exps[1] + exps[2] + exps[3]
    mg_ref[...] = jnp.concatenate([e / denom for e in exps], axis=0)

    onehot = chosen.astype(_BF16)
    r = lax.broadcasted_iota(jnp.int32, (n_tok, n_tok), 0)
    c = lax.broadcasted_iota(jnp.int32, (n_tok, n_tok), 1)
    before = (r < c).astype(_BF16)
    cum = jnp.dot(onehot, before, preferred_element_type=_F32) + carry_ref[:, 0:1]
    ranks = [jnp.sum(jnp.where(eidx == s, cum, 0.0), axis=0, keepdims=True) for s in idxs]
    mi_ref[...] = jnp.concatenate(idxs + [rk.astype(jnp.int32) for rk in ranks], axis=0)
    carry_ref[...] = carry_ref[...] + jnp.sum(chosen.astype(_F32), axis=1, keepdims=True)
    cnt_ref[...] = carry_ref[...]


def _pool_kernel(x_ref, pw_ref, ps_ref, g_ref, b_ref, rwt_ref, rb_ref,
                 x1r_ref, mi_ref, mg_ref, cnt_ref, hist_ref, carry_ref, *, alpha):
    bi, si = pl.program_id(0), pl.program_id(1)
    ts, d = x_ref.shape[1], x_ref.shape[2]
    dg = d // len(POOL_WINDOWS)

    @pl.when(si == 0)
    def _():
        hist_ref[0:POOL_HALO, :] = jnp.zeros((POOL_HALO, d), _F32)

    x = x_ref[0]
    hist_ref[POOL_HALO:POOL_HALO + ts, :] = x
    pos = si * ts + lax.broadcasted_iota(jnp.int32, (ts, 1), 0)
    pieces = []
    for gi, win in enumerate(POOL_WINDOWS):
        c0 = gi * dg
        xg = x[:, c0:c0 + dg]
        acc = xg
        for j in range(1, win):
            acc = acc + hist_ref[POOL_HALO - j:POOL_HALO - j + ts, c0:c0 + dg]
        inv_count = 1.0 / jnp.minimum(pos + 1, win).astype(_F32)
        diff = acc * inv_count - xg
        hg = jnp.dot(diff.astype(_BF16), pw_ref[gi], preferred_element_type=_F32)
        pieces.append(alpha * xg + hg * ps_ref[:, c0:c0 + dg])
    hist_ref[0:POOL_HALO, :] = x[ts - POOL_HALO:, :]
    z = jnp.concatenate(pieces, axis=1)
    _post_norm_and_route(z, g_ref, b_ref, rwt_ref, rb_ref, (bi == 0) & (si == 0),
                         x1r_ref, mi_ref, mg_ref, cnt_ref, carry_ref)


def _short_conv_kernel(x_ref, win_ref, cw_ref, wout_ref, g_ref, b_ref, rwt_ref, rb_ref,
                       x1r_ref, mi_ref, mg_ref, cnt_ref, hist_ref, carry_ref, *, alpha):
    bi, si = pl.program_id(0), pl.program_id(1)
    ts, d = x_ref.shape[1], x_ref.shape[2]
    halo = SHORT_CONV_HALO

    @pl.when(si == 0)
    def _():
        hist_ref[0:halo, :] = jnp.zeros((halo, d), _F32)

    x = x_ref[0]
    xb = x.astype(_BF16)
    gate_b = jnp.dot(xb, win_ref[:, 0:d], preferred_element_type=_F32)
    gate_c = jnp.dot(xb, win_ref[:, d:2 * d], preferred_element_type=_F32)
    h = jnp.dot(xb, win_ref[:, 2 * d:3 * d], preferred_element_type=_F32)
    v = gate_c * h
    hist_ref[halo:halo + ts, :] = v
    width = cw_ref.shape[0]
    u = cw_ref[width - 1:width, :] * v
    for k in range(width - 1):
        shift = width - 1 - k
        u = u + cw_ref[k:k + 1, :] * hist_ref[halo - shift:halo - shift + ts, :]
    hist_ref[0:halo, :] = v[ts - halo:, :]
    y = jnp.dot((gate_b * u).astype(_BF16), wout_ref[...], preferred_element_type=_F32)
    _post_norm_and_route(alpha * x + y, g_ref, b_ref, rwt_ref, rb_ref, (bi == 0) & (si == 0),
                         x1r_ref, mi_ref, mg_ref, cnt_ref, carry_ref)


def _conformer_kernel(x_ref, win_ref, bin_ref, dww_ref, dwb_ref, lng_ref, lnb_ref, wout_ref, bout_ref,
                      g_ref, b_ref, rwt_ref, rb_ref,
                      x1r_ref, mi_ref, mg_ref, cnt_ref, hist_ref, carry_ref, *, alpha):
    bi, si = pl.program_id(0), pl.program_id(1)
    ts, d = x_ref.shape[1], x_ref.shape[2]
    halo = CONFORMER_HALO

    @pl.when(si == 0)
    def _():
        hist_ref[0:halo, :] = jnp.zeros((halo, d), _F32)

    x = x_ref[0]
    xb = x.astype(_BF16)
    a = jnp.dot(xb, win_ref[:, 0:d], preferred_element_type=_F32) + bin_ref[:, 0:d]
    gate = jnp.dot(xb, win_ref[:, d:2 * d], preferred_element_type=_F32) + bin_ref[:, d:2 * d]
    u = a * jax.nn.sigmoid(gate)
    hist_ref[halo:halo + ts, :] = u
    width = dww_ref.shape[0]
    acc = dwb_ref[...] + dww_ref[width - 1:width, :] * u
    for k in range(width - 1):
        shift = width - 1 - k
        acc = acc + dww_ref[k:k + 1, :] * hist_ref[halo - shift:halo - shift + ts, :]
    hist_ref[0:halo, :] = u[ts - halo:, :]
    un = _layer_norm(acc, lng_ref[...], lnb_ref[...])
    un = un * jax.nn.sigmoid(un)
    y = jnp.dot(un.astype(_BF16), wout_ref[...], preferred_element_type=_F32) + bout_ref[...]
    _post_norm_and_route(alpha * x + y, g_ref, b_ref, rwt_ref, rb_ref, (bi == 0) & (si == 0),
                         x1r_ref, mi_ref, mg_ref, cnt_ref, carry_ref)


def _mixer_call(kernel_fn, x, weights, ln_g, ln_b, router_w, router_b, halo, alpha):
    bsz, seq, d = x.shape
    n_exp = router_w.shape[1]
    ts = min(TOKEN_BLOCK, seq)
    rs = d // LANES
    n_tok = bsz * seq
    nsb = seq // ts

    def full(a):
        nd = a.ndim
        return pl.BlockSpec(a.shape, lambda bi, si, _nd=nd: (0,) * _nd)

    small = [ln_g.reshape(1, d), ln_b.reshape(1, d), router_w.T, router_b.reshape(n_exp, 1)]
    operands = [x] + list(weights) + small
    in_specs = [pl.BlockSpec((1, ts, d), lambda bi, si: (bi, si, 0))] + [full(a) for a in operands[1:]]
    tok_map = lambda bi, si: (0, bi * nsb + si)
    out_shape = [
        jax.ShapeDtypeStruct((n_tok * rs, LANES), _F32),
        jax.ShapeDtypeStruct((2 * TOP_K, n_tok), jnp.int32),
        jax.ShapeDtypeStruct((TOP_K, n_tok), _F32),
        jax.ShapeDtypeStruct((n_exp, LANES), _F32),
    ]
    out_specs = [
        pl.BlockSpec((ts * rs, LANES), lambda bi, si: (bi * nsb + si, 0)),
        pl.BlockSpec((2 * TOP_K, ts), tok_map),
        pl.BlockSpec((TOP_K, ts), tok_map),
        pl.BlockSpec((n_exp, LANES), lambda bi, si: (0, 0)),
    ]
    return pl.pallas_call(
        functools.partial(kernel_fn, alpha=alpha),
        grid=(bsz, nsb),
        in_specs=in_specs,
        out_specs=out_specs,
        out_shape=out_shape,
        scratch_shapes=[pltpu.VMEM((halo + ts, d), _F32), pltpu.VMEM((n_exp, LANES), _F32)],
        compiler_params=pltpu.CompilerParams(
            dimension_semantics=("arbitrary", "arbitrary"), vmem_limit_bytes=VMEM_LIMIT_BYTES),
        name=kernel_fn.__name__.strip("_"),
    )(*operands)


def _row_copy(src_hbm, src_row, dst_ref, dst_row, rs, sem):
    return pltpu.make_async_copy(src_hbm.at[pl.ds(src_row * rs, rs)], dst_ref.at[pl.ds(dst_row * rs, rs)], sem)


def _dispatch_kernel(dest_ref, pad_start_ref, pad_len_ref, nu_ref, x1r_hbm, xs_hbm, sem,
                     *, rs, chunk, n_exp, block_rows, n_blocks):
    i = pl.program_id(0)
    last = pl.num_programs(0) - 1

    def wait_rows(n):
        pltpu.make_async_copy(x1r_hbm.at[pl.ds(0, n * rs)], xs_hbm.at[pl.ds(0, n * rs)], sem).wait()

    def issue(n, carry):
        a = i * chunk + n
        _row_copy(x1r_hbm, a // TOP_K, xs_hbm, dest_ref[a], rs, sem).start()
        return carry

    lax.fori_loop(0, chunk, issue, 0)

    @pl.when(i > 0)
    def _():
        wait_rows(chunk)

    @pl.when(i == last)
    def _():
        wait_rows(chunk)

        def per_expert(e, total):
            start, n = pad_start_ref[e], pad_len_ref[e]

            def fill(r, carry):
                _row_copy(x1r_hbm, 0, xs_hbm, start + r, rs, sem).start()
                return carry

            lax.fori_loop(0, n, fill, 0)
            return total + n

        n_pad = lax.fori_loop(0, n_exp, per_expert, 0)

        @pl.when(n_pad > 0)
        def _():
            wait_rows(n_pad)

        n_used = nu_ref[0]
        blk = block_rows * rs

        def fill_block(b, carry):
            pltpu.make_async_copy(x1r_hbm.at[pl.ds(0, blk)], xs_hbm.at[pl.ds(b * blk, blk)], sem).start()
            return carry

        lax.fori_loop(n_used, n_blocks, fill_block, 0)

        @pl.when(n_used < n_blocks)
        def _():
            wait_rows((n_blocks - n_used) * block_rows)


def _dispatch_call(dest_flat, pad_start, pad_len, n_used, x1r, n_blocks, rs):
    n_assign = dest_flat.shape[0]
    chunk = min(DISPATCH_CHUNK, n_assign)
    n_exp = pad_start.shape[0]
    br = EXPERT_BLOCK_ROWS
    assert x1r.shape[0] >= br * rs
    return pl.pallas_call(
        functools.partial(_dispatch_kernel, rs=rs, chunk=chunk, n_exp=n_exp, block_rows=br, n_blocks=n_blocks),
        grid_spec=pltpu.PrefetchScalarGridSpec(
            num_scalar_prefetch=4,
            grid=(n_assign // chunk,),
            in_specs=[pl.BlockSpec(memory_space=pl.ANY)],
            out_specs=pl.BlockSpec(memory_space=pl.ANY),
            scratch_shapes=[pltpu.SemaphoreType.DMA],
        ),
        out_shape=jax.ShapeDtypeStruct((n_blocks * br * rs, LANES), _F32),
        compiler_params=pltpu.CompilerParams(dimension_semantics=("arbitrary",)),
        name="dispatch",
    )(dest_flat, pad_start, pad_len, n_used, x1r)


def _expert_kernel(be_ref, nu_ref, xs_ref, wg_ref, wu_ref, bg_ref, bu_ref, wd_ref, bd_ref, ys_ref, *, rs):
    rows = xs_ref.shape[0] // rs

    @pl.when(pl.program_id(0) < nu_ref[0])
    def _():
        x = jnp.concatenate(_load_rows(xs_ref, rows, rs), axis=1).astype(_BF16)
        g = jnp.dot(x, wg_ref[0], preferred_element_type=_F32) + bg_ref[0]
        up = jnp.dot(x, wu_ref[0], preferred_element_type=_F32) + bu_ref[0]
        g = jnp.minimum(g, SWIGLU_LIMIT)
        up = jnp.clip(up, -SWIGLU_LIMIT, SWIGLU_LIMIT)
        act = (up + 1.0) * (g * jax.nn.sigmoid(SWIGLU_ALPHA * g))
        y = jnp.dot(act.astype(_BF16), wd_ref[0], preferred_element_type=_F32) + bd_ref[0]
        _store_rows(ys_ref, y)


def _expert_call(block_expert, n_used, xs, w_g, w_u, b_g, b_u, w_d, b_d, rs):
    n_exp, d, f = w_g.shape
    br = EXPERT_BLOCK_ROWS
    n_blocks = xs.shape[0] // (br * rs)

    def row_map(b, be, nu):
        return (jnp.minimum(b, nu[0] - 1), 0)

    def w_map(b, be, nu):
        return (be[b], 0, 0)

    return pl.pallas_call(
        functools.partial(_expert_kernel, rs=rs),
        grid_spec=pltpu.PrefetchScalarGridSpec(
            num_scalar_prefetch=2,
            grid=(n_blocks,),
            in_specs=[
                pl.BlockSpec((br * rs, LANES), row_map),
                pl.BlockSpec((1, d, f), w_map),
                pl.BlockSpec((1, d, f), w_map),
                pl.BlockSpec((1, 1, f), w_map),
                pl.BlockSpec((1, 1, f), w_map),
                pl.BlockSpec((1, f, d), w_map),
                pl.BlockSpec((1, 1, d), w_map),
            ],
            out_specs=pl.BlockSpec((br * rs, LANES), row_map),
        ),
        out_shape=jax.ShapeDtypeStruct(xs.shape, _F32),
        input_output_aliases={2: 0},
        compiler_params=pltpu.CompilerParams(
            dimension_semantics=("arbitrary",), vmem_limit_bytes=VMEM_LIMIT_BYTES),
        name="experts",
    )(block_expert, n_used, xs, w_g, w_u, b_g, b_u, w_d, b_d)


def _combine_kernel(dest_ref, gates_ref, x1r_ref, g_ref, b_ref, ys_hbm, out_ref, gbuf_ref, sem, *, rs, alpha):
    i = pl.program_id(0)
    tb = out_ref.shape[0]

    def issue(n, carry):
        d = dest_ref[i * tb * TOP_K + n]
        pltpu.make_async_copy(ys_hbm.at[pl.ds(d * rs, rs)],
                              gbuf_ref.at[n % TOP_K, pl.ds((n // TOP_K) * rs, rs)], sem).start()
        return carry

    lax.fori_loop(0, tb * TOP_K, issue, 0)
    for k in range(TOP_K):
        pltpu.make_async_copy(ys_hbm.at[pl.ds(0, tb * rs)], gbuf_ref.at[k], sem).wait()

    gates = gates_ref[...]
    pieces = []
    for j in range(rs):
        piece = alpha * x1r_ref[pl.ds(j, tb, stride=rs), :]
        for k in range(TOP_K):
            piece = piece + gates[:, k:k + 1] * gbuf_ref[k, pl.ds(j, tb, stride=rs), :]
        pieces.append(piece)
    out_ref[...] = _layer_norm(jnp.concatenate(pieces, axis=1), g_ref[...], b_ref[...])


def _combine_call(dest_flat, gates_tk, x1r, ln_g, ln_b, ys, alpha, rs):
    n_tok = gates_tk.shape[0]
    d = rs * LANES
    tb = min(COMBINE_BLOCK, n_tok)
    return pl.pallas_call(
        functools.partial(_combine_kernel, rs=rs, alpha=alpha),
        grid_spec=pltpu.PrefetchScalarGridSpec(
            num_scalar_prefetch=1,
            grid=(n_tok // tb,),
            in_specs=[
                pl.BlockSpec((tb, TOP_K), lambda i, dest: (i, 0)),
                pl.BlockSpec((tb * rs, LANES), lambda i, dest: (i, 0)),
                pl.BlockSpec((1, d), lambda i, dest: (0, 0)),
                pl.BlockSpec((1, d), lambda i, dest: (0, 0)),
                pl.BlockSpec(memory_space=pl.ANY),
            ],
            out_specs=pl.BlockSpec((tb, d), lambda i, dest: (i, 0)),
            scratch_shapes=[pltpu.VMEM((TOP_K, tb * rs, LANES), _F32), pltpu.SemaphoreType.DMA],
        ),
        out_shape=jax.ShapeDtypeStruct((n_tok, d), _F32),
        compiler_params=pltpu.CompilerParams(
            dimension_semantics=("arbitrary",), vmem_limit_bytes=VMEM_LIMIT_BYTES),
        name="combine",
    )(dest_flat, gates_tk, x1r, ln_g.reshape(1, d), ln_b.reshape(1, d), ys)


def _routing_tables(meta_i, counts_f, n_blocks):
    n_exp = counts_f.shape[0]
    br = EXPERT_BLOCK_ROWS
    counts = counts_f[:, 0].astype(jnp.int32)
    padded = ((counts + br - 1) // br) * br
    pend = jnp.cumsum(padded)
    pstart = pend - padded
    dest = pstart[meta_i[:TOP_K]] + meta_i[TOP_K:]
    dest_flat = dest.T.reshape(-1)
    n_used = (pend[-1] // br).astype(jnp.int32)
    blk = jnp.minimum(jnp.arange(n_blocks, dtype=jnp.int32), n_used - 1)
    block_expert = jnp.minimum(jnp.searchsorted(pend, blk * br, side="right"), n_exp - 1).astype(jnp.int32)
    return dest_flat, block_expert, n_used.reshape(1), pstart + counts, padded - counts


def _moe_layer(x1r, meta_i, gates_t, counts_f, w_gu, b_gu, w_dn, b_dn, ln_g, ln_b, alpha, rs):
    n_exp, d, f2 = w_gu.shape
    f = f2 // 2
    n_tok = gates_t.shape[1]
    br = EXPERT_BLOCK_ROWS
    n_blocks = -(-(n_tok * TOP_K) // br) + n_exp
    dest_flat, block_expert, n_used, pad_start, pad_len = _routing_tables(meta_i, counts_f, n_blocks)
    xs = _dispatch_call(dest_flat, pad_start, pad_len, n_used, x1r, n_blocks, rs)
    w_g = w_gu[:, :, 0::2].astype(_BF16)
    w_u = w_gu[:, :, 1::2].astype(_BF16)
    b_g = b_gu[:, 0::2].reshape(n_exp, 1, f)
    b_u = b_gu[:, 1::2].reshape(n_exp, 1, f)
    ys = _expert_call(block_expert, n_used, xs, w_g, w_u, b_g, b_u,
                      w_dn.astype(_BF16), b_dn.reshape(n_exp, 1, d), rs)
    return _combine_call(dest_flat, gates_t.T, x1r, ln_g, ln_b, ys, alpha, rs)


def kernel(x, pool_w, pool_scale, sc_w_in, sc_conv_w, sc_w_out, cf_w_in, cf_b_in, cf_dw_w, cf_dw_b,
           cf_ln_g, cf_ln_b, cf_w_out, cf_b_out, mix_ln_g, mix_ln_b, router_w, router_b,
           moe_w_gu, moe_b_gu, moe_w_dn, moe_b_dn, ffn_ln_g, ffn_ln_b):
    bsz, seq, d = x.shape
    depth = mix_ln_g.shape[0]
    alpha = (2.0 * depth) ** 0.25
    rs = d // LANES
    ia = ib = ic = 0
    for layer in range(depth):
        kind = layer % 3
        route = (mix_ln_g[layer], mix_ln_b[layer], router_w[layer], router_b[layer])
        if kind == 0:
            weights = [pool_w[ia].astype(_BF16), pool_scale[ia].reshape(1, d)]
            outs = _mixer_call(_pool_kernel, x, weights, *route, POOL_HALO, alpha)
            ia += 1
        elif kind == 1:
            weights = [sc_w_in[ib].astype(_BF16), sc_conv_w[ib], sc_w_out[ib].astype(_BF16)]
            outs = _mixer_call(_short_conv_kernel, x, weights, *route, SHORT_CONV_HALO, alpha)
            ib += 1
        else:
            weights = [cf_w_in[ic].astype(_BF16), cf_b_in[ic].reshape(1, 2 * d), cf_dw_w[ic],
                       cf_dw_b[ic].reshape(1, d), cf_ln_g[ic].reshape(1, d), cf_ln_b[ic].reshape(1, d),
                       cf_w_out[ic].astype(_BF16), cf_b_out[ic].reshape(1, d)]
            outs = _mixer_call(_conformer_kernel, x, weights, *route, CONFORMER_HALO, alpha)
            ic += 1
        x1r, meta_i, gates_t, counts_f = outs
        x = _moe_layer(x1r, meta_i, gates_t, counts_f, moe_w_gu[layer], moe_b_gu[layer],
                       moe_w_dn[layer], moe_b_dn[layer], ffn_ln_g[layer], ffn_ln_b[layer], alpha, rs)
        x = x.reshape(bsz, seq, d)
    return x
```

```python
import functools

import jax
import jax.numpy as jnp
from jax import lax
from jax.experimental import pallas as pl
from jax.experimental.pallas import tpu as pltpu

LANES = 128
TOP_K = 4
POOL_WINDOWS = (2, 4, 8, 16)
POOL_HALO = 16
SHORT_CONV_HALO = 8
CONFORMER_HALO = 32
SWIGLU_LIMIT = 7.0
SWIGLU_ALPHA = 1.702
LN_EPS = 1e-5
TOKEN_BLOCK = 512
EXPERT_BLOCK_ROWS = 256
COMBINE_BLOCK = 256
DISPATCH_BLOCK = 512
GATE_UP_CHUNK = 2 * LANES
VMEM_LIMIT_BYTES = 56 * 1024 * 1024

_F32 = jnp.float32
_BF16 = jnp.bfloat16


def _layer_norm(z, g, b):
    mu = jnp.mean(z, axis=-1, keepdims=True)
    zc = z - mu
    var = jnp.mean(zc * zc, axis=-1, keepdims=True)
    return zc * lax.rsqrt(var + LN_EPS) * g + b


def _store_rows(row_ref, val):
    rows, d = val.shape
    rs = d // LANES
    for j in range(rs):
        row_ref[pl.ds(j, rows, stride=rs), :] = val[:, j * LANES:(j + 1) * LANES]


def _load_rows(row_ref, rows, rs):
    return [row_ref[pl.ds(j, rows, stride=rs), :] for j in range(rs)]


def _post_norm_and_route(z, g_ref, b_ref, rwt_ref, rb_ref, first,
                         x1r_ref, mi_ref, mg_ref, cnt_ref, carry_ref):
    n_tok = z.shape[0]
    n_exp = rwt_ref.shape[0]

    @pl.when(first)
    def _():
        carry_ref[...] = jnp.zeros_like(carry_ref)

    x1 = _layer_norm(z, g_ref[...], b_ref[...])
    _store_rows(x1r_ref, x1)

    logits = lax.dot_general(rwt_ref[...], x1, (((1,), (1,)), ((), ())),
                             precision=lax.Precision.HIGHEST,
                             preferred_element_type=_F32) + rb_ref[...]
    eidx = lax.broadcasted_iota(jnp.int32, logits.shape, 0)
    work = logits
    chosen = jnp.zeros(logits.shape, jnp.bool_)
    vals, idxs = [], []
    for _ in range(TOP_K):
        m = jnp.max(work, axis=0, keepdims=True)
        sel = jnp.min(jnp.where(work == m, eidx, n_exp), axis=0, keepdims=True)
        hit = eidx == sel
        vals.append(m)
        idxs.append(sel)
        chosen = jnp.logical_or(chosen, hit)
        work = jnp.where(hit, -jnp.inf, work)
    exps = [jnp.exp(v - vals[0]) for v in vals]
    denom = exps[0] + exps[1] + exps[2] + exps[3]
    gate_rows = [e / denom for e in exps] + [jnp.zeros_like(denom)] * (mg_ref.shape[0] - TOP_K)
    mg_ref[...] = jnp.concatenate(gate_rows, axis=0)

    onehot = chosen.astype(_BF16)
    r = lax.broadcasted_iota(jnp.int32, (n_tok, n_tok), 0)
    c = lax.broadcasted_iota(jnp.int32, (n_tok, n_tok), 1)
    before = (r < c).astype(_BF16)
    cum = jnp.dot(onehot, before, preferred_element_type=_F32) + carry_ref[:, 0:1]
    ranks = [jnp.sum(jnp.where(eidx == s, cum, 0.0), axis=0, keepdims=True) for s in idxs]
    mi_ref[...] = jnp.concatenate(idxs + [rk.astype(jnp.int32) for rk in ranks], axis=0)
    carry_ref[...] = carry_ref[...] + jnp.sum(chosen.astype(_F32), axis=1, keepdims=True)
    cnt_ref[...] = carry_ref[...]


def _pool_kernel(x_ref, pw_ref, ps_ref, g_ref, b_ref, rwt_ref, rb_ref,
                 x1r_ref, mi_ref, mg_ref, cnt_ref, hist_ref, carry_ref, *, alpha):
    bi, si = pl.program_id(0), pl.program_id(1)
    ts, d = x_ref.shape[1], x_ref.shape[2]
    dg = d // len(POOL_WINDOWS)

    @pl.when(si == 0)
    def _():
        hist_ref[0:POOL_HALO, :] = jnp.zeros((POOL_HALO, d), _F32)

    x = x_ref[0]
    hist_ref[POOL_HALO:POOL_HALO + ts, :] = x
    pos = si * ts + lax.broadcasted_iota(jnp.int32, (ts, 1), 0)
    pieces = []
    for gi, win in enumerate(POOL_WINDOWS):
        c0 = gi * dg
        xg = x[:, c0:c0 + dg]
        acc = xg
        for j in range(1, win):
            acc = acc + hist_ref[POOL_HALO - j:POOL_HALO - j + ts, c0:c0 + dg]
        inv_count = 1.0 / jnp.minimum(pos + 1, win).astype(_F32)
        diff = acc * inv_count - xg
        hg = jnp.dot(diff.astype(_BF16), pw_ref[gi], preferred_element_type=_F32)
        pieces.append(alpha * xg + hg * ps_ref[:, c0:c0 + dg])
    hist_ref[0:POOL_HALO, :] = x[ts - POOL_HALO:, :]
    z = jnp.concatenate(pieces, axis=1)
    _post_norm_and_route(z, g_ref, b_ref, rwt_ref, rb_ref, (bi == 0) & (si == 0),
                         x1r_ref, mi_ref, mg_ref, cnt_ref, carry_ref)


def _short_conv_kernel(x_ref, win_ref, cw_ref, wout_ref, g_ref, b_ref, rwt_ref, rb_ref,
                       x1r_ref, mi_ref, mg_ref, cnt_ref, hist_ref, carry_ref, *, alpha):
    bi, si = pl.program_id(0), pl.program_id(1)
    ts, d = x_ref.shape[1], x_ref.shape[2]
    halo = SHORT_CONV_HALO

    @pl.when(si == 0)
    def _():
        hist_ref[0:halo, :] = jnp.zeros((halo, d), _F32)

    x = x_ref[0]
    xb = x.astype(_BF16)
    gate_b = jnp.dot(xb, win_ref[:, 0:d], preferred_element_type=_F32)
    gate_c = jnp.dot(xb, win_ref[:, d:2 * d], preferred_element_type=_F32)
    h = jnp.dot(xb, win_ref[:, 2 * d:3 * d], preferred_element_type=_F32)
    v = gate_c * h
    hist_ref[halo:halo + ts, :] = v
    width = cw_ref.shape[0]
    u = cw_ref[width - 1:width, :] * v
    for k in range(width - 1):
        shift = width - 1 - k
        u = u + cw_ref[k:k + 1, :] * hist_ref[halo - shift:halo - shift + ts, :]
    hist_ref[0:halo, :] = v[ts - halo:, :]
    y = jnp.dot((gate_b * u).astype(_BF16), wout_ref[...], preferred_element_type=_F32)
    _post_norm_and_route(alpha * x + y, g_ref, b_ref, rwt_ref, rb_ref, (bi == 0) & (si == 0),
                         x1r_ref, mi_ref, mg_ref, cnt_ref, carry_ref)


def _conformer_kernel(x_ref, win_ref, bin_ref, dww_ref, dwb_ref, lng_ref, lnb_ref, wout_ref, bout_ref,
                      g_ref, b_ref, rwt_ref, rb_ref,
                      x1r_ref, mi_ref, mg_ref, cnt_ref, hist_ref, carry_ref, *, alpha):
    bi, si = pl.program_id(0), pl.program_id(1)
    ts, d = x_ref.shape[1], x_ref.shape[2]
    halo = CONFORMER_HALO

    @pl.when(si == 0)
    def _():
        hist_ref[0:halo, :] = jnp.zeros((halo, d), _F32)

    x = x_ref[0]
    xb = x.astype(_BF16)
    a = jnp.dot(xb, win_ref[:, 0:d], preferred_element_type=_F32) + bin_ref[:, 0:d]
    gate = jnp.dot(xb, win_ref[:, d:2 * d], preferred_element_type=_F32) + bin_ref[:, d:2 * d]
    u = a * jax.nn.sigmoid(gate)
    hist_ref[halo:halo + ts, :] = u
    width = dww_ref.shape[0]
    acc = dwb_ref[...] + dww_ref[width - 1:width, :] * u
    for k in range(width - 1):
        shift = width - 1 - k
        acc = acc + dww_ref[k:k + 1, :] * hist_ref[halo - shift:halo - shift + ts, :]
    hist_ref[0:halo, :] = u[ts - halo:, :]
    un = _layer_norm(acc, lng_ref[...], lnb_ref[...])
    un = un * jax.nn.sigmoid(un)
    y = jnp.dot(un.astype(_BF16), wout_ref[...], preferred_element_type=_F32) + bout_ref[...]
    _post_norm_and_route(alpha * x + y, g_ref, b_ref, rwt_ref, rb_ref, (bi == 0) & (si == 0),
                         x1r_ref, mi_ref, mg_ref, cnt_ref, carry_ref)


def _mixer_call(kernel_fn, x, weights, ln_g, ln_b, router_w, router_b, halo, alpha):
    bsz, seq, d = x.shape
    n_exp = router_w.shape[1]
    ts = min(TOKEN_BLOCK, seq)
    rs = d // LANES
    n_tok = bsz * seq
    nsb = seq // ts

    def full(a):
        nd = a.ndim
        return pl.BlockSpec(a.shape, lambda bi, si, _nd=nd: (0,) * _nd)

    small = [ln_g.reshape(1, d), ln_b.reshape(1, d), router_w.T, router_b.reshape(n_exp, 1)]
    operands = [x] + list(weights) + small
    in_specs = [pl.BlockSpec((1, ts, d), lambda bi, si: (bi, si, 0))] + [full(a) for a in operands[1:]]
    tok_map = lambda bi, si: (0, bi * nsb + si)
    out_shape = [
        jax.ShapeDtypeStruct((n_tok * rs, LANES), _F32),
        jax.ShapeDtypeStruct((2 * TOP_K, n_tok), jnp.int32),
        jax.ShapeDtypeStruct((2 * TOP_K, n_tok), _F32),
        jax.ShapeDtypeStruct((n_exp, LANES), _F32),
    ]
    out_specs = [
        pl.BlockSpec((ts * rs, LANES), lambda bi, si: (bi * nsb + si, 0)),
        pl.BlockSpec((2 * TOP_K, ts), tok_map),
        pl.BlockSpec((2 * TOP_K, ts), tok_map),
        pl.BlockSpec((n_exp, LANES), lambda bi, si: (0, 0)),
    ]
    return pl.pallas_call(
        functools.partial(kernel_fn, alpha=alpha),
        grid=(bsz, nsb),
        in_specs=in_specs,
        out_specs=out_specs,
        out_shape=out_shape,
        scratch_shapes=[pltpu.VMEM((halo + ts, d), _F32), pltpu.VMEM((n_exp, LANES), _F32)],
        compiler_params=pltpu.CompilerParams(
            dimension_semantics=("arbitrary", "arbitrary"), vmem_limit_bytes=VMEM_LIMIT_BYTES),
        name=kernel_fn.__name__.strip("_"),
    )(*operands)


def _dispatch_kernel(dest_ref, pad_start_ref, pad_len_ref, nu_ref, x1r_ref, xs_hbm, sem,
                     *, rs, n_tok, n_exp, block_rows, n_blocks):
    i = pl.program_id(0)
    tb = x1r_ref.shape[0] // rs

    def wait_rows(n):
        pltpu.make_async_copy(xs_hbm.at[pl.ds(0, n * rs)], xs_hbm.at[pl.ds(0, n * rs)], sem).wait()

    def put_row(src_row, dst_row):
        pltpu.make_async_copy(x1r_ref.at[pl.ds(pl.multiple_of(src_row * rs, rs), rs)],
                              xs_hbm.at[pl.ds(pl.multiple_of(dst_row * rs, rs), rs)], sem).start()

    def issue(tl, carry):
        t = i * tb + tl
        for k in range(TOP_K):
            put_row(tl, dest_ref[k * n_tok + t])
        return carry

    lax.fori_loop(0, tb, issue, 0)
    wait_rows(tb * TOP_K)

    @pl.when(i == pl.num_programs(0) - 1)
    def _():
        def per_expert(e, total):
            start, n = pad_start_ref[e], pad_len_ref[e]

            def fill(r, carry):
                put_row(0, start + r)
                return carry

            lax.fori_loop(0, n, fill, 0)
            return total + n

        n_pad = lax.fori_loop(0, n_exp, per_expert, 0)

        @pl.when(n_pad > 0)
        def _():
            wait_rows(n_pad)

        n_used = nu_ref[0]
        blk = block_rows * rs

        def fill_block(b, carry):
            pltpu.make_async_copy(x1r_ref.at[pl.ds(0, blk)], xs_hbm.at[pl.ds(b * blk, blk)], sem).start()
            return carry

        lax.fori_loop(n_used, n_blocks, fill_block, 0)

        @pl.when(n_used < n_blocks)
        def _():
            wait_rows((n_blocks - n_used) * block_rows)


def _dispatch_call(dest_flat, pad_start, pad_len, n_used, x1r, n_blocks, rs):
    n_tok = x1r.shape[0] // rs
    tb = min(DISPATCH_BLOCK, n_tok)
    n_exp = pad_start.shape[0]
    br = EXPERT_BLOCK_ROWS
    assert tb >= br and n_tok % tb == 0
    return pl.pallas_call(
        functools.partial(_dispatch_kernel, rs=rs, n_tok=n_tok, n_exp=n_exp, block_rows=br, n_blocks=n_blocks),
        grid_spec=pltpu.PrefetchScalarGridSpec(
            num_scalar_prefetch=4,
            grid=(n_tok // tb,),
            in_specs=[pl.BlockSpec((tb * rs, LANES), lambda i, *_: (i, 0))],
            out_specs=pl.BlockSpec(memory_space=pl.ANY),
            scratch_shapes=[pltpu.SemaphoreType.DMA],
        ),
        out_shape=jax.ShapeDtypeStruct((n_blocks * br * rs, LANES), _F32),
        compiler_params=pltpu.CompilerParams(dimension_semantics=("arbitrary",)),
        name="dispatch",
    )(dest_flat, pad_start, pad_len, n_used, x1r)


def _expert_kernel(be_ref, nu_ref, xs_ref, wgu_ref, bgu_ref, wdn_ref, bdn_ref, ys_ref, wgu_s, wdn_s, *, rs):
    b = pl.program_id(0)
    rows = xs_ref.shape[0] // rs
    n_chunks = wgu_ref.shape[2] // GATE_UP_CHUNK
    half = GATE_UP_CHUNK // 2

    @pl.when(b < nu_ref[0])
    def _():
        @pl.when((b == 0) | (be_ref[b] != be_ref[jnp.maximum(b - 1, 0)]))
        def _():
            r = lax.broadcasted_iota(jnp.int32, (GATE_UP_CHUNK, GATE_UP_CHUNK), 0)
            c = lax.broadcasted_iota(jnp.int32, (GATE_UP_CHUNK, GATE_UP_CHUNK), 1)
            perm = (r == jnp.where(c < half, 2 * c, 2 * (c - half) + 1)).astype(_BF16)
            for ch in range(n_chunks):
                cols = slice(ch * GATE_UP_CHUNK, (ch + 1) * GATE_UP_CHUNK)
                w = wgu_ref[0, :, cols].astype(_BF16)
                wgu_s[:, cols] = jnp.dot(w, perm, preferred_element_type=_F32).astype(_BF16)
            wdn_s[...] = wdn_ref[0].astype(_BF16)

        x = jnp.concatenate(_load_rows(xs_ref, rows, rs), axis=1).astype(_BF16)
        h = jnp.dot(x, wgu_s[...], preferred_element_type=_F32) + bgu_ref[0]
        acts = []
        for ch in range(n_chunks):
            g = jnp.minimum(h[:, ch * GATE_UP_CHUNK:ch * GATE_UP_CHUNK + half], SWIGLU_LIMIT)
            up = jnp.clip(h[:, ch * GATE_UP_CHUNK + half:(ch + 1) * GATE_UP_CHUNK], -SWIGLU_LIMIT, SWIGLU_LIMIT)
            acts.append(((up + 1.0) * (g * jax.nn.sigmoid(SWIGLU_ALPHA * g))).astype(_BF16))
        y = jnp.dot(jnp.concatenate(acts, axis=1), wdn_s[...], preferred_element_type=_F32) + bdn_ref[0]
        _store_rows(ys_ref, y)


def _expert_call(block_expert, n_used, xs, layer, w_gu_all, b_gu_grouped, w_dn_all, b_dn, rs):
    _, n_exp, d, f2 = w_gu_all.shape
    f = f2 // 2
    br = EXPERT_BLOCK_ROWS
    n_blocks = xs.shape[0] // (br * rs)
    assert f2 % GATE_UP_CHUNK == 0

    def row_map(b, be, nu):
        return (jnp.minimum(b, nu[0] - 1), 0)

    def w_map(b, be, nu):
        return (be[b], 0, 0)

    def stack_map(b, be, nu):
        return (layer, be[b], 0, 0)

    return pl.pallas_call(
        functools.partial(_expert_kernel, rs=rs),
        grid_spec=pltpu.PrefetchScalarGridSpec(
            num_scalar_prefetch=2,
            grid=(n_blocks,),
            in_specs=[
                pl.BlockSpec((br * rs, LANES), row_map),
                pl.BlockSpec((None, 1, d, f2), stack_map),
                pl.BlockSpec((1, 1, f2), w_map),
                pl.BlockSpec((None, 1, f, d), stack_map),
                pl.BlockSpec((1, 1, d), w_map),
            ],
            out_specs=pl.BlockSpec((br * rs, LANES), row_map),
            scratch_shapes=[pltpu.VMEM((d, f2), _BF16), pltpu.VMEM((f, d), _BF16)],
        ),
        out_shape=jax.ShapeDtypeStruct(xs.shape, _F32),
        input_output_aliases={2: 0},
        compiler_params=pltpu.CompilerParams(
            dimension_semantics=("arbitrary",), vmem_limit_bytes=VMEM_LIMIT_BYTES),
        name="experts",
    )(block_expert, n_used, xs, w_gu_all, b_gu_grouped, w_dn_all, b_dn)


def _combine_kernel(dest_ref, gates_ref, x1r_ref, g_ref, b_ref, ys_hbm, out_ref, gbuf_ref, sem,
                    *, rs, alpha, n_tok):
    i = pl.program_id(0)
    n_steps = pl.num_programs(0)
    tb = out_ref.shape[0]
    slot = i % 2

    def gather_block(blk, to_slot):
        def issue(tl, carry):
            t = blk * tb + tl
            for k in range(TOP_K):
                d = dest_ref[k * n_tok + t]
                pltpu.make_async_copy(ys_hbm.at[pl.ds(pl.multiple_of(d * rs, rs), rs)],
                                      gbuf_ref.at[to_slot, k, pl.ds(pl.multiple_of(tl * rs, rs), rs)],
                                      sem.at[to_slot]).start()
            return carry

        lax.fori_loop(0, tb, issue, 0)

    @pl.when(i == 0)
    def _():
        gather_block(0, 0)

    @pl.when(i + 1 < n_steps)
    def _():
        gather_block(i + 1, 1 - slot)

    for k in range(TOP_K):
        pltpu.make_async_copy(ys_hbm.at[pl.ds(0, tb * rs)], gbuf_ref.at[slot, k], sem.at[slot]).wait()

    gates = gates_ref[...].T
    pieces = []
    for j in range(rs):
        piece = alpha * x1r_ref[pl.ds(j, tb, stride=rs), :]
        for k in range(TOP_K):
            piece = piece + gates[:, k:k + 1] * gbuf_ref[slot, k, pl.ds(j, tb, stride=rs), :]
        pieces.append(piece)
    out_ref[...] = _layer_norm(jnp.concatenate(pieces, axis=1), g_ref[...], b_ref[...])


def _combine_call(dest_flat, gates_t, x1r, ln_g, ln_b, ys, alpha, rs):
    n_tok = gates_t.shape[1]
    d = rs * LANES
    tb = min(COMBINE_BLOCK, n_tok)
    return pl.pallas_call(
        functools.partial(_combine_kernel, rs=rs, alpha=alpha, n_tok=n_tok),
        grid_spec=pltpu.PrefetchScalarGridSpec(
            num_scalar_prefetch=1,
            grid=(n_tok // tb,),
            in_specs=[
                pl.BlockSpec((2 * TOP_K, tb), lambda i, dest: (0, i)),
                pl.BlockSpec((tb * rs, LANES), lambda i, dest: (i, 0)),
                pl.BlockSpec((1, d), lambda i, dest: (0, 0)),
                pl.BlockSpec((1, d), lambda i, dest: (0, 0)),
                pl.BlockSpec(memory_space=pl.ANY),
            ],
            out_specs=pl.BlockSpec((tb, d), lambda i, dest: (i, 0)),
            scratch_shapes=[pltpu.VMEM((2, TOP_K, tb * rs, LANES), _F32), pltpu.SemaphoreType.DMA((2,))],
        ),
        out_shape=jax.ShapeDtypeStruct((n_tok, d), _F32),
        compiler_params=pltpu.CompilerParams(
            dimension_semantics=("arbitrary",), vmem_limit_bytes=VMEM_LIMIT_BYTES),
        name="combine",
    )(dest_flat, gates_t, x1r, ln_g.reshape(1, d), ln_b.reshape(1, d), ys)


def _routing_tables(meta_i, counts_f, n_blocks):
    n_exp = counts_f.shape[0]
    br = EXPERT_BLOCK_ROWS
    counts = counts_f[:, 0].astype(jnp.int32)
    padded = ((counts + br - 1) // br) * br
    pend = jnp.cumsum(padded)
    pstart = pend - padded
    eids = jnp.arange(n_exp, dtype=jnp.int32)
    idx, rank = meta_i[:TOP_K], meta_i[TOP_K:]
    dest = jnp.sum(jnp.where(idx[..., None] == eids, pstart, 0), axis=-1) + rank
    n_used = (pend[-1] // br).astype(jnp.int32)
    blk = jnp.minimum(jnp.arange(n_blocks, dtype=jnp.int32), n_used - 1)
    block_expert = jnp.minimum(jnp.sum((pend[None, :] <= (blk * br)[:, None]).astype(jnp.int32), axis=1),
                               n_exp - 1)
    return dest.reshape(-1), block_expert, n_used.reshape(1), pstart + counts, padded - counts


def _moe_layer(x1r, meta_i, gates_t, counts_f, layer, w_gu_all, b_gu, w_dn_all, b_dn, ln_g, ln_b, alpha, rs):
    _, n_exp, d, f2 = w_gu_all.shape
    n_tok = gates_t.shape[1]
    br = EXPERT_BLOCK_ROWS
    n_blocks = -(-(n_tok * TOP_K) // br) + n_exp
    dest_flat, block_expert, n_used, pad_start, pad_len = _routing_tables(meta_i, counts_f, n_blocks)
    xs = _dispatch_call(dest_flat, pad_start, pad_len, n_used, x1r, n_blocks, rs)
    half = GATE_UP_CHUNK // 2
    b_gu_grouped = b_gu.reshape(n_exp, f2 // GATE_UP_CHUNK, half, 2).transpose(0, 1, 3, 2).reshape(n_exp, 1, f2)
    ys = _expert_call(block_expert, n_used, xs, layer, w_gu_all, b_gu_grouped, w_dn_all,
                      b_dn.reshape(n_exp, 1, d), rs)
    return _combine_call(dest_flat, gates_t, x1r, ln_g, ln_b, ys, alpha, rs)


def kernel(x, pool_w, pool_scale, sc_w_in, sc_conv_w, sc_w_out, cf_w_in, cf_b_in, cf_dw_w, cf_dw_b,
           cf_ln_g, cf_ln_b, cf_w_out, cf_b_out, mix_ln_g, mix_ln_b, router_w, router_b,
           moe_w_gu, moe_b_gu, moe_w_dn, moe_b_dn, ffn_ln_g, ffn_ln_b):
    bsz, seq, d = x.shape
    depth = mix_ln_g.shape[0]
    alpha = (2.0 * depth) ** 0.25
    rs = d // LANES
    ia = ib = ic = 0
    for layer in range(depth):
        kind = layer % 3
        route = (mix_ln_g[layer], mix_ln_b[layer], router_w[layer], router_b[layer])
        if kind == 0:
            weights = [pool_w[ia].astype(_BF16), pool_scale[ia].reshape(1, d)]
            outs = _mixer_call(_pool_kernel, x, weights, *route, POOL_HALO, alpha)
            ia += 1
        elif kind == 1:
            weights = [sc_w_in[ib].astype(_BF16), sc_conv_w[ib], sc_w_out[ib].astype(_BF16)]
            outs = _mixer_call(_short_conv_kernel, x, weights, *route, SHORT_CONV_HALO, alpha)
            ib += 1
        else:
            weights = [cf_w_in[ic].astype(_BF16), cf_b_in[ic].reshape(1, 2 * d), cf_dw_w[ic],
                       cf_dw_b[ic].reshape(1, d), cf_ln_g[ic].reshape(1, d), cf_ln_b[ic].reshape(1, d),
                       cf_w_out[ic].astype(_BF16), cf_b_out[ic].reshape(1, d)]
            outs = _mixer_call(_conformer_kernel, x, weights, *route, CONFORMER_HALO, alpha)
            ic += 1
        x1r, meta_i, gates_t, counts_f = outs
        x = _moe_layer(x1r, meta_i, gates_t, counts_f, layer, moe_w_gu, moe_b_gu[layer],
                       moe_w_dn, moe_b_dn[layer], ffn_ln_g[layer], ffn_ln_b[layer], alpha, rs)
        x = x.reshape(bsz, seq, d)
    return x
```

```python
import functools

import jax
import jax.numpy as jnp
from jax import lax
from jax.experimental import pallas as pl
from jax.experimental.pallas import tpu as pltpu

LANES = 128
SUBLANES = 8
TOP_K = 4
POOL_WINDOWS = (2, 4, 8, 16)
POOL_HALO = 16
SHORT_CONV_HALO = 8
CONFORMER_HALO = 32
SWIGLU_LIMIT = 7.0
SWIGLU_ALPHA = 1.702
LN_EPS = 1e-5
TOKEN_BLOCK = 512
EXPERT_BLOCK_ROWS = 512
COMBINE_BLOCK = 256
DISPATCH_BLOCK = 512
GATE_UP_CHUNK = 2 * LANES
VMEM_LIMIT_BYTES = 56 * 1024 * 1024

_F32 = jnp.float32
_BF16 = jnp.bfloat16


def _layer_norm(z, g, b):
    mu = jnp.mean(z, axis=-1, keepdims=True)
    zc = z - mu
    var = jnp.mean(zc * zc, axis=-1, keepdims=True)
    return zc * lax.rsqrt(var + LN_EPS) * g + b


def _store_rows(row_ref, val):
    rows, d = val.shape
    rs = d // LANES
    for j in range(rs):
        row_ref[pl.ds(j, rows, stride=rs), :] = val[:, j * LANES:(j + 1) * LANES]


def _load_rows(row_ref, rows, rs):
    return [row_ref[pl.ds(j, rows, stride=rs), :] for j in range(rs)]


def _post_norm_and_route(z, g_ref, b_ref, rwt_ref, rb_ref, first,
                         x1r_ref, mi_ref, mg_ref, cnt_ref, carry_ref):
    n_tok = z.shape[0]
    n_exp = rwt_ref.shape[0]

    @pl.when(first)
    def _():
        carry_ref[...] = jnp.zeros_like(carry_ref)

    x1 = _layer_norm(z, g_ref[...], b_ref[...])
    _store_rows(x1r_ref, x1)

    logits = lax.dot_general(rwt_ref[...], x1, (((1,), (1,)), ((), ())),
                             precision=lax.Precision.HIGHEST,
                             preferred_element_type=_F32) + rb_ref[...]
    eidx = lax.broadcasted_iota(jnp.int32, logits.shape, 0)
    work = logits
    chosen = jnp.zeros(logits.shape, jnp.bool_)
    vals, idxs = [], []
    for _ in range(TOP_K):
        m = jnp.max(work, axis=0, keepdims=True)
        sel = jnp.min(jnp.where(work == m, eidx, n_exp), axis=0, keepdims=True)
        hit = eidx == sel
        vals.append(m)
        idxs.append(sel)
        chosen = jnp.logical_or(chosen, hit)
        work = jnp.where(hit, -jnp.inf, work)
    exps = [jnp.exp(v - vals[0]) for v in vals]
    denom = exps[0] + exps[1] + exps[2] + exps[3]
    gate_rows = [e / denom for e in exps] + [jnp.zeros_like(denom)] * (mg_ref.shape[0] - TOP_K)
    mg_ref[...] = jnp.concatenate(gate_rows, axis=0)

    onehot = chosen.astype(_BF16)
    r = lax.broadcasted_iota(jnp.int32, (n_tok, n_tok), 0)
    c = lax.broadcasted_iota(jnp.int32, (n_tok, n_tok), 1)
    before = (r < c).astype(_BF16)
    cum = jnp.dot(onehot, before, preferred_element_type=_F32) + carry_ref[:, 0:1]
    ranks = [jnp.sum(jnp.where(eidx == s, cum, 0.0), axis=0, keepdims=True) for s in idxs]
    mi_ref[...] = jnp.concatenate(idxs + [rk.astype(jnp.int32) for rk in ranks], axis=0)
    carry_ref[...] = carry_ref[...] + jnp.sum(chosen.astype(_F32), axis=1, keepdims=True)
    cnt_ref[...] = carry_ref[...]


def _pool_kernel(x_ref, pw_ref, ps_ref, g_ref, b_ref, rwt_ref, rb_ref,
                 x1r_ref, mi_ref, mg_ref, cnt_ref, hist_ref, carry_ref, *, alpha):
    bi, si = pl.program_id(0), pl.program_id(1)
    ts, d = x_ref.shape[1], x_ref.shape[2]
    dg = d // len(POOL_WINDOWS)

    @pl.when(si == 0)
    def _():
        hist_ref[0:POOL_HALO, :] = jnp.zeros((POOL_HALO, d), _F32)

    x = x_ref[0]
    hist_ref[POOL_HALO:POOL_HALO + ts, :] = x
    pos = si * ts + lax.broadcasted_iota(jnp.int32, (ts, 1), 0)
    pieces = []
    for gi, win in enumerate(POOL_WINDOWS):
        c0 = gi * dg
        xg = x[:, c0:c0 + dg]
        assert win & (win - 1) == 0 and win <= POOL_HALO
        ext = hist_ref[:, c0:c0 + dg]
        span = 1
        while span < win:
            ext = ext + pltpu.roll(ext, span, 0)
            span *= 2
        acc = ext[POOL_HALO:, :]
        inv_count = 1.0 / jnp.minimum(pos + 1, win).astype(_F32)
        diff = acc * inv_count - xg
        hg = jnp.dot(diff.astype(_BF16), pw_ref[gi], preferred_element_type=_F32)
        pieces.append(alpha * xg + hg * ps_ref[:, c0:c0 + dg])
    hist_ref[0:POOL_HALO, :] = x[ts - POOL_HALO:, :]
    z = jnp.concatenate(pieces, axis=1)
    _post_norm_and_route(z, g_ref, b_ref, rwt_ref, rb_ref, (bi == 0) & (si == 0),
                         x1r_ref, mi_ref, mg_ref, cnt_ref, carry_ref)


def _short_conv_kernel(x_ref, win_ref, cw_ref, wout_ref, g_ref, b_ref, rwt_ref, rb_ref,
                       x1r_ref, mi_ref, mg_ref, cnt_ref, hist_ref, carry_ref, *, alpha):
    bi, si = pl.program_id(0), pl.program_id(1)
    ts, d = x_ref.shape[1], x_ref.shape[2]
    halo = SHORT_CONV_HALO

    @pl.when(si == 0)
    def _():
        hist_ref[0:halo, :] = jnp.zeros((halo, d), _F32)

    x = x_ref[0]
    xb = x.astype(_BF16)
    gate_b = jnp.dot(xb, win_ref[:, 0:d], preferred_element_type=_F32)
    gate_c = jnp.dot(xb, win_ref[:, d:2 * d], preferred_element_type=_F32)
    h = jnp.dot(xb, win_ref[:, 2 * d:3 * d], preferred_element_type=_F32)
    v = gate_c * h
    hist_ref[halo:halo + ts, :] = v
    width = cw_ref.shape[0]
    u = cw_ref[width - 1:width, :] * v
    for k in range(width - 1):
        shift = width - 1 - k
        u = u + cw_ref[k:k + 1, :] * hist_ref[halo - shift:halo - shift + ts, :]
    hist_ref[0:halo, :] = v[ts - halo:, :]
    y = jnp.dot((gate_b * u).astype(_BF16), wout_ref[...], preferred_element_type=_F32)
    _post_norm_and_route(alpha * x + y, g_ref, b_ref, rwt_ref, rb_ref, (bi == 0) & (si == 0),
                         x1r_ref, mi_ref, mg_ref, cnt_ref, carry_ref)


def _conformer_kernel(x_ref, win_ref, bin_ref, dww_ref, dwb_ref, lng_ref, lnb_ref, wout_ref, bout_ref,
                      g_ref, b_ref, rwt_ref, rb_ref,
                      x1r_ref, mi_ref, mg_ref, cnt_ref, hist_ref, carry_ref, *, alpha):
    bi, si = pl.program_id(0), pl.program_id(1)
    ts, d = x_ref.shape[1], x_ref.shape[2]
    halo = CONFORMER_HALO

    @pl.when(si == 0)
    def _():
        hist_ref[0:halo, :] = jnp.zeros((halo, d), _F32)

    x = x_ref[0]
    xb = x.astype(_BF16)
    a = jnp.dot(xb, win_ref[:, 0:d], preferred_element_type=_F32) + bin_ref[:, 0:d]
    gate = jnp.dot(xb, win_ref[:, d:2 * d], preferred_element_type=_F32) + bin_ref[:, d:2 * d]
    u = a * jax.nn.sigmoid(gate)
    hist_ref[halo:halo + ts, :] = u
    width = dww_ref.shape[0]
    hist = hist_ref[...]
    acc = dwb_ref[...]
    for r in range(SUBLANES):
        rolled = hist if r == 0 else pltpu.roll(hist, r, 0)
        for q in range(halo // SUBLANES):
            shift = SUBLANES * q + r
            if shift < width:
                k = width - 1 - shift
                start = halo - SUBLANES * q
                acc = acc + dww_ref[k:k + 1, :] * rolled[start:start + ts, :]
    hist_ref[0:halo, :] = u[ts - halo:, :]
    un = _layer_norm(acc, lng_ref[...], lnb_ref[...])
    un = un * jax.nn.sigmoid(un)
    y = jnp.dot(un.astype(_BF16), wout_ref[...], preferred_element_type=_F32) + bout_ref[...]
    _post_norm_and_route(alpha * x + y, g_ref, b_ref, rwt_ref, rb_ref, (bi == 0) & (si == 0),
                         x1r_ref, mi_ref, mg_ref, cnt_ref, carry_ref)


def _mixer_call(kernel_fn, x, weights, ln_g, ln_b, router_w, router_b, halo, alpha):
    bsz, seq, d = x.shape
    n_exp = router_w.shape[1]
    ts = min(TOKEN_BLOCK, seq)
    rs = d // LANES
    n_tok = bsz * seq
    nsb = seq // ts

    def full(a):
        nd = a.ndim
        return pl.BlockSpec(a.shape, lambda bi, si, _nd=nd: (0,) * _nd)

    small = [ln_g.reshape(1, d), ln_b.reshape(1, d), router_w.T, router_b.reshape(n_exp, 1)]
    operands = [x] + list(weights) + small
    in_specs = [pl.BlockSpec((1, ts, d), lambda bi, si: (bi, si, 0))] + [full(a) for a in operands[1:]]
    tok_map = lambda bi, si: (0, bi * nsb + si)
    out_shape = [
        jax.ShapeDtypeStruct((n_tok * rs, LANES), _F32),
        jax.ShapeDtypeStruct((2 * TOP_K, n_tok), jnp.int32),
        jax.ShapeDtypeStruct((2 * TOP_K, n_tok), _F32),
        jax.ShapeDtypeStruct((n_exp, LANES), _F32),
    ]
    out_specs = [
        pl.BlockSpec((ts * rs, LANES), lambda bi, si: (bi * nsb + si, 0)),
        pl.BlockSpec((2 * TOP_K, ts), tok_map),
        pl.BlockSpec((2 * TOP_K, ts), tok_map),
        pl.BlockSpec((n_exp, LANES), lambda bi, si: (0, 0)),
    ]
    return pl.pallas_call(
        functools.partial(kernel_fn, alpha=alpha),
        grid=(bsz, nsb),
        in_specs=in_specs,
        out_specs=out_specs,
        out_shape=out_shape,
        scratch_shapes=[pltpu.VMEM((halo + ts, d), _F32), pltpu.VMEM((n_exp, LANES), _F32)],
        compiler_params=pltpu.CompilerParams(
            dimension_semantics=("arbitrary", "arbitrary"), vmem_limit_bytes=VMEM_LIMIT_BYTES),
        name=kernel_fn.__name__.strip("_"),
    )(*operands)


def _dispatch_kernel(dest_ref, pad_start_ref, pad_len_ref, nu_ref, x1r_ref, xs_hbm, sem,
                     *, rs, n_tok, n_exp, block_rows, n_blocks):
    i = pl.program_id(0)
    tb = x1r_ref.shape[0] // rs

    def wait_rows(n):
        pltpu.make_async_copy(xs_hbm.at[pl.ds(0, n * rs)], xs_hbm.at[pl.ds(0, n * rs)], sem).wait()

    def put_row(src_row, dst_row):
        pltpu.make_async_copy(x1r_ref.at[pl.ds(pl.multiple_of(src_row * rs, rs), rs)],
                              xs_hbm.at[pl.ds(pl.multiple_of(dst_row * rs, rs), rs)], sem).start()

    def issue(tl, carry):
        t = i * tb + tl
        for k in range(TOP_K):
            put_row(tl, dest_ref[k * n_tok + t])
        return carry

    lax.fori_loop(0, tb, issue, 0)
    wait_rows(tb * TOP_K)

    @pl.when(i == pl.num_programs(0) - 1)
    def _():
        def per_expert(e, total):
            start, n = pad_start_ref[e], pad_len_ref[e]

            def fill(r, carry):
                put_row(0, start + r)
                return carry

            lax.fori_loop(0, n, fill, 0)
            return total + n

        n_pad = lax.fori_loop(0, n_exp, per_expert, 0)

        @pl.when(n_pad > 0)
        def _():
            wait_rows(n_pad)

        n_used = nu_ref[0]
        blk = block_rows * rs

        def fill_block(b, carry):
            pltpu.make_async_copy(x1r_ref.at[pl.ds(0, blk)], xs_hbm.at[pl.ds(b * blk, blk)], sem).start()
            return carry

        lax.fori_loop(n_used, n_blocks, fill_block, 0)

        @pl.when(n_used < n_blocks)
        def _():
            wait_rows((n_blocks - n_used) * block_rows)


def _dispatch_call(dest_flat, pad_start, pad_len, n_used, x1r, n_blocks, rs):
    n_tok = x1r.shape[0] // rs
    tb = min(DISPATCH_BLOCK, n_tok)
    n_exp = pad_start.shape[0]
    br = EXPERT_BLOCK_ROWS
    assert tb >= br and n_tok % tb == 0
    return pl.pallas_call(
        functools.partial(_dispatch_kernel, rs=rs, n_tok=n_tok, n_exp=n_exp, block_rows=br, n_blocks=n_blocks),
        grid_spec=pltpu.PrefetchScalarGridSpec(
            num_scalar_prefetch=4,
            grid=(n_tok // tb,),
            in_specs=[pl.BlockSpec((tb * rs, LANES), lambda i, *_: (i, 0))],
            out_specs=pl.BlockSpec(memory_space=pl.ANY),
            scratch_shapes=[pltpu.SemaphoreType.DMA],
        ),
        out_shape=jax.ShapeDtypeStruct((n_blocks * br * rs, LANES), _F32),
        compiler_params=pltpu.CompilerParams(dimension_semantics=("arbitrary",)),
        name="dispatch",
    )(dest_flat, pad_start, pad_len, n_used, x1r)


def _expert_kernel(be_ref, nu_ref, xs_ref, wgu_ref, bgu_ref, wdn_ref, bdn_ref, ys_ref, wgu_s, wdn_s, *, rs):
    b = pl.program_id(0)
    rows = xs_ref.shape[0] // rs
    n_chunks = wgu_ref.shape[2] // GATE_UP_CHUNK
    half = GATE_UP_CHUNK // 2

    @pl.when(b < nu_ref[0])
    def _():
        @pl.when((b == 0) | (be_ref[b] != be_ref[jnp.maximum(b - 1, 0)]))
        def _():
            r = lax.broadcasted_iota(jnp.int32, (GATE_UP_CHUNK, GATE_UP_CHUNK), 0)
            c = lax.broadcasted_iota(jnp.int32, (GATE_UP_CHUNK, GATE_UP_CHUNK), 1)
            perm = (r == jnp.where(c < half, 2 * c, 2 * (c - half) + 1)).astype(_BF16)
            for ch in range(n_chunks):
                cols = slice(ch * GATE_UP_CHUNK, (ch + 1) * GATE_UP_CHUNK)
                w = wgu_ref[0, :, cols].astype(_BF16)
                wgu_s[:, cols] = jnp.dot(w, perm, preferred_element_type=_F32).astype(_BF16)
            wdn_s[...] = wdn_ref[0].astype(_BF16)

        x = jnp.concatenate(_load_rows(xs_ref, rows, rs), axis=1).astype(_BF16)
        h = jnp.dot(x, wgu_s[...], preferred_element_type=_F32) + bgu_ref[0]
        acts = []
        for ch in range(n_chunks):
            g = jnp.minimum(h[:, ch * GATE_UP_CHUNK:ch * GATE_UP_CHUNK + half], SWIGLU_LIMIT)
            up = jnp.clip(h[:, ch * GATE_UP_CHUNK + half:(ch + 1) * GATE_UP_CHUNK], -SWIGLU_LIMIT, SWIGLU_LIMIT)
            acts.append(((up + 1.0) * (g * jax.nn.sigmoid(SWIGLU_ALPHA * g))).astype(_BF16))
        y = jnp.dot(jnp.concatenate(acts, axis=1), wdn_s[...], preferred_element_type=_F32) + bdn_ref[0]
        _store_rows(ys_ref, y)


def _expert_call(block_expert, n_used, xs, layer, w_gu_all, b_gu_grouped, w_dn_all, b_dn, rs):
    _, n_exp, d, f2 = w_gu_all.shape
    f = f2 // 2
    br = EXPERT_BLOCK_ROWS
    n_blocks = xs.shape[0] // (br * rs)
    assert f2 % GATE_UP_CHUNK == 0

    def row_map(b, be, nu):
        return (jnp.minimum(b, nu[0] - 1), 0)

    def w_map(b, be, nu):
        return (be[b], 0, 0)

    def stack_map(b, be, nu):
        return (layer, be[b], 0, 0)

    return pl.pallas_call(
        functools.partial(_expert_kernel, rs=rs),
        grid_spec=pltpu.PrefetchScalarGridSpec(
            num_scalar_prefetch=2,
            grid=(n_blocks,),
            in_specs=[
                pl.BlockSpec((br * rs, LANES), row_map),
                pl.BlockSpec((None, 1, d, f2), stack_map),
                pl.BlockSpec((1, 1, f2), w_map),
                pl.BlockSpec((None, 1, f, d), stack_map),
                pl.BlockSpec((1, 1, d), w_map),
            ],
            out_specs=pl.BlockSpec((br * rs, LANES), row_map),
            scratch_shapes=[pltpu.VMEM((d, f2), _BF16), pltpu.VMEM((f, d), _BF16)],
        ),
        out_shape=jax.ShapeDtypeStruct(xs.shape, _F32),
        input_output_aliases={2: 0},
        compiler_params=pltpu.CompilerParams(
            dimension_semantics=("arbitrary",), vmem_limit_bytes=VMEM_LIMIT_BYTES),
        name="experts",
    )(block_expert, n_used, xs, w_gu_all, b_gu_grouped, w_dn_all, b_dn)


def _combine_kernel(dest_ref, gates_ref, x1r_ref, g_ref, b_ref, ys_hbm, out_ref, gbuf_ref, sem,
                    *, rs, alpha, n_tok):
    i = pl.program_id(0)
    n_steps = pl.num_programs(0)
    tb = out_ref.shape[0]
    slot = i % 2

    def gather_block(blk, to_slot):
        def issue(tl, carry):
            t = blk * tb + tl
            for k in range(TOP_K):
                d = dest_ref[k * n_tok + t]
                pltpu.make_async_copy(ys_hbm.at[pl.ds(pl.multiple_of(d * rs, rs), rs)],
                                      gbuf_ref.at[to_slot, k, pl.ds(pl.multiple_of(tl * rs, rs), rs)],
                                      sem.at[to_slot]).start()
            return carry

        lax.fori_loop(0, tb, issue, 0)

    @pl.when(i == 0)
    def _():
        gather_block(0, 0)

    @pl.when(i + 1 < n_steps)
    def _():
        gather_block(i + 1, 1 - slot)

    for k in range(TOP_K):
        pltpu.make_async_copy(ys_hbm.at[pl.ds(0, tb * rs)], gbuf_ref.at[slot, k], sem.at[slot]).wait()

    gates = gates_ref[...].T
    pieces = []
    for j in range(rs):
        piece = alpha * x1r_ref[pl.ds(j, tb, stride=rs), :]
        for k in range(TOP_K):
            piece = piece + gates[:, k:k + 1] * gbuf_ref[slot, k, pl.ds(j, tb, stride=rs), :]
        pieces.append(piece)
    out_ref[...] = _layer_norm(jnp.concatenate(pieces, axis=1), g_ref[...], b_ref[...])


def _combine_call(dest_flat, gates_t, x1r, ln_g, ln_b, ys, alpha, rs):
    n_tok = gates_t.shape[1]
    d = rs * LANES
    tb = min(COMBINE_BLOCK, n_tok)
    return pl.pallas_call(
        functools.partial(_combine_kernel, rs=rs, alpha=alpha, n_tok=n_tok),
        grid_spec=pltpu.PrefetchScalarGridSpec(
            num_scalar_prefetch=1,
            grid=(n_tok // tb,),
            in_specs=[
                pl.BlockSpec((2 * TOP_K, tb), lambda i, dest: (0, i)),
                pl.BlockSpec((tb * rs, LANES), lambda i, dest: (i, 0)),
                pl.BlockSpec((1, d), lambda i, dest: (0, 0)),
                pl.BlockSpec((1, d), lambda i, dest: (0, 0)),
                pl.BlockSpec(memory_space=pl.ANY),
            ],
            out_specs=pl.BlockSpec((tb, d), lambda i, dest: (i, 0)),
            scratch_shapes=[pltpu.VMEM((2, TOP_K, tb * rs, LANES), _F32), pltpu.SemaphoreType.DMA((2,))],
        ),
        out_shape=jax.ShapeDtypeStruct((n_tok, d), _F32),
        compiler_params=pltpu.CompilerParams(
            dimension_semantics=("arbitrary",), vmem_limit_bytes=VMEM_LIMIT_BYTES),
        name="combine",
    )(dest_flat, gates_t, x1r, ln_g.reshape(1, d), ln_b.reshape(1, d), ys)


def _routing_tables(meta_i, counts_f, n_blocks):
    n_exp = counts_f.shape[0]
    br = EXPERT_BLOCK_ROWS
    counts = counts_f[:, 0].astype(jnp.int32)
    padded = ((counts + br - 1) // br) * br
    pend = jnp.cumsum(padded)
    pstart = pend - padded
    eids = jnp.arange(n_exp, dtype=jnp.int32)
    idx, rank = meta_i[:TOP_K], meta_i[TOP_K:]
    dest = jnp.sum(jnp.where(idx[..., None] == eids, pstart, 0), axis=-1) + rank
    n_used = (pend[-1] // br).astype(jnp.int32)
    blk = jnp.minimum(jnp.arange(n_blocks, dtype=jnp.int32), n_used - 1)
    block_expert = jnp.minimum(jnp.sum((pend[None, :] <= (blk * br)[:, None]).astype(jnp.int32), axis=1),
                               n_exp - 1)
    return dest.reshape(-1), block_expert, n_used.reshape(1), pstart + counts, padded - counts


def _moe_layer(x1r, meta_i, gates_t, counts_f, layer, w_gu_all, b_gu, w_dn_all, b_dn, ln_g, ln_b, alpha, rs):
    _, n_exp, d, f2 = w_gu_all.shape
    n_tok = gates_t.shape[1]
    br = EXPERT_BLOCK_ROWS
    n_blocks = -(-(n_tok * TOP_K) // br) + n_exp
    dest_flat, block_expert, n_used, pad_start, pad_len = _routing_tables(meta_i, counts_f, n_blocks)
    xs = _dispatch_call(dest_flat, pad_start, pad_len, n_used, x1r, n_blocks, rs)
    half = GATE_UP_CHUNK // 2
    b_gu_grouped = b_gu.reshape(n_exp, f2 // GATE_UP_CHUNK, half, 2).transpose(0, 1, 3, 2).reshape(n_exp, 1, f2)
    ys = _expert_call(block_expert, n_used, xs, layer, w_gu_all, b_gu_grouped, w_dn_all,
                      b_dn.reshape(n_exp, 1, d), rs)
    return _combine_call(dest_flat, gates_t, x1r, ln_g, ln_b, ys, alpha, rs)


def kernel(x, pool_w, pool_scale, sc_w_in, sc_conv_w, sc_w_out, cf_w_in, cf_b_in, cf_dw_w, cf_dw_b,
           cf_ln_g, cf_ln_b, cf_w_out, cf_b_out, mix_ln_g, mix_ln_b, router_w, router_b,
           moe_w_gu, moe_b_gu, moe_w_dn, moe_b_dn, ffn_ln_g, ffn_ln_b):
    bsz, seq, d = x.shape
    depth = mix_ln_g.shape[0]
    alpha = (2.0 * depth) ** 0.25
    rs = d // LANES
    ia = ib = ic = 0
    for layer in range(depth):
        kind = layer % 3
        route = (mix_ln_g[layer], mix_ln_b[layer], router_w[layer], router_b[layer])
        if kind == 0:
            weights = [pool_w[ia].astype(_BF16), pool_scale[ia].reshape(1, d)]
            outs = _mixer_call(_pool_kernel, x, weights, *route, POOL_HALO, alpha)
            ia += 1
        elif kind == 1:
            weights = [sc_w_in[ib].astype(_BF16), sc_conv_w[ib], sc_w_out[ib].astype(_BF16)]
            outs = _mixer_call(_short_conv_kernel, x, weights, *route, SHORT_CONV_HALO, alpha)
            ib += 1
        else:
            weights = [cf_w_in[ic].astype(_BF16), cf_b_in[ic].reshape(1, 2 * d), cf_dw_w[ic],
                       cf_dw_b[ic].reshape(1, d), cf_ln_g[ic].reshape(1, d), cf_ln_b[ic].reshape(1, d),
                       cf_w_out[ic].astype(_BF16), cf_b_out[ic].reshape(1, d)]
            outs = _mixer_call(_conformer_kernel, x, weights, *route, CONFORMER_HALO, alpha)
            ic += 1
        x1r, meta_i, gates_t, counts_f = outs
        x = _moe_layer(x1r, meta_i, gates_t, counts_f, layer, moe_w_gu, moe_b_gu[layer],
                       moe_w_dn, moe_b_dn[layer], ffn_ln_g[layer], ffn_ln_b[layer], alpha, rs)
        x = x.reshape(bsz, seq, d)
    return x
```

```python
import functools

import jax
import jax.numpy as jnp
from jax import lax
from jax.experimental import pallas as pl
from jax.experimental.pallas import tpu as pltpu
from jax.experimental.pallas import tpu_sc as plsc

LANES = 128
SUBLANES = 8
TOP_K = 4
POOL_WINDOWS = (2, 4, 8, 16)
POOL_HALO = 16
SHORT_CONV_HALO = 8
CONFORMER_HALO = 32
SWIGLU_LIMIT = 7.0
SWIGLU_ALPHA = 1.702
LN_EPS = 1e-5
TOKEN_BLOCK = 512
EXPERT_BLOCK_ROWS = 512
COMBINE_BLOCK = 256
SC_INDEX_ROW = 128
SC_CHUNK_ROWS = 32
GATE_UP_CHUNK = 2 * LANES
VMEM_LIMIT_BYTES = 56 * 1024 * 1024

_F32 = jnp.float32
_BF16 = jnp.bfloat16


def _layer_norm(z, g, b):
    mu = jnp.mean(z, axis=-1, keepdims=True)
    zc = z - mu
    var = jnp.mean(zc * zc, axis=-1, keepdims=True)
    return zc * lax.rsqrt(var + LN_EPS) * g + b


def _store_rows(row_ref, val):
    rows, d = val.shape
    rs = d // LANES
    for j in range(rs):
        row_ref[pl.ds(j, rows, stride=rs), :] = val[:, j * LANES:(j + 1) * LANES]


def _load_rows(row_ref, rows, rs):
    return [row_ref[pl.ds(j, rows, stride=rs), :] for j in range(rs)]


def _post_norm_and_route(z, g_ref, b_ref, rwt_ref, rb_ref, first,
                         x1r_ref, mi_ref, mg_ref, cnt_ref, carry_ref):
    n_tok = z.shape[0]
    n_exp = rwt_ref.shape[0]

    @pl.when(first)
    def _():
        carry_ref[...] = jnp.zeros_like(carry_ref)

    x1 = _layer_norm(z, g_ref[...], b_ref[...])
    _store_rows(x1r_ref, x1)

    logits = lax.dot_general(rwt_ref[...], x1, (((1,), (1,)), ((), ())),
                             precision=lax.Precision.HIGHEST,
                             preferred_element_type=_F32) + rb_ref[...]
    eidx = lax.broadcasted_iota(jnp.int32, logits.shape, 0)
    work = logits
    chosen = jnp.zeros(logits.shape, jnp.bool_)
    vals, idxs = [], []
    for _ in range(TOP_K):
        m = jnp.max(work, axis=0, keepdims=True)
        sel = jnp.min(jnp.where(work == m, eidx, n_exp), axis=0, keepdims=True)
        hit = eidx == sel
        vals.append(m)
        idxs.append(sel)
        chosen = jnp.logical_or(chosen, hit)
        work = jnp.where(hit, -jnp.inf, work)
    exps = [jnp.exp(v - vals[0]) for v in vals]
    denom = exps[0] + exps[1] + exps[2] + exps[3]
    gate_rows = [e / denom for e in exps] + [jnp.zeros_like(denom)] * (mg_ref.shape[0] - TOP_K)
    mg_ref[...] = jnp.concatenate(gate_rows, axis=0)

    onehot = chosen.astype(_BF16)
    r = lax.broadcasted_iota(jnp.int32, (n_tok, n_tok), 0)
    c = lax.broadcasted_iota(jnp.int32, (n_tok, n_tok), 1)
    before = (r < c).astype(_BF16)
    cum = jnp.dot(onehot, before, preferred_element_type=_F32) + carry_ref[:, 0:1]
    ranks = [jnp.sum(jnp.where(eidx == s, cum, 0.0), axis=0, keepdims=True) for s in idxs]
    mi_ref[...] = jnp.concatenate(idxs + [rk.astype(jnp.int32) for rk in ranks], axis=0)
    carry_ref[...] = carry_ref[...] + jnp.sum(chosen.astype(_F32), axis=1, keepdims=True)
    cnt_ref[...] = carry_ref[...]


def _pool_kernel(x_ref, pw_ref, ps_ref, g_ref, b_ref, rwt_ref, rb_ref,
                 x1r_ref, mi_ref, mg_ref, cnt_ref, hist_ref, carry_ref, *, alpha):
    bi, si = pl.program_id(0), pl.program_id(1)
    ts, d = x_ref.shape[1], x_ref.shape[2]
    dg = d // len(POOL_WINDOWS)

    @pl.when(si == 0)
    def _():
        hist_ref[0:POOL_HALO, :] = jnp.zeros((POOL_HALO, d), _F32)

    x = x_ref[0]
    hist_ref[POOL_HALO:POOL_HALO + ts, :] = x
    pos = si * ts + lax.broadcasted_iota(jnp.int32, (ts, 1), 0)
    pieces = []
    for gi, win in enumerate(POOL_WINDOWS):
        c0 = gi * dg
        xg = x[:, c0:c0 + dg]
        assert win & (win - 1) == 0 and win <= POOL_HALO
        ext = hist_ref[:, c0:c0 + dg]
        span = 1
        while span < win:
            ext = ext + pltpu.roll(ext, span, 0)
            span *= 2
        acc = ext[POOL_HALO:, :]
        inv_count = 1.0 / jnp.minimum(pos + 1, win).astype(_F32)
        diff = acc * inv_count - xg
        hg = jnp.dot(diff.astype(_BF16), pw_ref[gi], preferred_element_type=_F32)
        pieces.append(alpha * xg + hg * ps_ref[:, c0:c0 + dg])
    hist_ref[0:POOL_HALO, :] = x[ts - POOL_HALO:, :]
    z = jnp.concatenate(pieces, axis=1)
    _post_norm_and_route(z, g_ref, b_ref, rwt_ref, rb_ref, (bi == 0) & (si == 0),
                         x1r_ref, mi_ref, mg_ref, cnt_ref, carry_ref)


def _short_conv_kernel(x_ref, win_ref, cw_ref, wout_ref, g_ref, b_ref, rwt_ref, rb_ref,
                       x1r_ref, mi_ref, mg_ref, cnt_ref, hist_ref, carry_ref, *, alpha):
    bi, si = pl.program_id(0), pl.program_id(1)
    ts, d = x_ref.shape[1], x_ref.shape[2]
    halo = SHORT_CONV_HALO

    @pl.when(si == 0)
    def _():
        hist_ref[0:halo, :] = jnp.zeros((halo, d), _F32)

    x = x_ref[0]
    xb = x.astype(_BF16)
    gate_b = jnp.dot(xb, win_ref[:, 0:d], preferred_element_type=_F32)
    gate_c = jnp.dot(xb, win_ref[:, d:2 * d], preferred_element_type=_F32)
    h = jnp.dot(xb, win_ref[:, 2 * d:3 * d], preferred_element_type=_F32)
    v = gate_c * h
    hist_ref[halo:halo + ts, :] = v
    width = cw_ref.shape[0]
    u = cw_ref[width - 1:width, :] * v
    for k in range(width - 1):
        shift = width - 1 - k
        u = u + cw_ref[k:k + 1, :] * hist_ref[halo - shift:halo - shift + ts, :]
    hist_ref[0:halo, :] = v[ts - halo:, :]
    y = jnp.dot((gate_b * u).astype(_BF16), wout_ref[...], preferred_element_type=_F32)
    _post_norm_and_route(alpha * x + y, g_ref, b_ref, rwt_ref, rb_ref, (bi == 0) & (si == 0),
                         x1r_ref, mi_ref, mg_ref, cnt_ref, carry_ref)


def _conformer_kernel(x_ref, win_ref, bin_ref, dww_ref, dwb_ref, lng_ref, lnb_ref, wout_ref, bout_ref,
                      g_ref, b_ref, rwt_ref, rb_ref,
                      x1r_ref, mi_ref, mg_ref, cnt_ref, hist_ref, carry_ref, *, alpha):
    bi, si = pl.program_id(0), pl.program_id(1)
    ts, d = x_ref.shape[1], x_ref.shape[2]
    halo = CONFORMER_HALO

    @pl.when(si == 0)
    def _():
        hist_ref[0:halo, :] = jnp.zeros((halo, d), _F32)

    x = x_ref[0]
    xb = x.astype(_BF16)
    a = jnp.dot(xb, win_ref[:, 0:d], preferred_element_type=_F32) + bin_ref[:, 0:d]
    gate = jnp.dot(xb, win_ref[:, d:2 * d], preferred_element_type=_F32) + bin_ref[:, d:2 * d]
    u = a * jax.nn.sigmoid(gate)
    hist_ref[halo:halo + ts, :] = u
    width = dww_ref.shape[0]
    hist = hist_ref[...]
    acc = dwb_ref[...]
    for r in range(SUBLANES):
        rolled = hist if r == 0 else pltpu.roll(hist, r, 0)
        for q in range(halo // SUBLANES):
            shift = SUBLANES * q + r
            if shift < width:
                k = width - 1 - shift
                start = halo - SUBLANES * q
                acc = acc + dww_ref[k:k + 1, :] * rolled[start:start + ts, :]
    hist_ref[0:halo, :] = u[ts - halo:, :]
    un = _layer_norm(acc, lng_ref[...], lnb_ref[...])
    un = un * jax.nn.sigmoid(un)
    y = jnp.dot(un.astype(_BF16), wout_ref[...], preferred_element_type=_F32) + bout_ref[...]
    _post_norm_and_route(alpha * x + y, g_ref, b_ref, rwt_ref, rb_ref, (bi == 0) & (si == 0),
                         x1r_ref, mi_ref, mg_ref, cnt_ref, carry_ref)


def _mixer_call(kernel_fn, x, weights, ln_g, ln_b, router_w, router_b, halo, alpha):
    bsz, seq, d = x.shape
    n_exp = router_w.shape[1]
    ts = min(TOKEN_BLOCK, seq)
    rs = d // LANES
    n_tok = bsz * seq
    nsb = seq // ts

    def full(a):
        nd = a.ndim
        return pl.BlockSpec(a.shape, lambda bi, si, _nd=nd: (0,) * _nd)

    small = [ln_g.reshape(1, d), ln_b.reshape(1, d), router_w.T, router_b.reshape(n_exp, 1)]
    operands = [x] + list(weights) + small
    in_specs = [pl.BlockSpec((1, ts, d), lambda bi, si: (bi, si, 0))] + [full(a) for a in operands[1:]]
    tok_map = lambda bi, si: (0, bi * nsb + si)
    out_shape = [
        jax.ShapeDtypeStruct((n_tok * rs, LANES), _F32),
        jax.ShapeDtypeStruct((2 * TOP_K, n_tok), jnp.int32),
        jax.ShapeDtypeStruct((2 * TOP_K, n_tok), _F32),
        jax.ShapeDtypeStruct((n_exp, LANES), _F32),
    ]
    out_specs = [
        pl.BlockSpec((ts * rs, LANES), lambda bi, si: (bi * nsb + si, 0)),
        pl.BlockSpec((2 * TOP_K, ts), tok_map),
        pl.BlockSpec((2 * TOP_K, ts), tok_map),
        pl.BlockSpec((n_exp, LANES), lambda bi, si: (0, 0)),
    ]
    return pl.pallas_call(
        functools.partial(kernel_fn, alpha=alpha),
        grid=(bsz, nsb),
        in_specs=in_specs,
        out_specs=out_specs,
        out_shape=out_shape,
        scratch_shapes=[pltpu.VMEM((halo + ts, d), _F32), pltpu.VMEM((n_exp, LANES), _F32)],
        compiler_params=pltpu.CompilerParams(
            dimension_semantics=("arbitrary", "arbitrary"), vmem_limit_bytes=VMEM_LIMIT_BYTES),
        name=kernel_fn.__name__.strip("_"),
    )(*operands)


def _sc_workers():
    info = plsc.get_sparse_core_info()
    return info.num_cores, info.num_subcores, info.num_lanes


def _sc_scatter_rows(x3, dest3, n_rows):
    n_cores, n_sub, n_lanes = _sc_workers()
    n_tok, rs, _ = x3.shape
    top_k = dest3.shape[0]
    chunk = SC_CHUNK_ROWS
    tok_per_w = n_tok // (n_cores * n_sub)
    rows_per_w = tok_per_w // SC_INDEX_ROW
    chunks_per_row = SC_INDEX_ROW // chunk
    assert rows_per_w * SC_INDEX_ROW * n_cores * n_sub == n_tok and chunks_per_row % 2 == 0
    mesh = plsc.VectorSubcoreMesh(core_axis_name="core", subcore_axis_name="subcore")

    @pl.kernel(out_type=jax.ShapeDtypeStruct((n_rows, rs, LANES), _F32), mesh=mesh,
               scratch_types=[pltpu.VMEM((top_k, rows_per_w, SC_INDEX_ROW), jnp.int32),
                              pltpu.VMEM((2, chunk, rs, LANES), _F32),
                              pltpu.SemaphoreType.DMA((2,)), pltpu.SemaphoreType.DMA((2,))])
    def scatter_kernel(x_hbm, d_hbm, o_hbm, idx_v, buf, rsem, ssem):
        wid = lax.axis_index("subcore") * n_cores + lax.axis_index("core")
        for k in range(top_k):
            pltpu.sync_copy(d_hbm.at[k, pl.ds(wid * rows_per_w, rows_per_w)], idx_v.at[k])
        base = wid * tok_per_w

        def read(j, c, slot):
            return pltpu.make_async_copy(x_hbm.at[pl.ds(base + j * SC_INDEX_ROW + c * chunk, chunk)],
                                         buf.at[slot], rsem.at[slot])

        def scatters(j, c, slot):
            copies = []
            for k in range(top_k):
                for h in range(chunk // n_lanes):
                    rows = idx_v[k, j, pl.ds(c * chunk + h * n_lanes, n_lanes)]
                    copies.append(pltpu.make_async_copy(buf.at[slot, pl.ds(h * n_lanes, n_lanes)],
                                                        o_hbm.at[rows], ssem.at[slot]))
            return copies

        def wait_scatters(j, c, slot):
            for cp in scatters(j, c, slot):
                cp.wait()

        read(0, 0, 0).start()

        def per_index_row(j, carry):
            for c in range(chunks_per_row):
                slot = c % 2
                if c == 0:
                    @pl.when(j > 0)
                    def _():
                        wait_scatters(j - 1, chunks_per_row - 1, 1 - slot)
                    read(j, c + 1, 1 - slot).start()
                elif c < chunks_per_row - 1:
                    wait_scatters(j, c - 1, 1 - slot)
                    read(j, c + 1, 1 - slot).start()
                else:
                    @pl.when(j + 1 < rows_per_w)
                    def _():
                        wait_scatters(j, c - 1, 1 - slot)
                        read(j + 1, 0, 1 - slot).start()
                read(j, c, slot).wait()
                for cp in scatters(j, c, slot):
                    cp.start()
            return carry

        lax.fori_loop(0, rows_per_w, per_index_row, 0)
        wait_scatters(rows_per_w - 1, chunks_per_row - 2, 0)
        wait_scatters(rows_per_w - 1, chunks_per_row - 1, 1)

    return scatter_kernel(x3, dest3)


def _sc_gather_rows(table3, idx2):
    n_cores, n_sub, _ = _sc_workers()
    n_idx_rows = idx2.shape[0]
    rs = table3.shape[1]
    chunk = SC_CHUNK_ROWS
    rows_per_w = n_idx_rows // (n_cores * n_sub)
    chunks_per_row = SC_INDEX_ROW // chunk
    assert rows_per_w * n_cores * n_sub == n_idx_rows and chunks_per_row % 2 == 0
    mesh = plsc.VectorSubcoreMesh(core_axis_name="core", subcore_axis_name="subcore")

    @pl.kernel(out_type=jax.ShapeDtypeStruct((n_idx_rows * SC_INDEX_ROW, rs, LANES), _F32), mesh=mesh,
               scratch_types=[pltpu.VMEM((rows_per_w, SC_INDEX_ROW), jnp.int32),
                              pltpu.VMEM((2, chunk, rs, LANES), _F32),
                              pltpu.SemaphoreType.DMA((2,)), pltpu.SemaphoreType.DMA((2,))])
    def gather_kernel(t_hbm, i_hbm, o_hbm, idx_v, buf, gsem, wsem):
        wid = lax.axis_index("subcore") * n_cores + lax.axis_index("core")
        pltpu.sync_copy(i_hbm.at[pl.ds(wid * rows_per_w, rows_per_w)], idx_v)
        base = wid * rows_per_w * SC_INDEX_ROW

        def gather(j, c, slot):
            return pltpu.make_async_copy(t_hbm.at[idx_v.at[j, pl.ds(c * chunk, chunk)]], buf.at[slot], gsem.at[slot])

        def write(j, c, slot):
            return pltpu.make_async_copy(buf.at[slot], o_hbm.at[pl.ds(base + j * SC_INDEX_ROW + c * chunk, chunk)],
                                         wsem.at[slot])

        gather(0, 0, 0).start()

        def per_index_row(j, carry):
            for c in range(chunks_per_row):
                slot = c % 2
                if c == 0:
                    @pl.when(j > 0)
                    def _():
                        write(j - 1, chunks_per_row - 1, 1 - slot).wait()
                    gather(j, c + 1, 1 - slot).start()
                elif c < chunks_per_row - 1:
                    write(j, c - 1, 1 - slot).wait()
                    gather(j, c + 1, 1 - slot).start()
                else:
                    @pl.when(j + 1 < rows_per_w)
                    def _():
                        write(j, c - 1, 1 - slot).wait()
                        gather(j + 1, 0, 1 - slot).start()
                gather(j, c, slot).wait()
                write(j, c, slot).start()
            return carry

        lax.fori_loop(0, rows_per_w, per_index_row, 0)
        write(rows_per_w - 1, chunks_per_row - 2, 0).wait()
        write(rows_per_w - 1, chunks_per_row - 1, 1).wait()

    return gather_kernel(table3, idx2)


def _pad_fill_kernel(pad_start_ref, pad_len_ref, nu_ref, xs_in_hbm, xs_hbm, zeros_ref, sem,
                     *, rs, n_exp, block_rows, n_blocks):
    del xs_in_hbm
    zeros_ref[...] = jnp.zeros_like(zeros_ref)
    blk = block_rows * rs

    def wait_rows(n):
        pltpu.make_async_copy(xs_hbm.at[pl.ds(0, n * rs)], xs_hbm.at[pl.ds(0, n * rs)], sem).wait()

    def per_expert(e, total):
        start, n = pad_start_ref[e], pad_len_ref[e]

        def fill(r, carry):
            pltpu.make_async_copy(zeros_ref.at[pl.ds(0, rs)],
                                  xs_hbm.at[pl.ds(pl.multiple_of((start + r) * rs, rs), rs)], sem).start()
            return carry

        lax.fori_loop(0, n, fill, 0)
        return total + n

    n_pad = lax.fori_loop(0, n_exp, per_expert, 0)

    @pl.when(n_pad > 0)
    def _():
        wait_rows(n_pad)

    n_used = nu_ref[0]

    def fill_block(b, carry):
        pltpu.make_async_copy(zeros_ref, xs_hbm.at[pl.ds(b * blk, blk)], sem).start()
        return carry

    lax.fori_loop(n_used, n_blocks, fill_block, 0)

    @pl.when(n_used < n_blocks)
    def _():
        wait_rows((n_blocks - n_used) * block_rows)


def _pad_fill_call(pad_start, pad_len, n_used, xs, n_blocks, rs):
    n_exp = pad_start.shape[0]
    br = EXPERT_BLOCK_ROWS
    return pl.pallas_call(
        functools.partial(_pad_fill_kernel, rs=rs, n_exp=n_exp, block_rows=br, n_blocks=n_blocks),
        grid_spec=pltpu.PrefetchScalarGridSpec(
            num_scalar_prefetch=3,
            grid=(1,),
            in_specs=[pl.BlockSpec(memory_space=pl.ANY)],
            out_specs=pl.BlockSpec(memory_space=pl.ANY),
            scratch_shapes=[pltpu.VMEM((br * rs, LANES), _F32), pltpu.SemaphoreType.DMA],
        ),
        out_shape=jax.ShapeDtypeStruct(xs.shape, _F32),
        input_output_aliases={3: 0},
        compiler_params=pltpu.CompilerParams(dimension_semantics=("arbitrary",)),
        name="pad_fill",
    )(pad_start, pad_len, n_used, xs)


def _expert_kernel(be_ref, nu_ref, xs_ref, wgu_ref, bgu_ref, wdn_ref, bdn_ref, ys_ref, wgu_s, wdn_s, *, rs):
    b = pl.program_id(0)
    rows = xs_ref.shape[0] // rs
    n_chunks = wgu_ref.shape[2] // GATE_UP_CHUNK
    half = GATE_UP_CHUNK // 2

    @pl.when(b < nu_ref[0])
    def _():
        @pl.when((b == 0) | (be_ref[b] != be_ref[jnp.maximum(b - 1, 0)]))
        def _():
            r = lax.broadcasted_iota(jnp.int32, (GATE_UP_CHUNK, GATE_UP_CHUNK), 0)
            c = lax.broadcasted_iota(jnp.int32, (GATE_UP_CHUNK, GATE_UP_CHUNK), 1)
            perm = (r == jnp.where(c < half, 2 * c, 2 * (c - half) + 1)).astype(_BF16)
            for ch in range(n_chunks):
                cols = slice(ch * GATE_UP_CHUNK, (ch + 1) * GATE_UP_CHUNK)
                w = wgu_ref[0, :, cols].astype(_BF16)
                wgu_s[:, cols] = jnp.dot(w, perm, preferred_element_type=_F32).astype(_BF16)
            wdn_s[...] = wdn_ref[0].astype(_BF16)

        x = jnp.concatenate(_load_rows(xs_ref, rows, rs), axis=1).astype(_BF16)
        h = jnp.dot(x, wgu_s[...], preferred_element_type=_F32) + bgu_ref[0]
        acts = []
        for ch in range(n_chunks):
            g = jnp.minimum(h[:, ch * GATE_UP_CHUNK:ch * GATE_UP_CHUNK + half], SWIGLU_LIMIT)
            up = jnp.clip(h[:, ch * GATE_UP_CHUNK + half:(ch + 1) * GATE_UP_CHUNK], -SWIGLU_LIMIT, SWIGLU_LIMIT)
            acts.append(((up + 1.0) * (g * jax.nn.sigmoid(SWIGLU_ALPHA * g))).astype(_BF16))
        y = jnp.dot(jnp.concatenate(acts, axis=1), wdn_s[...], preferred_element_type=_F32) + bdn_ref[0]
        _store_rows(ys_ref, y)


def _expert_call(block_expert, n_used, xs, layer, w_gu_all, b_gu_grouped, w_dn_all, b_dn, rs):
    _, n_exp, d, f2 = w_gu_all.shape
    f = f2 // 2
    br = EXPERT_BLOCK_ROWS
    n_blocks = xs.shape[0] // (br * rs)
    assert f2 % GATE_UP_CHUNK == 0

    def row_map(b, be, nu):
        return (jnp.minimum(b, nu[0] - 1), 0)

    def w_map(b, be, nu):
        return (be[b], 0, 0)

    def stack_map(b, be, nu):
        return (layer, be[b], 0, 0)

    return pl.pallas_call(
        functools.partial(_expert_kernel, rs=rs),
        grid_spec=pltpu.PrefetchScalarGridSpec(
            num_scalar_prefetch=2,
            grid=(n_blocks,),
            in_specs=[
                pl.BlockSpec((br * rs, LANES), row_map),
                pl.BlockSpec((None, 1, d, f2), stack_map),
                pl.BlockSpec((1, 1, f2), w_map),
                pl.BlockSpec((None, 1, f, d), stack_map),
                pl.BlockSpec((1, 1, d), w_map),
            ],
            out_specs=pl.BlockSpec((br * rs, LANES), row_map),
            scratch_shapes=[pltpu.VMEM((d, f2), _BF16), pltpu.VMEM((f, d), _BF16)],
        ),
        out_shape=jax.ShapeDtypeStruct(xs.shape, _F32),
        input_output_aliases={2: 0},
        compiler_params=pltpu.CompilerParams(
            dimension_semantics=("arbitrary",), vmem_limit_bytes=VMEM_LIMIT_BYTES),
        name="experts",
    )(block_expert, n_used, xs, w_gu_all, b_gu_grouped, w_dn_all, b_dn)


def _combine_kernel(gates_ref, x1r_ref, g_ref, b_ref, *rest, rs, alpha):
    yk_refs, out_ref = rest[:TOP_K], rest[TOP_K]
    tb = out_ref.shape[0]
    gates = gates_ref[...].T
    pieces = []
    for j in range(rs):
        piece = alpha * x1r_ref[pl.ds(j, tb, stride=rs), :]
        for k in range(TOP_K):
            piece = piece + gates[:, k:k + 1] * yk_refs[k][pl.ds(j, tb, stride=rs), :]
        pieces.append(piece)
    out_ref[...] = _layer_norm(jnp.concatenate(pieces, axis=1), g_ref[...], b_ref[...])


def _combine_call(gates_t, x1r, ln_g, ln_b, yk, alpha, rs):
    n_tok = gates_t.shape[1]
    d = rs * LANES
    tb = min(COMBINE_BLOCK, n_tok)
    steps = n_tok // tb
    yk_specs = [pl.BlockSpec((tb * rs, LANES), lambda i, _k=k: (_k * steps + i, 0)) for k in range(TOP_K)]
    return pl.pallas_call(
        functools.partial(_combine_kernel, rs=rs, alpha=alpha),
        grid=(steps,),
        in_specs=[
            pl.BlockSpec((2 * TOP_K, tb), lambda i: (0, i)),
            pl.BlockSpec((tb * rs, LANES), lambda i: (i, 0)),
            pl.BlockSpec((1, d), lambda i: (0, 0)),
            pl.BlockSpec((1, d), lambda i: (0, 0)),
        ] + yk_specs,
        out_specs=pl.BlockSpec((tb, d), lambda i: (i, 0)),
        out_shape=jax.ShapeDtypeStruct((n_tok, d), _F32),
        compiler_params=pltpu.CompilerParams(
            dimension_semantics=("arbitrary",), vmem_limit_bytes=VMEM_LIMIT_BYTES),
        name="combine",
    )(gates_t, x1r, ln_g.reshape(1, d), ln_b.reshape(1, d), *([yk] * TOP_K))


def _routing_tables(meta_i, counts_f, n_blocks):
    n_exp = counts_f.shape[0]
    br = EXPERT_BLOCK_ROWS
    counts = counts_f[:, 0].astype(jnp.int32)
    padded = ((counts + br - 1) // br) * br
    pend = jnp.cumsum(padded)
    pstart = pend - padded
    eids = jnp.arange(n_exp, dtype=jnp.int32)
    idx, rank = meta_i[:TOP_K], meta_i[TOP_K:]
    dest = jnp.sum(jnp.where(idx[..., None] == eids, pstart, 0), axis=-1) + rank
    n_used = (pend[-1] // br).astype(jnp.int32)
    blk = jnp.minimum(jnp.arange(n_blocks, dtype=jnp.int32), n_used - 1)
    block_expert = jnp.minimum(jnp.sum((pend[None, :] <= (blk * br)[:, None]).astype(jnp.int32), axis=1),
                               n_exp - 1)
    return dest, block_expert, n_used.reshape(1), pstart + counts, padded - counts


def _moe_layer(x1r, meta_i, gates_t, counts_f, layer, w_gu_all, b_gu, w_dn_all, b_dn, ln_g, ln_b, alpha, rs):
    _, n_exp, d, f2 = w_gu_all.shape
    n_tok = gates_t.shape[1]
    br = EXPERT_BLOCK_ROWS
    n_blocks = -(-(n_tok * TOP_K) // br) + n_exp
    dest, block_expert, n_used, pad_start, pad_len = _routing_tables(meta_i, counts_f, n_blocks)
    n_rows = n_blocks * br
    xs = _sc_scatter_rows(x1r.reshape(n_tok, rs, LANES), dest.reshape(TOP_K, n_tok // SC_INDEX_ROW, SC_INDEX_ROW),
                          n_rows)
    xs = _pad_fill_call(pad_start, pad_len, n_used, xs.reshape(n_rows * rs, LANES), n_blocks, rs)
    half = GATE_UP_CHUNK // 2
    b_gu_grouped = b_gu.reshape(n_exp, f2 // GATE_UP_CHUNK, half, 2).transpose(0, 1, 3, 2).reshape(n_exp, 1, f2)
    ys = _expert_call(block_expert, n_used, xs, layer, w_gu_all, b_gu_grouped, w_dn_all,
                      b_dn.reshape(n_exp, 1, d), rs)
    yk = _sc_gather_rows(ys.reshape(n_rows, rs, LANES),
                         dest.reshape(TOP_K * n_tok // SC_INDEX_ROW, SC_INDEX_ROW))
    return _combine_call(gates_t, x1r, ln_g, ln_b, yk.reshape(TOP_K * n_tok * rs, LANES), alpha, rs)


def kernel(x, pool_w, pool_scale, sc_w_in, sc_conv_w, sc_w_out, cf_w_in, cf_b_in, cf_dw_w, cf_dw_b,
           cf_ln_g, cf_ln_b, cf_w_out, cf_b_out, mix_ln_g, mix_ln_b, router_w, router_b,
           moe_w_gu, moe_b_gu, moe_w_dn, moe_b_dn, ffn_ln_g, ffn_ln_b):
    bsz, seq, d = x.shape
    depth = mix_ln_g.shape[0]
    alpha = (2.0 * depth) ** 0.25
    rs = d // LANES
    ia = ib = ic = 0
    for layer in range(depth):
        kind = layer % 3
        route = (mix_ln_g[layer], mix_ln_b[layer], router_w[layer], router_b[layer])
        if kind == 0:
            weights = [pool_w[ia].astype(_BF16), pool_scale[ia].reshape(1, d)]
            outs = _mixer_call(_pool_kernel, x, weights, *route, POOL_HALO, alpha)
            ia += 1
        elif kind == 1:
            weights = [sc_w_in[ib].astype(_BF16), sc_conv_w[ib], sc_w_out[ib].astype(_BF16)]
            outs = _mixer_call(_short_conv_kernel, x, weights, *route, SHORT_CONV_HALO, alpha)
            ib += 1
        else:
            weights = [cf_w_in[ic].astype(_BF16), cf_b_in[ic].reshape(1, 2 * d), cf_dw_w[ic],
                       cf_dw_b[ic].reshape(1, d), cf_ln_g[ic].reshape(1, d), cf_ln_b[ic].reshape(1, d),
                       cf_w_out[ic].astype(_BF16), cf_b_out[ic].reshape(1, d)]
            outs = _mixer_call(_conformer_kernel, x, weights, *route, CONFORMER_HALO, alpha)
            ic += 1
        x1r, meta_i, gates_t, counts_f = outs
        x = _moe_layer(x1r, meta_i, gates_t, counts_f, layer, moe_w_gu, moe_b_gu[layer],
                       moe_w_dn, moe_b_dn[layer], ffn_ln_g[layer], ffn_ln_b[layer], alpha, rs)
        x = x.reshape(bsz, seq, d)
    return x
```

```python
import functools

import jax
import jax.numpy as jnp
from jax import lax
from jax.experimental import pallas as pl
from jax.experimental.pallas import tpu as pltpu
from jax.experimental.pallas import tpu_sc as plsc

LANES = 128
SUBLANES = 8
TOP_K = 4
POOL_WINDOWS = (2, 4, 8, 16)
POOL_HALO = 16
SHORT_CONV_HALO = 8
CONFORMER_HALO = 32
SWIGLU_LIMIT = 7.0
SWIGLU_ALPHA = 1.702
LN_EPS = 1e-5
TOKEN_BLOCK = 512
EXPERT_BLOCK_ROWS = 512
COMBINE_BLOCK = 256
BATCH_CHAINS = 2
SC_INDEX_ROW = 128
SC_CHUNK_ROWS = 32
GATE_UP_CHUNK = 2 * LANES
VMEM_LIMIT_BYTES = 56 * 1024 * 1024

_F32 = jnp.float32
_BF16 = jnp.bfloat16


def _layer_norm(z, g, b):
    mu = jnp.mean(z, axis=-1, keepdims=True)
    zc = z - mu
    var = jnp.mean(zc * zc, axis=-1, keepdims=True)
    return zc * lax.rsqrt(var + LN_EPS) * g + b


def _store_rows(row_ref, val):
    rows, d = val.shape
    rs = d // LANES
    for j in range(rs):
        row_ref[pl.ds(j, rows, stride=rs), :] = val[:, j * LANES:(j + 1) * LANES]


def _load_rows(row_ref, rows, rs):
    return [row_ref[pl.ds(j, rows, stride=rs), :] for j in range(rs)]


def _post_norm_and_route(z, g_ref, b_ref, rwt_ref, rb_ref, first,
                         x1r_ref, mi_ref, mg_ref, cnt_ref, carry_ref):
    n_tok = z.shape[0]
    n_exp = rwt_ref.shape[0]

    @pl.when(first)
    def _():
        carry_ref[...] = jnp.zeros_like(carry_ref)

    x1 = _layer_norm(z, g_ref[...], b_ref[...])
    _store_rows(x1r_ref, x1)

    logits = lax.dot_general(rwt_ref[...], x1, (((1,), (1,)), ((), ())),
                             precision=lax.Precision.HIGHEST,
                             preferred_element_type=_F32) + rb_ref[...]
    eidx = lax.broadcasted_iota(jnp.int32, logits.shape, 0)
    work = logits
    chosen = jnp.zeros(logits.shape, jnp.bool_)
    vals, idxs = [], []
    for _ in range(TOP_K):
        m = jnp.max(work, axis=0, keepdims=True)
        sel = jnp.min(jnp.where(work == m, eidx, n_exp), axis=0, keepdims=True)
        hit = eidx == sel
        vals.append(m)
        idxs.append(sel)
        chosen = jnp.logical_or(chosen, hit)
        work = jnp.where(hit, -jnp.inf, work)
    exps = [jnp.exp(v - vals[0]) for v in vals]
    denom = exps[0] + exps[1] + exps[2] + exps[3]
    gate_rows = [e / denom for e in exps] + [jnp.zeros_like(denom)] * (mg_ref.shape[0] - TOP_K)
    mg_ref[...] = jnp.concatenate(gate_rows, axis=0)

    onehot = chosen.astype(_BF16)
    r = lax.broadcasted_iota(jnp.int32, (n_tok, n_tok), 0)
    c = lax.broadcasted_iota(jnp.int32, (n_tok, n_tok), 1)
    before = (r < c).astype(_BF16)
    cum = jnp.dot(onehot, before, preferred_element_type=_F32) + carry_ref[:, 0:1]
    ranks = [jnp.sum(jnp.where(eidx == s, cum, 0.0), axis=0, keepdims=True) for s in idxs]
    mi_ref[...] = jnp.concatenate(idxs + [rk.astype(jnp.int32) for rk in ranks], axis=0)
    carry_ref[...] = carry_ref[...] + jnp.sum(chosen.astype(_F32), axis=1, keepdims=True)
    cnt_ref[...] = carry_ref[...]


def _pool_kernel(x_ref, pw_ref, ps_ref, g_ref, b_ref, rwt_ref, rb_ref,
                 x1r_ref, mi_ref, mg_ref, cnt_ref, hist_ref, carry_ref, *, alpha):
    bi, si = pl.program_id(0), pl.program_id(1)
    ts, d = x_ref.shape[1], x_ref.shape[2]
    dg = d // len(POOL_WINDOWS)

    @pl.when(si == 0)
    def _():
        hist_ref[0:POOL_HALO, :] = jnp.zeros((POOL_HALO, d), _F32)

    x = x_ref[0]
    hist_ref[POOL_HALO:POOL_HALO + ts, :] = x
    pos = si * ts + lax.broadcasted_iota(jnp.int32, (ts, 1), 0)
    pieces = []
    for gi, win in enumerate(POOL_WINDOWS):
        c0 = gi * dg
        xg = x[:, c0:c0 + dg]
        assert win & (win - 1) == 0 and win <= POOL_HALO
        ext = hist_ref[:, c0:c0 + dg]
        span = 1
        while span < win:
            ext = ext + pltpu.roll(ext, span, 0)
            span *= 2
        acc = ext[POOL_HALO:, :]
        inv_count = 1.0 / jnp.minimum(pos + 1, win).astype(_F32)
        diff = acc * inv_count - xg
        hg = jnp.dot(diff.astype(_BF16), pw_ref[gi], preferred_element_type=_F32)
        pieces.append(alpha * xg + hg * ps_ref[:, c0:c0 + dg])
    hist_ref[0:POOL_HALO, :] = x[ts - POOL_HALO:, :]
    z = jnp.concatenate(pieces, axis=1)
    _post_norm_and_route(z, g_ref, b_ref, rwt_ref, rb_ref, (bi == 0) & (si == 0),
                         x1r_ref, mi_ref, mg_ref, cnt_ref, carry_ref)


def _short_conv_kernel(x_ref, win_ref, cw_ref, wout_ref, g_ref, b_ref, rwt_ref, rb_ref,
                       x1r_ref, mi_ref, mg_ref, cnt_ref, hist_ref, carry_ref, *, alpha):
    bi, si = pl.program_id(0), pl.program_id(1)
    ts, d = x_ref.shape[1], x_ref.shape[2]
    halo = SHORT_CONV_HALO

    @pl.when(si == 0)
    def _():
        hist_ref[0:halo, :] = jnp.zeros((halo, d), _F32)

    x = x_ref[0]
    xb = x.astype(_BF16)
    gate_b = jnp.dot(xb, win_ref[:, 0:d], preferred_element_type=_F32)
    gate_c = jnp.dot(xb, win_ref[:, d:2 * d], preferred_element_type=_F32)
    h = jnp.dot(xb, win_ref[:, 2 * d:3 * d], preferred_element_type=_F32)
    v = gate_c * h
    hist_ref[halo:halo + ts, :] = v
    width = cw_ref.shape[0]
    u = cw_ref[width - 1:width, :] * v
    for k in range(width - 1):
        shift = width - 1 - k
        u = u + cw_ref[k:k + 1, :] * hist_ref[halo - shift:halo - shift + ts, :]
    hist_ref[0:halo, :] = v[ts - halo:, :]
    y = jnp.dot((gate_b * u).astype(_BF16), wout_ref[...], preferred_element_type=_F32)
    _post_norm_and_route(alpha * x + y, g_ref, b_ref, rwt_ref, rb_ref, (bi == 0) & (si == 0),
                         x1r_ref, mi_ref, mg_ref, cnt_ref, carry_ref)


def _conformer_kernel(x_ref, win_ref, bin_ref, dww_ref, dwb_ref, lng_ref, lnb_ref, wout_ref, bout_ref,
                      g_ref, b_ref, rwt_ref, rb_ref,
                      x1r_ref, mi_ref, mg_ref, cnt_ref, hist_ref, carry_ref, *, alpha):
    bi, si = pl.program_id(0), pl.program_id(1)
    ts, d = x_ref.shape[1], x_ref.shape[2]
    halo = CONFORMER_HALO

    @pl.when(si == 0)
    def _():
        hist_ref[0:halo, :] = jnp.zeros((halo, d), _F32)

    x = x_ref[0]
    xb = x.astype(_BF16)
    a = jnp.dot(xb, win_ref[:, 0:d], preferred_element_type=_F32) + bin_ref[:, 0:d]
    gate = jnp.dot(xb, win_ref[:, d:2 * d], preferred_element_type=_F32) + bin_ref[:, d:2 * d]
    u = a * jax.nn.sigmoid(gate)
    hist_ref[halo:halo + ts, :] = u
    width = dww_ref.shape[0]
    hist = hist_ref[...]
    acc = dwb_ref[...]
    for r in range(SUBLANES):
        rolled = hist if r == 0 else pltpu.roll(hist, r, 0)
        for q in range(halo // SUBLANES):
            shift = SUBLANES * q + r
            if shift < width:
                k = width - 1 - shift
                start = halo - SUBLANES * q
                acc = acc + dww_ref[k:k + 1, :] * rolled[start:start + ts, :]
    hist_ref[0:halo, :] = u[ts - halo:, :]
    un = _layer_norm(acc, lng_ref[...], lnb_ref[...])
    un = un * jax.nn.sigmoid(un)
    y = jnp.dot(un.astype(_BF16), wout_ref[...], preferred_element_type=_F32) + bout_ref[...]
    _post_norm_and_route(alpha * x + y, g_ref, b_ref, rwt_ref, rb_ref, (bi == 0) & (si == 0),
                         x1r_ref, mi_ref, mg_ref, cnt_ref, carry_ref)


def _mixer_call(kernel_fn, x, batch_lo, bsz, weights, ln_g, ln_b, router_w, router_b, halo, alpha):
    _, seq, d = x.shape
    n_exp = router_w.shape[1]
    ts = min(TOKEN_BLOCK, seq)
    rs = d // LANES
    n_tok = bsz * seq
    nsb = seq // ts

    def full(a):
        nd = a.ndim
        return pl.BlockSpec(a.shape, lambda bi, si, _nd=nd: (0,) * _nd)

    small = [ln_g.reshape(1, d), ln_b.reshape(1, d), router_w.T, router_b.reshape(n_exp, 1)]
    operands = [x] + list(weights) + small
    in_specs = [pl.BlockSpec((1, ts, d), lambda bi, si: (batch_lo + bi, si, 0))] + [full(a) for a in operands[1:]]
    tok_map = lambda bi, si: (0, bi * nsb + si)
    out_shape = [
        jax.ShapeDtypeStruct((n_tok * rs, LANES), _F32),
        jax.ShapeDtypeStruct((2 * TOP_K, n_tok), jnp.int32),
        jax.ShapeDtypeStruct((2 * TOP_K, n_tok), _F32),
        jax.ShapeDtypeStruct((n_exp, LANES), _F32),
    ]
    out_specs = [
        pl.BlockSpec((ts * rs, LANES), lambda bi, si: (bi * nsb + si, 0)),
        pl.BlockSpec((2 * TOP_K, ts), tok_map),
        pl.BlockSpec((2 * TOP_K, ts), tok_map),
        pl.BlockSpec((n_exp, LANES), lambda bi, si: (0, 0)),
    ]
    return pl.pallas_call(
        functools.partial(kernel_fn, alpha=alpha),
        grid=(bsz, nsb),
        in_specs=in_specs,
        out_specs=out_specs,
        out_shape=out_shape,
        scratch_shapes=[pltpu.VMEM((halo + ts, d), _F32), pltpu.VMEM((n_exp, LANES), _F32)],
        compiler_params=pltpu.CompilerParams(
            dimension_semantics=("arbitrary", "arbitrary"), vmem_limit_bytes=VMEM_LIMIT_BYTES),
        name=kernel_fn.__name__.strip("_"),
    )(*operands)


def _sc_workers():
    info = plsc.get_sparse_core_info()
    return info.num_cores, info.num_subcores, info.num_lanes


def _sc_scatter_rows(x3, dest3, n_rows):
    n_cores, n_sub, n_lanes = _sc_workers()
    n_tok, rs, _ = x3.shape
    top_k = dest3.shape[0]
    chunk = SC_CHUNK_ROWS
    tok_per_w = n_tok // (n_cores * n_sub)
    rows_per_w = tok_per_w // SC_INDEX_ROW
    chunks_per_row = SC_INDEX_ROW // chunk
    assert rows_per_w * SC_INDEX_ROW * n_cores * n_sub == n_tok and chunks_per_row % 2 == 0
    mesh = plsc.VectorSubcoreMesh(core_axis_name="core", subcore_axis_name="subcore")

    @pl.kernel(out_type=jax.ShapeDtypeStruct((n_rows, rs, LANES), _F32), mesh=mesh,
               scratch_types=[pltpu.VMEM((top_k, rows_per_w, SC_INDEX_ROW), jnp.int32),
                              pltpu.VMEM((2, chunk, rs, LANES), _F32),
                              pltpu.SemaphoreType.DMA((2,)), pltpu.SemaphoreType.DMA((2,))])
    def scatter_kernel(x_hbm, d_hbm, o_hbm, idx_v, buf, rsem, ssem):
        wid = lax.axis_index("subcore") * n_cores + lax.axis_index("core")
        for k in range(top_k):
            pltpu.sync_copy(d_hbm.at[k, pl.ds(wid * rows_per_w, rows_per_w)], idx_v.at[k])
        base = wid * tok_per_w

        def read(j, c, slot):
            return pltpu.make_async_copy(x_hbm.at[pl.ds(base + j * SC_INDEX_ROW + c * chunk, chunk)],
                                         buf.at[slot], rsem.at[slot])

        def scatters(j, c, slot):
            copies = []
            for k in range(top_k):
                for h in range(chunk // n_lanes):
                    rows = idx_v[k, j, pl.ds(c * chunk + h * n_lanes, n_lanes)]
                    copies.append(pltpu.make_async_copy(buf.at[slot, pl.ds(h * n_lanes, n_lanes)],
                                                        o_hbm.at[rows], ssem.at[slot]))
            return copies

        def wait_scatters(j, c, slot):
            for cp in scatters(j, c, slot):
                cp.wait()

        read(0, 0, 0).start()

        def per_index_row(j, carry):
            for c in range(chunks_per_row):
                slot = c % 2
                if c == 0:
                    @pl.when(j > 0)
                    def _():
                        wait_scatters(j - 1, chunks_per_row - 1, 1 - slot)
                    read(j, c + 1, 1 - slot).start()
                elif c < chunks_per_row - 1:
                    wait_scatters(j, c - 1, 1 - slot)
                    read(j, c + 1, 1 - slot).start()
                else:
                    @pl.when(j + 1 < rows_per_w)
                    def _():
                        wait_scatters(j, c - 1, 1 - slot)
                        read(j + 1, 0, 1 - slot).start()
                read(j, c, slot).wait()
                for cp in scatters(j, c, slot):
                    cp.start()
            return carry

        lax.fori_loop(0, rows_per_w, per_index_row, 0)
        wait_scatters(rows_per_w - 1, chunks_per_row - 2, 0)
        wait_scatters(rows_per_w - 1, chunks_per_row - 1, 1)

    return scatter_kernel(x3, dest3)


def _sc_gather_rows(table3, idx2):
    n_cores, n_sub, _ = _sc_workers()
    n_idx_rows = idx2.shape[0]
    rs = table3.shape[1]
    chunk = SC_CHUNK_ROWS
    rows_per_w = n_idx_rows // (n_cores * n_sub)
    chunks_per_row = SC_INDEX_ROW // chunk
    assert rows_per_w * n_cores * n_sub == n_idx_rows and chunks_per_row % 2 == 0
    mesh = plsc.VectorSubcoreMesh(core_axis_name="core", subcore_axis_name="subcore")

    @pl.kernel(out_type=jax.ShapeDtypeStruct((n_idx_rows * SC_INDEX_ROW, rs, LANES), _F32), mesh=mesh,
               scratch_types=[pltpu.VMEM((rows_per_w, SC_INDEX_ROW), jnp.int32),
                              pltpu.VMEM((2, chunk, rs, LANES), _F32),
                              pltpu.SemaphoreType.DMA((2,)), pltpu.SemaphoreType.DMA((2,))])
    def gather_kernel(t_hbm, i_hbm, o_hbm, idx_v, buf, gsem, wsem):
        wid = lax.axis_index("subcore") * n_cores + lax.axis_index("core")
        pltpu.sync_copy(i_hbm.at[pl.ds(wid * rows_per_w, rows_per_w)], idx_v)
        base = wid * rows_per_w * SC_INDEX_ROW

        def gather(j, c, slot):
            return pltpu.make_async_copy(t_hbm.at[idx_v.at[j, pl.ds(c * chunk, chunk)]], buf.at[slot], gsem.at[slot])

        def write(j, c, slot):
            return pltpu.make_async_copy(buf.at[slot], o_hbm.at[pl.ds(base + j * SC_INDEX_ROW + c * chunk, chunk)],
                                         wsem.at[slot])

        gather(0, 0, 0).start()

        def per_index_row(j, carry):
            for c in range(chunks_per_row):
                slot = c % 2
                if c == 0:
                    @pl.when(j > 0)
                    def _():
                        write(j - 1, chunks_per_row - 1, 1 - slot).wait()
                    gather(j, c + 1, 1 - slot).start()
                elif c < chunks_per_row - 1:
                    write(j, c - 1, 1 - slot).wait()
                    gather(j, c + 1, 1 - slot).start()
                else:
                    @pl.when(j + 1 < rows_per_w)
                    def _():
                        write(j, c - 1, 1 - slot).wait()
                        gather(j + 1, 0, 1 - slot).start()
                gather(j, c, slot).wait()
                write(j, c, slot).start()
            return carry

        lax.fori_loop(0, rows_per_w, per_index_row, 0)
        write(rows_per_w - 1, chunks_per_row - 2, 0).wait()
        write(rows_per_w - 1, chunks_per_row - 1, 1).wait()

    return gather_kernel(table3, idx2)


def _pad_fill_kernel(pad_start_ref, pad_len_ref, nu_ref, xs_in_hbm, xs_hbm, zeros_ref, sem,
                     *, rs, n_exp, block_rows, n_blocks):
    del xs_in_hbm
    zeros_ref[...] = jnp.zeros_like(zeros_ref)
    blk = block_rows * rs

    def wait_rows(n):
        pltpu.make_async_copy(xs_hbm.at[pl.ds(0, n * rs)], xs_hbm.at[pl.ds(0, n * rs)], sem).wait()

    def per_expert(e, total):
        start, n = pad_start_ref[e], pad_len_ref[e]

        def fill(r, carry):
            pltpu.make_async_copy(zeros_ref.at[pl.ds(0, rs)],
                                  xs_hbm.at[pl.ds(pl.multiple_of((start + r) * rs, rs), rs)], sem).start()
            return carry

        lax.fori_loop(0, n, fill, 0)
        return total + n

    n_pad = lax.fori_loop(0, n_exp, per_expert, 0)

    @pl.when(n_pad > 0)
    def _():
        wait_rows(n_pad)

    n_used = nu_ref[0]

    def fill_block(b, carry):
        pltpu.make_async_copy(zeros_ref, xs_hbm.at[pl.ds(b * blk, blk)], sem).start()
        return carry

    lax.fori_loop(n_used, n_blocks, fill_block, 0)

    @pl.when(n_used < n_blocks)
    def _():
        wait_rows((n_blocks - n_used) * block_rows)


def _pad_fill_call(pad_start, pad_len, n_used, xs, n_blocks, rs):
    n_exp = pad_start.shape[0]
    br = EXPERT_BLOCK_ROWS
    return pl.pallas_call(
        functools.partial(_pad_fill_kernel, rs=rs, n_exp=n_exp, block_rows=br, n_blocks=n_blocks),
        grid_spec=pltpu.PrefetchScalarGridSpec(
            num_scalar_prefetch=3,
            grid=(1,),
            in_specs=[pl.BlockSpec(memory_space=pl.ANY)],
            out_specs=pl.BlockSpec(memory_space=pl.ANY),
            scratch_shapes=[pltpu.VMEM((br * rs, LANES), _F32), pltpu.SemaphoreType.DMA],
        ),
        out_shape=jax.ShapeDtypeStruct(xs.shape, _F32),
        input_output_aliases={3: 0},
        compiler_params=pltpu.CompilerParams(dimension_semantics=("arbitrary",)),
        name="pad_fill",
    )(pad_start, pad_len, n_used, xs)


def _expert_kernel(be_ref, nu_ref, xs_ref, wgu_ref, bgu_ref, wdn_ref, bdn_ref, ys_ref, wgu_s, wdn_s, *, rs):
    b = pl.program_id(0)
    rows = xs_ref.shape[0] // rs
    n_chunks = wgu_ref.shape[2] // GATE_UP_CHUNK
    half = GATE_UP_CHUNK // 2

    @pl.when(b < nu_ref[0])
    def _():
        @pl.when((b == 0) | (be_ref[b] != be_ref[jnp.maximum(b - 1, 0)]))
        def _():
            r = lax.broadcasted_iota(jnp.int32, (GATE_UP_CHUNK, GATE_UP_CHUNK), 0)
            c = lax.broadcasted_iota(jnp.int32, (GATE_UP_CHUNK, GATE_UP_CHUNK), 1)
            perm = (r == jnp.where(c < half, 2 * c, 2 * (c - half) + 1)).astype(_BF16)
            for ch in range(n_chunks):
                cols = slice(ch * GATE_UP_CHUNK, (ch + 1) * GATE_UP_CHUNK)
                w = wgu_ref[0, :, cols].astype(_BF16)
                wgu_s[:, cols] = jnp.dot(w, perm, preferred_element_type=_F32).astype(_BF16)
            wdn_s[...] = wdn_ref[0].astype(_BF16)

        x = jnp.concatenate(_load_rows(xs_ref, rows, rs), axis=1).astype(_BF16)
        h = jnp.dot(x, wgu_s[...], preferred_element_type=_F32) + bgu_ref[0]
        acts = []
        for ch in range(n_chunks):
            g = jnp.minimum(h[:, ch * GATE_UP_CHUNK:ch * GATE_UP_CHUNK + half], SWIGLU_LIMIT)
            up = jnp.clip(h[:, ch * GATE_UP_CHUNK + half:(ch + 1) * GATE_UP_CHUNK], -SWIGLU_LIMIT, SWIGLU_LIMIT)
            acts.append(((up + 1.0) * (g * jax.nn.sigmoid(SWIGLU_ALPHA * g))).astype(_BF16))
        y = jnp.dot(jnp.concatenate(acts, axis=1), wdn_s[...], preferred_element_type=_F32) + bdn_ref[0]
        _store_rows(ys_ref, y)


def _expert_call(block_expert, n_used, xs, layer, w_gu_all, b_gu_grouped, w_dn_all, b_dn, rs):
    _, n_exp, d, f2 = w_gu_all.shape
    f = f2 // 2
    br = EXPERT_BLOCK_ROWS
    n_blocks = xs.shape[0] // (br * rs)
    assert f2 % GATE_UP_CHUNK == 0

    def row_map(b, be, nu):
        return (jnp.minimum(b, nu[0] - 1), 0)

    def w_map(b, be, nu):
        return (be[b], 0, 0)

    def stack_map(b, be, nu):
        return (layer, be[b], 0, 0)

    return pl.pallas_call(
        functools.partial(_expert_kernel, rs=rs),
        grid_spec=pltpu.PrefetchScalarGridSpec(
            num_scalar_prefetch=2,
            grid=(n_blocks,),
            in_specs=[
                pl.BlockSpec((br * rs, LANES), row_map),
                pl.BlockSpec((None, 1, d, f2), stack_map),
                pl.BlockSpec((1, 1, f2), w_map),
                pl.BlockSpec((None, 1, f, d), stack_map),
                pl.BlockSpec((1, 1, d), w_map),
            ],
            out_specs=pl.BlockSpec((br * rs, LANES), row_map),
            scratch_shapes=[pltpu.VMEM((d, f2), _BF16), pltpu.VMEM((f, d), _BF16)],
        ),
        out_shape=jax.ShapeDtypeStruct(xs.shape, _F32),
        input_output_aliases={2: 0},
        compiler_params=pltpu.CompilerParams(
            dimension_semantics=("arbitrary",), vmem_limit_bytes=VMEM_LIMIT_BYTES),
        name="experts",
    )(block_expert, n_used, xs, w_gu_all, b_gu_grouped, w_dn_all, b_dn)


def _combine_kernel(gates_ref, x1r_ref, g_ref, b_ref, *rest, rs, alpha):
    yk_refs, out_ref = rest[:TOP_K], rest[TOP_K]
    tb = out_ref.shape[0]
    gates = gates_ref[...].T
    pieces = []
    for j in range(rs):
        piece = alpha * x1r_ref[pl.ds(j, tb, stride=rs), :]
        for k in range(TOP_K):
            piece = piece + gates[:, k:k + 1] * yk_refs[k][pl.ds(j, tb, stride=rs), :]
        pieces.append(piece)
    out_ref[...] = _layer_norm(jnp.concatenate(pieces, axis=1), g_ref[...], b_ref[...])


def _combine_call(gates_t, x1r, ln_g, ln_b, yk, alpha, rs):
    n_tok = gates_t.shape[1]
    d = rs * LANES
    tb = min(COMBINE_BLOCK, n_tok)
    steps = n_tok // tb
    yk_specs = [pl.BlockSpec((tb * rs, LANES), lambda i, _k=k: (_k * steps + i, 0)) for k in range(TOP_K)]
    return pl.pallas_call(
        functools.partial(_combine_kernel, rs=rs, alpha=alpha),
        grid=(steps,),
        in_specs=[
            pl.BlockSpec((2 * TOP_K, tb), lambda i: (0, i)),
            pl.BlockSpec((tb * rs, LANES), lambda i: (i, 0)),
            pl.BlockSpec((1, d), lambda i: (0, 0)),
            pl.BlockSpec((1, d), lambda i: (0, 0)),
        ] + yk_specs,
        out_specs=pl.BlockSpec((tb, d), lambda i: (i, 0)),
        out_shape=jax.ShapeDtypeStruct((n_tok, d), _F32),
        compiler_params=pltpu.CompilerParams(
            dimension_semantics=("arbitrary",), vmem_limit_bytes=VMEM_LIMIT_BYTES),
        name="combine",
    )(gates_t, x1r, ln_g.reshape(1, d), ln_b.reshape(1, d), *([yk] * TOP_K))


def _routing_tables(meta_i, counts_f, n_blocks):
    n_exp = counts_f.shape[0]
    br = EXPERT_BLOCK_ROWS
    counts = counts_f[:, 0].astype(jnp.int32)
    padded = ((counts + br - 1) // br) * br
    pend = jnp.cumsum(padded)
    pstart = pend - padded
    eids = jnp.arange(n_exp, dtype=jnp.int32)
    idx, rank = meta_i[:TOP_K], meta_i[TOP_K:]
    dest = jnp.sum(jnp.where(idx[..., None] == eids, pstart, 0), axis=-1) + rank
    n_used = (pend[-1] // br).astype(jnp.int32)
    blk = jnp.minimum(jnp.arange(n_blocks, dtype=jnp.int32), n_used - 1)
    block_expert = jnp.minimum(jnp.sum((pend[None, :] <= (blk * br)[:, None]).astype(jnp.int32), axis=1),
                               n_exp - 1)
    return dest, block_expert, n_used.reshape(1), pstart + counts, padded - counts


def _moe_layer(x1r, meta_i, gates_t, counts_f, layer, w_gu_all, b_gu, w_dn_all, b_dn, ln_g, ln_b, alpha, rs):
    _, n_exp, d, f2 = w_gu_all.shape
    n_tok = gates_t.shape[1]
    br = EXPERT_BLOCK_ROWS
    n_blocks = -(-(n_tok * TOP_K) // br) + n_exp
    dest, block_expert, n_used, pad_start, pad_len = _routing_tables(meta_i, counts_f, n_blocks)
    n_rows = n_blocks * br
    xs = _sc_scatter_rows(x1r.reshape(n_tok, rs, LANES), dest.reshape(TOP_K, n_tok // SC_INDEX_ROW, SC_INDEX_ROW),
                          n_rows)
    xs = _pad_fill_call(pad_start, pad_len, n_used, xs.reshape(n_rows * rs, LANES), n_blocks, rs)
    half = GATE_UP_CHUNK // 2
    b_gu_grouped = b_gu.reshape(n_exp, f2 // GATE_UP_CHUNK, half, 2).transpose(0, 1, 3, 2).reshape(n_exp, 1, f2)
    ys = _expert_call(block_expert, n_used, xs, layer, w_gu_all, b_gu_grouped, w_dn_all,
                      b_dn.reshape(n_exp, 1, d), rs)
    yk = _sc_gather_rows(ys.reshape(n_rows, rs, LANES),
                         dest.reshape(TOP_K * n_tok // SC_INDEX_ROW, SC_INDEX_ROW))
    return _combine_call(gates_t, x1r, ln_g, ln_b, yk.reshape(TOP_K * n_tok * rs, LANES), alpha, rs)


def kernel(x, pool_w, pool_scale, sc_w_in, sc_conv_w, sc_w_out, cf_w_in, cf_b_in, cf_dw_w, cf_dw_b,
           cf_ln_g, cf_ln_b, cf_w_out, cf_b_out, mix_ln_g, mix_ln_b, router_w, router_b,
           moe_w_gu, moe_b_gu, moe_w_dn, moe_b_dn, ffn_ln_g, ffn_ln_b):
    bsz, seq, d = x.shape
    depth = mix_ln_g.shape[0]
    alpha = (2.0 * depth) ** 0.25
    rs = d // LANES
    n_chains = BATCH_CHAINS if bsz % BATCH_CHAINS == 0 else 1
    cb = bsz // n_chains
    chains = [(x, c * cb) for c in range(n_chains)]
    ia = ib = ic = 0
    for layer in range(depth):
        kind = layer % 3
        route = (mix_ln_g[layer], mix_ln_b[layer], router_w[layer], router_b[layer])
        if kind == 0:
            mixer, halo = _pool_kernel, POOL_HALO
            weights = [pool_w[ia].astype(_BF16), pool_scale[ia].reshape(1, d)]
            ia += 1
        elif kind == 1:
            mixer, halo = _short_conv_kernel, SHORT_CONV_HALO
            weights = [sc_w_in[ib].astype(_BF16), sc_conv_w[ib], sc_w_out[ib].astype(_BF16)]
            ib += 1
        else:
            mixer, halo = _conformer_kernel, CONFORMER_HALO
            weights = [cf_w_in[ic].astype(_BF16), cf_b_in[ic].reshape(1, 2 * d), cf_dw_w[ic],
                       cf_dw_b[ic].reshape(1, d), cf_ln_g[ic].reshape(1, d), cf_ln_b[ic].reshape(1, d),
                       cf_w_out[ic].astype(_BF16), cf_b_out[ic].reshape(1, d)]
            ic += 1
        routed = [_mixer_call(mixer, xc, lo, cb, weights, *route, halo, alpha) for xc, lo in chains]
        chains = []
        for x1r, meta_i, gates_t, counts_f in routed:
            xc = _moe_layer(x1r, meta_i, gates_t, counts_f, layer, moe_w_gu, moe_b_gu[layer],
                            moe_w_dn, moe_b_dn[layer], ffn_ln_g[layer], ffn_ln_b[layer], alpha, rs)
            chains.append((xc.reshape(cb, seq, d), 0))
    return jnp.concatenate([xc for xc, _ in chains], axis=0)
```

```python
import functools

import jax
import jax.numpy as jnp
from jax import lax
from jax.experimental import pallas as pl
from jax.experimental.pallas import tpu as pltpu
from jax.experimental.pallas import tpu_sc as plsc

LANES = 128
SUBLANES = 8
TOP_K = 4
POOL_WINDOWS = (2, 4, 8, 16)
POOL_HALO = 16
SHORT_CONV_HALO = 8
CONFORMER_HALO = 32
SWIGLU_LIMIT = 7.0
SWIGLU_ALPHA = 1.702
LN_EPS = 1e-5
TOKEN_BLOCK = 512
EXPERT_BLOCK_ROWS = 512
COMBINE_BLOCK = 256
BATCH_CHAINS = 2
SC_INDEX_ROW = 128
SC_CHUNK_ROWS = 32
GATE_UP_CHUNK = 2 * LANES
VMEM_LIMIT_BYTES = 56 * 1024 * 1024

_F32 = jnp.float32
_BF16 = jnp.bfloat16


def _layer_norm(z, g, b):
    mu = jnp.mean(z, axis=-1, keepdims=True)
    zc = z - mu
    var = jnp.mean(zc * zc, axis=-1, keepdims=True)
    return zc * lax.rsqrt(var + LN_EPS) * g + b


def _store_rows(row_ref, val):
    rows, d = val.shape
    rs = d // LANES
    for j in range(rs):
        row_ref[pl.ds(j, rows, stride=rs), :] = val[:, j * LANES:(j + 1) * LANES]


def _load_rows(row_ref, rows, rs):
    return [row_ref[pl.ds(j, rows, stride=rs), :] for j in range(rs)]


def _post_norm_and_route(z, g_ref, b_ref, rwt_ref, rb_ref, first,
                         x1r_ref, mi_ref, mg_ref, cnt_ref, carry_ref):
    n_tok = z.shape[0]
    n_exp = rwt_ref.shape[0]

    @pl.when(first)
    def _():
        carry_ref[...] = jnp.zeros_like(carry_ref)

    x1 = _layer_norm(z, g_ref[...], b_ref[...])
    _store_rows(x1r_ref, x1)

    logits = lax.dot_general(rwt_ref[...], x1, (((1,), (1,)), ((), ())),
                             precision=lax.Precision.HIGHEST,
                             preferred_element_type=_F32) + rb_ref[...]
    eidx = lax.broadcasted_iota(jnp.int32, logits.shape, 0)
    work = logits
    chosen = jnp.zeros(logits.shape, jnp.bool_)
    vals, idxs = [], []
    for _ in range(TOP_K):
        m = jnp.max(work, axis=0, keepdims=True)
        sel = jnp.min(jnp.where(work == m, eidx, n_exp), axis=0, keepdims=True)
        hit = eidx == sel
        vals.append(m)
        idxs.append(sel)
        chosen = jnp.logical_or(chosen, hit)
        work = jnp.where(hit, -jnp.inf, work)
    exps = [jnp.exp(v - vals[0]) for v in vals]
    denom = exps[0] + exps[1] + exps[2] + exps[3]
    gate_rows = [e / denom for e in exps] + [jnp.zeros_like(denom)] * (mg_ref.shape[0] - TOP_K)
    mg_ref[...] = jnp.concatenate(gate_rows, axis=0)

    onehot = chosen.astype(_BF16)
    r = lax.broadcasted_iota(jnp.int32, (n_tok, n_tok), 0)
    c = lax.broadcasted_iota(jnp.int32, (n_tok, n_tok), 1)
    before = (r < c).astype(_BF16)
    cum = jnp.dot(onehot, before, preferred_element_type=_F32) + carry_ref[:, 0:1]
    ranks = [jnp.sum(jnp.where(eidx == s, cum, 0.0), axis=0, keepdims=True) for s in idxs]
    mi_ref[...] = jnp.concatenate(idxs + [rk.astype(jnp.int32) for rk in ranks], axis=0)
    carry_ref[...] = carry_ref[...] + jnp.sum(chosen.astype(_F32), axis=1, keepdims=True)
    cnt_ref[...] = carry_ref[...]


def _pool_kernel(x_ref, pw_ref, ps_ref, g_ref, b_ref, rwt_ref, rb_ref,
                 x1r_ref, mi_ref, mg_ref, cnt_ref, hist_ref, carry_ref, *, alpha):
    bi, si = pl.program_id(0), pl.program_id(1)
    ts, d = x_ref.shape[1], x_ref.shape[2]
    dg = d // len(POOL_WINDOWS)

    @pl.when(si == 0)
    def _():
        hist_ref[0:POOL_HALO, :] = jnp.zeros((POOL_HALO, d), _F32)

    x = x_ref[0]
    hist_ref[POOL_HALO:POOL_HALO + ts, :] = x
    pos = si * ts + lax.broadcasted_iota(jnp.int32, (ts, 1), 0)
    pieces = []
    for gi, win in enumerate(POOL_WINDOWS):
        c0 = gi * dg
        xg = x[:, c0:c0 + dg]
        assert win & (win - 1) == 0 and win <= POOL_HALO
        ext = hist_ref[:, c0:c0 + dg]
        span = 1
        while span < win:
            ext = ext + pltpu.roll(ext, span, 0)
            span *= 2
        acc = ext[POOL_HALO:, :]
        inv_count = 1.0 / jnp.minimum(pos + 1, win).astype(_F32)
        diff = acc * inv_count - xg
        hg = jnp.dot(diff.astype(_BF16), pw_ref[gi], preferred_element_type=_F32)
        pieces.append(alpha * xg + hg * ps_ref[:, c0:c0 + dg])
    hist_ref[0:POOL_HALO, :] = x[ts - POOL_HALO:, :]
    z = jnp.concatenate(pieces, axis=1)
    _post_norm_and_route(z, g_ref, b_ref, rwt_ref, rb_ref, (bi == 0) & (si == 0),
                         x1r_ref, mi_ref, mg_ref, cnt_ref, carry_ref)


def _short_conv_kernel(x_ref, win_ref, cw_ref, wout_ref, g_ref, b_ref, rwt_ref, rb_ref,
                       x1r_ref, mi_ref, mg_ref, cnt_ref, hist_ref, carry_ref, *, alpha):
    bi, si = pl.program_id(0), pl.program_id(1)
    ts, d = x_ref.shape[1], x_ref.shape[2]
    halo = SHORT_CONV_HALO

    @pl.when(si == 0)
    def _():
        hist_ref[0:halo, :] = jnp.zeros((halo, d), _F32)

    x = x_ref[0]
    xb = x.astype(_BF16)
    gate_b = jnp.dot(xb, win_ref[:, 0:d], preferred_element_type=_F32)
    gate_c = jnp.dot(xb, win_ref[:, d:2 * d], preferred_element_type=_F32)
    h = jnp.dot(xb, win_ref[:, 2 * d:3 * d], preferred_element_type=_F32)
    v = gate_c * h
    hist_ref[halo:halo + ts, :] = v
    width = cw_ref.shape[0]
    u = cw_ref[width - 1:width, :] * v
    for k in range(width - 1):
        shift = width - 1 - k
        u = u + cw_ref[k:k + 1, :] * hist_ref[halo - shift:halo - shift + ts, :]
    hist_ref[0:halo, :] = v[ts - halo:, :]
    y = jnp.dot((gate_b * u).astype(_BF16), wout_ref[...], preferred_element_type=_F32)
    _post_norm_and_route(alpha * x + y, g_ref, b_ref, rwt_ref, rb_ref, (bi == 0) & (si == 0),
                         x1r_ref, mi_ref, mg_ref, cnt_ref, carry_ref)


def _conformer_kernel(x_ref, win_ref, bin_ref, dww_ref, dwb_ref, lng_ref, lnb_ref, wout_ref, bout_ref,
                      g_ref, b_ref, rwt_ref, rb_ref,
                      x1r_ref, mi_ref, mg_ref, cnt_ref, hist_ref, carry_ref, *, alpha):
    bi, si = pl.program_id(0), pl.program_id(1)
    ts, d = x_ref.shape[1], x_ref.shape[2]
    halo = CONFORMER_HALO

    @pl.when(si == 0)
    def _():
        hist_ref[0:halo, :] = jnp.zeros((halo, d), _F32)

    x = x_ref[0]
    xb = x.astype(_BF16)
    a = jnp.dot(xb, win_ref[:, 0:d], preferred_element_type=_F32) + bin_ref[:, 0:d]
    gate = jnp.dot(xb, win_ref[:, d:2 * d], preferred_element_type=_F32) + bin_ref[:, d:2 * d]
    u = a * jax.nn.sigmoid(gate)
    hist_ref[halo:halo + ts, :] = u
    width = dww_ref.shape[0]
    hist = hist_ref[...]
    acc = dwb_ref[...]
    for r in range(SUBLANES):
        rolled = hist if r == 0 else pltpu.roll(hist, r, 0)
        for q in range(halo // SUBLANES):
            shift = SUBLANES * q + r
            if shift < width:
                k = width - 1 - shift
                start = halo - SUBLANES * q
                acc = acc + dww_ref[k:k + 1, :] * rolled[start:start + ts, :]
    hist_ref[0:halo, :] = u[ts - halo:, :]
    un = _layer_norm(acc, lng_ref[...], lnb_ref[...])
    un = un * jax.nn.sigmoid(un)
    y = jnp.dot(un.astype(_BF16), wout_ref[...], preferred_element_type=_F32) + bout_ref[...]
    _post_norm_and_route(alpha * x + y, g_ref, b_ref, rwt_ref, rb_ref, (bi == 0) & (si == 0),
                         x1r_ref, mi_ref, mg_ref, cnt_ref, carry_ref)


def _mixer_call(kernel_fn, x, batch_lo, bsz, weights, ln_g, ln_b, router_w, router_b, halo, alpha):
    _, seq, d = x.shape
    n_exp = router_w.shape[1]
    ts = min(TOKEN_BLOCK, seq)
    rs = d // LANES
    n_tok = bsz * seq
    nsb = seq // ts

    def full(a):
        nd = a.ndim
        return pl.BlockSpec(a.shape, lambda bi, si, _nd=nd: (0,) * _nd)

    small = [ln_g.reshape(1, d), ln_b.reshape(1, d), router_w.T, router_b.reshape(n_exp, 1)]
    operands = [x] + list(weights) + small
    in_specs = [pl.BlockSpec((1, ts, d), lambda bi, si: (batch_lo + bi, si, 0))] + [full(a) for a in operands[1:]]
    tok_map = lambda bi, si: (0, bi * nsb + si)
    out_shape = [
        jax.ShapeDtypeStruct((n_tok * rs, LANES), _F32),
        jax.ShapeDtypeStruct((2 * TOP_K, n_tok), jnp.int32),
        jax.ShapeDtypeStruct((2 * TOP_K, n_tok), _F32),
        jax.ShapeDtypeStruct((n_exp, LANES), _F32),
    ]
    out_specs = [
        pl.BlockSpec((ts * rs, LANES), lambda bi, si: (bi * nsb + si, 0)),
        pl.BlockSpec((2 * TOP_K, ts), tok_map),
        pl.BlockSpec((2 * TOP_K, ts), tok_map),
        pl.BlockSpec((n_exp, LANES), lambda bi, si: (0, 0)),
    ]
    return pl.pallas_call(
        functools.partial(kernel_fn, alpha=alpha),
        grid=(bsz, nsb),
        in_specs=in_specs,
        out_specs=out_specs,
        out_shape=out_shape,
        scratch_shapes=[pltpu.VMEM((halo + ts, d), _F32), pltpu.VMEM((n_exp, LANES), _F32)],
        compiler_params=pltpu.CompilerParams(
            dimension_semantics=("arbitrary", "arbitrary"), vmem_limit_bytes=VMEM_LIMIT_BYTES),
        name=kernel_fn.__name__.strip("_"),
    )(*operands)


def _sc_workers():
    info = plsc.get_sparse_core_info()
    return info.num_cores, info.num_subcores, info.num_lanes


def _sc_scatter_rows(x3, dest3, n_rows):
    n_cores, n_sub, n_lanes = _sc_workers()
    n_tok, rs, _ = x3.shape
    top_k = dest3.shape[0]
    chunk = SC_CHUNK_ROWS
    tok_per_w = n_tok // (n_cores * n_sub)
    rows_per_w = tok_per_w // SC_INDEX_ROW
    chunks_per_row = SC_INDEX_ROW // chunk
    assert rows_per_w * SC_INDEX_ROW * n_cores * n_sub == n_tok and chunks_per_row % 2 == 0
    mesh = plsc.VectorSubcoreMesh(core_axis_name="core", subcore_axis_name="subcore")

    @pl.kernel(out_type=jax.ShapeDtypeStruct((n_rows, rs, LANES), _F32), mesh=mesh,
               scratch_types=[pltpu.VMEM((top_k, rows_per_w, SC_INDEX_ROW), jnp.int32),
                              pltpu.VMEM((2, chunk, rs, LANES), _F32),
                              pltpu.SemaphoreType.DMA((2,)), pltpu.SemaphoreType.DMA((2,))])
    def scatter_kernel(x_hbm, d_hbm, o_hbm, idx_v, buf, rsem, ssem):
        wid = lax.axis_index("subcore") * n_cores + lax.axis_index("core")
        for k in range(top_k):
            pltpu.sync_copy(d_hbm.at[k, pl.ds(wid * rows_per_w, rows_per_w)], idx_v.at[k])
        base = wid * tok_per_w

        def read(j, c, slot):
            return pltpu.make_async_copy(x_hbm.at[pl.ds(base + j * SC_INDEX_ROW + c * chunk, chunk)],
                                         buf.at[slot], rsem.at[slot])

        def scatters(j, c, slot):
            copies = []
            for k in range(top_k):
                for h in range(chunk // n_lanes):
                    rows = idx_v[k, j, pl.ds(c * chunk + h * n_lanes, n_lanes)]
                    copies.append(pltpu.make_async_copy(buf.at[slot, pl.ds(h * n_lanes, n_lanes)],
                                                        o_hbm.at[rows], ssem.at[slot]))
            return copies

        def wait_scatters(j, c, slot):
            for cp in scatters(j, c, slot):
                cp.wait()

        read(0, 0, 0).start()

        def per_index_row(j, carry):
            for c in range(chunks_per_row):
                slot = c % 2
                if c == 0:
                    @pl.when(j > 0)
                    def _():
                        wait_scatters(j - 1, chunks_per_row - 1, 1 - slot)
                    read(j, c + 1, 1 - slot).start()
                elif c < chunks_per_row - 1:
                    wait_scatters(j, c - 1, 1 - slot)
                    read(j, c + 1, 1 - slot).start()
                else:
                    @pl.when(j + 1 < rows_per_w)
                    def _():
                        wait_scatters(j, c - 1, 1 - slot)
                        read(j + 1, 0, 1 - slot).start()
                read(j, c, slot).wait()
                for cp in scatters(j, c, slot):
                    cp.start()
            return carry

        lax.fori_loop(0, rows_per_w, per_index_row, 0)
        wait_scatters(rows_per_w - 1, chunks_per_row - 2, 0)
        wait_scatters(rows_per_w - 1, chunks_per_row - 1, 1)

    return scatter_kernel(x3, dest3)


def _sc_gather_rows(table3, idx2):
    n_cores, n_sub, _ = _sc_workers()
    n_idx_rows = idx2.shape[0]
    rs = table3.shape[1]
    chunk = SC_CHUNK_ROWS * (LANES * SUBLANES) // (rs * LANES)
    rows_per_w = n_idx_rows // (n_cores * n_sub)
    chunks_per_row = SC_INDEX_ROW // chunk
    assert rows_per_w * n_cores * n_sub == n_idx_rows and chunks_per_row % 2 == 0
    mesh = plsc.VectorSubcoreMesh(core_axis_name="core", subcore_axis_name="subcore")

    @pl.kernel(out_type=jax.ShapeDtypeStruct((n_idx_rows * SC_INDEX_ROW, rs, LANES), table3.dtype), mesh=mesh,
               scratch_types=[pltpu.VMEM((rows_per_w, SC_INDEX_ROW), jnp.int32),
                              pltpu.VMEM((2, chunk, rs, LANES), table3.dtype),
                              pltpu.SemaphoreType.DMA((2,)), pltpu.SemaphoreType.DMA((2,))])
    def gather_kernel(t_hbm, i_hbm, o_hbm, idx_v, buf, gsem, wsem):
        wid = lax.axis_index("subcore") * n_cores + lax.axis_index("core")
        pltpu.sync_copy(i_hbm.at[pl.ds(wid * rows_per_w, rows_per_w)], idx_v)
        base = wid * rows_per_w * SC_INDEX_ROW

        def gather(j, c, slot):
            return pltpu.make_async_copy(t_hbm.at[idx_v.at[j, pl.ds(c * chunk, chunk)]], buf.at[slot], gsem.at[slot])

        def write(j, c, slot):
            return pltpu.make_async_copy(buf.at[slot], o_hbm.at[pl.ds(base + j * SC_INDEX_ROW + c * chunk, chunk)],
                                         wsem.at[slot])

        gather(0, 0, 0).start()

        def per_index_row(j, carry):
            for c in range(chunks_per_row):
                slot = c % 2
                if c == 0:
                    @pl.when(j > 0)
                    def _():
                        write(j - 1, chunks_per_row - 1, 1 - slot).wait()
                    gather(j, c + 1, 1 - slot).start()
                elif c < chunks_per_row - 1:
                    write(j, c - 1, 1 - slot).wait()
                    gather(j, c + 1, 1 - slot).start()
                else:
                    @pl.when(j + 1 < rows_per_w)
                    def _():
                        write(j, c - 1, 1 - slot).wait()
                        gather(j + 1, 0, 1 - slot).start()
                gather(j, c, slot).wait()
                write(j, c, slot).start()
            return carry

        lax.fori_loop(0, rows_per_w, per_index_row, 0)
        write(rows_per_w - 1, chunks_per_row - 2, 0).wait()
        write(rows_per_w - 1, chunks_per_row - 1, 1).wait()

    return gather_kernel(table3, idx2)


def _pack_bf16_pairs(v):
    half = v.shape[1] // 2
    lo = lax.bitcast_convert_type(v[:, :half].astype(_BF16).astype(_F32), jnp.uint32)
    hi = lax.bitcast_convert_type(v[:, half:].astype(_BF16).astype(_F32), jnp.uint32)
    return (lo >> 16) | (hi & jnp.uint32(0xFFFF0000))


def _unpack_bf16_pairs(w):
    return (lax.bitcast_convert_type(w << 16, _F32),
            lax.bitcast_convert_type(w & jnp.uint32(0xFFFF0000), _F32))


def _expert_kernel(be_ref, nv_ref, nu_ref, xs_ref, wgu_ref, bgu_ref, wdn_ref, bdn_ref, ys_ref, wgu_s, wdn_s, *, rs):
    b = pl.program_id(0)
    rows = xs_ref.shape[0] // rs
    n_chunks = wgu_ref.shape[2] // GATE_UP_CHUNK
    half = GATE_UP_CHUNK // 2

    @pl.when(b < nu_ref[0])
    def _():
        @pl.when((b == 0) | (be_ref[b] != be_ref[jnp.maximum(b - 1, 0)]))
        def _():
            r = lax.broadcasted_iota(jnp.int32, (GATE_UP_CHUNK, GATE_UP_CHUNK), 0)
            c = lax.broadcasted_iota(jnp.int32, (GATE_UP_CHUNK, GATE_UP_CHUNK), 1)
            perm = (r == jnp.where(c < half, 2 * c, 2 * (c - half) + 1)).astype(_BF16)
            for ch in range(n_chunks):
                cols = slice(ch * GATE_UP_CHUNK, (ch + 1) * GATE_UP_CHUNK)
                w = wgu_ref[0, :, cols].astype(_BF16)
                wgu_s[:, cols] = jnp.dot(w, perm, preferred_element_type=_F32).astype(_BF16)
            wdn_s[...] = wdn_ref[0].astype(_BF16)

        x = jnp.concatenate(_load_rows(xs_ref, rows, rs), axis=1)
        defined = lax.broadcasted_iota(jnp.int32, (rows, 1), 0) < nv_ref[b]
        x = jnp.where(defined, x, 0.0).astype(_BF16)
        h = jnp.dot(x, wgu_s[...], preferred_element_type=_F32) + bgu_ref[0]
        acts = []
        for ch in range(n_chunks):
            g = jnp.minimum(h[:, ch * GATE_UP_CHUNK:ch * GATE_UP_CHUNK + half], SWIGLU_LIMIT)
            up = jnp.clip(h[:, ch * GATE_UP_CHUNK + half:(ch + 1) * GATE_UP_CHUNK], -SWIGLU_LIMIT, SWIGLU_LIMIT)
            acts.append(((up + 1.0) * (g * jax.nn.sigmoid(SWIGLU_ALPHA * g))).astype(_BF16))
        y = jnp.dot(jnp.concatenate(acts, axis=1), wdn_s[...], preferred_element_type=_F32) + bdn_ref[0]
        _store_rows(ys_ref, _pack_bf16_pairs(y))


def _expert_call(block_expert, block_valid, n_used, xs, layer, w_gu_all, b_gu_grouped, w_dn_all, b_dn, rs):
    _, n_exp, d, f2 = w_gu_all.shape
    f = f2 // 2
    br = EXPERT_BLOCK_ROWS
    n_blocks = xs.shape[0] // (br * rs)
    assert f2 % GATE_UP_CHUNK == 0 and rs % 2 == 0

    def row_map(b, be, nv, nu):
        return (jnp.minimum(b, nu[0] - 1), 0)

    def w_map(b, be, nv, nu):
        return (be[b], 0, 0)

    def stack_map(b, be, nv, nu):
        return (layer, be[b], 0, 0)

    return pl.pallas_call(
        functools.partial(_expert_kernel, rs=rs),
        grid_spec=pltpu.PrefetchScalarGridSpec(
            num_scalar_prefetch=3,
            grid=(n_blocks,),
            in_specs=[
                pl.BlockSpec((br * rs, LANES), row_map),
                pl.BlockSpec((None, 1, d, f2), stack_map),
                pl.BlockSpec((1, 1, f2), w_map),
                pl.BlockSpec((None, 1, f, d), stack_map),
                pl.BlockSpec((1, 1, d), w_map),
            ],
            out_specs=pl.BlockSpec((br * rs // 2, LANES), row_map),
            scratch_shapes=[pltpu.VMEM((d, f2), _BF16), pltpu.VMEM((f, d), _BF16)],
        ),
        out_shape=jax.ShapeDtypeStruct((xs.shape[0] // 2, LANES), jnp.uint32),
        compiler_params=pltpu.CompilerParams(
            dimension_semantics=("arbitrary",), vmem_limit_bytes=VMEM_LIMIT_BYTES),
        name="experts",
    )(block_expert, block_valid, n_used, xs, w_gu_all, b_gu_grouped, w_dn_all, b_dn)


def _combine_kernel(gates_ref, x1r_ref, g_ref, b_ref, *rest, rs, alpha):
    yk_refs, out_ref = rest[:TOP_K], rest[-1]
    tb = out_ref.shape[0]
    rp = rs // 2
    gates = gates_ref[...].T
    lo_pieces, hi_pieces = [], []
    for j in range(rp):
        lo = alpha * x1r_ref[pl.ds(j, tb, stride=rs), :]
        hi = alpha * x1r_ref[pl.ds(rp + j, tb, stride=rs), :]
        for k in range(TOP_K):
            y_lo, y_hi = _unpack_bf16_pairs(yk_refs[k][pl.ds(j, tb, stride=rp), :])
            lo = lo + gates[:, k:k + 1] * y_lo
            hi = hi + gates[:, k:k + 1] * y_hi
        lo_pieces.append(lo)
        hi_pieces.append(hi)
    out_ref[...] = _layer_norm(jnp.concatenate(lo_pieces + hi_pieces, axis=1), g_ref[...], b_ref[...])


def _combine_call(gates_t, x1r, ln_g, ln_b, yk, alpha, rs, out_tokens, token_lo, out_buf):
    n_tok = gates_t.shape[1]
    d = rs * LANES
    rp = rs // 2
    tb = min(COMBINE_BLOCK, n_tok)
    steps = n_tok // tb
    block_lo = token_lo // tb
    yk_specs = [pl.BlockSpec((tb * rp, LANES), lambda i, _k=k: (_k * steps + i, 0)) for k in range(TOP_K)]
    operands = [gates_t, x1r, ln_g.reshape(1, d), ln_b.reshape(1, d)] + [yk] * TOP_K
    in_specs = [
        pl.BlockSpec((2 * TOP_K, tb), lambda i: (0, i)),
        pl.BlockSpec((tb * rs, LANES), lambda i: (i, 0)),
        pl.BlockSpec((1, d), lambda i: (0, 0)),
        pl.BlockSpec((1, d), lambda i: (0, 0)),
    ] + yk_specs
    aliases = {}
    if out_buf is not None:
        aliases = {len(operands): 0}
        operands.append(out_buf)
        in_specs.append(pl.BlockSpec(memory_space=pl.ANY))
    return pl.pallas_call(
        functools.partial(_combine_kernel, rs=rs, alpha=alpha),
        grid=(steps,),
        in_specs=in_specs,
        out_specs=pl.BlockSpec((tb, d), lambda i: (block_lo + i, 0)),
        out_shape=jax.ShapeDtypeStruct((out_tokens, d), _F32),
        input_output_aliases=aliases,
        compiler_params=pltpu.CompilerParams(
            dimension_semantics=("arbitrary",), vmem_limit_bytes=VMEM_LIMIT_BYTES),
        name="combine",
    )(*operands)


def _routing_tables(meta_i, counts_f, n_blocks):
    n_exp = counts_f.shape[0]
    br = EXPERT_BLOCK_ROWS
    counts = counts_f[:, 0].astype(jnp.int32)
    padded = ((counts + br - 1) // br) * br
    pend = jnp.cumsum(padded)
    pstart = pend - padded
    eids = jnp.arange(n_exp, dtype=jnp.int32)
    idx, rank = meta_i[:TOP_K], meta_i[TOP_K:]
    dest = jnp.sum(jnp.where(idx[..., None] == eids, pstart, 0), axis=-1) + rank
    n_used = (pend[-1] // br).astype(jnp.int32)
    blk = jnp.minimum(jnp.arange(n_blocks, dtype=jnp.int32), n_used - 1)
    block_expert = jnp.minimum(jnp.sum((pend[None, :] <= (blk * br)[:, None]).astype(jnp.int32), axis=1),
                               n_exp - 1)
    group_end = jnp.sum(jnp.where(block_expert[:, None] == eids, pstart + counts, 0), axis=-1)
    block_valid = jnp.clip(group_end - blk * br, 0, br).astype(jnp.int32)
    return dest, block_expert, block_valid, n_used.reshape(1)


def _moe_layer(x1r, meta_i, gates_t, counts_f, layer, w_gu_all, b_gu, w_dn_all, b_dn, ln_g, ln_b, alpha, rs,
               out_tokens, token_lo, out_buf):
    _, n_exp, d, f2 = w_gu_all.shape
    n_tok = gates_t.shape[1]
    br = EXPERT_BLOCK_ROWS
    rp = rs // 2
    n_blocks = -(-(n_tok * TOP_K) // br) + n_exp
    dest, block_expert, block_valid, n_used = _routing_tables(meta_i, counts_f, n_blocks)
    n_rows = n_blocks * br
    xs = _sc_scatter_rows(x1r.reshape(n_tok, rs, LANES), dest.reshape(TOP_K, n_tok // SC_INDEX_ROW, SC_INDEX_ROW),
                          n_rows)
    half = GATE_UP_CHUNK // 2
    b_gu_grouped = b_gu.reshape(n_exp, f2 // GATE_UP_CHUNK, half, 2).transpose(0, 1, 3, 2).reshape(n_exp, 1, f2)
    ys = _expert_call(block_expert, block_valid, n_used, xs.reshape(n_rows * rs, LANES), layer, w_gu_all,
                      b_gu_grouped, w_dn_all, b_dn.reshape(n_exp, 1, d), rs)
    yk = _sc_gather_rows(ys.reshape(n_rows, rp, LANES),
                         dest.reshape(TOP_K * n_tok // SC_INDEX_ROW, SC_INDEX_ROW))
    return _combine_call(gates_t, x1r, ln_g, ln_b, yk.reshape(TOP_K * n_tok * rp, LANES), alpha, rs,
                         out_tokens, token_lo, out_buf)


def kernel(x, pool_w, pool_scale, sc_w_in, sc_conv_w, sc_w_out, cf_w_in, cf_b_in, cf_dw_w, cf_dw_b,
           cf_ln_g, cf_ln_b, cf_w_out, cf_b_out, mix_ln_g, mix_ln_b, router_w, router_b,
           moe_w_gu, moe_b_gu, moe_w_dn, moe_b_dn, ffn_ln_g, ffn_ln_b):
    bsz, seq, d = x.shape
    depth = mix_ln_g.shape[0]
    alpha = (2.0 * depth) ** 0.25
    rs = d // LANES
    n_chains = BATCH_CHAINS if bsz % BATCH_CHAINS == 0 else 1
    cb = bsz // n_chains
    chains = [(x, c * cb) for c in range(n_chains)]
    ia = ib = ic = 0
    for layer in range(depth):
        kind = layer % 3
        route = (mix_ln_g[layer], mix_ln_b[layer], router_w[layer], router_b[layer])
        if kind == 0:
            mixer, halo = _pool_kernel, POOL_HALO
            weights = [pool_w[ia].astype(_BF16), pool_scale[ia].reshape(1, d)]
            ia += 1
        elif kind == 1:
            mixer, halo = _short_conv_kernel, SHORT_CONV_HALO
            weights = [sc_w_in[ib].astype(_BF16), sc_conv_w[ib], sc_w_out[ib].astype(_BF16)]
            ib += 1
        else:
            mixer, halo = _conformer_kernel, CONFORMER_HALO
            weights = [cf_w_in[ic].astype(_BF16), cf_b_in[ic].reshape(1, 2 * d), cf_dw_w[ic],
                       cf_dw_b[ic].reshape(1, d), cf_ln_g[ic].reshape(1, d), cf_ln_b[ic].reshape(1, d),
                       cf_w_out[ic].astype(_BF16), cf_b_out[ic].reshape(1, d)]
            ic += 1
        routed = [_mixer_call(mixer, xc, lo, cb, weights, *route, halo, alpha) for xc, lo in chains]
        last = layer == depth - 1
        chains, out_buf = [], None
        for c, (x1r, meta_i, gates_t, counts_f) in enumerate(routed):
            out_tokens, token_lo = (bsz * seq, c * cb * seq) if last else (cb * seq, 0)
            xc = _moe_layer(x1r, meta_i, gates_t, counts_f, layer, moe_w_gu, moe_b_gu[layer],
                            moe_w_dn, moe_b_dn[layer], ffn_ln_g[layer], ffn_ln_b[layer], alpha, rs,
                            out_tokens, token_lo, out_buf)
            if last:
                out_buf = xc
            else:
                chains.append((xc.reshape(cb, seq, d), 0))
    return out_buf.reshape(bsz, seq, d)
```

```python
import functools

import jax
import jax.numpy as jnp
from jax import lax
from jax.experimental import pallas as pl
from jax.experimental.pallas import tpu as pltpu
from jax.experimental.pallas import tpu_sc as plsc

LANES = 128
SUBLANES = 8
TOP_K = 4
POOL_WINDOWS = (2, 4, 8, 16)
POOL_HALO = 16
SHORT_CONV_HALO = 8
CONFORMER_HALO = 32
SWIGLU_LIMIT = 7.0
SWIGLU_ALPHA = 1.702
LN_EPS = 1e-5
TOKEN_BLOCK = 512
EXPERT_BLOCK_ROWS = 512
COMBINE_BLOCK = 256
BATCH_CHAINS = 2
SC_INDEX_ROW = 128
SC_CHUNK_ROWS = 32
GATE_UP_CHUNK = 2 * LANES
VMEM_LIMIT_BYTES = 56 * 1024 * 1024

_F32 = jnp.float32
_BF16 = jnp.bfloat16


def _layer_norm(z, g, b):
    mu = jnp.mean(z, axis=-1, keepdims=True)
    zc = z - mu
    var = jnp.mean(zc * zc, axis=-1, keepdims=True)
    return zc * lax.rsqrt(var + LN_EPS) * g + b


def _store_rows(row_ref, val):
    rows, d = val.shape
    rs = d // LANES
    for j in range(rs):
        row_ref[pl.ds(j, rows, stride=rs), :] = val[:, j * LANES:(j + 1) * LANES]


def _load_rows(row_ref, rows, rs):
    return [row_ref[pl.ds(j, rows, stride=rs), :] for j in range(rs)]


def _post_norm_and_route(z, g_ref, b_ref, rwt_ref, rb_ref, first,
                         x1r_ref, x1p_ref, mi_ref, mg_ref, cnt_ref, carry_ref):
    n_tok = z.shape[0]
    n_exp = rwt_ref.shape[0]

    @pl.when(first)
    def _():
        carry_ref[...] = jnp.zeros_like(carry_ref)

    x1 = _layer_norm(z, g_ref[...], b_ref[...])
    _store_rows(x1r_ref, x1)
    _store_rows(x1p_ref, _pack_bf16_pairs(x1))

    logits = lax.dot_general(rwt_ref[...], x1, (((1,), (1,)), ((), ())),
                             precision=lax.Precision.HIGHEST,
                             preferred_element_type=_F32) + rb_ref[...]
    eidx = lax.broadcasted_iota(jnp.int32, logits.shape, 0)
    work = logits
    chosen = jnp.zeros(logits.shape, jnp.bool_)
    vals, idxs = [], []
    for _ in range(TOP_K):
        m = jnp.max(work, axis=0, keepdims=True)
        sel = jnp.min(jnp.where(work == m, eidx, n_exp), axis=0, keepdims=True)
        hit = eidx == sel
        vals.append(m)
        idxs.append(sel)
        chosen = jnp.logical_or(chosen, hit)
        work = jnp.where(hit, -jnp.inf, work)
    exps = [jnp.exp(v - vals[0]) for v in vals]
    denom = exps[0] + exps[1] + exps[2] + exps[3]
    gate_rows = [e / denom for e in exps] + [jnp.zeros_like(denom)] * (mg_ref.shape[0] - TOP_K)
    mg_ref[...] = jnp.concatenate(gate_rows, axis=0)

    onehot = chosen.astype(_BF16)
    r = lax.broadcasted_iota(jnp.int32, (n_tok, n_tok), 0)
    c = lax.broadcasted_iota(jnp.int32, (n_tok, n_tok), 1)
    before = (r < c).astype(_BF16)
    cum = jnp.dot(onehot, before, preferred_element_type=_F32) + carry_ref[:, 0:1]
    ranks = [jnp.sum(jnp.where(eidx == s, cum, 0.0), axis=0, keepdims=True) for s in idxs]
    mi_ref[...] = jnp.concatenate(idxs + [rk.astype(jnp.int32) for rk in ranks], axis=0)
    carry_ref[...] = carry_ref[...] + jnp.sum(chosen.astype(_F32), axis=1, keepdims=True)
    cnt_ref[...] = carry_ref[...]


def _pool_kernel(x_ref, pw_ref, ps_ref, g_ref, b_ref, rwt_ref, rb_ref,
                 x1r_ref, x1p_ref, mi_ref, mg_ref, cnt_ref, hist_ref, carry_ref, *, alpha):
    bi, si = pl.program_id(0), pl.program_id(1)
    ts, d = x_ref.shape[1], x_ref.shape[2]
    dg = d // len(POOL_WINDOWS)

    @pl.when(si == 0)
    def _():
        hist_ref[0:POOL_HALO, :] = jnp.zeros((POOL_HALO, d), _F32)

    x = x_ref[0]
    hist_ref[POOL_HALO:POOL_HALO + ts, :] = x
    pos = si * ts + lax.broadcasted_iota(jnp.int32, (ts, 1), 0)
    pieces = []
    for gi, win in enumerate(POOL_WINDOWS):
        c0 = gi * dg
        xg = x[:, c0:c0 + dg]
        assert win & (win - 1) == 0 and win <= POOL_HALO
        ext = hist_ref[:, c0:c0 + dg]
        span = 1
        while span < win:
            ext = ext + pltpu.roll(ext, span, 0)
            span *= 2
        acc = ext[POOL_HALO:, :]
        inv_count = 1.0 / jnp.minimum(pos + 1, win).astype(_F32)
        diff = acc * inv_count - xg
        hg = jnp.dot(diff.astype(_BF16), pw_ref[gi], preferred_element_type=_F32)
        pieces.append(alpha * xg + hg * ps_ref[:, c0:c0 + dg])
    hist_ref[0:POOL_HALO, :] = x[ts - POOL_HALO:, :]
    z = jnp.concatenate(pieces, axis=1)
    _post_norm_and_route(z, g_ref, b_ref, rwt_ref, rb_ref, (bi == 0) & (si == 0),
                         x1r_ref, x1p_ref, mi_ref, mg_ref, cnt_ref, carry_ref)


def _short_conv_kernel(x_ref, win_ref, cw_ref, wout_ref, g_ref, b_ref, rwt_ref, rb_ref,
                       x1r_ref, x1p_ref, mi_ref, mg_ref, cnt_ref, hist_ref, carry_ref, *, alpha):
    bi, si = pl.program_id(0), pl.program_id(1)
    ts, d = x_ref.shape[1], x_ref.shape[2]
    halo = SHORT_CONV_HALO

    @pl.when(si == 0)
    def _():
        hist_ref[0:halo, :] = jnp.zeros((halo, d), _F32)

    x = x_ref[0]
    xb = x.astype(_BF16)
    gate_b = jnp.dot(xb, win_ref[:, 0:d], preferred_element_type=_F32)
    gate_c = jnp.dot(xb, win_ref[:, d:2 * d], preferred_element_type=_F32)
    h = jnp.dot(xb, win_ref[:, 2 * d:3 * d], preferred_element_type=_F32)
    v = gate_c * h
    hist_ref[halo:halo + ts, :] = v
    width = cw_ref.shape[0]
    u = cw_ref[width - 1:width, :] * v
    for k in range(width - 1):
        shift = width - 1 - k
        u = u + cw_ref[k:k + 1, :] * hist_ref[halo - shift:halo - shift + ts, :]
    hist_ref[0:halo, :] = v[ts - halo:, :]
    y = jnp.dot((gate_b * u).astype(_BF16), wout_ref[...], preferred_element_type=_F32)
    _post_norm_and_route(alpha * x + y, g_ref, b_ref, rwt_ref, rb_ref, (bi == 0) & (si == 0),
                         x1r_ref, x1p_ref, mi_ref, mg_ref, cnt_ref, carry_ref)


def _conformer_kernel(x_ref, win_ref, bin_ref, dww_ref, dwb_ref, lng_ref, lnb_ref, wout_ref, bout_ref,
                      g_ref, b_ref, rwt_ref, rb_ref,
                      x1r_ref, x1p_ref, mi_ref, mg_ref, cnt_ref, hist_ref, carry_ref, *, alpha):
    bi, si = pl.program_id(0), pl.program_id(1)
    ts, d = x_ref.shape[1], x_ref.shape[2]
    halo = CONFORMER_HALO

    @pl.when(si == 0)
    def _():
        hist_ref[0:halo, :] = jnp.zeros((halo, d), _F32)

    x = x_ref[0]
    xb = x.astype(_BF16)
    a = jnp.dot(xb, win_ref[:, 0:d], preferred_element_type=_F32) + bin_ref[:, 0:d]
    gate = jnp.dot(xb, win_ref[:, d:2 * d], preferred_element_type=_F32) + bin_ref[:, d:2 * d]
    u = a * jax.nn.sigmoid(gate)
    hist_ref[halo:halo + ts, :] = u
    width = dww_ref.shape[0]
    hist = hist_ref[...]
    acc = dwb_ref[...]
    for r in range(SUBLANES):
        rolled = hist if r == 0 else pltpu.roll(hist, r, 0)
        for q in range(halo // SUBLANES):
            shift = SUBLANES * q + r
            if shift < width:
                k = width - 1 - shift
                start = halo - SUBLANES * q
                acc = acc + dww_ref[k:k + 1, :] * rolled[start:start + ts, :]
    hist_ref[0:halo, :] = u[ts - halo:, :]
    un = _layer_norm(acc, lng_ref[...], lnb_ref[...])
    un = un * jax.nn.sigmoid(un)
    y = jnp.dot(un.astype(_BF16), wout_ref[...], preferred_element_type=_F32) + bout_ref[...]
    _post_norm_and_route(alpha * x + y, g_ref, b_ref, rwt_ref, rb_ref, (bi == 0) & (si == 0),
                         x1r_ref, x1p_ref, mi_ref, mg_ref, cnt_ref, carry_ref)


def _mixer_call(kernel_fn, x, batch_lo, bsz, weights, ln_g, ln_b, router_w, router_b, halo, alpha):
    _, seq, d = x.shape
    n_exp = router_w.shape[1]
    ts = min(TOKEN_BLOCK, seq)
    rs = d // LANES
    n_tok = bsz * seq
    nsb = seq // ts

    def full(a):
        nd = a.ndim
        return pl.BlockSpec(a.shape, lambda bi, si, _nd=nd: (0,) * _nd)

    small = [ln_g.reshape(1, d), ln_b.reshape(1, d), router_w.T, router_b.reshape(n_exp, 1)]
    operands = [x] + list(weights) + small
    in_specs = [pl.BlockSpec((1, ts, d), lambda bi, si: (batch_lo + bi, si, 0))] + [full(a) for a in operands[1:]]
    tok_map = lambda bi, si: (0, bi * nsb + si)
    out_shape = [
        jax.ShapeDtypeStruct((n_tok * rs, LANES), _F32),
        jax.ShapeDtypeStruct((n_tok * rs // 2, LANES), jnp.uint32),
        jax.ShapeDtypeStruct((2 * TOP_K, n_tok), jnp.int32),
        jax.ShapeDtypeStruct((2 * TOP_K, n_tok), _F32),
        jax.ShapeDtypeStruct((n_exp, LANES), _F32),
    ]
    out_specs = [
        pl.BlockSpec((ts * rs, LANES), lambda bi, si: (bi * nsb + si, 0)),
        pl.BlockSpec((ts * rs // 2, LANES), lambda bi, si: (bi * nsb + si, 0)),
        pl.BlockSpec((2 * TOP_K, ts), tok_map),
        pl.BlockSpec((2 * TOP_K, ts), tok_map),
        pl.BlockSpec((n_exp, LANES), lambda bi, si: (0, 0)),
    ]
    return pl.pallas_call(
        functools.partial(kernel_fn, alpha=alpha),
        grid=(bsz, nsb),
        in_specs=in_specs,
        out_specs=out_specs,
        out_shape=out_shape,
        scratch_shapes=[pltpu.VMEM((halo + ts, d), _F32), pltpu.VMEM((n_exp, LANES), _F32)],
        compiler_params=pltpu.CompilerParams(
            dimension_semantics=("arbitrary", "arbitrary"), vmem_limit_bytes=VMEM_LIMIT_BYTES),
        name=kernel_fn.__name__.strip("_"),
    )(*operands)


def _sc_workers():
    info = plsc.get_sparse_core_info()
    return info.num_cores, info.num_subcores, info.num_lanes


def _sc_scatter_rows(x3, dest3, n_rows):
    n_cores, n_sub, n_lanes = _sc_workers()
    n_tok, rs, _ = x3.shape
    top_k = dest3.shape[0]
    chunk = SC_CHUNK_ROWS * (LANES * SUBLANES) // (rs * LANES)
    tok_per_w = n_tok // (n_cores * n_sub)
    rows_per_w = tok_per_w // SC_INDEX_ROW
    chunks_per_row = SC_INDEX_ROW // chunk
    assert rows_per_w * SC_INDEX_ROW * n_cores * n_sub == n_tok and chunks_per_row % 2 == 0
    mesh = plsc.VectorSubcoreMesh(core_axis_name="core", subcore_axis_name="subcore")

    @pl.kernel(out_type=jax.ShapeDtypeStruct((n_rows, rs, LANES), x3.dtype), mesh=mesh,
               scratch_types=[pltpu.VMEM((top_k, rows_per_w, SC_INDEX_ROW), jnp.int32),
                              pltpu.VMEM((2, chunk, rs, LANES), x3.dtype),
                              pltpu.SemaphoreType.DMA((2,)), pltpu.SemaphoreType.DMA((2,))])
    def scatter_kernel(x_hbm, d_hbm, o_hbm, idx_v, buf, rsem, ssem):
        wid = lax.axis_index("subcore") * n_cores + lax.axis_index("core")
        for k in range(top_k):
            pltpu.sync_copy(d_hbm.at[k, pl.ds(wid * rows_per_w, rows_per_w)], idx_v.at[k])
        base = wid * tok_per_w

        def read(j, c, slot):
            return pltpu.make_async_copy(x_hbm.at[pl.ds(base + j * SC_INDEX_ROW + c * chunk, chunk)],
                                         buf.at[slot], rsem.at[slot])

        def scatters(j, c, slot):
            copies = []
            for k in range(top_k):
                for h in range(chunk // n_lanes):
                    rows = idx_v[k, j, pl.ds(c * chunk + h * n_lanes, n_lanes)]
                    copies.append(pltpu.make_async_copy(buf.at[slot, pl.ds(h * n_lanes, n_lanes)],
                                                        o_hbm.at[rows], ssem.at[slot]))
            return copies

        def wait_scatters(j, c, slot):
            for cp in scatters(j, c, slot):
                cp.wait()

        read(0, 0, 0).start()

        def per_index_row(j, carry):
            for c in range(chunks_per_row):
                slot = c % 2
                if c == 0:
                    @pl.when(j > 0)
                    def _():
                        wait_scatters(j - 1, chunks_per_row - 1, 1 - slot)
                    read(j, c + 1, 1 - slot).start()
                elif c < chunks_per_row - 1:
                    wait_scatters(j, c - 1, 1 - slot)
                    read(j, c + 1, 1 - slot).start()
                else:
                    @pl.when(j + 1 < rows_per_w)
                    def _():
                        wait_scatters(j, c - 1, 1 - slot)
                        read(j + 1, 0, 1 - slot).start()
                read(j, c, slot).wait()
                for cp in scatters(j, c, slot):
                    cp.start()
            return carry

        lax.fori_loop(0, rows_per_w, per_index_row, 0)
        wait_scatters(rows_per_w - 1, chunks_per_row - 2, 0)
        wait_scatters(rows_per_w - 1, chunks_per_row - 1, 1)

    return scatter_kernel(x3, dest3)


def _sc_gather_rows(table3, idx2):
    n_cores, n_sub, _ = _sc_workers()
    n_idx_rows = idx2.shape[0]
    rs = table3.shape[1]
    chunk = SC_CHUNK_ROWS * (LANES * SUBLANES) // (rs * LANES)
    rows_per_w = n_idx_rows // (n_cores * n_sub)
    chunks_per_row = SC_INDEX_ROW // chunk
    assert rows_per_w * n_cores * n_sub == n_idx_rows and chunks_per_row % 2 == 0
    mesh = plsc.VectorSubcoreMesh(core_axis_name="core", subcore_axis_name="subcore")

    @pl.kernel(out_type=jax.ShapeDtypeStruct((n_idx_rows * SC_INDEX_ROW, rs, LANES), table3.dtype), mesh=mesh,
               scratch_types=[pltpu.VMEM((rows_per_w, SC_INDEX_ROW), jnp.int32),
                              pltpu.VMEM((2, chunk, rs, LANES), table3.dtype),
                              pltpu.SemaphoreType.DMA((2,)), pltpu.SemaphoreType.DMA((2,))])
    def gather_kernel(t_hbm, i_hbm, o_hbm, idx_v, buf, gsem, wsem):
        wid = lax.axis_index("subcore") * n_cores + lax.axis_index("core")
        pltpu.sync_copy(i_hbm.at[pl.ds(wid * rows_per_w, rows_per_w)], idx_v)
        base = wid * rows_per_w * SC_INDEX_ROW

        def gather(j, c, slot):
            return pltpu.make_async_copy(t_hbm.at[idx_v.at[j, pl.ds(c * chunk, chunk)]], buf.at[slot], gsem.at[slot])

        def write(j, c, slot):
            return pltpu.make_async_copy(buf.at[slot], o_hbm.at[pl.ds(base + j * SC_INDEX_ROW + c * chunk, chunk)],
                                         wsem.at[slot])

        gather(0, 0, 0).start()

        def per_index_row(j, carry):
            for c in range(chunks_per_row):
                slot = c % 2
                if c == 0:
                    @pl.when(j > 0)
                    def _():
                        write(j - 1, chunks_per_row - 1, 1 - slot).wait()
                    gather(j, c + 1, 1 - slot).start()
                elif c < chunks_per_row - 1:
                    write(j, c - 1, 1 - slot).wait()
                    gather(j, c + 1, 1 - slot).start()
                else:
                    @pl.when(j + 1 < rows_per_w)
                    def _():
                        write(j, c - 1, 1 - slot).wait()
                        gather(j + 1, 0, 1 - slot).start()
                gather(j, c, slot).wait()
                write(j, c, slot).start()
            return carry

        lax.fori_loop(0, rows_per_w, per_index_row, 0)
        write(rows_per_w - 1, chunks_per_row - 2, 0).wait()
        write(rows_per_w - 1, chunks_per_row - 1, 1).wait()

    return gather_kernel(table3, idx2)


def _pack_bf16_pairs(v):
    half = v.shape[1] // 2
    lo = lax.bitcast_convert_type(v[:, :half].astype(_BF16).astype(_F32), jnp.uint32)
    hi = lax.bitcast_convert_type(v[:, half:].astype(_BF16).astype(_F32), jnp.uint32)
    return (lo >> 16) | (hi & jnp.uint32(0xFFFF0000))


def _unpack_bf16_pairs(w):
    return (lax.bitcast_convert_type(w << 16, _F32),
            lax.bitcast_convert_type(w & jnp.uint32(0xFFFF0000), _F32))


def _expert_kernel(be_ref, nv_ref, nu_ref, xs_ref, wgu_ref, bgu_ref, wdn_ref, bdn_ref, ys_ref, wgu_s, wdn_s, *, rs):
    b = pl.program_id(0)
    rows = ys_ref.shape[0] // (rs // 2)
    n_chunks = wgu_ref.shape[2] // GATE_UP_CHUNK
    half = GATE_UP_CHUNK // 2

    @pl.when(b < nu_ref[0])
    def _():
        @pl.when((b == 0) | (be_ref[b] != be_ref[jnp.maximum(b - 1, 0)]))
        def _():
            r = lax.broadcasted_iota(jnp.int32, (GATE_UP_CHUNK, GATE_UP_CHUNK), 0)
            c = lax.broadcasted_iota(jnp.int32, (GATE_UP_CHUNK, GATE_UP_CHUNK), 1)
            perm = (r == jnp.where(c < half, 2 * c, 2 * (c - half) + 1)).astype(_BF16)
            for ch in range(n_chunks):
                cols = slice(ch * GATE_UP_CHUNK, (ch + 1) * GATE_UP_CHUNK)
                w = wgu_ref[0, :, cols].astype(_BF16)
                wgu_s[:, cols] = jnp.dot(w, perm, preferred_element_type=_F32).astype(_BF16)
            wdn_s[...] = wdn_ref[0].astype(_BF16)

        defined = lax.broadcasted_iota(jnp.int32, (rows, 1), 0) < nv_ref[b]
        words = [jnp.where(defined, w, jnp.uint32(0)) for w in _load_rows(xs_ref, rows, rs // 2)]
        halves = [_unpack_bf16_pairs(w) for w in words]
        x = jnp.concatenate([lo for lo, _ in halves] + [hi for _, hi in halves], axis=1).astype(_BF16)
        h = jnp.dot(x, wgu_s[...], preferred_element_type=_F32) + bgu_ref[0]
        acts = []
        for ch in range(n_chunks):
            g = jnp.minimum(h[:, ch * GATE_UP_CHUNK:ch * GATE_UP_CHUNK + half], SWIGLU_LIMIT)
            up = jnp.clip(h[:, ch * GATE_UP_CHUNK + half:(ch + 1) * GATE_UP_CHUNK], -SWIGLU_LIMIT, SWIGLU_LIMIT)
            acts.append(((up + 1.0) * (g * jax.nn.sigmoid(SWIGLU_ALPHA * g))).astype(_BF16))
        y = jnp.dot(jnp.concatenate(acts, axis=1), wdn_s[...], preferred_element_type=_F32) + bdn_ref[0]
        _store_rows(ys_ref, _pack_bf16_pairs(y))


def _expert_call(block_expert, block_valid, n_used, xs, layer, w_gu_all, b_gu_grouped, w_dn_all, b_dn, rs):
    _, n_exp, d, f2 = w_gu_all.shape
    f = f2 // 2
    br = EXPERT_BLOCK_ROWS
    n_blocks = xs.shape[0] // (br * rs // 2)
    assert f2 % GATE_UP_CHUNK == 0 and rs % 2 == 0

    def row_map(b, be, nv, nu):
        return (jnp.minimum(b, nu[0] - 1), 0)

    def w_map(b, be, nv, nu):
        return (be[b], 0, 0)

    def stack_map(b, be, nv, nu):
        return (layer, be[b], 0, 0)

    return pl.pallas_call(
        functools.partial(_expert_kernel, rs=rs),
        grid_spec=pltpu.PrefetchScalarGridSpec(
            num_scalar_prefetch=3,
            grid=(n_blocks,),
            in_specs=[
                pl.BlockSpec((br * rs // 2, LANES), row_map),
                pl.BlockSpec((None, 1, d, f2), stack_map),
                pl.BlockSpec((1, 1, f2), w_map),
                pl.BlockSpec((None, 1, f, d), stack_map),
                pl.BlockSpec((1, 1, d), w_map),
            ],
            out_specs=pl.BlockSpec((br * rs // 2, LANES), row_map),
            scratch_shapes=[pltpu.VMEM((d, f2), _BF16), pltpu.VMEM((f, d), _BF16)],
        ),
        out_shape=jax.ShapeDtypeStruct(xs.shape, jnp.uint32),
        compiler_params=pltpu.CompilerParams(
            dimension_semantics=("arbitrary",), vmem_limit_bytes=VMEM_LIMIT_BYTES),
        name="experts",
    )(block_expert, block_valid, n_used, xs, w_gu_all, b_gu_grouped, w_dn_all, b_dn)


def _combine_kernel(gates_ref, x1r_ref, g_ref, b_ref, *rest, rs, alpha):
    yk_refs, out_ref = rest[:TOP_K], rest[-1]
    tb = out_ref.shape[0]
    rp = rs // 2
    gates = gates_ref[...].T
    lo_pieces, hi_pieces = [], []
    for j in range(rp):
        lo = alpha * x1r_ref[pl.ds(j, tb, stride=rs), :]
        hi = alpha * x1r_ref[pl.ds(rp + j, tb, stride=rs), :]
        for k in range(TOP_K):
            y_lo, y_hi = _unpack_bf16_pairs(yk_refs[k][pl.ds(j, tb, stride=rp), :])
            lo = lo + gates[:, k:k + 1] * y_lo
            hi = hi + gates[:, k:k + 1] * y_hi
        lo_pieces.append(lo)
        hi_pieces.append(hi)
    out_ref[...] = _layer_norm(jnp.concatenate(lo_pieces + hi_pieces, axis=1), g_ref[...], b_ref[...])


def _combine_call(gates_t, x1r, ln_g, ln_b, yk, alpha, rs, out_tokens, token_lo, out_buf):
    n_tok = gates_t.shape[1]
    d = rs * LANES
    rp = rs // 2
    tb = min(COMBINE_BLOCK, n_tok)
    steps = n_tok // tb
    block_lo = token_lo // tb
    yk_specs = [pl.BlockSpec((tb * rp, LANES), lambda i, _k=k: (_k * steps + i, 0)) for k in range(TOP_K)]
    operands = [gates_t, x1r, ln_g.reshape(1, d), ln_b.reshape(1, d)] + [yk] * TOP_K
    in_specs = [
        pl.BlockSpec((2 * TOP_K, tb), lambda i: (0, i)),
        pl.BlockSpec((tb * rs, LANES), lambda i: (i, 0)),
        pl.BlockSpec((1, d), lambda i: (0, 0)),
        pl.BlockSpec((1, d), lambda i: (0, 0)),
    ] + yk_specs
    aliases = {}
    if out_buf is not None:
        aliases = {len(operands): 0}
        operands.append(out_buf)
        in_specs.append(pl.BlockSpec(memory_space=pl.ANY))
    return pl.pallas_call(
        functools.partial(_combine_kernel, rs=rs, alpha=alpha),
        grid=(steps,),
        in_specs=in_specs,
        out_specs=pl.BlockSpec((tb, d), lambda i: (block_lo + i, 0)),
        out_shape=jax.ShapeDtypeStruct((out_tokens, d), _F32),
        input_output_aliases=aliases,
        compiler_params=pltpu.CompilerParams(
            dimension_semantics=("arbitrary",), vmem_limit_bytes=VMEM_LIMIT_BYTES),
        name="combine",
    )(*operands)


def _routing_tables(meta_i, counts_f, n_blocks):
    n_exp = counts_f.shape[0]
    br = EXPERT_BLOCK_ROWS
    counts = counts_f[:, 0].astype(jnp.int32)
    padded = ((counts + br - 1) // br) * br
    pend = jnp.cumsum(padded)
    pstart = pend - padded
    eids = jnp.arange(n_exp, dtype=jnp.int32)
    idx, rank = meta_i[:TOP_K], meta_i[TOP_K:]
    dest = jnp.sum(jnp.where(idx[..., None] == eids, pstart, 0), axis=-1) + rank
    n_used = (pend[-1] // br).astype(jnp.int32)
    blk = jnp.minimum(jnp.arange(n_blocks, dtype=jnp.int32), n_used - 1)
    block_expert = jnp.minimum(jnp.sum((pend[None, :] <= (blk * br)[:, None]).astype(jnp.int32), axis=1),
                               n_exp - 1)
    group_end = jnp.sum(jnp.where(block_expert[:, None] == eids, pstart + counts, 0), axis=-1)
    block_valid = jnp.clip(group_end - blk * br, 0, br).astype(jnp.int32)
    return dest, block_expert, block_valid, n_used.reshape(1)


def _moe_layer(x1r, x1p, meta_i, gates_t, counts_f, layer, w_gu_all, b_gu, w_dn_all, b_dn, ln_g, ln_b, alpha, rs,
               out_tokens, token_lo, out_buf):
    _, n_exp, d, f2 = w_gu_all.shape
    n_tok = gates_t.shape[1]
    br = EXPERT_BLOCK_ROWS
    rp = rs // 2
    n_blocks = -(-(n_tok * TOP_K) // br) + n_exp
    dest, block_expert, block_valid, n_used = _routing_tables(meta_i, counts_f, n_blocks)
    n_rows = n_blocks * br
    xs = _sc_scatter_rows(x1p.reshape(n_tok, rp, LANES), dest.reshape(TOP_K, n_tok // SC_INDEX_ROW, SC_INDEX_ROW),
                          n_rows)
    half = GATE_UP_CHUNK // 2
    b_gu_grouped = b_gu.reshape(n_exp, f2 // GATE_UP_CHUNK, half, 2).transpose(0, 1, 3, 2).reshape(n_exp, 1, f2)
    ys = _expert_call(block_expert, block_valid, n_used, xs.reshape(n_rows * rp, LANES), layer, w_gu_all,
                      b_gu_grouped, w_dn_all, b_dn.reshape(n_exp, 1, d), rs)
    yk = _sc_gather_rows(ys.reshape(n_rows, rp, LANES),
                         dest.reshape(TOP_K * n_tok // SC_INDEX_ROW, SC_INDEX_ROW))
    return _combine_call(gates_t, x1r, ln_g, ln_b, yk.reshape(TOP_K * n_tok * rp, LANES), alpha, rs,
                         out_tokens, token_lo, out_buf)


def kernel(x, pool_w, pool_scale, sc_w_in, sc_conv_w, sc_w_out, cf_w_in, cf_b_in, cf_dw_w, cf_dw_b,
           cf_ln_g, cf_ln_b, cf_w_out, cf_b_out, mix_ln_g, mix_ln_b, router_w, router_b,
           moe_w_gu, moe_b_gu, moe_w_dn, moe_b_dn, ffn_ln_g, ffn_ln_b):
    bsz, seq, d = x.shape
    depth = mix_ln_g.shape[0]
    alpha = (2.0 * depth) ** 0.25
    rs = d // LANES
    n_chains = BATCH_CHAINS if bsz % BATCH_CHAINS == 0 else 1
    cb = bsz // n_chains
    chains = [(x, c * cb) for c in range(n_chains)]
    ia = ib = ic = 0
    for layer in range(depth):
        kind = layer % 3
        route = (mix_ln_g[layer], mix_ln_b[layer], router_w[layer], router_b[layer])
        if kind == 0:
            mixer, halo = _pool_kernel, POOL_HALO
            weights = [pool_w[ia].astype(_BF16), pool_scale[ia].reshape(1, d)]
            ia += 1
        elif kind == 1:
            mixer, halo = _short_conv_kernel, SHORT_CONV_HALO
            weights = [sc_w_in[ib].astype(_BF16), sc_conv_w[ib], sc_w_out[ib].astype(_BF16)]
            ib += 1
        else:
            mixer, halo = _conformer_kernel, CONFORMER_HALO
            weights = [cf_w_in[ic].astype(_BF16), cf_b_in[ic].reshape(1, 2 * d), cf_dw_w[ic],
                       cf_dw_b[ic].reshape(1, d), cf_ln_g[ic].reshape(1, d), cf_ln_b[ic].reshape(1, d),
                       cf_w_out[ic].astype(_BF16), cf_b_out[ic].reshape(1, d)]
            ic += 1
        routed = [_mixer_call(mixer, xc, lo, cb, weights, *route, halo, alpha) for xc, lo in chains]
        last = layer == depth - 1
        chains, out_buf = [], None
        for c, (x1r, x1p, meta_i, gates_t, counts_f) in enumerate(routed):
            out_tokens, token_lo = (bsz * seq, c * cb * seq) if last else (cb * seq, 0)
            xc = _moe_layer(x1r, x1p, meta_i, gates_t, counts_f, layer, moe_w_gu, moe_b_gu[layer],
                            moe_w_dn, moe_b_dn[layer], ffn_ln_g[layer], ffn_ln_b[layer], alpha, rs,
                            out_tokens, token_lo, out_buf)
            if last:
                out_buf = xc
            else:
                chains.append((xc.reshape(cb, seq, d), 0))
    return out_buf.reshape(bsz, seq, d)
```

```python
import functools

import jax
import jax.numpy as jnp
from jax import lax
from jax.experimental import pallas as pl
from jax.experimental.pallas import tpu as pltpu
from jax.experimental.pallas import tpu_sc as plsc

LANES = 128
SUBLANES = 8
TOP_K = 4
POOL_WINDOWS = (2, 4, 8, 16)
POOL_HALO = 16
SHORT_CONV_HALO = 8
CONFORMER_HALO = 32
SWIGLU_LIMIT = 7.0
SWIGLU_ALPHA = 1.702
LN_EPS = 1e-5
TOKEN_BLOCK = 512
EXPERT_BLOCK_ROWS = 512
COMBINE_BLOCK = 256
BATCH_CHAINS = 2
SC_INDEX_ROW = 128
SC_CHUNK_ROWS = 32
GATE_UP_CHUNK = 2 * LANES
VMEM_LIMIT_BYTES = 56 * 1024 * 1024

_F32 = jnp.float32
_BF16 = jnp.bfloat16


def _layer_norm(z, g, b):
    mu = jnp.mean(z, axis=-1, keepdims=True)
    zc = z - mu
    var = jnp.mean(zc * zc, axis=-1, keepdims=True)
    return zc * lax.rsqrt(var + LN_EPS) * g + b


def _store_rows(row_ref, val):
    rows, d = val.shape
    rs = d // LANES
    for j in range(rs):
        row_ref[pl.ds(j, rows, stride=rs), :] = val[:, j * LANES:(j + 1) * LANES]


def _load_rows(row_ref, rows, rs):
    return [row_ref[pl.ds(j, rows, stride=rs), :] for j in range(rs)]


def _post_norm_and_route(z, g_ref, b_ref, rwt_ref, rb_ref, first,
                         x1r_ref, x1p_ref, mi_ref, mg_ref, cnt_ref, carry_ref):
    n_tok = z.shape[0]
    n_exp = rwt_ref.shape[0]

    @pl.when(first)
    def _():
        carry_ref[...] = jnp.zeros_like(carry_ref)

    x1 = _layer_norm(z, g_ref[...], b_ref[...])
    _store_rows(x1r_ref, x1)
    _store_rows(x1p_ref, _pack_bf16_pairs(x1))

    logits = lax.dot_general(rwt_ref[...], x1, (((1,), (1,)), ((), ())),
                             precision=lax.Precision.HIGHEST,
                             preferred_element_type=_F32) + rb_ref[...]
    eidx = lax.broadcasted_iota(jnp.int32, logits.shape, 0)
    work = logits
    chosen = jnp.zeros(logits.shape, jnp.bool_)
    vals, idxs = [], []
    for _ in range(TOP_K):
        m = jnp.max(work, axis=0, keepdims=True)
        sel = jnp.min(jnp.where(work == m, eidx, n_exp), axis=0, keepdims=True)
        hit = eidx == sel
        vals.append(m)
        idxs.append(sel)
        chosen = jnp.logical_or(chosen, hit)
        work = jnp.where(hit, -jnp.inf, work)
    exps = [jnp.exp(v - vals[0]) for v in vals]
    denom = exps[0] + exps[1] + exps[2] + exps[3]
    gate_rows = [e / denom for e in exps] + [jnp.zeros_like(denom)] * (mg_ref.shape[0] - TOP_K)
    mg_ref[...] = jnp.concatenate(gate_rows, axis=0)

    onehot = chosen.astype(_BF16)
    r = lax.broadcasted_iota(jnp.int32, (n_tok, n_tok), 0)
    c = lax.broadcasted_iota(jnp.int32, (n_tok, n_tok), 1)
    before = (r < c).astype(_BF16)
    cum = jnp.dot(onehot, before, preferred_element_type=_F32) + carry_ref[:, 0:1]
    ranks = [jnp.sum(jnp.where(eidx == s, cum, 0.0), axis=0, keepdims=True) for s in idxs]
    mi_ref[...] = jnp.concatenate(idxs + [rk.astype(jnp.int32) for rk in ranks], axis=0)
    carry_ref[...] = carry_ref[...] + jnp.sum(chosen.astype(_F32), axis=1, keepdims=True)
    cnt_ref[...] = carry_ref[...]


def _pool_kernel(x_ref, pw_ref, ps_ref, g_ref, b_ref, rwt_ref, rb_ref,
                 x1r_ref, x1p_ref, mi_ref, mg_ref, cnt_ref, hist_ref, carry_ref, *, alpha):
    bi, si = pl.program_id(0), pl.program_id(1)
    ts, d = x_ref.shape[1], x_ref.shape[2]
    dg = d // len(POOL_WINDOWS)

    @pl.when(si == 0)
    def _():
        hist_ref[0:POOL_HALO, :] = jnp.zeros((POOL_HALO, d), _F32)

    x = x_ref[0]
    hist_ref[POOL_HALO:POOL_HALO + ts, :] = x
    pos = si * ts + lax.broadcasted_iota(jnp.int32, (ts, 1), 0)
    pieces = []
    for gi, win in enumerate(POOL_WINDOWS):
        c0 = gi * dg
        xg = x[:, c0:c0 + dg]
        assert win & (win - 1) == 0 and win <= POOL_HALO
        ext = hist_ref[:, c0:c0 + dg]
        span = 1
        while span < win:
            ext = ext + pltpu.roll(ext, span, 0)
            span *= 2
        acc = ext[POOL_HALO:, :]
        inv_count = 1.0 / jnp.minimum(pos + 1, win).astype(_F32)
        diff = acc * inv_count - xg
        hg = jnp.dot(diff.astype(_BF16), pw_ref[gi], preferred_element_type=_F32)
        pieces.append(alpha * xg + hg * ps_ref[:, c0:c0 + dg])
    hist_ref[0:POOL_HALO, :] = x[ts - POOL_HALO:, :]
    z = jnp.concatenate(pieces, axis=1)
    _post_norm_and_route(z, g_ref, b_ref, rwt_ref, rb_ref, (bi == 0) & (si == 0),
                         x1r_ref, x1p_ref, mi_ref, mg_ref, cnt_ref, carry_ref)


def _short_conv_kernel(x_ref, win_ref, cw_ref, wout_ref, g_ref, b_ref, rwt_ref, rb_ref,
                       x1r_ref, x1p_ref, mi_ref, mg_ref, cnt_ref, hist_ref, carry_ref, *, alpha):
    bi, si = pl.program_id(0), pl.program_id(1)
    ts, d = x_ref.shape[1], x_ref.shape[2]
    halo = SHORT_CONV_HALO

    @pl.when(si == 0)
    def _():
        hist_ref[0:halo, :] = jnp.zeros((halo, d), _F32)

    x = x_ref[0]
    xb = x.astype(_BF16)
    gate_b = jnp.dot(xb, win_ref[:, 0:d], preferred_element_type=_F32)
    gate_c = jnp.dot(xb, win_ref[:, d:2 * d], preferred_element_type=_F32)
    h = jnp.dot(xb, win_ref[:, 2 * d:3 * d], preferred_element_type=_F32)
    v = gate_c * h
    hist_ref[halo:halo + ts, :] = v
    width = cw_ref.shape[0]
    u = cw_ref[width - 1:width, :] * v
    for k in range(width - 1):
        shift = width - 1 - k
        u = u + cw_ref[k:k + 1, :] * hist_ref[halo - shift:halo - shift + ts, :]
    hist_ref[0:halo, :] = v[ts - halo:, :]
    y = jnp.dot((gate_b * u).astype(_BF16), wout_ref[...], preferred_element_type=_F32)
    _post_norm_and_route(alpha * x + y, g_ref, b_ref, rwt_ref, rb_ref, (bi == 0) & (si == 0),
                         x1r_ref, x1p_ref, mi_ref, mg_ref, cnt_ref, carry_ref)


def _conformer_kernel(x_ref, win_ref, bin_ref, dww_ref, dwb_ref, lng_ref, lnb_ref, wout_ref, bout_ref,
                      g_ref, b_ref, rwt_ref, rb_ref,
                      x1r_ref, x1p_ref, mi_ref, mg_ref, cnt_ref, hist_ref, carry_ref, *, alpha):
    bi, si = pl.program_id(0), pl.program_id(1)
    ts, d = x_ref.shape[1], x_ref.shape[2]
    halo = CONFORMER_HALO

    @pl.when(si == 0)
    def _():
        hist_ref[0:halo, :] = jnp.zeros((halo, d), _F32)

    x = x_ref[0]
    xb = x.astype(_BF16)
    a = jnp.dot(xb, win_ref[:, 0:d], preferred_element_type=_F32) + bin_ref[:, 0:d]
    gate = jnp.dot(xb, win_ref[:, d:2 * d], preferred_element_type=_F32) + bin_ref[:, d:2 * d]
    u = a * jax.nn.sigmoid(gate)
    hist_ref[halo:halo + ts, :] = u
    width = dww_ref.shape[0]
    hist = hist_ref[...]
    acc = dwb_ref[...]
    for r in range(SUBLANES):
        rolled = hist if r == 0 else pltpu.roll(hist, r, 0)
        for q in range(halo // SUBLANES):
            shift = SUBLANES * q + r
            if shift < width:
                k = width - 1 - shift
                start = halo - SUBLANES * q
                acc = acc + dww_ref[k:k + 1, :] * rolled[start:start + ts, :]
    hist_ref[0:halo, :] = u[ts - halo:, :]
    un = _layer_norm(acc, lng_ref[...], lnb_ref[...])
    un = un * jax.nn.sigmoid(un)
    y = jnp.dot(un.astype(_BF16), wout_ref[...], preferred_element_type=_F32) + bout_ref[...]
    _post_norm_and_route(alpha * x + y, g_ref, b_ref, rwt_ref, rb_ref, (bi == 0) & (si == 0),
                         x1r_ref, x1p_ref, mi_ref, mg_ref, cnt_ref, carry_ref)


def _mixer_call(kernel_fn, x, batch_lo, bsz, weights, ln_g, ln_b, router_w, router_b, halo, alpha):
    _, seq, d = x.shape
    n_exp = router_w.shape[1]
    ts = min(TOKEN_BLOCK, seq)
    rs = d // LANES
    n_tok = bsz * seq
    nsb = seq // ts

    def full(a):
        nd = a.ndim
        return pl.BlockSpec(a.shape, lambda bi, si, _nd=nd: (0,) * _nd)

    small = [ln_g.reshape(1, d), ln_b.reshape(1, d), router_w.T, router_b.reshape(n_exp, 1)]
    operands = [x] + list(weights) + small
    in_specs = [pl.BlockSpec((1, ts, d), lambda bi, si: (batch_lo + bi, si, 0))] + [full(a) for a in operands[1:]]
    tok_map = lambda bi, si: (0, bi * nsb + si)
    out_shape = [
        jax.ShapeDtypeStruct((n_tok * rs, LANES), _F32),
        jax.ShapeDtypeStruct((n_tok * rs // 2, LANES), jnp.uint32),
        jax.ShapeDtypeStruct((2 * TOP_K, n_tok), jnp.int32),
        jax.ShapeDtypeStruct((2 * TOP_K, n_tok), _F32),
        jax.ShapeDtypeStruct((n_exp, LANES), _F32),
    ]
    out_specs = [
        pl.BlockSpec((ts * rs, LANES), lambda bi, si: (bi * nsb + si, 0)),
        pl.BlockSpec((ts * rs // 2, LANES), lambda bi, si: (bi * nsb + si, 0)),
        pl.BlockSpec((2 * TOP_K, ts), tok_map),
        pl.BlockSpec((2 * TOP_K, ts), tok_map),
        pl.BlockSpec((n_exp, LANES), lambda bi, si: (0, 0)),
    ]
    return pl.pallas_call(
        functools.partial(kernel_fn, alpha=alpha),
        grid=(bsz, nsb),
        in_specs=in_specs,
        out_specs=out_specs,
        out_shape=out_shape,
        scratch_shapes=[pltpu.VMEM((halo + ts, d), _F32), pltpu.VMEM((n_exp, LANES), _F32)],
        compiler_params=pltpu.CompilerParams(
            dimension_semantics=("arbitrary", "arbitrary"), vmem_limit_bytes=VMEM_LIMIT_BYTES),
        name=kernel_fn.__name__.strip("_"),
    )(*operands)


def _sc_workers():
    info = plsc.get_sparse_core_info()
    return info.num_cores, info.num_subcores, info.num_lanes


def _sc_scatter_rows(x3, dest3, n_rows):
    n_cores, n_sub, n_lanes = _sc_workers()
    n_tok, rs, _ = x3.shape
    top_k = dest3.shape[0]
    chunk = SC_CHUNK_ROWS * (LANES * SUBLANES) // (rs * LANES)
    tok_per_w = n_tok // (n_cores * n_sub)
    rows_per_w = tok_per_w // SC_INDEX_ROW
    chunks_per_row = SC_INDEX_ROW // chunk
    assert rows_per_w * SC_INDEX_ROW * n_cores * n_sub == n_tok and chunks_per_row % 2 == 0
    mesh = plsc.VectorSubcoreMesh(core_axis_name="core", subcore_axis_name="subcore")

    @pl.kernel(out_type=jax.ShapeDtypeStruct((n_rows, rs, LANES), x3.dtype), mesh=mesh,
               scratch_types=[pltpu.VMEM((top_k, rows_per_w, SC_INDEX_ROW), jnp.int32),
                              pltpu.VMEM((2, chunk, rs, LANES), x3.dtype),
                              pltpu.SemaphoreType.DMA((2,)), pltpu.SemaphoreType.DMA((2,))])
    def scatter_kernel(x_hbm, d_hbm, o_hbm, idx_v, buf, rsem, ssem):
        wid = lax.axis_index("subcore") * n_cores + lax.axis_index("core")
        for k in range(top_k):
            pltpu.sync_copy(d_hbm.at[k, pl.ds(wid * rows_per_w, rows_per_w)], idx_v.at[k])
        base = wid * tok_per_w

        def read(j, c, slot):
            return pltpu.make_async_copy(x_hbm.at[pl.ds(base + j * SC_INDEX_ROW + c * chunk, chunk)],
                                         buf.at[slot], rsem.at[slot])

        def scatters(j, c, slot):
            copies = []
            for k in range(top_k):
                for h in range(chunk // n_lanes):
                    rows = idx_v[k, j, pl.ds(c * chunk + h * n_lanes, n_lanes)]
                    copies.append(pltpu.make_async_copy(buf.at[slot, pl.ds(h * n_lanes, n_lanes)],
                                                        o_hbm.at[rows], ssem.at[slot]))
            return copies

        def wait_scatters(j, c, slot):
            for cp in scatters(j, c, slot):
                cp.wait()

        read(0, 0, 0).start()

        def per_index_row(j, carry):
            for c in range(chunks_per_row):
                slot = c % 2
                if c == 0:
                    @pl.when(j > 0)
                    def _():
                        wait_scatters(j - 1, chunks_per_row - 1, 1 - slot)
                    read(j, c + 1, 1 - slot).start()
                elif c < chunks_per_row - 1:
                    wait_scatters(j, c - 1, 1 - slot)
                    read(j, c + 1, 1 - slot).start()
                else:
                    @pl.when(j + 1 < rows_per_w)
                    def _():
                        wait_scatters(j, c - 1, 1 - slot)
                        read(j + 1, 0, 1 - slot).start()
                read(j, c, slot).wait()
                for cp in scatters(j, c, slot):
                    cp.start()
            return carry

        lax.fori_loop(0, rows_per_w, per_index_row, 0)
        wait_scatters(rows_per_w - 1, chunks_per_row - 2, 0)
        wait_scatters(rows_per_w - 1, chunks_per_row - 1, 1)

    return scatter_kernel(x3, dest3)


def _sc_gather_rows(table3, idx2):
    n_cores, n_sub, _ = _sc_workers()
    n_idx_rows = idx2.shape[0]
    rs = table3.shape[1]
    chunk = SC_CHUNK_ROWS * (LANES * SUBLANES) // (rs * LANES)
    rows_per_w = n_idx_rows // (n_cores * n_sub)
    chunks_per_row = SC_INDEX_ROW // chunk
    assert rows_per_w * n_cores * n_sub == n_idx_rows and chunks_per_row % 2 == 0
    mesh = plsc.VectorSubcoreMesh(core_axis_name="core", subcore_axis_name="subcore")

    @pl.kernel(out_type=jax.ShapeDtypeStruct((n_idx_rows * SC_INDEX_ROW, rs, LANES), table3.dtype), mesh=mesh,
               scratch_types=[pltpu.VMEM((rows_per_w, SC_INDEX_ROW), jnp.int32),
                              pltpu.VMEM((2, chunk, rs, LANES), table3.dtype),
                              pltpu.SemaphoreType.DMA((2,)), pltpu.SemaphoreType.DMA((2,))])
    def gather_kernel(t_hbm, i_hbm, o_hbm, idx_v, buf, gsem, wsem):
        wid = lax.axis_index("subcore") * n_cores + lax.axis_index("core")
        pltpu.sync_copy(i_hbm.at[pl.ds(wid * rows_per_w, rows_per_w)], idx_v)
        base = wid * rows_per_w * SC_INDEX_ROW

        def gather(j, c, slot):
            return pltpu.make_async_copy(t_hbm.at[idx_v.at[j, pl.ds(c * chunk, chunk)]], buf.at[slot], gsem.at[slot])

        def write(j, c, slot):
            return pltpu.make_async_copy(buf.at[slot], o_hbm.at[pl.ds(base + j * SC_INDEX_ROW + c * chunk, chunk)],
                                         wsem.at[slot])

        gather(0, 0, 0).start()

        def per_index_row(j, carry):
            for c in range(chunks_per_row):
                slot = c % 2
                if c == 0:
                    @pl.when(j > 0)
                    def _():
                        write(j - 1, chunks_per_row - 1, 1 - slot).wait()
                    gather(j, c + 1, 1 - slot).start()
                elif c < chunks_per_row - 1:
                    write(j, c - 1, 1 - slot).wait()
                    gather(j, c + 1, 1 - slot).start()
                else:
                    @pl.when(j + 1 < rows_per_w)
                    def _():
                        write(j, c - 1, 1 - slot).wait()
                        gather(j + 1, 0, 1 - slot).start()
                gather(j, c, slot).wait()
                write(j, c, slot).start()
            return carry

        lax.fori_loop(0, rows_per_w, per_index_row, 0)
        write(rows_per_w - 1, chunks_per_row - 2, 0).wait()
        write(rows_per_w - 1, chunks_per_row - 1, 1).wait()

    return gather_kernel(table3, idx2)


def _pack_bf16_pairs(v):
    half = v.shape[1] // 2
    lo = lax.bitcast_convert_type(v[:, :half].astype(_BF16).astype(_F32), jnp.uint32)
    hi = lax.bitcast_convert_type(v[:, half:].astype(_BF16).astype(_F32), jnp.uint32)
    return (lo >> 16) | (hi & jnp.uint32(0xFFFF0000))


def _unpack_bf16_pairs(w):
    return (lax.bitcast_convert_type(w << 16, _F32),
            lax.bitcast_convert_type(w & jnp.uint32(0xFFFF0000), _F32))


def _expert_kernel(be_ref, nx_ref, nv_ref, nu_ref, xs_ref, bgu_ref, bdn_ref, wgu_hbm, wdn_hbm, ys_ref,
                   wgu_f, wdn_f, wgu_s, wdn_s, sem, *, rs, layer):
    b = pl.program_id(0)
    rows = ys_ref.shape[0] // (rs // 2)
    n_chunks = wgu_f.shape[1] // GATE_UP_CHUNK
    half = GATE_UP_CHUNK // 2

    def fetch(e):
        return (pltpu.make_async_copy(wgu_hbm.at[layer, e], wgu_f, sem.at[0]),
                pltpu.make_async_copy(wdn_hbm.at[layer, e], wdn_f, sem.at[1]))

    @pl.when(b < nu_ref[0])
    def _():
        @pl.when((b == 0) | (be_ref[b] != be_ref[jnp.maximum(b - 1, 0)]))
        def _():
            @pl.when(b == 0)
            def _():
                for cp in fetch(be_ref[0]):
                    cp.start()

            for cp in fetch(be_ref[b]):
                cp.wait()
            r = lax.broadcasted_iota(jnp.int32, (GATE_UP_CHUNK, GATE_UP_CHUNK), 0)
            c = lax.broadcasted_iota(jnp.int32, (GATE_UP_CHUNK, GATE_UP_CHUNK), 1)
            perm = (r == jnp.where(c < half, 2 * c, 2 * (c - half) + 1)).astype(_BF16)
            for ch in range(n_chunks):
                cols = slice(ch * GATE_UP_CHUNK, (ch + 1) * GATE_UP_CHUNK)
                w = wgu_f[:, cols].astype(_BF16)
                wgu_s[:, cols] = jnp.dot(w, perm, preferred_element_type=_F32).astype(_BF16)
            wdn_s[...] = wdn_f[...].astype(_BF16)

            @pl.when(nx_ref[b] >= 0)
            def _():
                for cp in fetch(nx_ref[b]):
                    cp.start()

        defined = lax.broadcasted_iota(jnp.int32, (rows, 1), 0) < nv_ref[b]
        words = [jnp.where(defined, w, jnp.uint32(0)) for w in _load_rows(xs_ref, rows, rs // 2)]
        halves = [_unpack_bf16_pairs(w) for w in words]
        x = jnp.concatenate([lo for lo, _ in halves] + [hi for _, hi in halves], axis=1).astype(_BF16)
        h = jnp.dot(x, wgu_s[...], preferred_element_type=_F32) + bgu_ref[0]
        acts = []
        for ch in range(n_chunks):
            g = jnp.minimum(h[:, ch * GATE_UP_CHUNK:ch * GATE_UP_CHUNK + half], SWIGLU_LIMIT)
            up = jnp.clip(h[:, ch * GATE_UP_CHUNK + half:(ch + 1) * GATE_UP_CHUNK], -SWIGLU_LIMIT, SWIGLU_LIMIT)
            acts.append(((up + 1.0) * (g * jax.nn.sigmoid(SWIGLU_ALPHA * g))).astype(_BF16))
        y = jnp.dot(jnp.concatenate(acts, axis=1), wdn_s[...], preferred_element_type=_F32) + bdn_ref[0]
        _store_rows(ys_ref, _pack_bf16_pairs(y))


def _expert_call(block_expert, block_next, block_valid, n_used, xs, layer, w_gu_all, b_gu_grouped, w_dn_all,
                 b_dn, rs):
    _, n_exp, d, f2 = w_gu_all.shape
    f = f2 // 2
    br = EXPERT_BLOCK_ROWS
    n_blocks = xs.shape[0] // (br * rs // 2)
    assert f2 % GATE_UP_CHUNK == 0 and rs % 2 == 0

    def row_map(b, be, nx, nv, nu):
        return (jnp.minimum(b, nu[0] - 1), 0)

    def w_map(b, be, nx, nv, nu):
        return (be[b], 0, 0)

    return pl.pallas_call(
        functools.partial(_expert_kernel, rs=rs, layer=layer),
        grid_spec=pltpu.PrefetchScalarGridSpec(
            num_scalar_prefetch=4,
            grid=(n_blocks,),
            in_specs=[
                pl.BlockSpec((br * rs // 2, LANES), row_map),
                pl.BlockSpec((1, 1, f2), w_map),
                pl.BlockSpec((1, 1, d), w_map),
                pl.BlockSpec(memory_space=pl.ANY),
                pl.BlockSpec(memory_space=pl.ANY),
            ],
            out_specs=pl.BlockSpec((br * rs // 2, LANES), row_map),
            scratch_shapes=[pltpu.VMEM((d, f2), _F32), pltpu.VMEM((f, d), _F32),
                            pltpu.VMEM((d, f2), _BF16), pltpu.VMEM((f, d), _BF16),
                            pltpu.SemaphoreType.DMA((2,))],
        ),
        out_shape=jax.ShapeDtypeStruct(xs.shape, jnp.uint32),
        compiler_params=pltpu.CompilerParams(
            dimension_semantics=("arbitrary",), vmem_limit_bytes=VMEM_LIMIT_BYTES),
        name="experts",
    )(block_expert, block_next, block_valid, n_used, xs, b_gu_grouped, b_dn, w_gu_all, w_dn_all)


def _combine_kernel(gates_ref, x1r_ref, g_ref, b_ref, *rest, rs, alpha):
    yk_refs, out_ref = rest[:TOP_K], rest[-1]
    tb = out_ref.shape[0]
    rp = rs // 2
    gates = gates_ref[...].T
    lo_pieces, hi_pieces = [], []
    for j in range(rp):
        lo = alpha * x1r_ref[pl.ds(j, tb, stride=rs), :]
        hi = alpha * x1r_ref[pl.ds(rp + j, tb, stride=rs), :]
        for k in range(TOP_K):
            y_lo, y_hi = _unpack_bf16_pairs(yk_refs[k][pl.ds(j, tb, stride=rp), :])
            lo = lo + gates[:, k:k + 1] * y_lo
            hi = hi + gates[:, k:k + 1] * y_hi
        lo_pieces.append(lo)
        hi_pieces.append(hi)
    out_ref[...] = _layer_norm(jnp.concatenate(lo_pieces + hi_pieces, axis=1), g_ref[...], b_ref[...])


def _combine_call(gates_t, x1r, ln_g, ln_b, yk, alpha, rs, out_tokens, token_lo, out_buf):
    n_tok = gates_t.shape[1]
    d = rs * LANES
    rp = rs // 2
    tb = min(COMBINE_BLOCK, n_tok)
    steps = n_tok // tb
    block_lo = token_lo // tb
    yk_specs = [pl.BlockSpec((tb * rp, LANES), lambda i, _k=k: (_k * steps + i, 0)) for k in range(TOP_K)]
    operands = [gates_t, x1r, ln_g.reshape(1, d), ln_b.reshape(1, d)] + [yk] * TOP_K
    in_specs = [
        pl.BlockSpec((2 * TOP_K, tb), lambda i: (0, i)),
        pl.BlockSpec((tb * rs, LANES), lambda i: (i, 0)),
        pl.BlockSpec((1, d), lambda i: (0, 0)),
        pl.BlockSpec((1, d), lambda i: (0, 0)),
    ] + yk_specs
    aliases = {}
    if out_buf is not None:
        aliases = {len(operands): 0}
        operands.append(out_buf)
        in_specs.append(pl.BlockSpec(memory_space=pl.ANY))
    return pl.pallas_call(
        functools.partial(_combine_kernel, rs=rs, alpha=alpha),
        grid=(steps,),
        in_specs=in_specs,
        out_specs=pl.BlockSpec((tb, d), lambda i: (block_lo + i, 0)),
        out_shape=jax.ShapeDtypeStruct((out_tokens, d), _F32),
        input_output_aliases=aliases,
        compiler_params=pltpu.CompilerParams(
            dimension_semantics=("arbitrary",), vmem_limit_bytes=VMEM_LIMIT_BYTES),
        name="combine",
    )(*operands)


def _routing_tables(meta_i, counts_f, n_blocks):
    n_exp = counts_f.shape[0]
    br = EXPERT_BLOCK_ROWS
    counts = counts_f[:, 0].astype(jnp.int32)
    padded = ((counts + br - 1) // br) * br
    pend = jnp.cumsum(padded)
    pstart = pend - padded
    eids = jnp.arange(n_exp, dtype=jnp.int32)
    idx, rank = meta_i[:TOP_K], meta_i[TOP_K:]
    dest = jnp.sum(jnp.where(idx[..., None] == eids, pstart, 0), axis=-1) + rank
    n_used = (pend[-1] // br).astype(jnp.int32)
    blk = jnp.minimum(jnp.arange(n_blocks, dtype=jnp.int32), n_used - 1)
    block_expert = jnp.minimum(jnp.sum((pend[None, :] <= (blk * br)[:, None]).astype(jnp.int32), axis=1),
                               n_exp - 1)
    group_end = jnp.sum(jnp.where(block_expert[:, None] == eids, pstart + counts, 0), axis=-1)
    block_valid = jnp.clip(group_end - blk * br, 0, br).astype(jnp.int32)
    later = (eids[None, :] > block_expert[:, None]) & (counts[None, :] > 0)
    block_next = jnp.min(jnp.where(later, eids[None, :], n_exp), axis=1)
    block_next = jnp.where(block_next == n_exp, -1, block_next).astype(jnp.int32)
    return dest, block_expert, block_next, block_valid, n_used.reshape(1)


def _moe_layer(x1r, x1p, meta_i, gates_t, counts_f, layer, w_gu_all, b_gu, w_dn_all, b_dn, ln_g, ln_b, alpha, rs,
               out_tokens, token_lo, out_buf):
    _, n_exp, d, f2 = w_gu_all.shape
    n_tok = gates_t.shape[1]
    br = EXPERT_BLOCK_ROWS
    rp = rs // 2
    n_blocks = -(-(n_tok * TOP_K) // br) + n_exp
    dest, block_expert, block_next, block_valid, n_used = _routing_tables(meta_i, counts_f, n_blocks)
    n_rows = n_blocks * br
    xs = _sc_scatter_rows(x1p.reshape(n_tok, rp, LANES), dest.reshape(TOP_K, n_tok // SC_INDEX_ROW, SC_INDEX_ROW),
                          n_rows)
    half = GATE_UP_CHUNK // 2
    b_gu_grouped = b_gu.reshape(n_exp, f2 // GATE_UP_CHUNK, half, 2).transpose(0, 1, 3, 2).reshape(n_exp, 1, f2)
    ys = _expert_call(block_expert, block_next, block_valid, n_used, xs.reshape(n_rows * rp, LANES), layer, w_gu_all,
                      b_gu_grouped, w_dn_all, b_dn.reshape(n_exp, 1, d), rs)
    yk = _sc_gather_rows(ys.reshape(n_rows, rp, LANES),
                         dest.reshape(TOP_K * n_tok // SC_INDEX_ROW, SC_INDEX_ROW))
    return _combine_call(gates_t, x1r, ln_g, ln_b, yk.reshape(TOP_K * n_tok * rp, LANES), alpha, rs,
                         out_tokens, token_lo, out_buf)


def kernel(x, pool_w, pool_scale, sc_w_in, sc_conv_w, sc_w_out, cf_w_in, cf_b_in, cf_dw_w, cf_dw_b,
           cf_ln_g, cf_ln_b, cf_w_out, cf_b_out, mix_ln_g, mix_ln_b, router_w, router_b,
           moe_w_gu, moe_b_gu, moe_w_dn, moe_b_dn, ffn_ln_g, ffn_ln_b):
    bsz, seq, d = x.shape
    depth = mix_ln_g.shape[0]
    alpha = (2.0 * depth) ** 0.25
    rs = d // LANES
    n_chains = BATCH_CHAINS if bsz % BATCH_CHAINS == 0 else 1
    cb = bsz // n_chains
    chains = [(x, c * cb) for c in range(n_chains)]
    ia = ib = ic = 0
    for layer in range(depth):
        kind = layer % 3
        route = (mix_ln_g[layer], mix_ln_b[layer], router_w[layer], router_b[layer])
        if kind == 0:
            mixer, halo = _pool_kernel, POOL_HALO
            weights = [pool_w[ia].astype(_BF16), pool_scale[ia].reshape(1, d)]
            ia += 1
        elif kind == 1:
            mixer, halo = _short_conv_kernel, SHORT_CONV_HALO
            weights = [sc_w_in[ib].astype(_BF16), sc_conv_w[ib], sc_w_out[ib].astype(_BF16)]
            ib += 1
        else:
            mixer, halo = _conformer_kernel, CONFORMER_HALO
            weights = [cf_w_in[ic].astype(_BF16), cf_b_in[ic].reshape(1, 2 * d), cf_dw_w[ic],
                       cf_dw_b[ic].reshape(1, d), cf_ln_g[ic].reshape(1, d), cf_ln_b[ic].reshape(1, d),
                       cf_w_out[ic].astype(_BF16), cf_b_out[ic].reshape(1, d)]
            ic += 1
        routed = [_mixer_call(mixer, xc, lo, cb, weights, *route, halo, alpha) for xc, lo in chains]
        last = layer == depth - 1
        chains, out_buf = [], None
        for c, (x1r, x1p, meta_i, gates_t, counts_f) in enumerate(routed):
            out_tokens, token_lo = (bsz * seq, c * cb * seq) if last else (cb * seq, 0)
            xc = _moe_layer(x1r, x1p, meta_i, gates_t, counts_f, layer, moe_w_gu, moe_b_gu[layer],
                            moe_w_dn, moe_b_dn[layer], ffn_ln_g[layer], ffn_ln_b[layer], alpha, rs,
                            out_tokens, token_lo, out_buf)
            if last:
                out_buf = xc
            else:
                chains.append((xc.reshape(cb, seq, d), 0))
    return out_buf.reshape(bsz, seq, d)
```

```python
import functools

import jax
import jax.numpy as jnp
from jax import lax
from jax.experimental import pallas as pl
from jax.experimental.pallas import tpu as pltpu
from jax.experimental.pallas import tpu_sc as plsc

LANES = 128
SUBLANES = 8
TOP_K = 4
POOL_WINDOWS = (2, 4, 8, 16)
POOL_HALO = 16
SHORT_CONV_HALO = 8
CONFORMER_HALO = 32
SWIGLU_LIMIT = 7.0
SWIGLU_ALPHA = 1.702
LN_EPS = 1e-5
TOKEN_BLOCK = 512
EXPERT_BLOCK_ROWS = 2048
EXPERT_SUB_ROWS = 256
COMBINE_BLOCK = 512
BATCH_CHAINS = 2
SC_INDEX_ROW = 128
SC_CHUNK_ROWS = 32
GATE_UP_CHUNK = 2 * LANES
VMEM_LIMIT_BYTES = 56 * 1024 * 1024

_F32 = jnp.float32
_BF16 = jnp.bfloat16


def _layer_norm(z, g, b):
    mu = jnp.mean(z, axis=-1, keepdims=True)
    zc = z - mu
    var = jnp.mean(zc * zc, axis=-1, keepdims=True)
    return zc * lax.rsqrt(var + LN_EPS) * g + b


def _store_rows(row_ref, val, row0=0):
    rows, d = val.shape
    rs = d // LANES
    for j in range(rs):
        row_ref[pl.ds(row0 * rs + j, rows, stride=rs), :] = val[:, j * LANES:(j + 1) * LANES]


def _load_rows(row_ref, rows, rs, row0=0):
    return [row_ref[pl.ds(row0 * rs + j, rows, stride=rs), :] for j in range(rs)]


def _post_norm_and_route(z, g_ref, b_ref, rwt_ref, rb_ref, first,
                         x1r_ref, x1p_ref, mi_ref, mg_ref, cnt_ref, carry_ref):
    n_tok = z.shape[0]
    n_exp = rwt_ref.shape[0]

    @pl.when(first)
    def _():
        carry_ref[...] = jnp.zeros_like(carry_ref)

    x1 = _layer_norm(z, g_ref[...], b_ref[...])
    _store_rows(x1r_ref, x1)
    _store_rows(x1p_ref, _pack_bf16_pairs(x1))

    logits = lax.dot_general(rwt_ref[...], x1, (((1,), (1,)), ((), ())),
                             precision=lax.Precision.HIGHEST,
                             preferred_element_type=_F32) + rb_ref[...]
    eidx = lax.broadcasted_iota(jnp.int32, logits.shape, 0)
    work = logits
    chosen = jnp.zeros(logits.shape, jnp.bool_)
    vals, idxs = [], []
    for _ in range(TOP_K):
        m = jnp.max(work, axis=0, keepdims=True)
        sel = jnp.min(jnp.where(work == m, eidx, n_exp), axis=0, keepdims=True)
        hit = eidx == sel
        vals.append(m)
        idxs.append(sel)
        chosen = jnp.logical_or(chosen, hit)
        work = jnp.where(hit, -jnp.inf, work)
    exps = [jnp.exp(v - vals[0]) for v in vals]
    denom = exps[0] + exps[1] + exps[2] + exps[3]
    gate_rows = [e / denom for e in exps] + [jnp.zeros_like(denom)] * (mg_ref.shape[0] - TOP_K)
    mg_ref[...] = jnp.concatenate(gate_rows, axis=0)

    onehot = chosen.astype(_BF16)
    r = lax.broadcasted_iota(jnp.int32, (n_tok, n_tok), 0)
    c = lax.broadcasted_iota(jnp.int32, (n_tok, n_tok), 1)
    before = (r < c).astype(_BF16)
    cum = jnp.dot(onehot, before, preferred_element_type=_F32) + carry_ref[:, 0:1]
    ranks = [jnp.sum(jnp.where(eidx == s, cum, 0.0), axis=0, keepdims=True) for s in idxs]
    mi_ref[...] = jnp.concatenate(idxs + [rk.astype(jnp.int32) for rk in ranks], axis=0)
    carry_ref[...] = carry_ref[...] + jnp.sum(chosen.astype(_F32), axis=1, keepdims=True)
    cnt_ref[...] = carry_ref[...]


def _pool_kernel(x_ref, pw_ref, ps_ref, g_ref, b_ref, rwt_ref, rb_ref,
                 x1r_ref, x1p_ref, mi_ref, mg_ref, cnt_ref, hist_ref, carry_ref, *, alpha):
    bi, si = pl.program_id(0), pl.program_id(1)
    ts, d = x_ref.shape[1], x_ref.shape[2]
    dg = d // len(POOL_WINDOWS)

    @pl.when(si == 0)
    def _():
        hist_ref[0:POOL_HALO, :] = jnp.zeros((POOL_HALO, d), _F32)

    x = x_ref[0]
    hist_ref[POOL_HALO:POOL_HALO + ts, :] = x
    pos = si * ts + lax.broadcasted_iota(jnp.int32, (ts, 1), 0)
    pieces = []
    for gi, win in enumerate(POOL_WINDOWS):
        c0 = gi * dg
        xg = x[:, c0:c0 + dg]
        assert win & (win - 1) == 0 and win <= POOL_HALO
        ext = hist_ref[:, c0:c0 + dg]
        span = 1
        while span < win:
            ext = ext + pltpu.roll(ext, span, 0)
            span *= 2
        acc = ext[POOL_HALO:, :]
        inv_count = 1.0 / jnp.minimum(pos + 1, win).astype(_F32)
        diff = acc * inv_count - xg
        hg = jnp.dot(diff.astype(_BF16), pw_ref[gi], preferred_element_type=_F32)
        pieces.append(alpha * xg + hg * ps_ref[:, c0:c0 + dg])
    hist_ref[0:POOL_HALO, :] = x[ts - POOL_HALO:, :]
    z = jnp.concatenate(pieces, axis=1)
    _post_norm_and_route(z, g_ref, b_ref, rwt_ref, rb_ref, (bi == 0) & (si == 0),
                         x1r_ref, x1p_ref, mi_ref, mg_ref, cnt_ref, carry_ref)


def _short_conv_kernel(x_ref, win_ref, cw_ref, wout_ref, g_ref, b_ref, rwt_ref, rb_ref,
                       x1r_ref, x1p_ref, mi_ref, mg_ref, cnt_ref, hist_ref, carry_ref, *, alpha):
    bi, si = pl.program_id(0), pl.program_id(1)
    ts, d = x_ref.shape[1], x_ref.shape[2]
    halo = SHORT_CONV_HALO

    @pl.when(si == 0)
    def _():
        hist_ref[0:halo, :] = jnp.zeros((halo, d), _F32)

    x = x_ref[0]
    xb = x.astype(_BF16)
    gate_b = jnp.dot(xb, win_ref[:, 0:d], preferred_element_type=_F32)
    gate_c = jnp.dot(xb, win_ref[:, d:2 * d], preferred_element_type=_F32)
    h = jnp.dot(xb, win_ref[:, 2 * d:3 * d], preferred_element_type=_F32)
    v = gate_c * h
    hist_ref[halo:halo + ts, :] = v
    width = cw_ref.shape[0]
    u = cw_ref[width - 1:width, :] * v
    for k in range(width - 1):
        shift = width - 1 - k
        u = u + cw_ref[k:k + 1, :] * hist_ref[halo - shift:halo - shift + ts, :]
    hist_ref[0:halo, :] = v[ts - halo:, :]
    y = jnp.dot((gate_b * u).astype(_BF16), wout_ref[...], preferred_element_type=_F32)
    _post_norm_and_route(alpha * x + y, g_ref, b_ref, rwt_ref, rb_ref, (bi == 0) & (si == 0),
                         x1r_ref, x1p_ref, mi_ref, mg_ref, cnt_ref, carry_ref)


def _conformer_kernel(x_ref, win_ref, bin_ref, dww_ref, dwb_ref, lng_ref, lnb_ref, wout_ref, bout_ref,
                      g_ref, b_ref, rwt_ref, rb_ref,
                      x1r_ref, x1p_ref, mi_ref, mg_ref, cnt_ref, hist_ref, carry_ref, *, alpha):
    bi, si = pl.program_id(0), pl.program_id(1)
    ts, d = x_ref.shape[1], x_ref.shape[2]
    halo = CONFORMER_HALO

    @pl.when(si == 0)
    def _():
        hist_ref[0:halo, :] = jnp.zeros((halo, d), _F32)

    x = x_ref[0]
    xb = x.astype(_BF16)
    a = jnp.dot(xb, win_ref[:, 0:d], preferred_element_type=_F32) + bin_ref[:, 0:d]
    gate = jnp.dot(xb, win_ref[:, d:2 * d], preferred_element_type=_F32) + bin_ref[:, d:2 * d]
    u = a * jax.nn.sigmoid(gate)
    hist_ref[halo:halo + ts, :] = u
    width = dww_ref.shape[0]
    hist = hist_ref[...]
    acc = dwb_ref[...]
    for r in range(SUBLANES):
        rolled = hist if r == 0 else pltpu.roll(hist, r, 0)
        for q in range(halo // SUBLANES):
            shift = SUBLANES * q + r
            if shift < width:
                k = width - 1 - shift
                start = halo - SUBLANES * q
                acc = acc + dww_ref[k:k + 1, :] * rolled[start:start + ts, :]
    hist_ref[0:halo, :] = u[ts - halo:, :]
    un = _layer_norm(acc, lng_ref[...], lnb_ref[...])
    un = un * jax.nn.sigmoid(un)
    y = jnp.dot(un.astype(_BF16), wout_ref[...], preferred_element_type=_F32) + bout_ref[...]
    _post_norm_and_route(alpha * x + y, g_ref, b_ref, rwt_ref, rb_ref, (bi == 0) & (si == 0),
                         x1r_ref, x1p_ref, mi_ref, mg_ref, cnt_ref, carry_ref)


def _mixer_call(kernel_fn, x, batch_lo, bsz, weights, ln_g, ln_b, router_w, router_b, halo, alpha):
    _, seq, d = x.shape
    n_exp = router_w.shape[1]
    ts = min(TOKEN_BLOCK, seq)
    rs = d // LANES
    n_tok = bsz * seq
    nsb = seq // ts

    def full(a):
        nd = a.ndim
        return pl.BlockSpec(a.shape, lambda bi, si, _nd=nd: (0,) * _nd)

    small = [ln_g.reshape(1, d), ln_b.reshape(1, d), router_w.T, router_b.reshape(n_exp, 1)]
    operands = [x] + list(weights) + small
    in_specs = [pl.BlockSpec((1, ts, d), lambda bi, si: (batch_lo + bi, si, 0))] + [full(a) for a in operands[1:]]
    tok_map = lambda bi, si: (0, bi * nsb + si)
    out_shape = [
        jax.ShapeDtypeStruct((n_tok * rs, LANES), _F32),
        jax.ShapeDtypeStruct((n_tok * rs // 2, LANES), jnp.uint32),
        jax.ShapeDtypeStruct((2 * TOP_K, n_tok), jnp.int32),
        jax.ShapeDtypeStruct((2 * TOP_K, n_tok), _F32),
        jax.ShapeDtypeStruct((n_exp, LANES), _F32),
    ]
    out_specs = [
        pl.BlockSpec((ts * rs, LANES), lambda bi, si: (bi * nsb + si, 0)),
        pl.BlockSpec((ts * rs // 2, LANES), lambda bi, si: (bi * nsb + si, 0)),
        pl.BlockSpec((2 * TOP_K, ts), tok_map),
        pl.BlockSpec((2 * TOP_K, ts), tok_map),
        pl.BlockSpec((n_exp, LANES), lambda bi, si: (0, 0)),
    ]
    return pl.pallas_call(
        functools.partial(kernel_fn, alpha=alpha),
        grid=(bsz, nsb),
        in_specs=in_specs,
        out_specs=out_specs,
        out_shape=out_shape,
        scratch_shapes=[pltpu.VMEM((halo + ts, d), _F32), pltpu.VMEM((n_exp, LANES), _F32)],
        compiler_params=pltpu.CompilerParams(
            dimension_semantics=("arbitrary", "arbitrary"), vmem_limit_bytes=VMEM_LIMIT_BYTES),
        name=kernel_fn.__name__.strip("_"),
    )(*operands)


def _sc_workers():
    info = plsc.get_sparse_core_info()
    return info.num_cores, info.num_subcores, info.num_lanes


def _sc_scatter_rows(x3, dest3, n_rows):
    n_cores, n_sub, n_lanes = _sc_workers()
    n_tok, rs, _ = x3.shape
    top_k = dest3.shape[0]
    chunk = SC_CHUNK_ROWS * (LANES * SUBLANES) // (rs * LANES)
    tok_per_w = n_tok // (n_cores * n_sub)
    rows_per_w = tok_per_w // SC_INDEX_ROW
    chunks_per_row = SC_INDEX_ROW // chunk
    assert rows_per_w * SC_INDEX_ROW * n_cores * n_sub == n_tok and chunks_per_row % 2 == 0
    mesh = plsc.VectorSubcoreMesh(core_axis_name="core", subcore_axis_name="subcore")

    @pl.kernel(out_type=jax.ShapeDtypeStruct((n_rows, rs, LANES), x3.dtype), mesh=mesh,
               scratch_types=[pltpu.VMEM((top_k, rows_per_w, SC_INDEX_ROW), jnp.int32),
                              pltpu.VMEM((2, chunk, rs, LANES), x3.dtype),
                              pltpu.SemaphoreType.DMA((2,)), pltpu.SemaphoreType.DMA((2,))])
    def scatter_kernel(x_hbm, d_hbm, o_hbm, idx_v, buf, rsem, ssem):
        wid = lax.axis_index("subcore") * n_cores + lax.axis_index("core")
        for k in range(top_k):
            pltpu.sync_copy(d_hbm.at[k, pl.ds(wid * rows_per_w, rows_per_w)], idx_v.at[k])
        base = wid * tok_per_w

        def read(j, c, slot):
            return pltpu.make_async_copy(x_hbm.at[pl.ds(base + j * SC_INDEX_ROW + c * chunk, chunk)],
                                         buf.at[slot], rsem.at[slot])

        def scatters(j, c, slot):
            copies = []
            for k in range(top_k):
                for h in range(chunk // n_lanes):
                    rows = idx_v[k, j, pl.ds(c * chunk + h * n_lanes, n_lanes)]
                    copies.append(pltpu.make_async_copy(buf.at[slot, pl.ds(h * n_lanes, n_lanes)],
                                                        o_hbm.at[rows], ssem.at[slot]))
            return copies

        def wait_scatters(j, c, slot):
            for cp in scatters(j, c, slot):
                cp.wait()

        read(0, 0, 0).start()

        def per_index_row(j, carry):
            for c in range(chunks_per_row):
                slot = c % 2
                if c == 0:
                    @pl.when(j > 0)
                    def _():
                        wait_scatters(j - 1, chunks_per_row - 1, 1 - slot)
                    read(j, c + 1, 1 - slot).start()
                elif c < chunks_per_row - 1:
                    wait_scatters(j, c - 1, 1 - slot)
                    read(j, c + 1, 1 - slot).start()
                else:
                    @pl.when(j + 1 < rows_per_w)
                    def _():
                        wait_scatters(j, c - 1, 1 - slot)
                        read(j + 1, 0, 1 - slot).start()
                read(j, c, slot).wait()
                for cp in scatters(j, c, slot):
                    cp.start()
            return carry

        lax.fori_loop(0, rows_per_w, per_index_row, 0)
        wait_scatters(rows_per_w - 1, chunks_per_row - 2, 0)
        wait_scatters(rows_per_w - 1, chunks_per_row - 1, 1)

    return scatter_kernel(x3, dest3)


def _sc_gather_rows(table3, idx2):
    n_cores, n_sub, _ = _sc_workers()
    n_idx_rows = idx2.shape[0]
    rs = table3.shape[1]
    chunk = SC_CHUNK_ROWS * (LANES * SUBLANES) // (rs * LANES)
    rows_per_w = n_idx_rows // (n_cores * n_sub)
    chunks_per_row = SC_INDEX_ROW // chunk
    assert rows_per_w * n_cores * n_sub == n_idx_rows and chunks_per_row % 2 == 0
    mesh = plsc.VectorSubcoreMesh(core_axis_name="core", subcore_axis_name="subcore")

    @pl.kernel(out_type=jax.ShapeDtypeStruct((n_idx_rows * SC_INDEX_ROW, rs, LANES), table3.dtype), mesh=mesh,
               scratch_types=[pltpu.VMEM((rows_per_w, SC_INDEX_ROW), jnp.int32),
                              pltpu.VMEM((2, chunk, rs, LANES), table3.dtype),
                              pltpu.SemaphoreType.DMA((2,)), pltpu.SemaphoreType.DMA((2,))])
    def gather_kernel(t_hbm, i_hbm, o_hbm, idx_v, buf, gsem, wsem):
        wid = lax.axis_index("subcore") * n_cores + lax.axis_index("core")
        pltpu.sync_copy(i_hbm.at[pl.ds(wid * rows_per_w, rows_per_w)], idx_v)
        base = wid * rows_per_w * SC_INDEX_ROW

        def gather(j, c, slot):
            return pltpu.make_async_copy(t_hbm.at[idx_v.at[j, pl.ds(c * chunk, chunk)]], buf.at[slot], gsem.at[slot])

        def write(j, c, slot):
            return pltpu.make_async_copy(buf.at[slot], o_hbm.at[pl.ds(base + j * SC_INDEX_ROW + c * chunk, chunk)],
                                         wsem.at[slot])

        gather(0, 0, 0).start()

        def per_index_row(j, carry):
            for c in range(chunks_per_row):
                slot = c % 2
                if c == 0:
                    @pl.when(j > 0)
                    def _():
                        write(j - 1, chunks_per_row - 1, 1 - slot).wait()
                    gather(j, c + 1, 1 - slot).start()
                elif c < chunks_per_row - 1:
                    write(j, c - 1, 1 - slot).wait()
                    gather(j, c + 1, 1 - slot).start()
                else:
                    @pl.when(j + 1 < rows_per_w)
                    def _():
                        write(j, c - 1, 1 - slot).wait()
                        gather(j + 1, 0, 1 - slot).start()
                gather(j, c, slot).wait()
                write(j, c, slot).start()
            return carry

        lax.fori_loop(0, rows_per_w, per_index_row, 0)
        write(rows_per_w - 1, chunks_per_row - 2, 0).wait()
        write(rows_per_w - 1, chunks_per_row - 1, 1).wait()

    return gather_kernel(table3, idx2)


def _pack_bf16_pairs(v):
    half = v.shape[1] // 2
    lo = lax.bitcast_convert_type(v[:, :half].astype(_BF16).astype(_F32), jnp.uint32)
    hi = lax.bitcast_convert_type(v[:, half:].astype(_BF16).astype(_F32), jnp.uint32)
    return (lo >> 16) | (hi & jnp.uint32(0xFFFF0000))


def _unpack_bf16_pairs(w):
    return (lax.bitcast_convert_type(w << 16, _F32),
            lax.bitcast_convert_type(w & jnp.uint32(0xFFFF0000), _F32))


def _expert_kernel(be_ref, nx_ref, nv_ref, nu_ref, xs_ref, bgu_ref, bdn_ref, wgu_hbm, wdn_hbm, ys_ref,
                   wgu_f, wdn_f, wgu_s, wdn_s, sem, *, rs, layer):
    b = pl.program_id(0)
    n_chunks = wgu_f.shape[1] // GATE_UP_CHUNK
    half = GATE_UP_CHUNK // 2

    def fetch(e):
        return (pltpu.make_async_copy(wgu_hbm.at[layer, e], wgu_f, sem.at[0]),
                pltpu.make_async_copy(wdn_hbm.at[layer, e], wdn_f, sem.at[1]))

    @pl.when(b < nu_ref[0])
    def _():
        @pl.when((b == 0) | (be_ref[b] != be_ref[jnp.maximum(b - 1, 0)]))
        def _():
            @pl.when(b == 0)
            def _():
                for cp in fetch(be_ref[0]):
                    cp.start()

            for cp in fetch(be_ref[b]):
                cp.wait()
            r = lax.broadcasted_iota(jnp.int32, (GATE_UP_CHUNK, GATE_UP_CHUNK), 0)
            c = lax.broadcasted_iota(jnp.int32, (GATE_UP_CHUNK, GATE_UP_CHUNK), 1)
            perm = (r == jnp.where(c < half, 2 * c, 2 * (c - half) + 1)).astype(_BF16)
            for ch in range(n_chunks):
                cols = slice(ch * GATE_UP_CHUNK, (ch + 1) * GATE_UP_CHUNK)
                w = wgu_f[:, cols].astype(_BF16)
                wgu_s[:, cols] = jnp.dot(w, perm, preferred_element_type=_F32).astype(_BF16)
            wdn_s[...] = wdn_f[...].astype(_BF16)

            @pl.when(nx_ref[b] >= 0)
            def _():
                for cp in fetch(nx_ref[b]):
                    cp.start()

        n_valid = nv_ref[b]
        sub = EXPERT_SUB_ROWS

        def sub_block(s, carry):
            row0 = pl.multiple_of(s * sub, sub)
            defined = row0 + lax.broadcasted_iota(jnp.int32, (sub, 1), 0) < n_valid
            words = [jnp.where(defined, w, jnp.uint32(0)) for w in _load_rows(xs_ref, sub, rs // 2, row0)]
            halves = [_unpack_bf16_pairs(w) for w in words]
            x = jnp.concatenate([lo for lo, _ in halves] + [hi for _, hi in halves], axis=1).astype(_BF16)
            h = jnp.dot(x, wgu_s[...], preferred_element_type=_F32) + bgu_ref[0]
            acts = []
            for ch in range(n_chunks):
                g = jnp.minimum(h[:, ch * GATE_UP_CHUNK:ch * GATE_UP_CHUNK + half], SWIGLU_LIMIT)
                up = jnp.clip(h[:, ch * GATE_UP_CHUNK + half:(ch + 1) * GATE_UP_CHUNK], -SWIGLU_LIMIT, SWIGLU_LIMIT)
                acts.append(((up + 1.0) * (g * jax.nn.sigmoid(SWIGLU_ALPHA * g))).astype(_BF16))
            y = jnp.dot(jnp.concatenate(acts, axis=1), wdn_s[...], preferred_element_type=_F32) + bdn_ref[0]
            _store_rows(ys_ref, _pack_bf16_pairs(y), row0)
            return carry

        lax.fori_loop(0, (n_valid + sub - 1) // sub, sub_block, 0)


def _expert_call(block_expert, block_next, block_valid, n_used, xs, layer, w_gu_all, b_gu_grouped, w_dn_all,
                 b_dn, rs):
    _, n_exp, d, f2 = w_gu_all.shape
    f = f2 // 2
    br = EXPERT_BLOCK_ROWS
    n_blocks = xs.shape[0] // (br * rs // 2)
    assert f2 % GATE_UP_CHUNK == 0 and rs % 2 == 0

    def row_map(b, be, nx, nv, nu):
        return (jnp.minimum(b, nu[0] - 1), 0)

    def w_map(b, be, nx, nv, nu):
        return (be[b], 0, 0)

    return pl.pallas_call(
        functools.partial(_expert_kernel, rs=rs, layer=layer),
        grid_spec=pltpu.PrefetchScalarGridSpec(
            num_scalar_prefetch=4,
            grid=(n_blocks,),
            in_specs=[
                pl.BlockSpec((br * rs // 2, LANES), row_map),
                pl.BlockSpec((1, 1, f2), w_map),
                pl.BlockSpec((1, 1, d), w_map),
                pl.BlockSpec(memory_space=pl.ANY),
                pl.BlockSpec(memory_space=pl.ANY),
            ],
            out_specs=pl.BlockSpec((br * rs // 2, LANES), row_map),
            scratch_shapes=[pltpu.VMEM((d, f2), _F32), pltpu.VMEM((f, d), _F32),
                            pltpu.VMEM((d, f2), _BF16), pltpu.VMEM((f, d), _BF16),
                            pltpu.SemaphoreType.DMA((2,))],
        ),
        out_shape=jax.ShapeDtypeStruct(xs.shape, jnp.uint32),
        compiler_params=pltpu.CompilerParams(
            dimension_semantics=("arbitrary",), vmem_limit_bytes=VMEM_LIMIT_BYTES),
        name="experts",
    )(block_expert, block_next, block_valid, n_used, xs, b_gu_grouped, b_dn, w_gu_all, w_dn_all)


def _combine_kernel(gates_ref, x1r_ref, g_ref, b_ref, *rest, rs, alpha):
    yk_refs, out_ref = rest[:TOP_K], rest[-1]
    tb = out_ref.shape[0]
    rp = rs // 2
    gates = gates_ref[...].T
    lo_pieces, hi_pieces = [], []
    for j in range(rp):
        lo = alpha * x1r_ref[pl.ds(j, tb, stride=rs), :]
        hi = alpha * x1r_ref[pl.ds(rp + j, tb, stride=rs), :]
        for k in range(TOP_K):
            y_lo, y_hi = _unpack_bf16_pairs(yk_refs[k][pl.ds(j, tb, stride=rp), :])
            lo = lo + gates[:, k:k + 1] * y_lo
            hi = hi + gates[:, k:k + 1] * y_hi
        lo_pieces.append(lo)
        hi_pieces.append(hi)
    out_ref[...] = _layer_norm(jnp.concatenate(lo_pieces + hi_pieces, axis=1), g_ref[...], b_ref[...])


def _combine_call(gates_t, x1r, ln_g, ln_b, yk, alpha, rs, out_tokens, token_lo, out_buf):
    n_tok = gates_t.shape[1]
    d = rs * LANES
    rp = rs // 2
    tb = min(COMBINE_BLOCK, n_tok)
    steps = n_tok // tb
    block_lo = token_lo // tb
    yk_specs = [pl.BlockSpec((tb * rp, LANES), lambda i, _k=k: (_k * steps + i, 0)) for k in range(TOP_K)]
    operands = [gates_t, x1r, ln_g.reshape(1, d), ln_b.reshape(1, d)] + [yk] * TOP_K
    in_specs = [
        pl.BlockSpec((2 * TOP_K, tb), lambda i: (0, i)),
        pl.BlockSpec((tb * rs, LANES), lambda i: (i, 0)),
        pl.BlockSpec((1, d), lambda i: (0, 0)),
        pl.BlockSpec((1, d), lambda i: (0, 0)),
    ] + yk_specs
    aliases = {}
    if out_buf is not None:
        aliases = {len(operands): 0}
        operands.append(out_buf)
        in_specs.append(pl.BlockSpec(memory_space=pl.ANY))
    return pl.pallas_call(
        functools.partial(_combine_kernel, rs=rs, alpha=alpha),
        grid=(steps,),
        in_specs=in_specs,
        out_specs=pl.BlockSpec((tb, d), lambda i: (block_lo + i, 0)),
        out_shape=jax.ShapeDtypeStruct((out_tokens, d), _F32),
        input_output_aliases=aliases,
        compiler_params=pltpu.CompilerParams(
            dimension_semantics=("arbitrary",), vmem_limit_bytes=VMEM_LIMIT_BYTES),
        name="combine",
    )(*operands)


def _routing_tables(meta_i, counts_f, n_blocks):
    n_exp = counts_f.shape[0]
    br = EXPERT_BLOCK_ROWS
    counts = counts_f[:, 0].astype(jnp.int32)
    padded = ((counts + br - 1) // br) * br
    pend = jnp.cumsum(padded)
    pstart = pend - padded
    eids = jnp.arange(n_exp, dtype=jnp.int32)
    idx, rank = meta_i[:TOP_K], meta_i[TOP_K:]
    dest = jnp.sum(jnp.where(idx[..., None] == eids, pstart, 0), axis=-1) + rank
    n_used = (pend[-1] // br).astype(jnp.int32)
    blk = jnp.minimum(jnp.arange(n_blocks, dtype=jnp.int32), n_used - 1)
    block_expert = jnp.minimum(jnp.sum((pend[None, :] <= (blk * br)[:, None]).astype(jnp.int32), axis=1),
                               n_exp - 1)
    group_end = jnp.sum(jnp.where(block_expert[:, None] == eids, pstart + counts, 0), axis=-1)
    block_valid = jnp.clip(group_end - blk * br, 0, br).astype(jnp.int32)
    later = (eids[None, :] > block_expert[:, None]) & (counts[None, :] > 0)
    block_next = jnp.min(jnp.where(later, eids[None, :], n_exp), axis=1)
    block_next = jnp.where(block_next == n_exp, -1, block_next).astype(jnp.int32)
    return dest, block_expert, block_next, block_valid, n_used.reshape(1)


def _moe_layer(x1r, x1p, meta_i, gates_t, counts_f, layer, w_gu_all, b_gu, w_dn_all, b_dn, ln_g, ln_b, alpha, rs,
               out_tokens, token_lo, out_buf):
    _, n_exp, d, f2 = w_gu_all.shape
    n_tok = gates_t.shape[1]
    br = EXPERT_BLOCK_ROWS
    rp = rs // 2
    n_blocks = -(-(n_tok * TOP_K) // br) + n_exp
    dest, block_expert, block_next, block_valid, n_used = _routing_tables(meta_i, counts_f, n_blocks)
    n_rows = n_blocks * br
    xs = _sc_scatter_rows(x1p.reshape(n_tok, rp, LANES), dest.reshape(TOP_K, n_tok // SC_INDEX_ROW, SC_INDEX_ROW),
                          n_rows)
    half = GATE_UP_CHUNK // 2
    b_gu_grouped = b_gu.reshape(n_exp, f2 // GATE_UP_CHUNK, half, 2).transpose(0, 1, 3, 2).reshape(n_exp, 1, f2)
    ys = _expert_call(block_expert, block_next, block_valid, n_used, xs.reshape(n_rows * rp, LANES), layer, w_gu_all,
                      b_gu_grouped, w_dn_all, b_dn.reshape(n_exp, 1, d), rs)
    yk = _sc_gather_rows(ys.reshape(n_rows, rp, LANES),
                         dest.reshape(TOP_K * n_tok // SC_INDEX_ROW, SC_INDEX_ROW))
    return _combine_call(gates_t, x1r, ln_g, ln_b, yk.reshape(TOP_K * n_tok * rp, LANES), alpha, rs,
                         out_tokens, token_lo, out_buf)


def kernel(x, pool_w, pool_scale, sc_w_in, sc_conv_w, sc_w_out, cf_w_in, cf_b_in, cf_dw_w, cf_dw_b,
           cf_ln_g, cf_ln_b, cf_w_out, cf_b_out, mix_ln_g, mix_ln_b, router_w, router_b,
           moe_w_gu, moe_b_gu, moe_w_dn, moe_b_dn, ffn_ln_g, ffn_ln_b):
    bsz, seq, d = x.shape
    depth = mix_ln_g.shape[0]
    alpha = (2.0 * depth) ** 0.25
    rs = d // LANES
    n_chains = BATCH_CHAINS if bsz % BATCH_CHAINS == 0 else 1
    cb = bsz // n_chains
    chains = [(x, c * cb) for c in range(n_chains)]
    ia = ib = ic = 0
    for layer in range(depth):
        kind = layer % 3
        route = (mix_ln_g[layer], mix_ln_b[layer], router_w[layer], router_b[layer])
        if kind == 0:
            mixer, halo = _pool_kernel, POOL_HALO
            weights = [pool_w[ia].astype(_BF16), pool_scale[ia].reshape(1, d)]
            ia += 1
        elif kind == 1:
            mixer, halo = _short_conv_kernel, SHORT_CONV_HALO
            weights = [sc_w_in[ib].astype(_BF16), sc_conv_w[ib], sc_w_out[ib].astype(_BF16)]
            ib += 1
        else:
            mixer, halo = _conformer_kernel, CONFORMER_HALO
            weights = [cf_w_in[ic].astype(_BF16), cf_b_in[ic].reshape(1, 2 * d), cf_dw_w[ic],
                       cf_dw_b[ic].reshape(1, d), cf_ln_g[ic].reshape(1, d), cf_ln_b[ic].reshape(1, d),
                       cf_w_out[ic].astype(_BF16), cf_b_out[ic].reshape(1, d)]
            ic += 1
        routed = [_mixer_call(mixer, xc, lo, cb, weights, *route, halo, alpha) for xc, lo in chains]
        last = layer == depth - 1
        chains, out_buf = [], None
        for c, (x1r, x1p, meta_i, gates_t, counts_f) in enumerate(routed):
            out_tokens, token_lo = (bsz * seq, c * cb * seq) if last else (cb * seq, 0)
            xc = _moe_layer(x1r, x1p, meta_i, gates_t, counts_f, layer, moe_w_gu, moe_b_gu[layer],
                            moe_w_dn, moe_b_dn[layer], ffn_ln_g[layer], ffn_ln_b[layer], alpha, rs,
                            out_tokens, token_lo, out_buf)
            if last:
                out_buf = xc
            else:
                chains.append((xc.reshape(cb, seq, d), 0))
    return out_buf.reshape(bsz, seq, d)
```

```python
import functools

import jax
import jax.numpy as jnp
from jax import lax
from jax.experimental import pallas as pl
from jax.experimental.pallas import tpu as pltpu
from jax.experimental.pallas import tpu_sc as plsc

LANES = 128
SUBLANES = 8
TOP_K = 4
POOL_WINDOWS = (2, 4, 8, 16)
POOL_HALO = 16
SHORT_CONV_HALO = 8
CONFORMER_HALO = 32
SWIGLU_LIMIT = 7.0
SWIGLU_ALPHA = 1.702
LN_EPS = 1e-5
TOKEN_BLOCK = 512
EXPERT_BLOCK_ROWS = 2048
EXPERT_SUB_ROWS = 256
COMBINE_BLOCK = 512
BATCH_CHAINS = 2
SC_INDEX_ROW = 128
SC_CHUNK_ROWS = 32
GATE_UP_CHUNK = 2 * LANES
VMEM_LIMIT_BYTES = 56 * 1024 * 1024

_F32 = jnp.float32
_BF16 = jnp.bfloat16


def _layer_norm(z, g, b):
    mu = jnp.mean(z, axis=-1, keepdims=True)
    zc = z - mu
    var = jnp.mean(zc * zc, axis=-1, keepdims=True)
    return zc * lax.rsqrt(var + LN_EPS) * g + b


def _store_rows(row_ref, val, row0=0):
    rows, d = val.shape
    rs = d // LANES
    for j in range(rs):
        row_ref[pl.ds(row0 * rs + j, rows, stride=rs), :] = val[:, j * LANES:(j + 1) * LANES]


def _load_rows(row_ref, rows, rs, row0=0):
    return [row_ref[pl.ds(row0 * rs + j, rows, stride=rs), :] for j in range(rs)]


def _post_norm_and_route(z, g_ref, b_ref, rwt_ref, rb_ref, first,
                         x1r_ref, x1p_ref, mi_ref, mg_ref, cnt_ref, carry_ref):
    n_tok = z.shape[0]
    n_exp = rwt_ref.shape[0]

    @pl.when(first)
    def _():
        carry_ref[...] = jnp.zeros_like(carry_ref)

    x1 = _layer_norm(z, g_ref[...], b_ref[...])
    _store_rows(x1r_ref, x1)
    _store_rows(x1p_ref, _pack_bf16_pairs(x1))

    logits = lax.dot_general(rwt_ref[...], x1, (((1,), (1,)), ((), ())),
                             precision=lax.Precision.HIGHEST,
                             preferred_element_type=_F32) + rb_ref[...]
    eidx = lax.broadcasted_iota(jnp.int32, logits.shape, 0)
    work = logits
    chosen = jnp.zeros(logits.shape, jnp.bool_)
    vals, idxs = [], []
    for _ in range(TOP_K):
        m = jnp.max(work, axis=0, keepdims=True)
        sel = jnp.min(jnp.where(work == m, eidx, n_exp), axis=0, keepdims=True)
        hit = eidx == sel
        vals.append(m)
        idxs.append(sel)
        chosen = jnp.logical_or(chosen, hit)
        work = jnp.where(hit, -jnp.inf, work)
    exps = [jnp.exp(v - vals[0]) for v in vals]
    denom = exps[0] + exps[1] + exps[2] + exps[3]
    gate_rows = [e / denom for e in exps] + [jnp.zeros_like(denom)] * (mg_ref.shape[0] - TOP_K)
    mg_ref[...] = jnp.concatenate(gate_rows, axis=0)

    onehot = chosen.astype(_BF16)
    r = lax.broadcasted_iota(jnp.int32, (n_tok, n_tok), 0)
    c = lax.broadcasted_iota(jnp.int32, (n_tok, n_tok), 1)
    before = (r < c).astype(_BF16)
    cum = jnp.dot(onehot, before, preferred_element_type=_F32) + carry_ref[:, 0:1]
    ranks = [jnp.sum(jnp.where(eidx == s, cum, 0.0), axis=0, keepdims=True) for s in idxs]
    mi_ref[...] = jnp.concatenate(idxs + [rk.astype(jnp.int32) for rk in ranks], axis=0)
    carry_ref[...] = carry_ref[...] + jnp.sum(chosen.astype(_F32), axis=1, keepdims=True)
    cnt_ref[...] = carry_ref[...]


def _pool_mix(x, si, weights, hist_ref, alpha):
    pw_ref, ps_ref = weights
    ts, d = x.shape
    dg = d // len(POOL_WINDOWS)

    @pl.when(si == 0)
    def _():
        hist_ref[0:POOL_HALO, :] = jnp.zeros((POOL_HALO, d), _F32)

    hist_ref[POOL_HALO:POOL_HALO + ts, :] = x
    pos = si * ts + lax.broadcasted_iota(jnp.int32, (ts, 1), 0)
    pieces = []
    for gi, win in enumerate(POOL_WINDOWS):
        c0 = gi * dg
        xg = x[:, c0:c0 + dg]
        assert win & (win - 1) == 0 and win <= POOL_HALO
        ext = hist_ref[:, c0:c0 + dg]
        span = 1
        while span < win:
            ext = ext + pltpu.roll(ext, span, 0)
            span *= 2
        acc = ext[POOL_HALO:, :]
        inv_count = 1.0 / jnp.minimum(pos + 1, win).astype(_F32)
        diff = acc * inv_count - xg
        hg = jnp.dot(diff.astype(_BF16), pw_ref[gi], preferred_element_type=_F32)
        pieces.append(alpha * xg + hg * ps_ref[:, c0:c0 + dg])
    hist_ref[0:POOL_HALO, :] = x[ts - POOL_HALO:, :]
    return jnp.concatenate(pieces, axis=1)


def _short_conv_mix(x, si, weights, hist_ref, alpha):
    win_ref, cw_ref, wout_ref = weights
    ts, d = x.shape
    halo = SHORT_CONV_HALO

    @pl.when(si == 0)
    def _():
        hist_ref[0:halo, :] = jnp.zeros((halo, d), _F32)

    xb = x.astype(_BF16)
    gate_b = jnp.dot(xb, win_ref[:, 0:d], preferred_element_type=_F32)
    gate_c = jnp.dot(xb, win_ref[:, d:2 * d], preferred_element_type=_F32)
    h = jnp.dot(xb, win_ref[:, 2 * d:3 * d], preferred_element_type=_F32)
    v = gate_c * h
    hist_ref[halo:halo + ts, :] = v
    width = cw_ref.shape[0]
    u = cw_ref[width - 1:width, :] * v
    for k in range(width - 1):
        shift = width - 1 - k
        u = u + cw_ref[k:k + 1, :] * hist_ref[halo - shift:halo - shift + ts, :]
    hist_ref[0:halo, :] = v[ts - halo:, :]
    y = jnp.dot((gate_b * u).astype(_BF16), wout_ref[...], preferred_element_type=_F32)
    return alpha * x + y


def _conformer_mix(x, si, weights, hist_ref, alpha):
    win_ref, bin_ref, dww_ref, dwb_ref, lng_ref, lnb_ref, wout_ref, bout_ref = weights
    ts, d = x.shape
    halo = CONFORMER_HALO

    @pl.when(si == 0)
    def _():
        hist_ref[0:halo, :] = jnp.zeros((halo, d), _F32)

    xb = x.astype(_BF16)
    a = jnp.dot(xb, win_ref[:, 0:d], preferred_element_type=_F32) + bin_ref[:, 0:d]
    gate = jnp.dot(xb, win_ref[:, d:2 * d], preferred_element_type=_F32) + bin_ref[:, d:2 * d]
    u = a * jax.nn.sigmoid(gate)
    hist_ref[halo:halo + ts, :] = u
    width = dww_ref.shape[0]
    hist = hist_ref[...]
    acc = dwb_ref[...]
    for r in range(SUBLANES):
        rolled = hist if r == 0 else pltpu.roll(hist, r, 0)
        for q in range(halo // SUBLANES):
            shift = SUBLANES * q + r
            if shift < width:
                k = width - 1 - shift
                start = halo - SUBLANES * q
                acc = acc + dww_ref[k:k + 1, :] * rolled[start:start + ts, :]
    hist_ref[0:halo, :] = u[ts - halo:, :]
    un = _layer_norm(acc, lng_ref[...], lnb_ref[...])
    un = un * jax.nn.sigmoid(un)
    y = jnp.dot(un.astype(_BF16), wout_ref[...], preferred_element_type=_F32) + bout_ref[...]
    return alpha * x + y


def _combine_rows(gates_ref, x1r_ref, g_ref, b_ref, yk_refs, rs, alpha):
    tb = gates_ref.shape[1]
    rp = rs // 2
    gates = gates_ref[...].T
    lo_pieces, hi_pieces = [], []
    for j in range(rp):
        lo = alpha * x1r_ref[pl.ds(j, tb, stride=rs), :]
        hi = alpha * x1r_ref[pl.ds(rp + j, tb, stride=rs), :]
        for k in range(TOP_K):
            y_lo, y_hi = _unpack_bf16_pairs(yk_refs[k][pl.ds(j, tb, stride=rp), :])
            lo = lo + gates[:, k:k + 1] * y_lo
            hi = hi + gates[:, k:k + 1] * y_hi
        lo_pieces.append(lo)
        hi_pieces.append(hi)
    return _layer_norm(jnp.concatenate(lo_pieces + hi_pieces, axis=1), g_ref[...], b_ref[...])


def _mixer_kernel(*refs, mix_fn, n_weights, from_moe, alpha, rs):
    bi, si = pl.program_id(0), pl.program_id(1)
    if from_moe:
        gates_ref, xprev_ref, cg_ref, cb_ref = refs[:4]
        x = _combine_rows(gates_ref, xprev_ref, cg_ref, cb_ref, refs[4:4 + TOP_K], rs, alpha)
        refs = refs[4 + TOP_K:]
    else:
        x = refs[0][0]
        refs = refs[1:]
    weights, refs = refs[:n_weights], refs[n_weights:]
    g_ref, b_ref, rwt_ref, rb_ref = refs[:4]
    x1r_ref, x1p_ref, mi_ref, mg_ref, cnt_ref, hist_ref, carry_ref = refs[4:]
    z = mix_fn(x, si, weights, hist_ref, alpha)
    _post_norm_and_route(z, g_ref, b_ref, rwt_ref, rb_ref, (bi == 0) & (si == 0),
                         x1r_ref, x1p_ref, mi_ref, mg_ref, cnt_ref, carry_ref)


def _mixer_call(mix_fn, source, bsz, seq, weights, ln_g, ln_b, router_w, router_b, halo, alpha):
    d = router_w.shape[0]
    n_exp = router_w.shape[1]
    ts = min(TOKEN_BLOCK, seq)
    rs = d // LANES
    rp = rs // 2
    n_tok = bsz * seq
    nsb = seq // ts

    def full(a):
        nd = a.ndim
        return pl.BlockSpec(a.shape, lambda bi, si, _nd=nd: (0,) * _nd)

    small = [ln_g.reshape(1, d), ln_b.reshape(1, d), router_w.T, router_b.reshape(n_exp, 1)]
    tail = list(weights) + small
    if source[0] == "x":
        _, x, batch_lo = source
        operands = [x] + tail
        in_specs = [pl.BlockSpec((1, ts, d), lambda bi, si: (batch_lo + bi, si, 0))]
    else:
        _, gates_t, x1r, cg, cb, yk = source
        steps = n_tok // ts
        operands = [gates_t, x1r, cg.reshape(1, d), cb.reshape(1, d)] + [yk] * TOP_K + tail
        in_specs = [
            pl.BlockSpec((2 * TOP_K, ts), lambda bi, si: (0, bi * nsb + si)),
            pl.BlockSpec((ts * rs, LANES), lambda bi, si: (bi * nsb + si, 0)),
            pl.BlockSpec((1, d), lambda bi, si: (0, 0)),
            pl.BlockSpec((1, d), lambda bi, si: (0, 0)),
        ] + [pl.BlockSpec((ts * rp, LANES), lambda bi, si, _k=k: (_k * steps + bi * nsb + si, 0))
             for k in range(TOP_K)]
    in_specs = in_specs + [full(a) for a in tail]
    tok_map = lambda bi, si: (0, bi * nsb + si)
    out_shape = [
        jax.ShapeDtypeStruct((n_tok * rs, LANES), _F32),
        jax.ShapeDtypeStruct((n_tok * rs // 2, LANES), jnp.uint32),
        jax.ShapeDtypeStruct((2 * TOP_K, n_tok), jnp.int32),
        jax.ShapeDtypeStruct((2 * TOP_K, n_tok), _F32),
        jax.ShapeDtypeStruct((n_exp, LANES), _F32),
    ]
    out_specs = [
        pl.BlockSpec((ts * rs, LANES), lambda bi, si: (bi * nsb + si, 0)),
        pl.BlockSpec((ts * rs // 2, LANES), lambda bi, si: (bi * nsb + si, 0)),
        pl.BlockSpec((2 * TOP_K, ts), tok_map),
        pl.BlockSpec((2 * TOP_K, ts), tok_map),
        pl.BlockSpec((n_exp, LANES), lambda bi, si: (0, 0)),
    ]
    return pl.pallas_call(
        functools.partial(_mixer_kernel, mix_fn=mix_fn, n_weights=len(weights), from_moe=source[0] == "moe",
                          alpha=alpha, rs=rs),
        grid=(bsz, nsb),
        in_specs=in_specs,
        out_specs=out_specs,
        out_shape=out_shape,
        scratch_shapes=[pltpu.VMEM((halo + ts, d), _F32), pltpu.VMEM((n_exp, LANES), _F32)],
        compiler_params=pltpu.CompilerParams(
            dimension_semantics=("arbitrary", "arbitrary"), vmem_limit_bytes=VMEM_LIMIT_BYTES),
        name=mix_fn.__name__.strip("_"),
    )(*operands)


def _sc_workers():
    info = plsc.get_sparse_core_info()
    return info.num_cores, info.num_subcores, info.num_lanes


def _sc_scatter_rows(x3, dest3, n_rows):
    n_cores, n_sub, n_lanes = _sc_workers()
    n_tok, rs, _ = x3.shape
    top_k = dest3.shape[0]
    chunk = SC_CHUNK_ROWS * (LANES * SUBLANES) // (rs * LANES)
    tok_per_w = n_tok // (n_cores * n_sub)
    rows_per_w = tok_per_w // SC_INDEX_ROW
    chunks_per_row = SC_INDEX_ROW // chunk
    assert rows_per_w * SC_INDEX_ROW * n_cores * n_sub == n_tok and chunks_per_row % 2 == 0
    mesh = plsc.VectorSubcoreMesh(core_axis_name="core", subcore_axis_name="subcore")

    @pl.kernel(out_type=jax.ShapeDtypeStruct((n_rows, rs, LANES), x3.dtype), mesh=mesh,
               scratch_types=[pltpu.VMEM((top_k, rows_per_w, SC_INDEX_ROW), jnp.int32),
                              pltpu.VMEM((2, chunk, rs, LANES), x3.dtype),
                              pltpu.SemaphoreType.DMA((2,)), pltpu.SemaphoreType.DMA((2,))])
    def scatter_kernel(x_hbm, d_hbm, o_hbm, idx_v, buf, rsem, ssem):
        wid = lax.axis_index("subcore") * n_cores + lax.axis_index("core")
        for k in range(top_k):
            pltpu.sync_copy(d_hbm.at[k, pl.ds(wid * rows_per_w, rows_per_w)], idx_v.at[k])
        base = wid * tok_per_w

        def read(j, c, slot):
            return pltpu.make_async_copy(x_hbm.at[pl.ds(base + j * SC_INDEX_ROW + c * chunk, chunk)],
                                         buf.at[slot], rsem.at[slot])

        def scatters(j, c, slot):
            copies = []
            for k in range(top_k):
                for h in range(chunk // n_lanes):
                    rows = idx_v[k, j, pl.ds(c * chunk + h * n_lanes, n_lanes)]
                    copies.append(pltpu.make_async_copy(buf.at[slot, pl.ds(h * n_lanes, n_lanes)],
                                                        o_hbm.at[rows], ssem.at[slot]))
            return copies

        def wait_scatters(j, c, slot):
            for cp in scatters(j, c, slot):
                cp.wait()

        read(0, 0, 0).start()

        def per_index_row(j, carry):
            for c in range(chunks_per_row):
                slot = c % 2
                if c == 0:
                    @pl.when(j > 0)
                    def _():
                        wait_scatters(j - 1, chunks_per_row - 1, 1 - slot)
                    read(j, c + 1, 1 - slot).start()
                elif c < chunks_per_row - 1:
                    wait_scatters(j, c - 1, 1 - slot)
                    read(j, c + 1, 1 - slot).start()
                else:
                    @pl.when(j + 1 < rows_per_w)
                    def _():
                        wait_scatters(j, c - 1, 1 - slot)
                        read(j + 1, 0, 1 - slot).start()
                read(j, c, slot).wait()
                for cp in scatters(j, c, slot):
                    cp.start()
            return carry

        lax.fori_loop(0, rows_per_w, per_index_row, 0)
        wait_scatters(rows_per_w - 1, chunks_per_row - 2, 0)
        wait_scatters(rows_per_w - 1, chunks_per_row - 1, 1)

    return scatter_kernel(x3, dest3)


def _sc_gather_rows(table3, idx2):
    n_cores, n_sub, _ = _sc_workers()
    n_idx_rows = idx2.shape[0]
    rs = table3.shape[1]
    chunk = SC_CHUNK_ROWS * (LANES * SUBLANES) // (rs * LANES)
    rows_per_w = n_idx_rows // (n_cores * n_sub)
    chunks_per_row = SC_INDEX_ROW // chunk
    assert rows_per_w * n_cores * n_sub == n_idx_rows and chunks_per_row % 2 == 0
    mesh = plsc.VectorSubcoreMesh(core_axis_name="core", subcore_axis_name="subcore")

    @pl.kernel(out_type=jax.ShapeDtypeStruct((n_idx_rows * SC_INDEX_ROW, rs, LANES), table3.dtype), mesh=mesh,
               scratch_types=[pltpu.VMEM((rows_per_w, SC_INDEX_ROW), jnp.int32),
                              pltpu.VMEM((2, chunk, rs, LANES), table3.dtype),
                              pltpu.SemaphoreType.DMA((2,)), pltpu.SemaphoreType.DMA((2,))])
    def gather_kernel(t_hbm, i_hbm, o_hbm, idx_v, buf, gsem, wsem):
        wid = lax.axis_index("subcore") * n_cores + lax.axis_index("core")
        pltpu.sync_copy(i_hbm.at[pl.ds(wid * rows_per_w, rows_per_w)], idx_v)
        base = wid * rows_per_w * SC_INDEX_ROW

        def gather(j, c, slot):
            return pltpu.make_async_copy(t_hbm.at[idx_v.at[j, pl.ds(c * chunk, chunk)]], buf.at[slot], gsem.at[slot])

        def write(j, c, slot):
            return pltpu.make_async_copy(buf.at[slot], o_hbm.at[pl.ds(base + j * SC_INDEX_ROW + c * chunk, chunk)],
                                         wsem.at[slot])

        gather(0, 0, 0).start()

        def per_index_row(j, carry):
            for c in range(chunks_per_row):
                slot = c % 2
                if c == 0:
                    @pl.when(j > 0)
                    def _():
                        write(j - 1, chunks_per_row - 1, 1 - slot).wait()
                    gather(j, c + 1, 1 - slot).start()
                elif c < chunks_per_row - 1:
                    write(j, c - 1, 1 - slot).wait()
                    gather(j, c + 1, 1 - slot).start()
                else:
                    @pl.when(j + 1 < rows_per_w)
                    def _():
                        write(j, c - 1, 1 - slot).wait()
                        gather(j + 1, 0, 1 - slot).start()
                gather(j, c, slot).wait()
                write(j, c, slot).start()
            return carry

        lax.fori_loop(0, rows_per_w, per_index_row, 0)
        write(rows_per_w - 1, chunks_per_row - 2, 0).wait()
        write(rows_per_w - 1, chunks_per_row - 1, 1).wait()

    return gather_kernel(table3, idx2)


def _pack_bf16_pairs(v):
    half = v.shape[1] // 2
    lo = lax.bitcast_convert_type(v[:, :half].astype(_BF16).astype(_F32), jnp.uint32)
    hi = lax.bitcast_convert_type(v[:, half:].astype(_BF16).astype(_F32), jnp.uint32)
    return (lo >> 16) | (hi & jnp.uint32(0xFFFF0000))


def _unpack_bf16_pairs(w):
    return (lax.bitcast_convert_type(w << 16, _F32),
            lax.bitcast_convert_type(w & jnp.uint32(0xFFFF0000), _F32))


def _expert_kernel(be_ref, nx_ref, nv_ref, nu_ref, xs_ref, bgu_ref, bdn_ref, wgu_hbm, wdn_hbm, ys_ref,
                   wgu_f, wdn_f, wgu_s, wdn_s, sem, *, rs, layer):
    b = pl.program_id(0)
    n_chunks = wgu_f.shape[1] // GATE_UP_CHUNK
    half = GATE_UP_CHUNK // 2

    def fetch(e):
        return (pltpu.make_async_copy(wgu_hbm.at[layer, e], wgu_f, sem.at[0]),
                pltpu.make_async_copy(wdn_hbm.at[layer, e], wdn_f, sem.at[1]))

    @pl.when(b < nu_ref[0])
    def _():
        @pl.when((b == 0) | (be_ref[b] != be_ref[jnp.maximum(b - 1, 0)]))
        def _():
            @pl.when(b == 0)
            def _():
                for cp in fetch(be_ref[0]):
                    cp.start()

            for cp in fetch(be_ref[b]):
                cp.wait()
            r = lax.broadcasted_iota(jnp.int32, (GATE_UP_CHUNK, GATE_UP_CHUNK), 0)
            c = lax.broadcasted_iota(jnp.int32, (GATE_UP_CHUNK, GATE_UP_CHUNK), 1)
            perm = (r == jnp.where(c < half, 2 * c, 2 * (c - half) + 1)).astype(_BF16)
            for ch in range(n_chunks):
                cols = slice(ch * GATE_UP_CHUNK, (ch + 1) * GATE_UP_CHUNK)
                w = wgu_f[:, cols].astype(_BF16)
                wgu_s[:, cols] = jnp.dot(w, perm, preferred_element_type=_F32).astype(_BF16)
            wdn_s[...] = wdn_f[...].astype(_BF16)

            @pl.when(nx_ref[b] >= 0)
            def _():
                for cp in fetch(nx_ref[b]):
                    cp.start()

        n_valid = nv_ref[b]
        sub = EXPERT_SUB_ROWS

        def sub_block(s):
            row0 = pl.multiple_of(s * sub, sub)
            defined = row0 + lax.broadcasted_iota(jnp.int32, (sub, 1), 0) < n_valid
            words = [jnp.where(defined, w, jnp.uint32(0)) for w in _load_rows(xs_ref, sub, rs // 2, row0)]
            halves = [_unpack_bf16_pairs(w) for w in words]
            x = jnp.concatenate([lo for lo, _ in halves] + [hi for _, hi in halves], axis=1).astype(_BF16)
            h = jnp.dot(x, wgu_s[...], preferred_element_type=_F32) + bgu_ref[0]
            acts = []
            for ch in range(n_chunks):
                g = jnp.minimum(h[:, ch * GATE_UP_CHUNK:ch * GATE_UP_CHUNK + half], SWIGLU_LIMIT)
                up = jnp.clip(h[:, ch * GATE_UP_CHUNK + half:(ch + 1) * GATE_UP_CHUNK], -SWIGLU_LIMIT, SWIGLU_LIMIT)
                acts.append(((up + 1.0) * (g * jax.nn.sigmoid(SWIGLU_ALPHA * g))).astype(_BF16))
            y = jnp.dot(jnp.concatenate(acts, axis=1), wdn_s[...], preferred_element_type=_F32) + bdn_ref[0]
            _store_rows(ys_ref, _pack_bf16_pairs(y), row0)

        n_sub = (n_valid + sub - 1) // sub

        def pair(i, carry):
            sub_block(2 * i)
            sub_block(2 * i + 1)
            return carry

        lax.fori_loop(0, n_sub // 2, pair, 0)

        @pl.when(n_sub % 2 == 1)
        def _():
            sub_block(n_sub - 1)


def _expert_call(block_expert, block_next, block_valid, n_used, xs, layer, w_gu_all, b_gu_grouped, w_dn_all,
                 b_dn, rs):
    _, n_exp, d, f2 = w_gu_all.shape
    f = f2 // 2
    br = EXPERT_BLOCK_ROWS
    n_blocks = xs.shape[0] // (br * rs // 2)
    assert f2 % GATE_UP_CHUNK == 0 and rs % 2 == 0

    def row_map(b, be, nx, nv, nu):
        return (jnp.minimum(b, nu[0] - 1), 0)

    def w_map(b, be, nx, nv, nu):
        return (be[b], 0, 0)

    return pl.pallas_call(
        functools.partial(_expert_kernel, rs=rs, layer=layer),
        grid_spec=pltpu.PrefetchScalarGridSpec(
            num_scalar_prefetch=4,
            grid=(n_blocks,),
            in_specs=[
                pl.BlockSpec((br * rs // 2, LANES), row_map),
                pl.BlockSpec((1, 1, f2), w_map),
                pl.BlockSpec((1, 1, d), w_map),
                pl.BlockSpec(memory_space=pl.ANY),
                pl.BlockSpec(memory_space=pl.ANY),
            ],
            out_specs=pl.BlockSpec((br * rs // 2, LANES), row_map),
            scratch_shapes=[pltpu.VMEM((d, f2), _F32), pltpu.VMEM((f, d), _F32),
                            pltpu.VMEM((d, f2), _BF16), pltpu.VMEM((f, d), _BF16),
                            pltpu.SemaphoreType.DMA((2,))],
        ),
        out_shape=jax.ShapeDtypeStruct(xs.shape, jnp.uint32),
        compiler_params=pltpu.CompilerParams(
            dimension_semantics=("arbitrary",), vmem_limit_bytes=VMEM_LIMIT_BYTES),
        name="experts",
    )(block_expert, block_next, block_valid, n_used, xs, b_gu_grouped, b_dn, w_gu_all, w_dn_all)


def _combine_kernel(gates_ref, x1r_ref, g_ref, b_ref, *rest, rs, alpha):
    yk_refs, out_ref = rest[:TOP_K], rest[-1]
    out_ref[...] = _combine_rows(gates_ref, x1r_ref, g_ref, b_ref, yk_refs, rs, alpha)


def _combine_call(gates_t, x1r, ln_g, ln_b, yk, alpha, rs, out_tokens, token_lo, out_buf):
    n_tok = gates_t.shape[1]
    d = rs * LANES
    rp = rs // 2
    tb = min(COMBINE_BLOCK, n_tok)
    steps = n_tok // tb
    block_lo = token_lo // tb
    yk_specs = [pl.BlockSpec((tb * rp, LANES), lambda i, _k=k: (_k * steps + i, 0)) for k in range(TOP_K)]
    operands = [gates_t, x1r, ln_g.reshape(1, d), ln_b.reshape(1, d)] + [yk] * TOP_K
    in_specs = [
        pl.BlockSpec((2 * TOP_K, tb), lambda i: (0, i)),
        pl.BlockSpec((tb * rs, LANES), lambda i: (i, 0)),
        pl.BlockSpec((1, d), lambda i: (0, 0)),
        pl.BlockSpec((1, d), lambda i: (0, 0)),
    ] + yk_specs
    aliases = {}
    if out_buf is not None:
        aliases = {len(operands): 0}
        operands.append(out_buf)
        in_specs.append(pl.BlockSpec(memory_space=pl.ANY))
    return pl.pallas_call(
        functools.partial(_combine_kernel, rs=rs, alpha=alpha),
        grid=(steps,),
        in_specs=in_specs,
        out_specs=pl.BlockSpec((tb, d), lambda i: (block_lo + i, 0)),
        out_shape=jax.ShapeDtypeStruct((out_tokens, d), _F32),
        input_output_aliases=aliases,
        compiler_params=pltpu.CompilerParams(
            dimension_semantics=("arbitrary",), vmem_limit_bytes=VMEM_LIMIT_BYTES),
        name="combine",
    )(*operands)


def _routing_tables(meta_i, counts_f, n_blocks):
    n_exp = counts_f.shape[0]
    br = EXPERT_BLOCK_ROWS
    counts = counts_f[:, 0].astype(jnp.int32)
    padded = ((counts + br - 1) // br) * br
    pend = jnp.cumsum(padded)
    pstart = pend - padded
    eids = jnp.arange(n_exp, dtype=jnp.int32)
    idx, rank = meta_i[:TOP_K], meta_i[TOP_K:]
    dest = jnp.sum(jnp.where(idx[..., None] == eids, pstart, 0), axis=-1) + rank
    n_used = (pend[-1] // br).astype(jnp.int32)
    blk = jnp.minimum(jnp.arange(n_blocks, dtype=jnp.int32), n_used - 1)
    block_expert = jnp.minimum(jnp.sum((pend[None, :] <= (blk * br)[:, None]).astype(jnp.int32), axis=1),
                               n_exp - 1)
    group_end = jnp.sum(jnp.where(block_expert[:, None] == eids, pstart + counts, 0), axis=-1)
    block_valid = jnp.clip(group_end - blk * br, 0, br).astype(jnp.int32)
    later = (eids[None, :] > block_expert[:, None]) & (counts[None, :] > 0)
    block_next = jnp.min(jnp.where(later, eids[None, :], n_exp), axis=1)
    block_next = jnp.where(block_next == n_exp, -1, block_next).astype(jnp.int32)
    return dest, block_expert, block_next, block_valid, n_used.reshape(1)


def _moe_layer(x1p, meta_i, counts_f, layer, w_gu_all, b_gu, w_dn_all, b_dn, rs):
    _, n_exp, d, f2 = w_gu_all.shape
    n_tok = meta_i.shape[1]
    br = EXPERT_BLOCK_ROWS
    rp = rs // 2
    n_blocks = -(-(n_tok * TOP_K) // br) + n_exp
    dest, block_expert, block_next, block_valid, n_used = _routing_tables(meta_i, counts_f, n_blocks)
    n_rows = n_blocks * br
    xs = _sc_scatter_rows(x1p.reshape(n_tok, rp, LANES), dest.reshape(TOP_K, n_tok // SC_INDEX_ROW, SC_INDEX_ROW),
                          n_rows)
    half = GATE_UP_CHUNK // 2
    b_gu_grouped = b_gu.reshape(n_exp, f2 // GATE_UP_CHUNK, half, 2).transpose(0, 1, 3, 2).reshape(n_exp, 1, f2)
    ys = _expert_call(block_expert, block_next, block_valid, n_used, xs.reshape(n_rows * rp, LANES), layer, w_gu_all,
                      b_gu_grouped, w_dn_all, b_dn.reshape(n_exp, 1, d), rs)
    yk = _sc_gather_rows(ys.reshape(n_rows, rp, LANES),
                         dest.reshape(TOP_K * n_tok // SC_INDEX_ROW, SC_INDEX_ROW))
    return yk.reshape(TOP_K * n_tok * rp, LANES)


def kernel(x, pool_w, pool_scale, sc_w_in, sc_conv_w, sc_w_out, cf_w_in, cf_b_in, cf_dw_w, cf_dw_b,
           cf_ln_g, cf_ln_b, cf_w_out, cf_b_out, mix_ln_g, mix_ln_b, router_w, router_b,
           moe_w_gu, moe_b_gu, moe_w_dn, moe_b_dn, ffn_ln_g, ffn_ln_b):
    bsz, seq, d = x.shape
    depth = mix_ln_g.shape[0]
    alpha = (2.0 * depth) ** 0.25
    rs = d // LANES
    n_chains = BATCH_CHAINS if bsz % BATCH_CHAINS == 0 else 1
    cb = bsz // n_chains
    sources = [("x", x, c * cb) for c in range(n_chains)]
    ia = ib = ic = 0
    for layer in range(depth):
        kind = layer % 3
        route = (mix_ln_g[layer], mix_ln_b[layer], router_w[layer], router_b[layer])
        if kind == 0:
            mixer, halo = _pool_mix, POOL_HALO
            weights = [pool_w[ia].astype(_BF16), pool_scale[ia].reshape(1, d)]
            ia += 1
        elif kind == 1:
            mixer, halo = _short_conv_mix, SHORT_CONV_HALO
            weights = [sc_w_in[ib].astype(_BF16), sc_conv_w[ib], sc_w_out[ib].astype(_BF16)]
            ib += 1
        else:
            mixer, halo = _conformer_mix, CONFORMER_HALO
            weights = [cf_w_in[ic].astype(_BF16), cf_b_in[ic].reshape(1, 2 * d), cf_dw_w[ic],
                       cf_dw_b[ic].reshape(1, d), cf_ln_g[ic].reshape(1, d), cf_ln_b[ic].reshape(1, d),
                       cf_w_out[ic].astype(_BF16), cf_b_out[ic].reshape(1, d)]
            ic += 1
        routed = [_mixer_call(mixer, src, cb, seq, weights, *route, halo, alpha) for src in sources]
        sources = []
        for x1r, x1p, meta_i, gates_t, counts_f in routed:
            yk = _moe_layer(x1p, meta_i, counts_f, layer, moe_w_gu, moe_b_gu[layer], moe_w_dn, moe_b_dn[layer], rs)
            sources.append(("moe", gates_t, x1r, ffn_ln_g[layer], ffn_ln_b[layer], yk))
    out = None
    for c, (_, gates_t, x1r, ln_g, ln_b, yk) in enumerate(sources):
        out = _combine_call(gates_t, x1r, ln_g, ln_b, yk, alpha, rs, bsz * seq, c * cb * seq, out)
    return out.reshape(bsz, seq, d)
```

```python
import functools

import jax
import jax.numpy as jnp
from jax import lax
from jax.experimental import pallas as pl
from jax.experimental.pallas import tpu as pltpu
from jax.experimental.pallas import tpu_sc as plsc

LANES = 128
SUBLANES = 8
TOP_K = 4
POOL_WINDOWS = (2, 4, 8, 16)
POOL_HALO = 16
SHORT_CONV_HALO = 8
CONFORMER_HALO = 32
SWIGLU_LIMIT = 7.0
SWIGLU_ALPHA = 1.702
LN_EPS = 1e-5
TOKEN_BLOCK = 512
EXPERT_BLOCK_ROWS = 2048
EXPERT_SUB_ROWS = 256
COMBINE_BLOCK = 512
BATCH_CHAINS = 2
SC_INDEX_ROW = 128
SC_CHUNK_ROWS = 32
GATE_UP_CHUNK = 2 * LANES
VMEM_LIMIT_BYTES = 56 * 1024 * 1024

_F32 = jnp.float32
_BF16 = jnp.bfloat16


def _layer_norm(z, g, b):
    mu = jnp.mean(z, axis=-1, keepdims=True)
    zc = z - mu
    var = jnp.mean(zc * zc, axis=-1, keepdims=True)
    return zc * lax.rsqrt(var + LN_EPS) * g + b


def _store_rows(row_ref, val, row0=0):
    rows, d = val.shape
    rs = d // LANES
    for j in range(rs):
        row_ref[pl.ds(row0 * rs + j, rows, stride=rs), :] = val[:, j * LANES:(j + 1) * LANES]


def _load_rows(row_ref, rows, rs, row0=0):
    return [row_ref[pl.ds(row0 * rs + j, rows, stride=rs), :] for j in range(rs)]


def _post_norm_and_route(z, g_ref, b_ref, rwt_ref, rb_ref, first,
                         x1r_ref, x1p_ref, mi_ref, mg_ref, cnt_ref, carry_ref):
    n_tok = z.shape[0]
    n_exp = rwt_ref.shape[0]

    @pl.when(first)
    def _():
        carry_ref[...] = jnp.zeros_like(carry_ref)

    x1 = _layer_norm(z, g_ref[...], b_ref[...])
    _store_rows(x1r_ref, x1)
    _store_rows(x1p_ref, _pack_bf16_pairs(x1))

    x_hi = x1.astype(_BF16)
    x_lo = (x1 - x_hi.astype(_F32)).astype(_BF16)
    w = rwt_ref[...]
    w_hi = w.astype(_BF16)
    w_lo = (w - w_hi.astype(_F32)).astype(_BF16)
    nt = (((1,), (1,)), ((), ()))
    logits = (lax.dot_general(w_hi, x_hi, nt, preferred_element_type=_F32)
              + lax.dot_general(w_hi, x_lo, nt, preferred_element_type=_F32)
              + lax.dot_general(w_lo, x_hi, nt, preferred_element_type=_F32)) + rb_ref[...]
    eidx = lax.broadcasted_iota(jnp.int32, logits.shape, 0)
    work = logits
    chosen = jnp.zeros(logits.shape, jnp.bool_)
    vals, idxs = [], []
    for _ in range(TOP_K):
        m = jnp.max(work, axis=0, keepdims=True)
        sel = jnp.min(jnp.where(work == m, eidx, n_exp), axis=0, keepdims=True)
        hit = eidx == sel
        vals.append(m)
        idxs.append(sel)
        chosen = jnp.logical_or(chosen, hit)
        work = jnp.where(hit, -jnp.inf, work)
    exps = [jnp.exp(v - vals[0]) for v in vals]
    denom = exps[0] + exps[1] + exps[2] + exps[3]
    gate_rows = [e / denom for e in exps] + [jnp.zeros_like(denom)] * (mg_ref.shape[0] - TOP_K)
    mg_ref[...] = jnp.concatenate(gate_rows, axis=0)

    onehot = chosen.astype(_BF16)
    r = lax.broadcasted_iota(jnp.int32, (n_tok, n_tok), 0)
    c = lax.broadcasted_iota(jnp.int32, (n_tok, n_tok), 1)
    before = (r < c).astype(_BF16)
    cum = jnp.dot(onehot, before, preferred_element_type=_F32) + carry_ref[:, 0:1]
    ranks = [jnp.sum(jnp.where(eidx == s, cum, 0.0), axis=0, keepdims=True) for s in idxs]
    mi_ref[...] = jnp.concatenate(idxs + [rk.astype(jnp.int32) for rk in ranks], axis=0)
    carry_ref[...] = carry_ref[...] + jnp.sum(chosen.astype(_F32), axis=1, keepdims=True)
    cnt_ref[...] = carry_ref[...]


def _pool_mix(x, si, weights, scratch, alpha):
    pw_ref, ps_ref = weights
    (hist_ref,) = scratch
    ts, d = x.shape
    dg = d // len(POOL_WINDOWS)

    @pl.when(si == 0)
    def _():
        hist_ref[0:POOL_HALO, :] = jnp.zeros((POOL_HALO, d), _F32)

    hist_ref[POOL_HALO:POOL_HALO + ts, :] = x
    pos = si * ts + lax.broadcasted_iota(jnp.int32, (ts, 1), 0)
    pieces = []
    for gi, win in enumerate(POOL_WINDOWS):
        c0 = gi * dg
        xg = x[:, c0:c0 + dg]
        assert win & (win - 1) == 0 and win <= POOL_HALO
        ext = hist_ref[:, c0:c0 + dg]
        span = 1
        while span < win:
            ext = ext + pltpu.roll(ext, span, 0)
            span *= 2
        acc = ext[POOL_HALO:, :]
        inv_count = 1.0 / jnp.minimum(pos + 1, win).astype(_F32)
        diff = acc * inv_count - xg
        hg = jnp.dot(diff.astype(_BF16), pw_ref[gi], preferred_element_type=_F32)
        pieces.append(alpha * xg + hg * ps_ref[:, c0:c0 + dg])
    hist_ref[0:POOL_HALO, :] = x[ts - POOL_HALO:, :]
    return jnp.concatenate(pieces, axis=1)


def _short_conv_mix(x, si, weights, scratch, alpha):
    win_ref, cw_ref, wout_ref = weights
    (hist_ref,) = scratch
    ts, d = x.shape
    halo = SHORT_CONV_HALO

    @pl.when(si == 0)
    def _():
        hist_ref[0:halo, :] = jnp.zeros((halo, d), _F32)

    xb = x.astype(_BF16)
    gate_b = jnp.dot(xb, win_ref[:, 0:d], preferred_element_type=_F32)
    gate_c = jnp.dot(xb, win_ref[:, d:2 * d], preferred_element_type=_F32)
    h = jnp.dot(xb, win_ref[:, 2 * d:3 * d], preferred_element_type=_F32)
    v = gate_c * h
    hist_ref[halo:halo + ts, :] = v
    width = cw_ref.shape[0]
    u = cw_ref[width - 1:width, :] * v
    for k in range(width - 1):
        shift = width - 1 - k
        u = u + cw_ref[k:k + 1, :] * hist_ref[halo - shift:halo - shift + ts, :]
    hist_ref[0:halo, :] = v[ts - halo:, :]
    y = jnp.dot((gate_b * u).astype(_BF16), wout_ref[...], preferred_element_type=_F32)
    return alpha * x + y


def _conformer_mix(x, si, weights, scratch, alpha):
    win_ref, bin_ref, dww_ref, dwb_ref, lng_ref, lnb_ref, wout_ref, bout_ref = weights
    (hist_ref,) = scratch
    ts, d = x.shape
    halo = CONFORMER_HALO

    @pl.when(si == 0)
    def _():
        hist_ref[0:halo, :] = jnp.zeros((halo, d), _F32)

    xb = x.astype(_BF16)
    a = jnp.dot(xb, win_ref[:, 0:d], preferred_element_type=_F32) + bin_ref[:, 0:d]
    gate = jnp.dot(xb, win_ref[:, d:2 * d], preferred_element_type=_F32) + bin_ref[:, d:2 * d]
    u = a * jax.nn.sigmoid(gate)
    hist_ref[halo:halo + ts, :] = u
    width = dww_ref.shape[0]
    hist = hist_ref[...]
    acc = dwb_ref[...]
    for r in range(SUBLANES):
        rolled = hist if r == 0 else pltpu.roll(hist, r, 0)
        for q in range(halo // SUBLANES):
            shift = SUBLANES * q + r
            if shift < width:
                k = width - 1 - shift
                start = halo - SUBLANES * q
                acc = acc + dww_ref[k:k + 1, :] * rolled[start:start + ts, :]
    hist_ref[0:halo, :] = u[ts - halo:, :]
    un = _layer_norm(acc, lng_ref[...], lnb_ref[...])
    un = un * jax.nn.sigmoid(un)
    y = jnp.dot(un.astype(_BF16), wout_ref[...], preferred_element_type=_F32) + bout_ref[...]
    return alpha * x + y


def _combine_rows(gates_ref, x1r_ref, g_ref, b_ref, yk_refs, rs, alpha):
    tb = gates_ref.shape[1]
    rp = rs // 2
    gates = gates_ref[...].T
    gate_cols = [jnp.broadcast_to(gates[:, k:k + 1], (tb, LANES)) for k in range(TOP_K)]
    lo_pieces, hi_pieces = [], []
    for j in range(rp):
        lo = alpha * x1r_ref[pl.ds(j, tb, stride=rs), :]
        hi = alpha * x1r_ref[pl.ds(rp + j, tb, stride=rs), :]
        for k in range(TOP_K):
            y_lo, y_hi = _unpack_bf16_pairs(yk_refs[k][pl.ds(j, tb, stride=rp), :])
            lo = lo + gate_cols[k] * y_lo
            hi = hi + gate_cols[k] * y_hi
        lo_pieces.append(lo)
        hi_pieces.append(hi)
    return _layer_norm(jnp.concatenate(lo_pieces + hi_pieces, axis=1), g_ref[...], b_ref[...])


def _mixer_kernel(*refs, mix_fn, n_weights, from_moe, alpha, rs):
    bi, si = pl.program_id(0), pl.program_id(1)
    if from_moe:
        gates_ref, xprev_ref, cg_ref, cb_ref = refs[:4]
        x = _combine_rows(gates_ref, xprev_ref, cg_ref, cb_ref, refs[4:4 + TOP_K], rs, alpha)
        refs = refs[4 + TOP_K:]
    else:
        x = refs[0][0]
        refs = refs[1:]
    weights, refs = refs[:n_weights], refs[n_weights:]
    g_ref, b_ref, rwt_ref, rb_ref = refs[:4]
    x1r_ref, x1p_ref, mi_ref, mg_ref, cnt_ref = refs[4:9]
    scratch, carry_ref = refs[9:-1], refs[-1]
    z = mix_fn(x, si, weights, scratch, alpha)
    _post_norm_and_route(z, g_ref, b_ref, rwt_ref, rb_ref, (bi == 0) & (si == 0),
                         x1r_ref, x1p_ref, mi_ref, mg_ref, cnt_ref, carry_ref)


def _mixer_call(mix_fn, source, bsz, seq, weights, ln_g, ln_b, router_w, router_b, halo, extra_scratch, alpha):
    d = router_w.shape[0]
    n_exp = router_w.shape[1]
    ts = min(TOKEN_BLOCK, seq)
    rs = d // LANES
    rp = rs // 2
    n_tok = bsz * seq
    nsb = seq // ts

    def full(a):
        nd = a.ndim
        return pl.BlockSpec(a.shape, lambda bi, si, _nd=nd: (0,) * _nd)

    small = [ln_g.reshape(1, d), ln_b.reshape(1, d), router_w.T, router_b.reshape(n_exp, 1)]
    tail = list(weights) + small
    if source[0] == "x":
        _, x, batch_lo = source
        operands = [x] + tail
        in_specs = [pl.BlockSpec((1, ts, d), lambda bi, si: (batch_lo + bi, si, 0))]
    else:
        _, gates_t, x1r, cg, cb, yk = source
        steps = n_tok // ts
        operands = [gates_t, x1r, cg.reshape(1, d), cb.reshape(1, d)] + [yk] * TOP_K + tail
        in_specs = [
            pl.BlockSpec((2 * TOP_K, ts), lambda bi, si: (0, bi * nsb + si)),
            pl.BlockSpec((ts * rs, LANES), lambda bi, si: (bi * nsb + si, 0)),
            pl.BlockSpec((1, d), lambda bi, si: (0, 0)),
            pl.BlockSpec((1, d), lambda bi, si: (0, 0)),
        ] + [pl.BlockSpec((ts * rp, LANES), lambda bi, si, _k=k: (_k * steps + bi * nsb + si, 0))
             for k in range(TOP_K)]
    in_specs = in_specs + [full(a) for a in tail]
    tok_map = lambda bi, si: (0, bi * nsb + si)
    out_shape = [
        jax.ShapeDtypeStruct((n_tok * rs, LANES), _F32),
        jax.ShapeDtypeStruct((n_tok * rs // 2, LANES), jnp.uint32),
        jax.ShapeDtypeStruct((2 * TOP_K, n_tok), jnp.int32),
        jax.ShapeDtypeStruct((2 * TOP_K, n_tok), _F32),
        jax.ShapeDtypeStruct((n_exp, LANES), _F32),
    ]
    out_specs = [
        pl.BlockSpec((ts * rs, LANES), lambda bi, si: (bi * nsb + si, 0)),
        pl.BlockSpec((ts * rs // 2, LANES), lambda bi, si: (bi * nsb + si, 0)),
        pl.BlockSpec((2 * TOP_K, ts), tok_map),
        pl.BlockSpec((2 * TOP_K, ts), tok_map),
        pl.BlockSpec((n_exp, LANES), lambda bi, si: (0, 0)),
    ]
    return pl.pallas_call(
        functools.partial(_mixer_kernel, mix_fn=mix_fn, n_weights=len(weights), from_moe=source[0] == "moe",
                          alpha=alpha, rs=rs),
        grid=(bsz, nsb),
        in_specs=in_specs,
        out_specs=out_specs,
        out_shape=out_shape,
        scratch_shapes=[pltpu.VMEM((halo + ts, d), _F32)] + [pltpu.VMEM(shape, _F32) for shape in extra_scratch]
        + [pltpu.VMEM((n_exp, LANES), _F32)],
        compiler_params=pltpu.CompilerParams(
            dimension_semantics=("arbitrary", "arbitrary"), vmem_limit_bytes=VMEM_LIMIT_BYTES),
        name=mix_fn.__name__.strip("_"),
    )(*operands)


def _sc_workers():
    info = plsc.get_sparse_core_info()
    return info.num_cores, info.num_subcores, info.num_lanes


def _sc_scatter_rows(x3, dest3, n_rows):
    n_cores, n_sub, n_lanes = _sc_workers()
    n_tok, rs, _ = x3.shape
    top_k = dest3.shape[0]
    chunk = SC_CHUNK_ROWS * (LANES * SUBLANES) // (rs * LANES)
    tok_per_w = n_tok // (n_cores * n_sub)
    rows_per_w = tok_per_w // SC_INDEX_ROW
    chunks_per_row = SC_INDEX_ROW // chunk
    assert rows_per_w * SC_INDEX_ROW * n_cores * n_sub == n_tok and chunks_per_row % 2 == 0
    mesh = plsc.VectorSubcoreMesh(core_axis_name="core", subcore_axis_name="subcore")

    @pl.kernel(out_type=jax.ShapeDtypeStruct((n_rows, rs, LANES), x3.dtype), mesh=mesh,
               scratch_types=[pltpu.VMEM((top_k, rows_per_w, SC_INDEX_ROW), jnp.int32),
                              pltpu.VMEM((2, chunk, rs, LANES), x3.dtype),
                              pltpu.SemaphoreType.DMA((2,)), pltpu.SemaphoreType.DMA((2,))])
    def scatter_kernel(x_hbm, d_hbm, o_hbm, idx_v, buf, rsem, ssem):
        wid = lax.axis_index("subcore") * n_cores + lax.axis_index("core")
        for k in range(top_k):
            pltpu.sync_copy(d_hbm.at[k, pl.ds(wid * rows_per_w, rows_per_w)], idx_v.at[k])
        base = wid * tok_per_w

        def read(j, c, slot):
            return pltpu.make_async_copy(x_hbm.at[pl.ds(base + j * SC_INDEX_ROW + c * chunk, chunk)],
                                         buf.at[slot], rsem.at[slot])

        def scatters(j, c, slot):
            copies = []
            for k in range(top_k):
                for h in range(chunk // n_lanes):
                    rows = idx_v[k, j, pl.ds(c * chunk + h * n_lanes, n_lanes)]
                    copies.append(pltpu.make_async_copy(buf.at[slot, pl.ds(h * n_lanes, n_lanes)],
                                                        o_hbm.at[rows], ssem.at[slot]))
            return copies

        def wait_scatters(j, c, slot):
            for cp in scatters(j, c, slot):
                cp.wait()

        read(0, 0, 0).start()

        def per_index_row(j, carry):
            for c in range(chunks_per_row):
                slot = c % 2
                if c == 0:
                    @pl.when(j > 0)
                    def _():
                        wait_scatters(j - 1, chunks_per_row - 1, 1 - slot)
                    read(j, c + 1, 1 - slot).start()
                elif c < chunks_per_row - 1:
                    wait_scatters(j, c - 1, 1 - slot)
                    read(j, c + 1, 1 - slot).start()
                else:
                    @pl.when(j + 1 < rows_per_w)
                    def _():
                        wait_scatters(j, c - 1, 1 - slot)
                        read(j + 1, 0, 1 - slot).start()
                read(j, c, slot).wait()
                for cp in scatters(j, c, slot):
                    cp.start()
            return carry

        lax.fori_loop(0, rows_per_w, per_index_row, 0)
        wait_scatters(rows_per_w - 1, chunks_per_row - 2, 0)
        wait_scatters(rows_per_w - 1, chunks_per_row - 1, 1)

    return scatter_kernel(x3, dest3)


def _sc_gather_rows(table3, idx2):
    n_cores, n_sub, _ = _sc_workers()
    n_idx_rows = idx2.shape[0]
    rs = table3.shape[1]
    chunk = SC_CHUNK_ROWS * (LANES * SUBLANES) // (rs * LANES)
    rows_per_w = n_idx_rows // (n_cores * n_sub)
    chunks_per_row = SC_INDEX_ROW // chunk
    assert rows_per_w * n_cores * n_sub == n_idx_rows and chunks_per_row % 2 == 0
    mesh = plsc.VectorSubcoreMesh(core_axis_name="core", subcore_axis_name="subcore")

    @pl.kernel(out_type=jax.ShapeDtypeStruct((n_idx_rows * SC_INDEX_ROW, rs, LANES), table3.dtype), mesh=mesh,
               scratch_types=[pltpu.VMEM((rows_per_w, SC_INDEX_ROW), jnp.int32),
                              pltpu.VMEM((2, chunk, rs, LANES), table3.dtype),
                              pltpu.SemaphoreType.DMA((2,)), pltpu.SemaphoreType.DMA((2,))])
    def gather_kernel(t_hbm, i_hbm, o_hbm, idx_v, buf, gsem, wsem):
        wid = lax.axis_index("subcore") * n_cores + lax.axis_index("core")
        pltpu.sync_copy(i_hbm.at[pl.ds(wid * rows_per_w, rows_per_w)], idx_v)
        base = wid * rows_per_w * SC_INDEX_ROW

        def gather(j, c, slot):
            return pltpu.make_async_copy(t_hbm.at[idx_v.at[j, pl.ds(c * chunk, chunk)]], buf.at[slot], gsem.at[slot])

        def write(j, c, slot):
            return pltpu.make_async_copy(buf.at[slot], o_hbm.at[pl.ds(base + j * SC_INDEX_ROW + c * chunk, chunk)],
                                         wsem.at[slot])

        gather(0, 0, 0).start()

        def per_index_row(j, carry):
            for c in range(chunks_per_row):
                slot = c % 2
                if c == 0:
                    @pl.when(j > 0)
                    def _():
                        write(j - 1, chunks_per_row - 1, 1 - slot).wait()
                    gather(j, c + 1, 1 - slot).start()
                elif c < chunks_per_row - 1:
                    write(j, c - 1, 1 - slot).wait()
                    gather(j, c + 1, 1 - slot).start()
                else:
                    @pl.when(j + 1 < rows_per_w)
                    def _():
                        write(j, c - 1, 1 - slot).wait()
                        gather(j + 1, 0, 1 - slot).start()
                gather(j, c, slot).wait()
                write(j, c, slot).start()
            return carry

        lax.fori_loop(0, rows_per_w, per_index_row, 0)
        write(rows_per_w - 1, chunks_per_row - 2, 0).wait()
        write(rows_per_w - 1, chunks_per_row - 1, 1).wait()

    return gather_kernel(table3, idx2)


def _pack_bf16_pairs(v):
    half = v.shape[1] // 2
    lo = lax.bitcast_convert_type(v[:, :half].astype(_BF16).astype(_F32), jnp.uint32)
    hi = lax.bitcast_convert_type(v[:, half:].astype(_BF16).astype(_F32), jnp.uint32)
    return (lo >> 16) | (hi & jnp.uint32(0xFFFF0000))


def _unpack_bf16_pairs(w):
    return (lax.bitcast_convert_type(w << 16, _F32),
            lax.bitcast_convert_type(w & jnp.uint32(0xFFFF0000), _F32))


def _expert_kernel(be_ref, nx_ref, nv_ref, nu_ref, xs_ref, bgu_ref, bdn_ref, wgu_hbm, wdn_hbm, ys_ref,
                   wgu_f, wdn_f, wgu_s, wdn_s, sem, *, rs, layer):
    b = pl.program_id(0)
    n_chunks = wgu_f.shape[1] // GATE_UP_CHUNK
    half = GATE_UP_CHUNK // 2

    def fetch(e):
        return (pltpu.make_async_copy(wgu_hbm.at[layer, e], wgu_f, sem.at[0]),
                pltpu.make_async_copy(wdn_hbm.at[layer, e], wdn_f, sem.at[1]))

    @pl.when(b < nu_ref[0])
    def _():
        @pl.when((b == 0) | (be_ref[b] != be_ref[jnp.maximum(b - 1, 0)]))
        def _():
            @pl.when(b == 0)
            def _():
                for cp in fetch(be_ref[0]):
                    cp.start()

            for cp in fetch(be_ref[b]):
                cp.wait()
            r = lax.broadcasted_iota(jnp.int32, (GATE_UP_CHUNK, GATE_UP_CHUNK), 0)
            c = lax.broadcasted_iota(jnp.int32, (GATE_UP_CHUNK, GATE_UP_CHUNK), 1)
            perm = (r == jnp.where(c < half, 2 * c, 2 * (c - half) + 1)).astype(_BF16)
            for ch in range(n_chunks):
                cols = slice(ch * GATE_UP_CHUNK, (ch + 1) * GATE_UP_CHUNK)
                w = wgu_f[:, cols].astype(_BF16)
                wgu_s[:, cols] = jnp.dot(w, perm, preferred_element_type=_F32).astype(_BF16)
            wdn_s[...] = wdn_f[...].astype(_BF16)

            @pl.when(nx_ref[b] >= 0)
            def _():
                for cp in fetch(nx_ref[b]):
                    cp.start()

        n_valid = nv_ref[b]
        sub = EXPERT_SUB_ROWS

        def sub_block(s):
            row0 = pl.multiple_of(s * sub, sub)
            defined = row0 + lax.broadcasted_iota(jnp.int32, (sub, 1), 0) < n_valid
            words = [jnp.where(defined, w, jnp.uint32(0)) for w in _load_rows(xs_ref, sub, rs // 2, row0)]
            halves = [_unpack_bf16_pairs(w) for w in words]
            x = jnp.concatenate([lo for lo, _ in halves] + [hi for _, hi in halves], axis=1).astype(_BF16)
            h = jnp.dot(x, wgu_s[...], preferred_element_type=_F32) + bgu_ref[0]
            acts = []
            for ch in range(n_chunks):
                g = jnp.minimum(h[:, ch * GATE_UP_CHUNK:ch * GATE_UP_CHUNK + half], SWIGLU_LIMIT)
                up = jnp.clip(h[:, ch * GATE_UP_CHUNK + half:(ch + 1) * GATE_UP_CHUNK], -SWIGLU_LIMIT, SWIGLU_LIMIT)
                acts.append(((up + 1.0) * (g * jax.nn.sigmoid(SWIGLU_ALPHA * g))).astype(_BF16))
            y = jnp.dot(jnp.concatenate(acts, axis=1), wdn_s[...], preferred_element_type=_F32) + bdn_ref[0]
            _store_rows(ys_ref, _pack_bf16_pairs(y), row0)

        n_sub = (n_valid + sub - 1) // sub

        def pair(i, carry):
            sub_block(2 * i)
            sub_block(2 * i + 1)
            return carry

        lax.fori_loop(0, n_sub // 2, pair, 0)

        @pl.when(n_sub % 2 == 1)
        def _():
            sub_block(n_sub - 1)


def _expert_call(block_expert, block_next, block_valid, n_used, xs, layer, w_gu_all, b_gu_grouped, w_dn_all,
                 b_dn, rs):
    _, n_exp, d, f2 = w_gu_all.shape
    f = f2 // 2
    br = EXPERT_BLOCK_ROWS
    n_blocks = xs.shape[0] // (br * rs // 2)
    assert f2 % GATE_UP_CHUNK == 0 and rs % 2 == 0

    def row_map(b, be, nx, nv, nu):
        return (jnp.minimum(b, nu[0] - 1), 0)

    def w_map(b, be, nx, nv, nu):
        return (be[b], 0, 0)

    return pl.pallas_call(
        functools.partial(_expert_kernel, rs=rs, layer=layer),
        grid_spec=pltpu.PrefetchScalarGridSpec(
            num_scalar_prefetch=4,
            grid=(n_blocks,),
            in_specs=[
                pl.BlockSpec((br * rs // 2, LANES), row_map),
                pl.BlockSpec((1, 1, f2), w_map),
                pl.BlockSpec((1, 1, d), w_map),
                pl.BlockSpec(memory_space=pl.ANY),
                pl.BlockSpec(memory_space=pl.ANY),
            ],
            out_specs=pl.BlockSpec((br * rs // 2, LANES), row_map),
            scratch_shapes=[pltpu.VMEM((d, f2), _F32), pltpu.VMEM((f, d), _F32),
                            pltpu.VMEM((d, f2), _BF16), pltpu.VMEM((f, d), _BF16),
                            pltpu.SemaphoreType.DMA((2,))],
        ),
        out_shape=jax.ShapeDtypeStruct(xs.shape, jnp.uint32),
        compiler_params=pltpu.CompilerParams(
            dimension_semantics=("arbitrary",), vmem_limit_bytes=VMEM_LIMIT_BYTES),
        name="experts",
    )(block_expert, block_next, block_valid, n_used, xs, b_gu_grouped, b_dn, w_gu_all, w_dn_all)


def _combine_kernel(gates_ref, x1r_ref, g_ref, b_ref, *rest, rs, alpha):
    yk_refs, out_ref = rest[:TOP_K], rest[-1]
    out_ref[...] = _combine_rows(gates_ref, x1r_ref, g_ref, b_ref, yk_refs, rs, alpha)


def _combine_call(gates_t, x1r, ln_g, ln_b, yk, alpha, rs, out_tokens, token_lo, out_buf):
    n_tok = gates_t.shape[1]
    d = rs * LANES
    rp = rs // 2
    tb = min(COMBINE_BLOCK, n_tok)
    steps = n_tok // tb
    block_lo = token_lo // tb
    yk_specs = [pl.BlockSpec((tb * rp, LANES), lambda i, _k=k: (_k * steps + i, 0)) for k in range(TOP_K)]
    operands = [gates_t, x1r, ln_g.reshape(1, d), ln_b.reshape(1, d)] + [yk] * TOP_K
    in_specs = [
        pl.BlockSpec((2 * TOP_K, tb), lambda i: (0, i)),
        pl.BlockSpec((tb * rs, LANES), lambda i: (i, 0)),
        pl.BlockSpec((1, d), lambda i: (0, 0)),
        pl.BlockSpec((1, d), lambda i: (0, 0)),
    ] + yk_specs
    aliases = {}
    if out_buf is not None:
        aliases = {len(operands): 0}
        operands.append(out_buf)
        in_specs.append(pl.BlockSpec(memory_space=pl.ANY))
    return pl.pallas_call(
        functools.partial(_combine_kernel, rs=rs, alpha=alpha),
        grid=(steps,),
        in_specs=in_specs,
        out_specs=pl.BlockSpec((tb, d), lambda i: (block_lo + i, 0)),
        out_shape=jax.ShapeDtypeStruct((out_tokens, d), _F32),
        input_output_aliases=aliases,
        compiler_params=pltpu.CompilerParams(
            dimension_semantics=("arbitrary",), vmem_limit_bytes=VMEM_LIMIT_BYTES),
        name="combine",
    )(*operands)


def _routing_tables(meta_i, counts_f, n_blocks):
    n_exp = counts_f.shape[0]
    br = EXPERT_BLOCK_ROWS
    counts = counts_f[:, 0].astype(jnp.int32)
    padded = ((counts + br - 1) // br) * br
    pend = jnp.cumsum(padded)
    pstart = pend - padded
    eids = jnp.arange(n_exp, dtype=jnp.int32)
    idx, rank = meta_i[:TOP_K], meta_i[TOP_K:]
    dest = jnp.sum(jnp.where(idx[..., None] == eids, pstart, 0), axis=-1) + rank
    n_used = (pend[-1] // br).astype(jnp.int32)
    blk = jnp.minimum(jnp.arange(n_blocks, dtype=jnp.int32), n_used - 1)
    block_expert = jnp.minimum(jnp.sum((pend[None, :] <= (blk * br)[:, None]).astype(jnp.int32), axis=1),
                               n_exp - 1)
    group_end = jnp.sum(jnp.where(block_expert[:, None] == eids, pstart + counts, 0), axis=-1)
    block_valid = jnp.clip(group_end - blk * br, 0, br).astype(jnp.int32)
    later = (eids[None, :] > block_expert[:, None]) & (counts[None, :] > 0)
    block_next = jnp.min(jnp.where(later, eids[None, :], n_exp), axis=1)
    block_next = jnp.where(block_next == n_exp, -1, block_next).astype(jnp.int32)
    return dest, block_expert, block_next, block_valid, n_used.reshape(1)


def _moe_layer(x1p, meta_i, counts_f, layer, w_gu_all, b_gu, w_dn_all, b_dn, rs):
    _, n_exp, d, f2 = w_gu_all.shape
    n_tok = meta_i.shape[1]
    br = EXPERT_BLOCK_ROWS
    rp = rs // 2
    n_blocks = -(-(n_tok * TOP_K) // br) + n_exp
    dest, block_expert, block_next, block_valid, n_used = _routing_tables(meta_i, counts_f, n_blocks)
    n_rows = n_blocks * br
    xs = _sc_scatter_rows(x1p.reshape(n_tok, rp, LANES), dest.reshape(TOP_K, n_tok // SC_INDEX_ROW, SC_INDEX_ROW),
                          n_rows)
    half = GATE_UP_CHUNK // 2
    b_gu_grouped = b_gu.reshape(n_exp, f2 // GATE_UP_CHUNK, half, 2).transpose(0, 1, 3, 2).reshape(n_exp, 1, f2)
    ys = _expert_call(block_expert, block_next, block_valid, n_used, xs.reshape(n_rows * rp, LANES), layer, w_gu_all,
                      b_gu_grouped, w_dn_all, b_dn.reshape(n_exp, 1, d), rs)
    yk = _sc_gather_rows(ys.reshape(n_rows, rp, LANES),
                         dest.reshape(TOP_K * n_tok // SC_INDEX_ROW, SC_INDEX_ROW))
    return yk.reshape(TOP_K * n_tok * rp, LANES)


def kernel(x, pool_w, pool_scale, sc_w_in, sc_conv_w, sc_w_out, cf_w_in, cf_b_in, cf_dw_w, cf_dw_b,
           cf_ln_g, cf_ln_b, cf_w_out, cf_b_out, mix_ln_g, mix_ln_b, router_w, router_b,
           moe_w_gu, moe_b_gu, moe_w_dn, moe_b_dn, ffn_ln_g, ffn_ln_b):
    bsz, seq, d = x.shape
    depth = mix_ln_g.shape[0]
    alpha = (2.0 * depth) ** 0.25
    rs = d // LANES
    n_chains = BATCH_CHAINS if bsz % BATCH_CHAINS == 0 else 1
    cb = bsz // n_chains
    sources = [("x", x, c * cb) for c in range(n_chains)]
    ia = ib = ic = 0
    for layer in range(depth):
        kind = layer % 3
        route = (mix_ln_g[layer], mix_ln_b[layer], router_w[layer], router_b[layer])
        if kind == 0:
            mixer, halo, extra = _pool_mix, POOL_HALO, []
            weights = [pool_w[ia].astype(_BF16), pool_scale[ia].reshape(1, d)]
            ia += 1
        elif kind == 1:
            mixer, halo, extra = _short_conv_mix, SHORT_CONV_HALO, []
            weights = [sc_w_in[ib].astype(_BF16), sc_conv_w[ib], sc_w_out[ib].astype(_BF16)]
            ib += 1
        else:
            mixer, halo, extra = _conformer_mix, CONFORMER_HALO, []
            weights = [cf_w_in[ic].astype(_BF16), cf_b_in[ic].reshape(1, 2 * d), cf_dw_w[ic],
                       cf_dw_b[ic].reshape(1, d), cf_ln_g[ic].reshape(1, d), cf_ln_b[ic].reshape(1, d),
                       cf_w_out[ic].astype(_BF16), cf_b_out[ic].reshape(1, d)]
            ic += 1
        routed = [_mixer_call(mixer, src, cb, seq, weights, *route, halo, extra, alpha) for src in sources]
        sources = []
        for x1r, x1p, meta_i, gates_t, counts_f in routed:
            yk = _moe_layer(x1p, meta_i, counts_f, layer, moe_w_gu, moe_b_gu[layer], moe_w_dn, moe_b_dn[layer], rs)
            sources.append(("moe", gates_t, x1r, ffn_ln_g[layer], ffn_ln_b[layer], yk))
    out = None
    for c, (_, gates_t, x1r, ln_g, ln_b, yk) in enumerate(sources):
        out = _combine_call(gates_t, x1r, ln_g, ln_b, yk, alpha, rs, bsz * seq, c * cb * seq, out)
    return out.reshape(bsz, seq, d)
```

```python
import functools

import jax
import jax.numpy as jnp
from jax import lax
from jax.experimental import pallas as pl
from jax.experimental.pallas import tpu as pltpu
from jax.experimental.pallas import tpu_sc as plsc

LANES = 128
SUBLANES = 8
TOP_K = 4
POOL_WINDOWS = (2, 4, 8, 16)
POOL_HALO = 16
SHORT_CONV_HALO = 8
CONFORMER_HALO = 32
SWIGLU_LIMIT = 7.0
SWIGLU_ALPHA = 1.702
LN_EPS = 1e-5
TOKEN_BLOCK = 512
EXPERT_BLOCK_ROWS = 2048
EXPERT_SUB_ROWS = 256
COMBINE_BLOCK = 512
BATCH_CHAINS = 2
SC_INDEX_ROW = 128
SC_CHUNK_ROWS = 32
GATE_UP_CHUNK = 2 * LANES
VMEM_LIMIT_BYTES = 56 * 1024 * 1024

_F32 = jnp.float32
_BF16 = jnp.bfloat16


def _layer_norm(z, g, b):
    mu = jnp.mean(z, axis=-1, keepdims=True)
    zc = z - mu
    var = jnp.mean(zc * zc, axis=-1, keepdims=True)
    return zc * lax.rsqrt(var + LN_EPS) * g + b


def _store_rows(row_ref, val, row0=0):
    rows, d = val.shape
    rs = d // LANES
    for j in range(rs):
        row_ref[pl.ds(row0 * rs + j, rows, stride=rs), :] = val[:, j * LANES:(j + 1) * LANES]


def _load_rows(row_ref, rows, rs, row0=0):
    return [row_ref[pl.ds(row0 * rs + j, rows, stride=rs), :] for j in range(rs)]


def _post_norm_and_route(z, g_ref, b_ref, rwt_ref, rb_ref, first,
                         x1r_ref, x1p_ref, mi_ref, mg_ref, cnt_ref, carry_ref):
    n_tok = z.shape[0]
    n_exp = rwt_ref.shape[0]

    @pl.when(first)
    def _():
        carry_ref[...] = jnp.zeros_like(carry_ref)

    x1 = _layer_norm(z, g_ref[...], b_ref[...])
    _store_rows(x1r_ref, x1)
    _store_rows(x1p_ref, _pack_bf16_pairs(x1))

    x_hi = x1.astype(_BF16)
    x_lo = (x1 - x_hi.astype(_F32)).astype(_BF16)
    w = rwt_ref[...]
    w_hi = w.astype(_BF16)
    w_lo = (w - w_hi.astype(_F32)).astype(_BF16)
    nt = (((1,), (1,)), ((), ()))
    logits = (lax.dot_general(w_hi, x_hi, nt, preferred_element_type=_F32)
              + lax.dot_general(w_hi, x_lo, nt, preferred_element_type=_F32)
              + lax.dot_general(w_lo, x_hi, nt, preferred_element_type=_F32)) + rb_ref[...]
    eidx = lax.broadcasted_iota(jnp.int32, logits.shape, 0)
    work = logits
    chosen = jnp.zeros(logits.shape, jnp.bool_)
    vals, idxs = [], []
    for _ in range(TOP_K):
        m = jnp.max(work, axis=0, keepdims=True)
        sel = jnp.min(jnp.where(work == m, eidx, n_exp), axis=0, keepdims=True)
        hit = eidx == sel
        vals.append(m)
        idxs.append(sel)
        chosen = jnp.logical_or(chosen, hit)
        work = jnp.where(hit, -jnp.inf, work)
    exps = [jnp.exp(v - vals[0]) for v in vals]
    denom = exps[0] + exps[1] + exps[2] + exps[3]
    gate_rows = [e / denom for e in exps] + [jnp.zeros_like(denom)] * (mg_ref.shape[0] - TOP_K)
    mg_ref[...] = jnp.concatenate(gate_rows, axis=0)

    onehot = chosen.astype(_BF16)
    r = lax.broadcasted_iota(jnp.int32, (n_tok, n_tok), 0)
    c = lax.broadcasted_iota(jnp.int32, (n_tok, n_tok), 1)
    before = (r < c).astype(_BF16)
    cum = jnp.dot(onehot, before, preferred_element_type=_F32) + carry_ref[:, 0:1]
    ranks = [jnp.sum(jnp.where(eidx == s, cum, 0.0), axis=0, keepdims=True) for s in idxs]
    mi_ref[...] = jnp.concatenate(idxs + [rk.astype(jnp.int32) for rk in ranks], axis=0)
    carry_ref[...] = carry_ref[...] + jnp.sum(chosen.astype(_F32), axis=1, keepdims=True)
    cnt_ref[...] = carry_ref[...]


def _pool_mix(x, si, weights, scratch, alpha):
    pw_ref, ps_ref = weights
    (hist_ref,) = scratch
    ts, d = x.shape
    dg = d // len(POOL_WINDOWS)

    @pl.when(si == 0)
    def _():
        hist_ref[0:POOL_HALO, :] = jnp.zeros((POOL_HALO, d), _F32)

    hist_ref[POOL_HALO:POOL_HALO + ts, :] = x
    pos = si * ts + lax.broadcasted_iota(jnp.int32, (ts, 1), 0)
    pieces = []
    for gi, win in enumerate(POOL_WINDOWS):
        c0 = gi * dg
        xg = x[:, c0:c0 + dg]
        assert win & (win - 1) == 0 and win <= POOL_HALO
        ext = hist_ref[:, c0:c0 + dg]
        span = 1
        while span < win:
            ext = ext + pltpu.roll(ext, span, 0)
            span *= 2
        acc = ext[POOL_HALO:, :]
        inv_count = 1.0 / jnp.minimum(pos + 1, win).astype(_F32)
        diff = acc * inv_count - xg
        hg = jnp.dot(diff.astype(_BF16), pw_ref[gi], preferred_element_type=_F32)
        pieces.append(alpha * xg + hg * ps_ref[:, c0:c0 + dg])
    hist_ref[0:POOL_HALO, :] = x[ts - POOL_HALO:, :]
    return jnp.concatenate(pieces, axis=1)


def _short_conv_mix(x, si, weights, scratch, alpha):
    win_ref, cw_ref, wout_ref = weights
    (hist_ref,) = scratch
    ts, d = x.shape
    halo = SHORT_CONV_HALO

    @pl.when(si == 0)
    def _():
        hist_ref[0:halo, :] = jnp.zeros((halo, d), _F32)

    xb = x.astype(_BF16)
    gate_b = jnp.dot(xb, win_ref[:, 0:d], preferred_element_type=_F32)
    gate_c = jnp.dot(xb, win_ref[:, d:2 * d], preferred_element_type=_F32)
    h = jnp.dot(xb, win_ref[:, 2 * d:3 * d], preferred_element_type=_F32)
    v = gate_c * h
    hist_ref[halo:halo + ts, :] = v
    width = cw_ref.shape[0]
    u = cw_ref[width - 1:width, :] * v
    for k in range(width - 1):
        shift = width - 1 - k
        u = u + cw_ref[k:k + 1, :] * hist_ref[halo - shift:halo - shift + ts, :]
    hist_ref[0:halo, :] = v[ts - halo:, :]
    y = jnp.dot((gate_b * u).astype(_BF16), wout_ref[...], preferred_element_type=_F32)
    return alpha * x + y


def _conformer_mix(x, si, weights, scratch, alpha):
    win_ref, bin_ref, dww_ref, dwb_ref, lng_ref, lnb_ref, wout_ref, bout_ref = weights
    (hist_ref,) = scratch
    ts, d = x.shape
    halo = CONFORMER_HALO

    @pl.when(si == 0)
    def _():
        hist_ref[0:halo, :] = jnp.zeros((halo, d), _F32)

    xb = x.astype(_BF16)
    a = jnp.dot(xb, win_ref[:, 0:d], preferred_element_type=_F32) + bin_ref[:, 0:d]
    gate = jnp.dot(xb, win_ref[:, d:2 * d], preferred_element_type=_F32) + bin_ref[:, d:2 * d]
    u = a * jax.nn.sigmoid(gate)
    hist_ref[halo:halo + ts, :] = u
    width = dww_ref.shape[0]
    hist = hist_ref[...]
    acc = dwb_ref[...]
    for r in range(SUBLANES):
        rolled = hist if r == 0 else pltpu.roll(hist, r, 0)
        for q in range(halo // SUBLANES):
            shift = SUBLANES * q + r
            if shift < width:
                k = width - 1 - shift
                start = halo - SUBLANES * q
                acc = acc + dww_ref[k:k + 1, :] * rolled[start:start + ts, :]
    hist_ref[0:halo, :] = u[ts - halo:, :]
    un = _layer_norm(acc, lng_ref[...], lnb_ref[...])
    un = un * jax.nn.sigmoid(un)
    y = jnp.dot(un.astype(_BF16), wout_ref[...], preferred_element_type=_F32) + bout_ref[...]
    return alpha * x + y


def _residual_norm(x1r_ref, h_ref, g_ref, b_ref, rs, alpha):
    tb = x1r_ref.shape[0] // rs
    pieces = [alpha * x1r_ref[pl.ds(j, tb, stride=rs), :] + h_ref[pl.ds(j, tb, stride=rs), :] for j in range(rs)]
    return _layer_norm(jnp.concatenate(pieces, axis=1), g_ref[...], b_ref[...])


def _mixer_kernel(*refs, mix_fn, n_weights, from_moe, alpha, rs):
    bi, si = pl.program_id(0), pl.program_id(1)
    if from_moe:
        xprev_ref, h_ref, cg_ref, cb_ref = refs[:4]
        x = _residual_norm(xprev_ref, h_ref, cg_ref, cb_ref, rs, alpha)
        refs = refs[4:]
    else:
        x = refs[0][0]
        refs = refs[1:]
    weights, refs = refs[:n_weights], refs[n_weights:]
    g_ref, b_ref, rwt_ref, rb_ref = refs[:4]
    x1r_ref, x1p_ref, mi_ref, mg_ref, cnt_ref = refs[4:9]
    scratch, carry_ref = refs[9:-1], refs[-1]
    z = mix_fn(x, si, weights, scratch, alpha)
    _post_norm_and_route(z, g_ref, b_ref, rwt_ref, rb_ref, (bi == 0) & (si == 0),
                         x1r_ref, x1p_ref, mi_ref, mg_ref, cnt_ref, carry_ref)


def _mixer_call(mix_fn, source, bsz, seq, weights, ln_g, ln_b, router_w, router_b, halo, extra_scratch, alpha):
    d = router_w.shape[0]
    n_exp = router_w.shape[1]
    ts = min(TOKEN_BLOCK, seq)
    rs = d // LANES
    n_tok = bsz * seq
    nsb = seq // ts

    def full(a):
        nd = a.ndim
        return pl.BlockSpec(a.shape, lambda bi, si, _nd=nd: (0,) * _nd)

    small = [ln_g.reshape(1, d), ln_b.reshape(1, d), router_w.T, router_b.reshape(n_exp, 1)]
    tail = list(weights) + small
    if source[0] == "x":
        _, x, batch_lo = source
        operands = [x] + tail
        in_specs = [pl.BlockSpec((1, ts, d), lambda bi, si: (batch_lo + bi, si, 0))]
    else:
        _, x1r, h, cg, cb = source
        operands = [x1r, h, cg.reshape(1, d), cb.reshape(1, d)] + tail
        in_specs = [
            pl.BlockSpec((ts * rs, LANES), lambda bi, si: (bi * nsb + si, 0)),
            pl.BlockSpec((ts * rs, LANES), lambda bi, si: (bi * nsb + si, 0)),
            pl.BlockSpec((1, d), lambda bi, si: (0, 0)),
            pl.BlockSpec((1, d), lambda bi, si: (0, 0)),
        ]
    in_specs = in_specs + [full(a) for a in tail]
    tok_map = lambda bi, si: (0, bi * nsb + si)
    out_shape = [
        jax.ShapeDtypeStruct((n_tok * rs, LANES), _F32),
        jax.ShapeDtypeStruct((n_tok * rs // 2, LANES), jnp.uint32),
        jax.ShapeDtypeStruct((2 * TOP_K, n_tok), jnp.int32),
        jax.ShapeDtypeStruct((2 * TOP_K, n_tok), _F32),
        jax.ShapeDtypeStruct((n_exp, LANES), _F32),
    ]
    out_specs = [
        pl.BlockSpec((ts * rs, LANES), lambda bi, si: (bi * nsb + si, 0)),
        pl.BlockSpec((ts * rs // 2, LANES), lambda bi, si: (bi * nsb + si, 0)),
        pl.BlockSpec((2 * TOP_K, ts), tok_map),
        pl.BlockSpec((2 * TOP_K, ts), tok_map),
        pl.BlockSpec((n_exp, LANES), lambda bi, si: (0, 0)),
    ]
    return pl.pallas_call(
        functools.partial(_mixer_kernel, mix_fn=mix_fn, n_weights=len(weights), from_moe=source[0] == "moe",
                          alpha=alpha, rs=rs),
        grid=(bsz, nsb),
        in_specs=in_specs,
        out_specs=out_specs,
        out_shape=out_shape,
        scratch_shapes=[pltpu.VMEM((halo + ts, d), _F32)] + [pltpu.VMEM(shape, _F32) for shape in extra_scratch]
        + [pltpu.VMEM((n_exp, LANES), _F32)],
        compiler_params=pltpu.CompilerParams(
            dimension_semantics=("arbitrary", "arbitrary"), vmem_limit_bytes=VMEM_LIMIT_BYTES),
        name=mix_fn.__name__.strip("_"),
    )(*operands)


def _sc_workers():
    info = plsc.get_sparse_core_info()
    return info.num_cores, info.num_subcores, info.num_lanes


def _sc_scatter_rows(x3, dest3, n_rows):
    n_cores, n_sub, n_lanes = _sc_workers()
    n_tok, rs, _ = x3.shape
    top_k = dest3.shape[0]
    chunk = SC_CHUNK_ROWS * (LANES * SUBLANES) // (rs * LANES)
    tok_per_w = n_tok // (n_cores * n_sub)
    rows_per_w = tok_per_w // SC_INDEX_ROW
    chunks_per_row = SC_INDEX_ROW // chunk
    assert rows_per_w * SC_INDEX_ROW * n_cores * n_sub == n_tok and chunks_per_row % 2 == 0
    mesh = plsc.VectorSubcoreMesh(core_axis_name="core", subcore_axis_name="subcore")

    @pl.kernel(out_type=jax.ShapeDtypeStruct((n_rows, rs, LANES), x3.dtype), mesh=mesh,
               compiler_params=pltpu.CompilerParams(needs_layout_passes=False),
               scratch_types=[pltpu.VMEM((top_k, rows_per_w, SC_INDEX_ROW), jnp.int32),
                              pltpu.VMEM((2, chunk, rs, LANES), x3.dtype),
                              pltpu.SemaphoreType.DMA((2,)), pltpu.SemaphoreType.DMA((2,))])
    def scatter_kernel(x_hbm, d_hbm, o_hbm, idx_v, buf, rsem, ssem):
        wid = lax.axis_index("subcore") * n_cores + lax.axis_index("core")
        for k in range(top_k):
            pltpu.sync_copy(d_hbm.at[k, pl.ds(wid * rows_per_w, rows_per_w)], idx_v.at[k])
        base = wid * tok_per_w

        def read(j, c, slot):
            return pltpu.make_async_copy(x_hbm.at[pl.ds(base + j * SC_INDEX_ROW + c * chunk, chunk)],
                                         buf.at[slot], rsem.at[slot])

        def scatters(j, c, slot):
            copies = []
            for k in range(top_k):
                for h in range(chunk // n_lanes):
                    rows = idx_v[k, j, pl.ds(c * chunk + h * n_lanes, n_lanes)]
                    copies.append(pltpu.make_async_copy(buf.at[slot, pl.ds(h * n_lanes, n_lanes)],
                                                        o_hbm.at[rows], ssem.at[slot]))
            return copies

        def wait_scatters(j, c, slot):
            for cp in scatters(j, c, slot):
                cp.wait()

        read(0, 0, 0).start()

        def per_index_row(j, carry):
            for c in range(chunks_per_row):
                slot = c % 2
                if c == 0:
                    @pl.when(j > 0)
                    def _():
                        wait_scatters(j - 1, chunks_per_row - 1, 1 - slot)
                    read(j, c + 1, 1 - slot).start()
                elif c < chunks_per_row - 1:
                    wait_scatters(j, c - 1, 1 - slot)
                    read(j, c + 1, 1 - slot).start()
                else:
                    @pl.when(j + 1 < rows_per_w)
                    def _():
                        wait_scatters(j, c - 1, 1 - slot)
                        read(j + 1, 0, 1 - slot).start()
                read(j, c, slot).wait()
                for cp in scatters(j, c, slot):
                    cp.start()
            return carry

        lax.fori_loop(0, rows_per_w, per_index_row, 0)
        wait_scatters(rows_per_w - 1, chunks_per_row - 2, 0)
        wait_scatters(rows_per_w - 1, chunks_per_row - 1, 1)

    return scatter_kernel(x3, dest3)


def _sc_gather_weighted_sum(table3, idx3, gates_b):
    n_cores, n_sub, n_lanes = _sc_workers()
    top_k, n_idx_rows, _ = idx3.shape
    n_tok = n_idx_rows * SC_INDEX_ROW
    rp = table3.shape[1]
    chunk = n_lanes
    tok_per_w = n_tok // (n_cores * n_sub)
    rows_per_w = tok_per_w // SC_INDEX_ROW
    n_chunks = tok_per_w // chunk
    chunks_per_row = SC_INDEX_ROW // chunk
    assert rows_per_w * SC_INDEX_ROW * n_cores * n_sub == n_tok and n_chunks % 2 == 0
    mesh = plsc.VectorSubcoreMesh(core_axis_name="core", subcore_axis_name="subcore")

    @pl.kernel(out_type=jax.ShapeDtypeStruct((n_tok, 2 * rp, LANES), _F32), mesh=mesh,
               compiler_params=pltpu.CompilerParams(needs_layout_passes=False),
               scratch_types=[pltpu.VMEM((top_k, rows_per_w, SC_INDEX_ROW), jnp.int32),
                              pltpu.VMEM((2, top_k, chunk, rp, LANES), jnp.uint32),
                              pltpu.VMEM((2, chunk, LANES), _F32),
                              pltpu.VMEM((2, chunk, 2 * rp, LANES), _F32),
                              pltpu.SemaphoreType.DMA((2,)), pltpu.SemaphoreType.DMA((2,))])
    def gather_sum_kernel(t_hbm, i_hbm, g_hbm, o_hbm, idx_v, ybuf, gbuf, obuf, isem, osem):
        wid = lax.axis_index("subcore") * n_cores + lax.axis_index("core")
        for k in range(top_k):
            pltpu.sync_copy(i_hbm.at[k, pl.ds(wid * rows_per_w, rows_per_w)], idx_v.at[k])
        base = wid * tok_per_w

        def inputs(c, slot):
            j, off = c // chunks_per_row, (c % chunks_per_row) * chunk
            copies = []
            for k in range(top_k):
                rows = idx_v[k, j, pl.ds(off, chunk)]
                copies.append(pltpu.make_async_copy(t_hbm.at[rows], ybuf.at[slot, k], isem.at[slot]))
            copies.append(pltpu.make_async_copy(g_hbm.at[pl.ds(base + c * chunk, chunk)], gbuf.at[slot], isem.at[slot]))
            return copies

        def output(c, slot):
            return pltpu.make_async_copy(obuf.at[slot], o_hbm.at[pl.ds(base + c * chunk, chunk)], osem.at[slot])

        def weighted_sum(slot):
            def token(t, carry):
                gate = [gbuf[slot, t, pl.ds(k * (LANES // top_k), n_lanes)] for k in range(top_k)]
                for r in range(rp):
                    for l in range(LANES // n_lanes):
                        lanes = pl.ds(l * n_lanes, n_lanes)
                        lo = jnp.zeros((n_lanes,), _F32)
                        hi = jnp.zeros((n_lanes,), _F32)
                        for k in range(top_k):
                            w = ybuf[slot, k, t, r, lanes]
                            lo = lo + gate[k] * plsc.bitcast(w << 16, _F32)
                            hi = hi + gate[k] * plsc.bitcast(w & jnp.uint32(0xFFFF0000), _F32)
                        obuf[slot, t, r, lanes] = lo
                        obuf[slot, t, rp + r, lanes] = hi
                return carry

            lax.fori_loop(0, chunk, token, 0)

        for cp in inputs(0, 0):
            cp.start()

        def chunk_pair(g, carry):
            for slot in range(2):
                c = 2 * g + slot

                @pl.when(c + 1 < n_chunks)
                def _():
                    for cp in inputs(c + 1, 1 - slot):
                        cp.start()

                for cp in inputs(c, slot):
                    cp.wait()

                @pl.when(c >= 2)
                def _():
                    output(c - 2, slot).wait()

                weighted_sum(slot)
                output(c, slot).start()
            return carry

        lax.fori_loop(0, n_chunks // 2, chunk_pair, 0)
        output(n_chunks - 2, 0).wait()
        output(n_chunks - 1, 1).wait()

    return gather_sum_kernel(table3, idx3, gates_b)


def _pack_bf16_pairs(v):
    half = v.shape[1] // 2
    lo = lax.bitcast_convert_type(v[:, :half].astype(_BF16).astype(_F32), jnp.uint32)
    hi = lax.bitcast_convert_type(v[:, half:].astype(_BF16).astype(_F32), jnp.uint32)
    return (lo >> 16) | (hi & jnp.uint32(0xFFFF0000))


def _unpack_bf16_pairs(w):
    return (lax.bitcast_convert_type(w << 16, _F32),
            lax.bitcast_convert_type(w & jnp.uint32(0xFFFF0000), _F32))


def _expert_kernel(be_ref, nx_ref, nv_ref, nu_ref, xs_ref, bgu_ref, bdn_ref, wgu_hbm, wdn_hbm, ys_ref,
                   wgu_f, wdn_f, wgu_s, wdn_s, sem, *, rs, layer):
    b = pl.program_id(0)
    n_chunks = wgu_f.shape[1] // GATE_UP_CHUNK
    half = GATE_UP_CHUNK // 2

    def fetch(e):
        return (pltpu.make_async_copy(wgu_hbm.at[layer, e], wgu_f, sem.at[0]),
                pltpu.make_async_copy(wdn_hbm.at[layer, e], wdn_f, sem.at[1]))

    @pl.when(b < nu_ref[0])
    def _():
        @pl.when((b == 0) | (be_ref[b] != be_ref[jnp.maximum(b - 1, 0)]))
        def _():
            @pl.when(b == 0)
            def _():
                for cp in fetch(be_ref[0]):
                    cp.start()

            for cp in fetch(be_ref[b]):
                cp.wait()
            r = lax.broadcasted_iota(jnp.int32, (GATE_UP_CHUNK, GATE_UP_CHUNK), 0)
            c = lax.broadcasted_iota(jnp.int32, (GATE_UP_CHUNK, GATE_UP_CHUNK), 1)
            perm = (r == jnp.where(c < half, 2 * c, 2 * (c - half) + 1)).astype(_BF16)
            for ch in range(n_chunks):
                cols = slice(ch * GATE_UP_CHUNK, (ch + 1) * GATE_UP_CHUNK)
                w = wgu_f[:, cols].astype(_BF16)
                wgu_s[:, cols] = jnp.dot(w, perm, preferred_element_type=_F32).astype(_BF16)
            wdn_s[...] = wdn_f[...].astype(_BF16)

            @pl.when(nx_ref[b] >= 0)
            def _():
                for cp in fetch(nx_ref[b]):
                    cp.start()

        n_valid = nv_ref[b]
        sub = EXPERT_SUB_ROWS

        def sub_block(s):
            row0 = pl.multiple_of(s * sub, sub)
            defined = row0 + lax.broadcasted_iota(jnp.int32, (sub, 1), 0) < n_valid
            words = [jnp.where(defined, w, jnp.uint32(0)) for w in _load_rows(xs_ref, sub, rs // 2, row0)]
            halves = [_unpack_bf16_pairs(w) for w in words]
            x = jnp.concatenate([lo for lo, _ in halves] + [hi for _, hi in halves], axis=1).astype(_BF16)
            h = jnp.dot(x, wgu_s[...], preferred_element_type=_F32) + bgu_ref[0]
            acts = []
            for ch in range(n_chunks):
                g = jnp.minimum(h[:, ch * GATE_UP_CHUNK:ch * GATE_UP_CHUNK + half], SWIGLU_LIMIT)
                up = jnp.clip(h[:, ch * GATE_UP_CHUNK + half:(ch + 1) * GATE_UP_CHUNK], -SWIGLU_LIMIT, SWIGLU_LIMIT)
                acts.append(((up + 1.0) * (g * jax.nn.sigmoid(SWIGLU_ALPHA * g))).astype(_BF16))
            y = jnp.dot(jnp.concatenate(acts, axis=1), wdn_s[...], preferred_element_type=_F32) + bdn_ref[0]
            _store_rows(ys_ref, _pack_bf16_pairs(y), row0)

        n_sub = (n_valid + sub - 1) // sub

        def pair(i, carry):
            sub_block(2 * i)
            sub_block(2 * i + 1)
            return carry

        lax.fori_loop(0, n_sub // 2, pair, 0)

        @pl.when(n_sub % 2 == 1)
        def _():
            sub_block(n_sub - 1)


def _expert_call(block_expert, block_next, block_valid, n_used, xs, layer, w_gu_all, b_gu_grouped, w_dn_all,
                 b_dn, rs):
    _, n_exp, d, f2 = w_gu_all.shape
    f = f2 // 2
    br = EXPERT_BLOCK_ROWS
    n_blocks = xs.shape[0] // (br * rs // 2)
    assert f2 % GATE_UP_CHUNK == 0 and rs % 2 == 0

    def row_map(b, be, nx, nv, nu):
        return (jnp.minimum(b, nu[0] - 1), 0)

    def w_map(b, be, nx, nv, nu):
        return (be[b], 0, 0)

    return pl.pallas_call(
        functools.partial(_expert_kernel, rs=rs, layer=layer),
        grid_spec=pltpu.PrefetchScalarGridSpec(
            num_scalar_prefetch=4,
            grid=(n_blocks,),
            in_specs=[
                pl.BlockSpec((br * rs // 2, LANES), row_map),
                pl.BlockSpec((1, 1, f2), w_map),
                pl.BlockSpec((1, 1, d), w_map),
                pl.BlockSpec(memory_space=pl.ANY),
                pl.BlockSpec(memory_space=pl.ANY),
            ],
            out_specs=pl.BlockSpec((br * rs // 2, LANES), row_map),
            scratch_shapes=[pltpu.VMEM((d, f2), _F32), pltpu.VMEM((f, d), _F32),
                            pltpu.VMEM((d, f2), _BF16), pltpu.VMEM((f, d), _BF16),
                            pltpu.SemaphoreType.DMA((2,))],
        ),
        out_shape=jax.ShapeDtypeStruct(xs.shape, jnp.uint32),
        compiler_params=pltpu.CompilerParams(
            dimension_semantics=("arbitrary",), vmem_limit_bytes=VMEM_LIMIT_BYTES),
        name="experts",
    )(block_expert, block_next, block_valid, n_used, xs, b_gu_grouped, b_dn, w_gu_all, w_dn_all)


def _combine_kernel(x1r_ref, h_ref, g_ref, b_ref, *rest, rs, alpha):
    rest[-1][...] = _residual_norm(x1r_ref, h_ref, g_ref, b_ref, rs, alpha)


def _combine_call(x1r, h, ln_g, ln_b, alpha, rs, out_tokens, token_lo, out_buf):
    n_tok = x1r.shape[0] // rs
    d = rs * LANES
    tb = min(COMBINE_BLOCK, n_tok)
    steps = n_tok // tb
    block_lo = token_lo // tb
    operands = [x1r, h, ln_g.reshape(1, d), ln_b.reshape(1, d)]
    in_specs = [
        pl.BlockSpec((tb * rs, LANES), lambda i: (i, 0)),
        pl.BlockSpec((tb * rs, LANES), lambda i: (i, 0)),
        pl.BlockSpec((1, d), lambda i: (0, 0)),
        pl.BlockSpec((1, d), lambda i: (0, 0)),
    ]
    aliases = {}
    if out_buf is not None:
        aliases = {len(operands): 0}
        operands.append(out_buf)
        in_specs.append(pl.BlockSpec(memory_space=pl.ANY))
    return pl.pallas_call(
        functools.partial(_combine_kernel, rs=rs, alpha=alpha),
        grid=(steps,),
        in_specs=in_specs,
        out_specs=pl.BlockSpec((tb, d), lambda i: (block_lo + i, 0)),
        out_shape=jax.ShapeDtypeStruct((out_tokens, d), _F32),
        input_output_aliases=aliases,
        compiler_params=pltpu.CompilerParams(
            dimension_semantics=("arbitrary",), vmem_limit_bytes=VMEM_LIMIT_BYTES),
        name="combine",
    )(*operands)


def _routing_tables(meta_i, counts_f, n_blocks):
    n_exp = counts_f.shape[0]
    br = EXPERT_BLOCK_ROWS
    counts = counts_f[:, 0].astype(jnp.int32)
    padded = ((counts + br - 1) // br) * br
    pend = jnp.cumsum(padded)
    pstart = pend - padded
    eids = jnp.arange(n_exp, dtype=jnp.int32)
    idx, rank = meta_i[:TOP_K], meta_i[TOP_K:]
    dest = jnp.sum(jnp.where(idx[..., None] == eids, pstart, 0), axis=-1) + rank
    n_used = (pend[-1] // br).astype(jnp.int32)
    blk = jnp.minimum(jnp.arange(n_blocks, dtype=jnp.int32), n_used - 1)
    block_expert = jnp.minimum(jnp.sum((pend[None, :] <= (blk * br)[:, None]).astype(jnp.int32), axis=1),
                               n_exp - 1)
    group_end = jnp.sum(jnp.where(block_expert[:, None] == eids, pstart + counts, 0), axis=-1)
    block_valid = jnp.clip(group_end - blk * br, 0, br).astype(jnp.int32)
    later = (eids[None, :] > block_expert[:, None]) & (counts[None, :] > 0)
    block_next = jnp.min(jnp.where(later, eids[None, :], n_exp), axis=1)
    block_next = jnp.where(block_next == n_exp, -1, block_next).astype(jnp.int32)
    return dest, block_expert, block_next, block_valid, n_used.reshape(1)


def _moe_layer(x1p, meta_i, gates_t, counts_f, layer, w_gu_all, b_gu, w_dn_all, b_dn, rs):
    _, n_exp, d, f2 = w_gu_all.shape
    n_tok = meta_i.shape[1]
    br = EXPERT_BLOCK_ROWS
    rp = rs // 2
    n_blocks = -(-(n_tok * TOP_K) // br) + n_exp
    dest, block_expert, block_next, block_valid, n_used = _routing_tables(meta_i, counts_f, n_blocks)
    n_rows = n_blocks * br
    xs = _sc_scatter_rows(x1p.reshape(n_tok, rp, LANES), dest.reshape(TOP_K, n_tok // SC_INDEX_ROW, SC_INDEX_ROW),
                          n_rows)
    half = GATE_UP_CHUNK // 2
    b_gu_grouped = b_gu.reshape(n_exp, f2 // GATE_UP_CHUNK, half, 2).transpose(0, 1, 3, 2).reshape(n_exp, 1, f2)
    ys = _expert_call(block_expert, block_next, block_valid, n_used, xs.reshape(n_rows * rp, LANES), layer, w_gu_all,
                      b_gu_grouped, w_dn_all, b_dn.reshape(n_exp, 1, d), rs)
    gates_b = jnp.repeat(gates_t[:TOP_K].T, LANES // TOP_K, axis=1)
    h = _sc_gather_weighted_sum(ys.reshape(n_rows, rp, LANES),
                                dest.reshape(TOP_K, n_tok // SC_INDEX_ROW, SC_INDEX_ROW), gates_b)
    return h.reshape(n_tok * rs, LANES)


def kernel(x, pool_w, pool_scale, sc_w_in, sc_conv_w, sc_w_out, cf_w_in, cf_b_in, cf_dw_w, cf_dw_b,
           cf_ln_g, cf_ln_b, cf_w_out, cf_b_out, mix_ln_g, mix_ln_b, router_w, router_b,
           moe_w_gu, moe_b_gu, moe_w_dn, moe_b_dn, ffn_ln_g, ffn_ln_b):
    bsz, seq, d = x.shape
    depth = mix_ln_g.shape[0]
    alpha = (2.0 * depth) ** 0.25
    rs = d // LANES
    n_chains = BATCH_CHAINS if bsz % BATCH_CHAINS == 0 else 1
    cb = bsz // n_chains
    sources = [("x", x, c * cb) for c in range(n_chains)]
    ia = ib = ic = 0
    for layer in range(depth):
        kind = layer % 3
        route = (mix_ln_g[layer], mix_ln_b[layer], router_w[layer], router_b[layer])
        if kind == 0:
            mixer, halo, extra = _pool_mix, POOL_HALO, []
            weights = [pool_w[ia].astype(_BF16), pool_scale[ia].reshape(1, d)]
            ia += 1
        elif kind == 1:
            mixer, halo, extra = _short_conv_mix, SHORT_CONV_HALO, []
            weights = [sc_w_in[ib].astype(_BF16), sc_conv_w[ib], sc_w_out[ib].astype(_BF16)]
            ib += 1
        else:
            mixer, halo, extra = _conformer_mix, CONFORMER_HALO, []
            weights = [cf_w_in[ic].astype(_BF16), cf_b_in[ic].reshape(1, 2 * d), cf_dw_w[ic],
                       cf_dw_b[ic].reshape(1, d), cf_ln_g[ic].reshape(1, d), cf_ln_b[ic].reshape(1, d),
                       cf_w_out[ic].astype(_BF16), cf_b_out[ic].reshape(1, d)]
            ic += 1
        routed = [_mixer_call(mixer, src, cb, seq, weights, *route, halo, extra, alpha) for src in sources]
        sources = []
        for x1r, x1p, meta_i, gates_t, counts_f in routed:
            h = _moe_layer(x1p, meta_i, gates_t, counts_f, layer, moe_w_gu, moe_b_gu[layer],
                           moe_w_dn, moe_b_dn[layer], rs)
            sources.append(("moe", x1r, h, ffn_ln_g[layer], ffn_ln_b[layer]))
    out = None
    for c, (_, x1r, h, ln_g, ln_b) in enumerate(sources):
        out = _combine_call(x1r, h, ln_g, ln_b, alpha, rs, bsz * seq, c * cb * seq, out)
    return out.reshape(bsz, seq, d)
```

```python
import functools

import jax
import jax.numpy as jnp
from jax import lax
from jax.experimental import pallas as pl
from jax.experimental.pallas import tpu as pltpu
from jax.experimental.pallas import tpu_sc as plsc

LANES = 128
SUBLANES = 8
TOP_K = 4
POOL_WINDOWS = (2, 4, 8, 16)
POOL_HALO = 16
SHORT_CONV_HALO = 8
CONFORMER_HALO = 32
SWIGLU_LIMIT = 7.0
SWIGLU_ALPHA = 1.702
LN_EPS = 1e-5
TOKEN_BLOCK = 512
EXPERT_BLOCK_ROWS = 2048
EXPERT_SUB_ROWS = 256
COMBINE_BLOCK = 512
BATCH_CHAINS = 2
SC_INDEX_ROW = 128
SC_CHUNK_ROWS = 32
GATE_UP_CHUNK = 2 * LANES
VMEM_LIMIT_BYTES = 56 * 1024 * 1024

_F32 = jnp.float32
_BF16 = jnp.bfloat16


def _layer_norm(z, g, b):
    mu = jnp.mean(z, axis=-1, keepdims=True)
    zc = z - mu
    var = jnp.mean(zc * zc, axis=-1, keepdims=True)
    return zc * lax.rsqrt(var + LN_EPS) * g + b


def _store_rows(row_ref, val, row0=0):
    rows, d = val.shape
    rs = d // LANES
    for j in range(rs):
        row_ref[pl.ds(row0 * rs + j, rows, stride=rs), :] = val[:, j * LANES:(j + 1) * LANES]


def _load_rows(row_ref, rows, rs, row0=0):
    return [row_ref[pl.ds(row0 * rs + j, rows, stride=rs), :] for j in range(rs)]


def _post_norm_and_route(z, g_ref, b_ref, rwt_ref, rb_ref, first,
                         x1r_ref, x1p_ref, mi_ref, mg_ref, cnt_ref, carry_ref):
    n_tok = z.shape[0]
    n_exp = rwt_ref.shape[0]

    @pl.when(first)
    def _():
        carry_ref[...] = jnp.zeros_like(carry_ref)

    x1 = _layer_norm(z, g_ref[...], b_ref[...])
    _store_rows(x1r_ref, x1)
    _store_rows(x1p_ref, _pack_bf16_pairs(x1))

    x_hi = x1.astype(_BF16)
    x_lo = (x1 - x_hi.astype(_F32)).astype(_BF16)
    w = rwt_ref[...]
    w_hi = w.astype(_BF16)
    w_lo = (w - w_hi.astype(_F32)).astype(_BF16)
    nt = (((1,), (1,)), ((), ()))
    logits = (lax.dot_general(w_hi, x_hi, nt, preferred_element_type=_F32)
              + lax.dot_general(w_hi, x_lo, nt, preferred_element_type=_F32)
              + lax.dot_general(w_lo, x_hi, nt, preferred_element_type=_F32)) + rb_ref[...]
    eidx = lax.broadcasted_iota(jnp.int32, logits.shape, 0)
    work = logits
    chosen = jnp.zeros(logits.shape, jnp.bool_)
    vals, idxs = [], []
    for _ in range(TOP_K):
        m = jnp.max(work, axis=0, keepdims=True)
        sel = jnp.min(jnp.where(work == m, eidx, n_exp), axis=0, keepdims=True)
        hit = eidx == sel
        vals.append(m)
        idxs.append(sel)
        chosen = jnp.logical_or(chosen, hit)
        work = jnp.where(hit, -jnp.inf, work)
    exps = [jnp.exp(v - vals[0]) for v in vals]
    denom = exps[0] + exps[1] + exps[2] + exps[3]
    gate_rows = [e / denom for e in exps] + [jnp.zeros_like(denom)] * (SUBLANES - TOP_K)
    gates_tok = jnp.concatenate(gate_rows, axis=0).T
    lane_group = lax.broadcasted_iota(jnp.int32, (n_tok, LANES), 1) // (LANES // TOP_K)
    spread = jnp.broadcast_to(gates_tok[:, 0:1], (n_tok, LANES))
    for k in range(1, TOP_K):
        spread = jnp.where(lane_group == k, jnp.broadcast_to(gates_tok[:, k:k + 1], (n_tok, LANES)), spread)
    mg_ref[...] = spread

    onehot = chosen.astype(_BF16)
    r = lax.broadcasted_iota(jnp.int32, (n_tok, n_tok), 0)
    c = lax.broadcasted_iota(jnp.int32, (n_tok, n_tok), 1)
    before = (r < c).astype(_BF16)
    cum = jnp.dot(onehot, before, preferred_element_type=_F32) + carry_ref[:, 0:1]
    ranks = [jnp.sum(jnp.where(eidx == s, cum, 0.0), axis=0, keepdims=True) for s in idxs]
    mi_ref[...] = jnp.concatenate(idxs + [rk.astype(jnp.int32) for rk in ranks], axis=0)
    carry_ref[...] = carry_ref[...] + jnp.sum(chosen.astype(_F32), axis=1, keepdims=True)
    cnt_ref[...] = carry_ref[...]


def _pool_mix(x, si, weights, scratch, alpha):
    pw_ref, ps_ref = weights
    (hist_ref,) = scratch
    ts, d = x.shape
    dg = d // len(POOL_WINDOWS)

    @pl.when(si == 0)
    def _():
        hist_ref[0:POOL_HALO, :] = jnp.zeros((POOL_HALO, d), _F32)

    hist_ref[POOL_HALO:POOL_HALO + ts, :] = x
    pos = si * ts + lax.broadcasted_iota(jnp.int32, (ts, 1), 0)
    pieces = []
    for gi, win in enumerate(POOL_WINDOWS):
        c0 = gi * dg
        xg = x[:, c0:c0 + dg]
        assert win & (win - 1) == 0 and win <= POOL_HALO
        ext = hist_ref[:, c0:c0 + dg]
        span = 1
        while span < win:
            ext = ext + pltpu.roll(ext, span, 0)
            span *= 2
        acc = ext[POOL_HALO:, :]
        inv_count = 1.0 / jnp.minimum(pos + 1, win).astype(_F32)
        diff = acc * inv_count - xg
        hg = jnp.dot(diff.astype(_BF16), pw_ref[gi], preferred_element_type=_F32)
        pieces.append(alpha * xg + hg * ps_ref[:, c0:c0 + dg])
    hist_ref[0:POOL_HALO, :] = x[ts - POOL_HALO:, :]
    return jnp.concatenate(pieces, axis=1)


def _short_conv_mix(x, si, weights, scratch, alpha):
    win_ref, cw_ref, wout_ref = weights
    (hist_ref,) = scratch
    ts, d = x.shape
    halo = SHORT_CONV_HALO

    @pl.when(si == 0)
    def _():
        hist_ref[0:halo, :] = jnp.zeros((halo, d), _F32)

    xb = x.astype(_BF16)
    gate_b = jnp.dot(xb, win_ref[:, 0:d], preferred_element_type=_F32)
    gate_c = jnp.dot(xb, win_ref[:, d:2 * d], preferred_element_type=_F32)
    h = jnp.dot(xb, win_ref[:, 2 * d:3 * d], preferred_element_type=_F32)
    v = gate_c * h
    hist_ref[halo:halo + ts, :] = v
    width = cw_ref.shape[0]
    u = cw_ref[width - 1:width, :] * v
    for k in range(width - 1):
        shift = width - 1 - k
        u = u + cw_ref[k:k + 1, :] * hist_ref[halo - shift:halo - shift + ts, :]
    hist_ref[0:halo, :] = v[ts - halo:, :]
    y = jnp.dot((gate_b * u).astype(_BF16), wout_ref[...], preferred_element_type=_F32)
    return alpha * x + y


def _conformer_mix(x, si, weights, scratch, alpha):
    win_ref, bin_ref, dww_ref, dwb_ref, lng_ref, lnb_ref, wout_ref, bout_ref = weights
    (hist_ref,) = scratch
    ts, d = x.shape
    halo = CONFORMER_HALO

    @pl.when(si == 0)
    def _():
        hist_ref[0:halo, :] = jnp.zeros((halo, d), _F32)

    xb = x.astype(_BF16)
    a = jnp.dot(xb, win_ref[:, 0:d], preferred_element_type=_F32) + bin_ref[:, 0:d]
    gate = jnp.dot(xb, win_ref[:, d:2 * d], preferred_element_type=_F32) + bin_ref[:, d:2 * d]
    u = a * jax.nn.sigmoid(gate)
    hist_ref[halo:halo + ts, :] = u
    width = dww_ref.shape[0]
    hist = hist_ref[...]
    acc = dwb_ref[...]
    for r in range(SUBLANES):
        rolled = hist if r == 0 else pltpu.roll(hist, r, 0)
        for q in range(halo // SUBLANES):
            shift = SUBLANES * q + r
            if shift < width:
                k = width - 1 - shift
                start = halo - SUBLANES * q
                acc = acc + dww_ref[k:k + 1, :] * rolled[start:start + ts, :]
    hist_ref[0:halo, :] = u[ts - halo:, :]
    un = _layer_norm(acc, lng_ref[...], lnb_ref[...])
    un = un * jax.nn.sigmoid(un)
    y = jnp.dot(un.astype(_BF16), wout_ref[...], preferred_element_type=_F32) + bout_ref[...]
    return alpha * x + y


def _residual_norm(x1r_ref, h_ref, g_ref, b_ref, rs, alpha):
    tb = x1r_ref.shape[0] // rs
    pieces = [alpha * x1r_ref[pl.ds(j, tb, stride=rs), :] + h_ref[pl.ds(j, tb, stride=rs), :] for j in range(rs)]
    return _layer_norm(jnp.concatenate(pieces, axis=1), g_ref[...], b_ref[...])


def _mixer_kernel(*refs, mix_fn, n_weights, from_moe, alpha, rs):
    bi, si = pl.program_id(0), pl.program_id(1)
    if from_moe:
        xprev_ref, h_ref, cg_ref, cb_ref = refs[:4]
        x = _residual_norm(xprev_ref, h_ref, cg_ref, cb_ref, rs, alpha)
        refs = refs[4:]
    else:
        x = refs[0][0]
        refs = refs[1:]
    weights, refs = refs[:n_weights], refs[n_weights:]
    g_ref, b_ref, rwt_ref, rb_ref = refs[:4]
    x1r_ref, x1p_ref, mi_ref, mg_ref, cnt_ref = refs[4:9]
    scratch, carry_ref = refs[9:-1], refs[-1]
    z = mix_fn(x, si, weights, scratch, alpha)
    _post_norm_and_route(z, g_ref, b_ref, rwt_ref, rb_ref, (bi == 0) & (si == 0),
                         x1r_ref, x1p_ref, mi_ref, mg_ref, cnt_ref, carry_ref)


def _mixer_call(mix_fn, source, bsz, seq, weights, ln_g, ln_b, router_w, router_b, halo, alpha):
    d = router_w.shape[0]
    n_exp = router_w.shape[1]
    ts = min(TOKEN_BLOCK, seq)
    rs = d // LANES
    n_tok = bsz * seq
    nsb = seq // ts

    def full(a):
        nd = a.ndim
        return pl.BlockSpec(a.shape, lambda bi, si, _nd=nd: (0,) * _nd)

    small = [ln_g.reshape(1, d), ln_b.reshape(1, d), router_w.T, router_b.reshape(n_exp, 1)]
    tail = list(weights) + small
    if source[0] == "x":
        _, x, batch_lo = source
        operands = [x] + tail
        in_specs = [pl.BlockSpec((1, ts, d), lambda bi, si: (batch_lo + bi, si, 0))]
    else:
        _, x1r, h, cg, cb = source
        operands = [x1r, h, cg.reshape(1, d), cb.reshape(1, d)] + tail
        in_specs = [
            pl.BlockSpec((ts * rs, LANES), lambda bi, si: (bi * nsb + si, 0)),
            pl.BlockSpec((ts * rs, LANES), lambda bi, si: (bi * nsb + si, 0)),
            pl.BlockSpec((1, d), lambda bi, si: (0, 0)),
            pl.BlockSpec((1, d), lambda bi, si: (0, 0)),
        ]
    in_specs = in_specs + [full(a) for a in tail]
    tok_map = lambda bi, si: (0, bi * nsb + si)
    out_shape = [
        jax.ShapeDtypeStruct((n_tok * rs, LANES), _F32),
        jax.ShapeDtypeStruct((n_tok * rs // 2, LANES), jnp.uint32),
        jax.ShapeDtypeStruct((2 * TOP_K, n_tok), jnp.int32),
        jax.ShapeDtypeStruct((n_tok, LANES), _F32),
        jax.ShapeDtypeStruct((n_exp, LANES), _F32),
    ]
    out_specs = [
        pl.BlockSpec((ts * rs, LANES), lambda bi, si: (bi * nsb + si, 0)),
        pl.BlockSpec((ts * rs // 2, LANES), lambda bi, si: (bi * nsb + si, 0)),
        pl.BlockSpec((2 * TOP_K, ts), tok_map),
        pl.BlockSpec((ts, LANES), lambda bi, si: (bi * nsb + si, 0)),
        pl.BlockSpec((n_exp, LANES), lambda bi, si: (0, 0)),
    ]
    return pl.pallas_call(
        functools.partial(_mixer_kernel, mix_fn=mix_fn, n_weights=len(weights), from_moe=source[0] == "moe",
                          alpha=alpha, rs=rs),
        grid=(bsz, nsb),
        in_specs=in_specs,
        out_specs=out_specs,
        out_shape=out_shape,
        scratch_shapes=[pltpu.VMEM((halo + ts, d), _F32), pltpu.VMEM((n_exp, LANES), _F32)],
        compiler_params=pltpu.CompilerParams(
            dimension_semantics=("arbitrary", "arbitrary"), vmem_limit_bytes=VMEM_LIMIT_BYTES),
        name=mix_fn.__name__.strip("_"),
    )(*operands)


def _sc_workers():
    info = plsc.get_sparse_core_info()
    return info.num_cores, info.num_subcores, info.num_lanes


def _sc_scatter_rows(x3, dest3, n_rows):
    n_cores, n_sub, n_lanes = _sc_workers()
    n_tok, rs, _ = x3.shape
    top_k = dest3.shape[0]
    chunk = SC_CHUNK_ROWS * (LANES * SUBLANES) // (rs * LANES)
    tok_per_w = n_tok // (n_cores * n_sub)
    rows_per_w = tok_per_w // SC_INDEX_ROW
    chunks_per_row = SC_INDEX_ROW // chunk
    assert rows_per_w * SC_INDEX_ROW * n_cores * n_sub == n_tok and chunks_per_row % 2 == 0
    mesh = plsc.VectorSubcoreMesh(core_axis_name="core", subcore_axis_name="subcore")

    @pl.kernel(out_type=jax.ShapeDtypeStruct((n_rows, rs, LANES), x3.dtype), mesh=mesh,
               compiler_params=pltpu.CompilerParams(needs_layout_passes=False),
               scratch_types=[pltpu.VMEM((top_k, rows_per_w, SC_INDEX_ROW), jnp.int32),
                              pltpu.VMEM((2, chunk, rs, LANES), x3.dtype),
                              pltpu.SemaphoreType.DMA((2,)), pltpu.SemaphoreType.DMA((2,))])
    def scatter_kernel(x_hbm, d_hbm, o_hbm, idx_v, buf, rsem, ssem):
        wid = lax.axis_index("subcore") * n_cores + lax.axis_index("core")
        for k in range(top_k):
            pltpu.sync_copy(d_hbm.at[k, pl.ds(wid * rows_per_w, rows_per_w)], idx_v.at[k])
        base = wid * tok_per_w

        def read(j, c, slot):
            return pltpu.make_async_copy(x_hbm.at[pl.ds(base + j * SC_INDEX_ROW + c * chunk, chunk)],
                                         buf.at[slot], rsem.at[slot])

        def scatters(j, c, slot):
            copies = []
            for k in range(top_k):
                for h in range(chunk // n_lanes):
                    rows = idx_v[k, j, pl.ds(c * chunk + h * n_lanes, n_lanes)]
                    copies.append(pltpu.make_async_copy(buf.at[slot, pl.ds(h * n_lanes, n_lanes)],
                                                        o_hbm.at[rows], ssem.at[slot]))
            return copies

        def wait_scatters(j, c, slot):
            for cp in scatters(j, c, slot):
                cp.wait()

        read(0, 0, 0).start()

        def per_index_row(j, carry):
            for c in range(chunks_per_row):
                slot = c % 2
                if c == 0:
                    @pl.when(j > 0)
                    def _():
                        wait_scatters(j - 1, chunks_per_row - 1, 1 - slot)
                    read(j, c + 1, 1 - slot).start()
                elif c < chunks_per_row - 1:
                    wait_scatters(j, c - 1, 1 - slot)
                    read(j, c + 1, 1 - slot).start()
                else:
                    @pl.when(j + 1 < rows_per_w)
                    def _():
                        wait_scatters(j, c - 1, 1 - slot)
                        read(j + 1, 0, 1 - slot).start()
                read(j, c, slot).wait()
                for cp in scatters(j, c, slot):
                    cp.start()
            return carry

        lax.fori_loop(0, rows_per_w, per_index_row, 0)
        wait_scatters(rows_per_w - 1, chunks_per_row - 2, 0)
        wait_scatters(rows_per_w - 1, chunks_per_row - 1, 1)

    return scatter_kernel(x3, dest3)


def _sc_gather_weighted_sum(table3, idx3, gates_b):
    n_cores, n_sub, n_lanes = _sc_workers()
    top_k, n_idx_rows, _ = idx3.shape
    n_tok = n_idx_rows * SC_INDEX_ROW
    rp = table3.shape[1]
    chunk = n_lanes
    tok_per_w = n_tok // (n_cores * n_sub)
    rows_per_w = tok_per_w // SC_INDEX_ROW
    n_chunks = tok_per_w // chunk
    chunks_per_row = SC_INDEX_ROW // chunk
    assert rows_per_w * SC_INDEX_ROW * n_cores * n_sub == n_tok and n_chunks % 2 == 0
    mesh = plsc.VectorSubcoreMesh(core_axis_name="core", subcore_axis_name="subcore")

    @pl.kernel(out_type=jax.ShapeDtypeStruct((n_tok, 2 * rp, LANES), _F32), mesh=mesh,
               compiler_params=pltpu.CompilerParams(needs_layout_passes=False),
               scratch_types=[pltpu.VMEM((top_k, rows_per_w, SC_INDEX_ROW), jnp.int32),
                              pltpu.VMEM((2, top_k, chunk, rp, LANES), jnp.uint32),
                              pltpu.VMEM((2, chunk, LANES), _F32),
                              pltpu.VMEM((2, chunk, 2 * rp, LANES), _F32),
                              pltpu.SemaphoreType.DMA((2,)), pltpu.SemaphoreType.DMA((2,))])
    def gather_sum_kernel(t_hbm, i_hbm, g_hbm, o_hbm, idx_v, ybuf, gbuf, obuf, isem, osem):
        wid = lax.axis_index("subcore") * n_cores + lax.axis_index("core")
        for k in range(top_k):
            pltpu.sync_copy(i_hbm.at[k, pl.ds(wid * rows_per_w, rows_per_w)], idx_v.at[k])
        base = wid * tok_per_w

        def inputs(c, slot):
            j, off = c // chunks_per_row, (c % chunks_per_row) * chunk
            copies = []
            for k in range(top_k):
                rows = idx_v[k, j, pl.ds(off, chunk)]
                copies.append(pltpu.make_async_copy(t_hbm.at[rows], ybuf.at[slot, k], isem.at[slot]))
            copies.append(pltpu.make_async_copy(g_hbm.at[pl.ds(base + c * chunk, chunk)], gbuf.at[slot], isem.at[slot]))
            return copies

        def output(c, slot):
            return pltpu.make_async_copy(obuf.at[slot], o_hbm.at[pl.ds(base + c * chunk, chunk)], osem.at[slot])

        def weighted_sum(slot):
            def token(t, carry):
                gate = [gbuf[slot, t, pl.ds(k * (LANES // top_k), n_lanes)] for k in range(top_k)]
                for r in range(rp):
                    for l in range(LANES // n_lanes):
                        lanes = pl.ds(l * n_lanes, n_lanes)
                        lo = jnp.zeros((n_lanes,), _F32)
                        hi = jnp.zeros((n_lanes,), _F32)
                        for k in range(top_k):
                            w = ybuf[slot, k, t, r, lanes]
                            lo = lo + gate[k] * plsc.bitcast(w << 16, _F32)
                            hi = hi + gate[k] * plsc.bitcast(w & jnp.uint32(0xFFFF0000), _F32)
                        obuf[slot, t, r, lanes] = lo
                        obuf[slot, t, rp + r, lanes] = hi
                return carry

            lax.fori_loop(0, chunk, token, 0)

        for cp in inputs(0, 0):
            cp.start()

        def chunk_pair(g, carry):
            for slot in range(2):
                c = 2 * g + slot

                @pl.when(c + 1 < n_chunks)
                def _():
                    for cp in inputs(c + 1, 1 - slot):
                        cp.start()

                for cp in inputs(c, slot):
                    cp.wait()

                @pl.when(c >= 2)
                def _():
                    output(c - 2, slot).wait()

                weighted_sum(slot)
                output(c, slot).start()
            return carry

        lax.fori_loop(0, n_chunks // 2, chunk_pair, 0)
        output(n_chunks - 2, 0).wait()
        output(n_chunks - 1, 1).wait()

    return gather_sum_kernel(table3, idx3, gates_b)


def _pack_bf16_pairs(v):
    half = v.shape[1] // 2
    lo = lax.bitcast_convert_type(v[:, :half].astype(_BF16).astype(_F32), jnp.uint32)
    hi = lax.bitcast_convert_type(v[:, half:].astype(_BF16).astype(_F32), jnp.uint32)
    return (lo >> 16) | (hi & jnp.uint32(0xFFFF0000))


def _unpack_bf16_pairs(w):
    return (lax.bitcast_convert_type(w << 16, _F32),
            lax.bitcast_convert_type(w & jnp.uint32(0xFFFF0000), _F32))


def _expert_kernel(be_ref, nx_ref, nv_ref, nu_ref, xs_ref, bgu_ref, bdn_ref, wgu_hbm, wdn_hbm, ys_ref,
                   wgu_f, wdn_f, wgu_s, wdn_s, sem, *, rs, layer):
    b = pl.program_id(0)
    n_chunks = wgu_f.shape[1] // GATE_UP_CHUNK
    half = GATE_UP_CHUNK // 2

    def fetch(e):
        return (pltpu.make_async_copy(wgu_hbm.at[layer, e], wgu_f, sem.at[0]),
                pltpu.make_async_copy(wdn_hbm.at[layer, e], wdn_f, sem.at[1]))

    @pl.when(b < nu_ref[0])
    def _():
        @pl.when((b == 0) | (be_ref[b] != be_ref[jnp.maximum(b - 1, 0)]))
        def _():
            @pl.when(b == 0)
            def _():
                for cp in fetch(be_ref[0]):
                    cp.start()

            for cp in fetch(be_ref[b]):
                cp.wait()
            r = lax.broadcasted_iota(jnp.int32, (GATE_UP_CHUNK, GATE_UP_CHUNK), 0)
            c = lax.broadcasted_iota(jnp.int32, (GATE_UP_CHUNK, GATE_UP_CHUNK), 1)
            perm = (r == jnp.where(c < half, 2 * c, 2 * (c - half) + 1)).astype(_BF16)
            for ch in range(n_chunks):
                cols = slice(ch * GATE_UP_CHUNK, (ch + 1) * GATE_UP_CHUNK)
                w = wgu_f[:, cols].astype(_BF16)
                wgu_s[:, cols] = jnp.dot(w, perm, preferred_element_type=_F32).astype(_BF16)
            wdn_s[...] = wdn_f[...].astype(_BF16)

            @pl.when(nx_ref[b] >= 0)
            def _():
                for cp in fetch(nx_ref[b]):
                    cp.start()

        n_valid = nv_ref[b]
        sub = EXPERT_SUB_ROWS

        def sub_block(s):
            row0 = pl.multiple_of(s * sub, sub)
            defined = row0 + lax.broadcasted_iota(jnp.int32, (sub, 1), 0) < n_valid
            words = [jnp.where(defined, w, jnp.uint32(0)) for w in _load_rows(xs_ref, sub, rs // 2, row0)]
            halves = [_unpack_bf16_pairs(w) for w in words]
            x = jnp.concatenate([lo for lo, _ in halves] + [hi for _, hi in halves], axis=1).astype(_BF16)
            h = jnp.dot(x, wgu_s[...], preferred_element_type=_F32) + bgu_ref[0]
            acts = []
            for ch in range(n_chunks):
                g = jnp.minimum(h[:, ch * GATE_UP_CHUNK:ch * GATE_UP_CHUNK + half], SWIGLU_LIMIT)
                up = jnp.clip(h[:, ch * GATE_UP_CHUNK + half:(ch + 1) * GATE_UP_CHUNK], -SWIGLU_LIMIT, SWIGLU_LIMIT)
                acts.append(((up + 1.0) * (g * jax.nn.sigmoid(SWIGLU_ALPHA * g))).astype(_BF16))
            y = jnp.dot(jnp.concatenate(acts, axis=1), wdn_s[...], preferred_element_type=_F32) + bdn_ref[0]
            _store_rows(ys_ref, _pack_bf16_pairs(y), row0)

        n_sub = (n_valid + sub - 1) // sub

        def pair(i, carry):
            sub_block(2 * i)
            sub_block(2 * i + 1)
            return carry

        lax.fori_loop(0, n_sub // 2, pair, 0)

        @pl.when(n_sub % 2 == 1)
        def _():
            sub_block(n_sub - 1)


def _expert_call(block_expert, block_next, block_valid, n_used, xs, layer, w_gu_all, b_gu_grouped, w_dn_all,
                 b_dn, rs):
    _, n_exp, d, f2 = w_gu_all.shape
    f = f2 // 2
    br = EXPERT_BLOCK_ROWS
    n_blocks = xs.shape[0] // (br * rs // 2)
    assert f2 % GATE_UP_CHUNK == 0 and rs % 2 == 0

    def row_map(b, be, nx, nv, nu):
        return (jnp.minimum(b, nu[0] - 1), 0)

    def w_map(b, be, nx, nv, nu):
        return (be[b], 0, 0)

    return pl.pallas_call(
        functools.partial(_expert_kernel, rs=rs, layer=layer),
        grid_spec=pltpu.PrefetchScalarGridSpec(
            num_scalar_prefetch=4,
            grid=(n_blocks,),
            in_specs=[
                pl.BlockSpec((br * rs // 2, LANES), row_map),
                pl.BlockSpec((1, 1, f2), w_map),
                pl.BlockSpec((1, 1, d), w_map),
                pl.BlockSpec(memory_space=pl.ANY),
                pl.BlockSpec(memory_space=pl.ANY),
            ],
            out_specs=pl.BlockSpec((br * rs // 2, LANES), row_map),
            scratch_shapes=[pltpu.VMEM((d, f2), _F32), pltpu.VMEM((f, d), _F32),
                            pltpu.VMEM((d, f2), _BF16), pltpu.VMEM((f, d), _BF16),
                            pltpu.SemaphoreType.DMA((2,))],
        ),
        out_shape=jax.ShapeDtypeStruct(xs.shape, jnp.uint32),
        compiler_params=pltpu.CompilerParams(
            dimension_semantics=("arbitrary",), vmem_limit_bytes=VMEM_LIMIT_BYTES),
        name="experts",
    )(block_expert, block_next, block_valid, n_used, xs, b_gu_grouped, b_dn, w_gu_all, w_dn_all)


def _combine_kernel(x1r_ref, h_ref, g_ref, b_ref, *rest, rs, alpha):
    rest[-1][...] = _residual_norm(x1r_ref, h_ref, g_ref, b_ref, rs, alpha)


def _combine_call(x1r, h, ln_g, ln_b, alpha, rs, out_tokens, token_lo, out_buf):
    n_tok = x1r.shape[0] // rs
    d = rs * LANES
    tb = min(COMBINE_BLOCK, n_tok)
    steps = n_tok // tb
    block_lo = token_lo // tb
    operands = [x1r, h, ln_g.reshape(1, d), ln_b.reshape(1, d)]
    in_specs = [
        pl.BlockSpec((tb * rs, LANES), lambda i: (i, 0)),
        pl.BlockSpec((tb * rs, LANES), lambda i: (i, 0)),
        pl.BlockSpec((1, d), lambda i: (0, 0)),
        pl.BlockSpec((1, d), lambda i: (0, 0)),
    ]
    aliases = {}
    if out_buf is not None:
        aliases = {len(operands): 0}
        operands.append(out_buf)
        in_specs.append(pl.BlockSpec(memory_space=pl.ANY))
    return pl.pallas_call(
        functools.partial(_combine_kernel, rs=rs, alpha=alpha),
        grid=(steps,),
        in_specs=in_specs,
        out_specs=pl.BlockSpec((tb, d), lambda i: (block_lo + i, 0)),
        out_shape=jax.ShapeDtypeStruct((out_tokens, d), _F32),
        input_output_aliases=aliases,
        compiler_params=pltpu.CompilerParams(
            dimension_semantics=("arbitrary",), vmem_limit_bytes=VMEM_LIMIT_BYTES),
        name="combine",
    )(*operands)


def _routing_tables(meta_i, counts_f, n_blocks):
    n_exp = counts_f.shape[0]
    br = EXPERT_BLOCK_ROWS
    counts = counts_f[:, 0].astype(jnp.int32)
    padded = ((counts + br - 1) // br) * br
    pend = jnp.cumsum(padded)
    pstart = pend - padded
    eids = jnp.arange(n_exp, dtype=jnp.int32)
    idx, rank = meta_i[:TOP_K], meta_i[TOP_K:]
    dest = jnp.sum(jnp.where(idx[..., None] == eids, pstart, 0), axis=-1) + rank
    n_used = (pend[-1] // br).astype(jnp.int32)
    blk = jnp.minimum(jnp.arange(n_blocks, dtype=jnp.int32), n_used - 1)
    block_expert = jnp.minimum(jnp.sum((pend[None, :] <= (blk * br)[:, None]).astype(jnp.int32), axis=1),
                               n_exp - 1)
    group_end = jnp.sum(jnp.where(block_expert[:, None] == eids, pstart + counts, 0), axis=-1)
    block_valid = jnp.clip(group_end - blk * br, 0, br).astype(jnp.int32)
    later = (eids[None, :] > block_expert[:, None]) & (counts[None, :] > 0)
    block_next = jnp.min(jnp.where(later, eids[None, :], n_exp), axis=1)
    block_next = jnp.where(block_next == n_exp, -1, block_next).astype(jnp.int32)
    return dest, block_expert, block_next, block_valid, n_used.reshape(1)


def _moe_layer(x1p, meta_i, gates_b, counts_f, layer, w_gu_all, b_gu, w_dn_all, b_dn, rs):
    _, n_exp, d, f2 = w_gu_all.shape
    n_tok = meta_i.shape[1]
    br = EXPERT_BLOCK_ROWS
    rp = rs // 2
    n_blocks = -(-(n_tok * TOP_K) // br) + n_exp
    dest, block_expert, block_next, block_valid, n_used = _routing_tables(meta_i, counts_f, n_blocks)
    n_rows = n_blocks * br
    xs = _sc_scatter_rows(x1p.reshape(n_tok, rp, LANES), dest.reshape(TOP_K, n_tok // SC_INDEX_ROW, SC_INDEX_ROW),
                          n_rows)
    half = GATE_UP_CHUNK // 2
    b_gu_grouped = b_gu.reshape(n_exp, f2 // GATE_UP_CHUNK, half, 2).transpose(0, 1, 3, 2).reshape(n_exp, 1, f2)
    ys = _expert_call(block_expert, block_next, block_valid, n_used, xs.reshape(n_rows * rp, LANES), layer, w_gu_all,
                      b_gu_grouped, w_dn_all, b_dn.reshape(n_exp, 1, d), rs)
    h = _sc_gather_weighted_sum(ys.reshape(n_rows, rp, LANES),
                                dest.reshape(TOP_K, n_tok // SC_INDEX_ROW, SC_INDEX_ROW), gates_b)
    return h.reshape(n_tok * rs, LANES)


def kernel(x, pool_w, pool_scale, sc_w_in, sc_conv_w, sc_w_out, cf_w_in, cf_b_in, cf_dw_w, cf_dw_b,
           cf_ln_g, cf_ln_b, cf_w_out, cf_b_out, mix_ln_g, mix_ln_b, router_w, router_b,
           moe_w_gu, moe_b_gu, moe_w_dn, moe_b_dn, ffn_ln_g, ffn_ln_b):
    bsz, seq, d = x.shape
    depth = mix_ln_g.shape[0]
    alpha = (2.0 * depth) ** 0.25
    rs = d // LANES
    n_chains = BATCH_CHAINS if bsz % BATCH_CHAINS == 0 else 1
    cb = bsz // n_chains
    sources = [("x", x, c * cb) for c in range(n_chains)]
    ia = ib = ic = 0
    for layer in range(depth):
        kind = layer % 3
        route = (mix_ln_g[layer], mix_ln_b[layer], router_w[layer], router_b[layer])
        if kind == 0:
            mixer, halo = _pool_mix, POOL_HALO
            weights = [pool_w[ia].astype(_BF16), pool_scale[ia].reshape(1, d)]
            ia += 1
        elif kind == 1:
            mixer, halo = _short_conv_mix, SHORT_CONV_HALO
            weights = [sc_w_in[ib].astype(_BF16), sc_conv_w[ib], sc_w_out[ib].astype(_BF16)]
            ib += 1
        else:
            mixer, halo = _conformer_mix, CONFORMER_HALO
            weights = [cf_w_in[ic].astype(_BF16), cf_b_in[ic].reshape(1, 2 * d), cf_dw_w[ic],
                       cf_dw_b[ic].reshape(1, d), cf_ln_g[ic].reshape(1, d), cf_ln_b[ic].reshape(1, d),
                       cf_w_out[ic].astype(_BF16), cf_b_out[ic].reshape(1, d)]
            ic += 1
        routed = [_mixer_call(mixer, src, cb, seq, weights, *route, halo, alpha) for src in sources]
        sources = []
        for x1r, x1p, meta_i, gates_b, counts_f in routed:
            h = _moe_layer(x1p, meta_i, gates_b, counts_f, layer, moe_w_gu, moe_b_gu[layer],
                           moe_w_dn, moe_b_dn[layer], rs)
            sources.append(("moe", x1r, h, ffn_ln_g[layer], ffn_ln_b[layer]))
    out = None
    for c, (_, x1r, h, ln_g, ln_b) in enumerate(sources):
        out = _combine_call(x1r, h, ln_g, ln_b, alpha, rs, bsz * seq, c * cb * seq, out)
    return out.reshape(bsz, seq, d)
```

```python
import functools

import jax
import jax.numpy as jnp
from jax import lax
from jax.experimental import pallas as pl
from jax.experimental.pallas import tpu as pltpu
from jax.experimental.pallas import tpu_sc as plsc

LANES = 128
SUBLANES = 8
TOP_K = 4
POOL_WINDOWS = (2, 4, 8, 16)
POOL_HALO = 16
SHORT_CONV_HALO = 8
CONFORMER_HALO = 32
SWIGLU_LIMIT = 7.0
SWIGLU_ALPHA = 1.702
LN_EPS = 1e-5
TOKEN_BLOCK = 512
EXPERT_BLOCK_ROWS = 2048
EXPERT_SUB_ROWS = 512
COMBINE_BLOCK = 512
BATCH_CHAINS = 2
SC_INDEX_ROW = 128
SC_CHUNK_ROWS = 32
GATE_UP_CHUNK = 2 * LANES
VMEM_LIMIT_BYTES = 56 * 1024 * 1024

_F32 = jnp.float32
_BF16 = jnp.bfloat16


def _layer_norm(z, g, b):
    mu = jnp.mean(z, axis=-1, keepdims=True)
    zc = z - mu
    var = jnp.mean(zc * zc, axis=-1, keepdims=True)
    return zc * lax.rsqrt(var + LN_EPS) * g + b


def _store_rows(row_ref, val, row0=0):
    rows, d = val.shape
    rs = d // LANES
    for j in range(rs):
        row_ref[pl.ds(row0 * rs + j, rows, stride=rs), :] = val[:, j * LANES:(j + 1) * LANES]


def _load_rows(row_ref, rows, rs, row0=0):
    return [row_ref[pl.ds(row0 * rs + j, rows, stride=rs), :] for j in range(rs)]


def _post_norm_and_route(z, g_ref, b_ref, rwt_ref, rb_ref, first,
                         x1r_ref, x1p_ref, mi_ref, mg_ref, cnt_ref, carry_ref):
    n_tok = z.shape[0]
    n_exp = rwt_ref.shape[0]

    @pl.when(first)
    def _():
        carry_ref[...] = jnp.zeros_like(carry_ref)

    x1 = _layer_norm(z, g_ref[...], b_ref[...])
    _store_rows(x1r_ref, x1)
    _store_rows(x1p_ref, _pack_bf16_pairs(x1))

    x_hi = x1.astype(_BF16)
    x_lo = (x1 - x_hi.astype(_F32)).astype(_BF16)
    w = rwt_ref[...]
    w_hi = w.astype(_BF16)
    w_lo = (w - w_hi.astype(_F32)).astype(_BF16)
    nt = (((1,), (1,)), ((), ()))
    logits = (lax.dot_general(w_hi, x_hi, nt, preferred_element_type=_F32)
              + lax.dot_general(w_hi, x_lo, nt, preferred_element_type=_F32)
              + lax.dot_general(w_lo, x_hi, nt, preferred_element_type=_F32)) + rb_ref[...]
    eidx = lax.broadcasted_iota(jnp.int32, logits.shape, 0)
    work = logits
    chosen = jnp.zeros(logits.shape, jnp.bool_)
    vals, idxs = [], []
    for _ in range(TOP_K):
        m = jnp.max(work, axis=0, keepdims=True)
        sel = jnp.min(jnp.where(work == m, eidx, n_exp), axis=0, keepdims=True)
        hit = eidx == sel
        vals.append(m)
        idxs.append(sel)
        chosen = jnp.logical_or(chosen, hit)
        work = jnp.where(hit, -jnp.inf, work)
    exps = [jnp.exp(v - vals[0]) for v in vals]
    denom = functools.reduce(lambda a, b: a + b, exps)
    gate_rows = [e / denom for e in exps] + [jnp.zeros_like(denom)] * (SUBLANES - TOP_K)
    gates_tok = jnp.concatenate(gate_rows, axis=0).T
    lane_group = lax.broadcasted_iota(jnp.int32, (n_tok, LANES), 1) // (LANES // TOP_K)
    spread = jnp.broadcast_to(gates_tok[:, 0:1], (n_tok, LANES))
    for k in range(1, TOP_K):
        spread = jnp.where(lane_group == k, jnp.broadcast_to(gates_tok[:, k:k + 1], (n_tok, LANES)), spread)
    mg_ref[...] = spread

    onehot = chosen.astype(_BF16)
    r = lax.broadcasted_iota(jnp.int32, (n_tok, n_tok), 0)
    c = lax.broadcasted_iota(jnp.int32, (n_tok, n_tok), 1)
    before = (r < c).astype(_BF16)
    cum = jnp.dot(onehot, before, preferred_element_type=_F32) + carry_ref[:, 0:1]
    ranks = [jnp.sum(jnp.where(eidx == s, cum, 0.0), axis=0, keepdims=True) for s in idxs]
    mi_ref[...] = jnp.concatenate(idxs + [rk.astype(jnp.int32) for rk in ranks], axis=0)
    carry_ref[...] = carry_ref[...] + jnp.sum(chosen.astype(_F32), axis=1, keepdims=True)
    cnt_ref[...] = carry_ref[...]


def _pool_mix(x, si, weights, scratch, alpha):
    pw_ref, ps_ref = weights
    (hist_ref,) = scratch
    ts, d = x.shape
    dg = d // len(POOL_WINDOWS)

    @pl.when(si == 0)
    def _():
        hist_ref[0:POOL_HALO, :] = jnp.zeros((POOL_HALO, d), _F32)

    hist_ref[POOL_HALO:POOL_HALO + ts, :] = x
    pos = si * ts + lax.broadcasted_iota(jnp.int32, (ts, 1), 0)
    pieces = []
    for gi, win in enumerate(POOL_WINDOWS):
        c0 = gi * dg
        xg = x[:, c0:c0 + dg]
        assert win & (win - 1) == 0 and win <= POOL_HALO
        ext = hist_ref[:, c0:c0 + dg]
        span = 1
        while span < win:
            ext = ext + pltpu.roll(ext, span, 0)
            span *= 2
        acc = ext[POOL_HALO:, :]
        inv_count = 1.0 / jnp.minimum(pos + 1, win).astype(_F32)
        diff = acc * inv_count - xg
        hg = jnp.dot(diff.astype(_BF16), pw_ref[gi], preferred_element_type=_F32)
        pieces.append(alpha * xg + hg * ps_ref[:, c0:c0 + dg])
    hist_ref[0:POOL_HALO, :] = x[ts - POOL_HALO:, :]
    return jnp.concatenate(pieces, axis=1)


def _short_conv_mix(x, si, weights, scratch, alpha):
    win_ref, cw_ref, wout_ref = weights
    (hist_ref,) = scratch
    ts, d = x.shape
    halo = SHORT_CONV_HALO

    @pl.when(si == 0)
    def _():
        hist_ref[0:halo, :] = jnp.zeros((halo, d), _F32)

    xb = x.astype(_BF16)
    gate_b = jnp.dot(xb, win_ref[:, 0:d], preferred_element_type=_F32)
    gate_c = jnp.dot(xb, win_ref[:, d:2 * d], preferred_element_type=_F32)
    h = jnp.dot(xb, win_ref[:, 2 * d:3 * d], preferred_element_type=_F32)
    v = gate_c * h
    hist_ref[halo:halo + ts, :] = v
    width = cw_ref.shape[0]
    u = cw_ref[width - 1:width, :] * v
    for k in range(width - 1):
        shift = width - 1 - k
        u = u + cw_ref[k:k + 1, :] * hist_ref[halo - shift:halo - shift + ts, :]
    hist_ref[0:halo, :] = v[ts - halo:, :]
    y = jnp.dot((gate_b * u).astype(_BF16), wout_ref[...], preferred_element_type=_F32)
    return alpha * x + y


def _conformer_mix(x, si, weights, scratch, alpha):
    win_ref, bin_ref, dww_ref, dwb_ref, lng_ref, lnb_ref, wout_ref, bout_ref = weights
    (hist_ref,) = scratch
    ts, d = x.shape
    halo = CONFORMER_HALO

    @pl.when(si == 0)
    def _():
        hist_ref[0:halo, :] = jnp.zeros((halo, d), _F32)

    xb = x.astype(_BF16)
    a = jnp.dot(xb, win_ref[:, 0:d], preferred_element_type=_F32) + bin_ref[:, 0:d]
    gate = jnp.dot(xb, win_ref[:, d:2 * d], preferred_element_type=_F32) + bin_ref[:, d:2 * d]
    u = a * jax.nn.sigmoid(gate)
    hist_ref[halo:halo + ts, :] = u
    width = dww_ref.shape[0]
    hist = hist_ref[...]
    acc = dwb_ref[...]
    for r in range(SUBLANES):
        rolled = hist if r == 0 else pltpu.roll(hist, r, 0)
        for q in range(halo // SUBLANES):
            shift = SUBLANES * q + r
            if shift < width:
                k = width - 1 - shift
                start = halo - SUBLANES * q
                acc = acc + dww_ref[k:k + 1, :] * rolled[start:start + ts, :]
    hist_ref[0:halo, :] = u[ts - halo:, :]
    un = _layer_norm(acc, lng_ref[...], lnb_ref[...])
    un = un * jax.nn.sigmoid(un)
    y = jnp.dot(un.astype(_BF16), wout_ref[...], preferred_element_type=_F32) + bout_ref[...]
    return alpha * x + y


def _residual_norm(x1r_ref, h_ref, g_ref, b_ref, rs, alpha):
    tb = x1r_ref.shape[0] // rs
    pieces = [alpha * x1r_ref[pl.ds(j, tb, stride=rs), :] + h_ref[pl.ds(j, tb, stride=rs), :] for j in range(rs)]
    return _layer_norm(jnp.concatenate(pieces, axis=1), g_ref[...], b_ref[...])


def _mixer_kernel(*refs, mix_fn, n_weights, from_moe, alpha, rs):
    bi, si = pl.program_id(0), pl.program_id(1)
    if from_moe:
        xprev_ref, h_ref, cg_ref, cb_ref = refs[:4]
        x = _residual_norm(xprev_ref, h_ref, cg_ref, cb_ref, rs, alpha)
        refs = refs[4:]
    else:
        x = refs[0][0]
        refs = refs[1:]
    weights, refs = refs[:n_weights], refs[n_weights:]
    g_ref, b_ref, rwt_ref, rb_ref = refs[:4]
    x1r_ref, x1p_ref, mi_ref, mg_ref, cnt_ref = refs[4:9]
    scratch, carry_ref = refs[9:-1], refs[-1]
    z = mix_fn(x, si, weights, scratch, alpha)
    _post_norm_and_route(z, g_ref, b_ref, rwt_ref, rb_ref, (bi == 0) & (si == 0),
                         x1r_ref, x1p_ref, mi_ref, mg_ref, cnt_ref, carry_ref)


def _mixer_call(mix_fn, source, bsz, seq, weights, ln_g, ln_b, router_w, router_b, halo, alpha):
    d = router_w.shape[0]
    n_exp = router_w.shape[1]
    ts = min(TOKEN_BLOCK, seq)
    rs = d // LANES
    n_tok = bsz * seq
    nsb = seq // ts

    def full(a):
        nd = a.ndim
        return pl.BlockSpec(a.shape, lambda bi, si, _nd=nd: (0,) * _nd)

    small = [ln_g.reshape(1, d), ln_b.reshape(1, d), router_w.T, router_b.reshape(n_exp, 1)]
    tail = list(weights) + small
    if source[0] == "x":
        _, x, batch_lo = source
        operands = [x] + tail
        in_specs = [pl.BlockSpec((1, ts, d), lambda bi, si: (batch_lo + bi, si, 0))]
    else:
        _, x1r, h, cg, cb = source
        operands = [x1r, h, cg.reshape(1, d), cb.reshape(1, d)] + tail
        in_specs = [
            pl.BlockSpec((ts * rs, LANES), lambda bi, si: (bi * nsb + si, 0)),
            pl.BlockSpec((ts * rs, LANES), lambda bi, si: (bi * nsb + si, 0)),
            pl.BlockSpec((1, d), lambda bi, si: (0, 0)),
            pl.BlockSpec((1, d), lambda bi, si: (0, 0)),
        ]
    in_specs = in_specs + [full(a) for a in tail]
    tok_map = lambda bi, si: (0, bi * nsb + si)
    out_shape = [
        jax.ShapeDtypeStruct((n_tok * rs, LANES), _F32),
        jax.ShapeDtypeStruct((n_tok * rs // 2, LANES), jnp.uint32),
        jax.ShapeDtypeStruct((2 * TOP_K, n_tok), jnp.int32),
        jax.ShapeDtypeStruct((n_tok, LANES), _F32),
        jax.ShapeDtypeStruct((n_exp, LANES), _F32),
    ]
    out_specs = [
        pl.BlockSpec((ts * rs, LANES), lambda bi, si: (bi * nsb + si, 0)),
        pl.BlockSpec((ts * rs // 2, LANES), lambda bi, si: (bi * nsb + si, 0)),
        pl.BlockSpec((2 * TOP_K, ts), tok_map),
        pl.BlockSpec((ts, LANES), lambda bi, si: (bi * nsb + si, 0)),
        pl.BlockSpec((n_exp, LANES), lambda bi, si: (0, 0)),
    ]
    return pl.pallas_call(
        functools.partial(_mixer_kernel, mix_fn=mix_fn, n_weights=len(weights), from_moe=source[0] == "moe",
                          alpha=alpha, rs=rs),
        grid=(bsz, nsb),
        in_specs=in_specs,
        out_specs=out_specs,
        out_shape=out_shape,
        scratch_shapes=[pltpu.VMEM((halo + ts, d), _F32), pltpu.VMEM((n_exp, LANES), _F32)],
        compiler_params=pltpu.CompilerParams(
            dimension_semantics=("arbitrary", "arbitrary"), vmem_limit_bytes=VMEM_LIMIT_BYTES),
        name=mix_fn.__name__.strip("_"),
    )(*operands)


def _sc_workers():
    info = plsc.get_sparse_core_info()
    return info.num_cores, info.num_subcores, info.num_lanes


def _sc_scatter_rows(x3, dest3, n_rows):
    n_cores, n_sub, n_lanes = _sc_workers()
    n_tok, rs, _ = x3.shape
    top_k = dest3.shape[0]
    chunk = SC_CHUNK_ROWS * (LANES * SUBLANES) // (rs * LANES)
    tok_per_w = n_tok // (n_cores * n_sub)
    rows_per_w = tok_per_w // SC_INDEX_ROW
    chunks_per_row = SC_INDEX_ROW // chunk
    assert rows_per_w * SC_INDEX_ROW * n_cores * n_sub == n_tok and chunks_per_row % 2 == 0
    mesh = plsc.VectorSubcoreMesh(core_axis_name="core", subcore_axis_name="subcore")

    @pl.kernel(out_type=jax.ShapeDtypeStruct((n_rows, rs, LANES), x3.dtype), mesh=mesh,
               compiler_params=pltpu.CompilerParams(needs_layout_passes=False),
               scratch_types=[pltpu.VMEM((top_k, rows_per_w, SC_INDEX_ROW), jnp.int32),
                              pltpu.VMEM((2, chunk, rs, LANES), x3.dtype),
                              pltpu.SemaphoreType.DMA((2,)), pltpu.SemaphoreType.DMA((2,))])
    def scatter_kernel(x_hbm, d_hbm, o_hbm, idx_v, buf, rsem, ssem):
        wid = lax.axis_index("subcore") * n_cores + lax.axis_index("core")
        for k in range(top_k):
            pltpu.sync_copy(d_hbm.at[k, pl.ds(wid * rows_per_w, rows_per_w)], idx_v.at[k])
        base = wid * tok_per_w

        def read(j, c, slot):
            return pltpu.make_async_copy(x_hbm.at[pl.ds(base + j * SC_INDEX_ROW + c * chunk, chunk)],
                                         buf.at[slot], rsem.at[slot])

        def scatters(j, c, slot):
            copies = []
            for k in range(top_k):
                for h in range(chunk // n_lanes):
                    rows = idx_v[k, j, pl.ds(c * chunk + h * n_lanes, n_lanes)]
                    copies.append(pltpu.make_async_copy(buf.at[slot, pl.ds(h * n_lanes, n_lanes)],
                                                        o_hbm.at[rows], ssem.at[slot]))
            return copies

        def wait_scatters(j, c, slot):
            for cp in scatters(j, c, slot):
                cp.wait()

        read(0, 0, 0).start()

        def per_index_row(j, carry):
            for c in range(chunks_per_row):
                slot = c % 2
                if c == 0:
                    @pl.when(j > 0)
                    def _():
                        wait_scatters(j - 1, chunks_per_row - 1, 1 - slot)
                    read(j, c + 1, 1 - slot).start()
                elif c < chunks_per_row - 1:
                    wait_scatters(j, c - 1, 1 - slot)
                    read(j, c + 1, 1 - slot).start()
                else:
                    @pl.when(j + 1 < rows_per_w)
                    def _():
                        wait_scatters(j, c - 1, 1 - slot)
                        read(j + 1, 0, 1 - slot).start()
                read(j, c, slot).wait()
                for cp in scatters(j, c, slot):
                    cp.start()
            return carry

        lax.fori_loop(0, rows_per_w, per_index_row, 0)
        wait_scatters(rows_per_w - 1, chunks_per_row - 2, 0)
        wait_scatters(rows_per_w - 1, chunks_per_row - 1, 1)

    return scatter_kernel(x3, dest3)


def _sc_gather_weighted_sum(table3, idx3, gates_b):
    n_cores, n_sub, n_lanes = _sc_workers()
    top_k, n_idx_rows, _ = idx3.shape
    n_tok = n_idx_rows * SC_INDEX_ROW
    rp = table3.shape[1]
    chunk = n_lanes
    tok_per_w = n_tok // (n_cores * n_sub)
    rows_per_w = tok_per_w // SC_INDEX_ROW
    n_chunks = tok_per_w // chunk
    chunks_per_row = SC_INDEX_ROW // chunk
    assert rows_per_w * SC_INDEX_ROW * n_cores * n_sub == n_tok and n_chunks % 2 == 0
    mesh = plsc.VectorSubcoreMesh(core_axis_name="core", subcore_axis_name="subcore")

    @pl.kernel(out_type=jax.ShapeDtypeStruct((n_tok, 2 * rp, LANES), _F32), mesh=mesh,
               compiler_params=pltpu.CompilerParams(needs_layout_passes=False),
               scratch_types=[pltpu.VMEM((top_k, rows_per_w, SC_INDEX_ROW), jnp.int32),
                              pltpu.VMEM((2, top_k, chunk, rp, LANES), jnp.uint32),
                              pltpu.VMEM((2, chunk, LANES), _F32),
                              pltpu.VMEM((2, chunk, 2 * rp, LANES), _F32),
                              pltpu.SemaphoreType.DMA((2,)), pltpu.SemaphoreType.DMA((2,))])
    def gather_sum_kernel(t_hbm, i_hbm, g_hbm, o_hbm, idx_v, ybuf, gbuf, obuf, isem, osem):
        wid = lax.axis_index("subcore") * n_cores + lax.axis_index("core")
        for k in range(top_k):
            pltpu.sync_copy(i_hbm.at[k, pl.ds(wid * rows_per_w, rows_per_w)], idx_v.at[k])
        base = wid * tok_per_w

        def inputs(c, slot):
            j, off = c // chunks_per_row, (c % chunks_per_row) * chunk
            copies = []
            for k in range(top_k):
                rows = idx_v[k, j, pl.ds(off, chunk)]
                copies.append(pltpu.make_async_copy(t_hbm.at[rows], ybuf.at[slot, k], isem.at[slot]))
            copies.append(pltpu.make_async_copy(g_hbm.at[pl.ds(base + c * chunk, chunk)], gbuf.at[slot], isem.at[slot]))
            return copies

        def output(c, slot):
            return pltpu.make_async_copy(obuf.at[slot], o_hbm.at[pl.ds(base + c * chunk, chunk)], osem.at[slot])

        def weighted_sum(slot):
            def token(t, carry):
                gate = [gbuf[slot, t, pl.ds(k * (LANES // top_k), n_lanes)] for k in range(top_k)]
                for r in range(rp):
                    for l in range(LANES // n_lanes):
                        lanes = pl.ds(l * n_lanes, n_lanes)
                        lo = jnp.zeros((n_lanes,), _F32)
                        hi = jnp.zeros((n_lanes,), _F32)
                        for k in range(top_k):
                            w = ybuf[slot, k, t, r, lanes]
                            lo = lo + gate[k] * plsc.bitcast(w << 16, _F32)
                            hi = hi + gate[k] * plsc.bitcast(w & jnp.uint32(0xFFFF0000), _F32)
                        obuf[slot, t, r, lanes] = lo
                        obuf[slot, t, rp + r, lanes] = hi
                return carry

            lax.fori_loop(0, chunk, token, 0)

        for cp in inputs(0, 0):
            cp.start()

        def chunk_pair(g, carry):
            for slot in range(2):
                c = 2 * g + slot

                @pl.when(c + 1 < n_chunks)
                def _():
                    for cp in inputs(c + 1, 1 - slot):
                        cp.start()

                for cp in inputs(c, slot):
                    cp.wait()

                @pl.when(c >= 2)
                def _():
                    output(c - 2, slot).wait()

                weighted_sum(slot)
                output(c, slot).start()
            return carry

        lax.fori_loop(0, n_chunks // 2, chunk_pair, 0)
        output(n_chunks - 2, 0).wait()
        output(n_chunks - 1, 1).wait()

    return gather_sum_kernel(table3, idx3, gates_b)


def _pack_bf16_pairs(v):
    half = v.shape[1] // 2
    lo = lax.bitcast_convert_type(v[:, :half].astype(_BF16).astype(_F32), jnp.uint32)
    hi = lax.bitcast_convert_type(v[:, half:].astype(_BF16).astype(_F32), jnp.uint32)
    return (lo >> 16) | (hi & jnp.uint32(0xFFFF0000))


def _unpack_bf16_pairs(w):
    return (lax.bitcast_convert_type(w << 16, _F32),
            lax.bitcast_convert_type(w & jnp.uint32(0xFFFF0000), _F32))


def _expert_kernel(be_ref, nx_ref, nv_ref, nu_ref, xs_ref, bgu_ref, bdn_ref, wgu_hbm, wdn_hbm, ys_ref,
                   wgu_f, wdn_f, wgu_s, wdn_s, sem, *, rs, layer):
    b = pl.program_id(0)
    n_chunks = wgu_f.shape[1] // GATE_UP_CHUNK
    half = GATE_UP_CHUNK // 2

    def fetch(e):
        return (pltpu.make_async_copy(wgu_hbm.at[layer, e], wgu_f, sem.at[0]),
                pltpu.make_async_copy(wdn_hbm.at[layer, e], wdn_f, sem.at[1]))

    @pl.when(b < nu_ref[0])
    def _():
        @pl.when((b == 0) | (be_ref[b] != be_ref[jnp.maximum(b - 1, 0)]))
        def _():
            @pl.when(b == 0)
            def _():
                for cp in fetch(be_ref[0]):
                    cp.start()

            for cp in fetch(be_ref[b]):
                cp.wait()
            r = lax.broadcasted_iota(jnp.int32, (GATE_UP_CHUNK, GATE_UP_CHUNK), 0)
            c = lax.broadcasted_iota(jnp.int32, (GATE_UP_CHUNK, GATE_UP_CHUNK), 1)
            perm = (r == jnp.where(c < half, 2 * c, 2 * (c - half) + 1)).astype(_BF16)
            for ch in range(n_chunks):
                cols = slice(ch * GATE_UP_CHUNK, (ch + 1) * GATE_UP_CHUNK)
                w = wgu_f[:, cols].astype(_BF16)
                wgu_s[:, cols] = jnp.dot(w, perm, preferred_element_type=_F32).astype(_BF16)
            wdn_s[...] = wdn_f[...].astype(_BF16)

            @pl.when(nx_ref[b] >= 0)
            def _():
                for cp in fetch(nx_ref[b]):
                    cp.start()

        n_valid = nv_ref[b]
        sub = EXPERT_SUB_ROWS

        def sub_block(s):
            row0 = pl.multiple_of(s * sub, sub)
            defined = row0 + lax.broadcasted_iota(jnp.int32, (sub, 1), 0) < n_valid
            words = [jnp.where(defined, w, jnp.uint32(0)) for w in _load_rows(xs_ref, sub, rs // 2, row0)]
            halves = [_unpack_bf16_pairs(w) for w in words]
            x = jnp.concatenate([lo for lo, _ in halves] + [hi for _, hi in halves], axis=1).astype(_BF16)
            h = jnp.dot(x, wgu_s[...], preferred_element_type=_F32) + bgu_ref[0]
            acts = []
            for ch in range(n_chunks):
                g = jnp.minimum(h[:, ch * GATE_UP_CHUNK:ch * GATE_UP_CHUNK + half], SWIGLU_LIMIT)
                up = jnp.clip(h[:, ch * GATE_UP_CHUNK + half:(ch + 1) * GATE_UP_CHUNK], -SWIGLU_LIMIT, SWIGLU_LIMIT)
                acts.append(((up + 1.0) * (g * jax.nn.sigmoid(SWIGLU_ALPHA * g))).astype(_BF16))
            y = jnp.dot(jnp.concatenate(acts, axis=1), wdn_s[...], preferred_element_type=_F32) + bdn_ref[0]
            _store_rows(ys_ref, _pack_bf16_pairs(y), row0)

        n_sub = (n_valid + sub - 1) // sub

        def pair(i, carry):
            sub_block(2 * i)
            sub_block(2 * i + 1)
            return carry

        lax.fori_loop(0, n_sub // 2, pair, 0)

        @pl.when(n_sub % 2 == 1)
        def _():
            sub_block(n_sub - 1)


def _expert_call(block_expert, block_next, block_valid, n_used, xs, layer, w_gu_all, b_gu_grouped, w_dn_all,
                 b_dn, rs):
    _, n_exp, d, f2 = w_gu_all.shape
    f = f2 // 2
    br = EXPERT_BLOCK_ROWS
    n_blocks = xs.shape[0] // (br * rs // 2)
    assert f2 % GATE_UP_CHUNK == 0 and rs % 2 == 0

    def row_map(b, be, nx, nv, nu):
        return (jnp.minimum(b, nu[0] - 1), 0)

    def w_map(b, be, nx, nv, nu):
        return (be[b], 0, 0)

    return pl.pallas_call(
        functools.partial(_expert_kernel, rs=rs, layer=layer),
        grid_spec=pltpu.PrefetchScalarGridSpec(
            num_scalar_prefetch=4,
            grid=(n_blocks,),
            in_specs=[
                pl.BlockSpec((br * rs // 2, LANES), row_map),
                pl.BlockSpec((1, 1, f2), w_map),
                pl.BlockSpec((1, 1, d), w_map),
                pl.BlockSpec(memory_space=pl.ANY),
                pl.BlockSpec(memory_space=pl.ANY),
            ],
            out_specs=pl.BlockSpec((br * rs // 2, LANES), row_map),
            scratch_shapes=[pltpu.VMEM((d, f2), _F32), pltpu.VMEM((f, d), _F32),
                            pltpu.VMEM((d, f2), _BF16), pltpu.VMEM((f, d), _BF16),
                            pltpu.SemaphoreType.DMA((2,))],
        ),
        out_shape=jax.ShapeDtypeStruct(xs.shape, jnp.uint32),
        compiler_params=pltpu.CompilerParams(
            dimension_semantics=("arbitrary",), vmem_limit_bytes=VMEM_LIMIT_BYTES),
        name="experts",
    )(block_expert, block_next, block_valid, n_used, xs, b_gu_grouped, b_dn, w_gu_all, w_dn_all)


def _combine_kernel(x1r_ref, h_ref, g_ref, b_ref, *rest, rs, alpha):
    rest[-1][...] = _residual_norm(x1r_ref, h_ref, g_ref, b_ref, rs, alpha)


def _combine_call(x1r, h, ln_g, ln_b, alpha, rs, out_tokens, token_lo, out_buf):
    n_tok = x1r.shape[0] // rs
    d = rs * LANES
    tb = min(COMBINE_BLOCK, n_tok)
    steps = n_tok // tb
    block_lo = token_lo // tb
    operands = [x1r, h, ln_g.reshape(1, d), ln_b.reshape(1, d)]
    in_specs = [
        pl.BlockSpec((tb * rs, LANES), lambda i: (i, 0)),
        pl.BlockSpec((tb * rs, LANES), lambda i: (i, 0)),
        pl.BlockSpec((1, d), lambda i: (0, 0)),
        pl.BlockSpec((1, d), lambda i: (0, 0)),
    ]
    aliases = {}
    if out_buf is not None:
        aliases = {len(operands): 0}
        operands.append(out_buf)
        in_specs.append(pl.BlockSpec(memory_space=pl.ANY))
    return pl.pallas_call(
        functools.partial(_combine_kernel, rs=rs, alpha=alpha),
        grid=(steps,),
        in_specs=in_specs,
        out_specs=pl.BlockSpec((tb, d), lambda i: (block_lo + i, 0)),
        out_shape=jax.ShapeDtypeStruct((out_tokens, d), _F32),
        input_output_aliases=aliases,
        compiler_params=pltpu.CompilerParams(
            dimension_semantics=("arbitrary",), vmem_limit_bytes=VMEM_LIMIT_BYTES),
        name="combine",
    )(*operands)


def _routing_tables(meta_i, counts_f, n_blocks):
    n_exp = counts_f.shape[0]
    br = EXPERT_BLOCK_ROWS
    counts = counts_f[:, 0].astype(jnp.int32)
    padded = ((counts + br - 1) // br) * br
    pend = jnp.cumsum(padded)
    pstart = pend - padded
    eids = jnp.arange(n_exp, dtype=jnp.int32)
    idx, rank = meta_i[:TOP_K], meta_i[TOP_K:]
    dest = jnp.sum(jnp.where(idx[..., None] == eids, pstart, 0), axis=-1) + rank
    n_used = (pend[-1] // br).astype(jnp.int32)
    blk = jnp.minimum(jnp.arange(n_blocks, dtype=jnp.int32), n_used - 1)
    block_expert = jnp.minimum(jnp.sum((pend[None, :] <= (blk * br)[:, None]).astype(jnp.int32), axis=1),
                               n_exp - 1)
    group_end = jnp.sum(jnp.where(block_expert[:, None] == eids, pstart + counts, 0), axis=-1)
    block_valid = jnp.clip(group_end - blk * br, 0, br).astype(jnp.int32)
    later = (eids[None, :] > block_expert[:, None]) & (counts[None, :] > 0)
    block_next = jnp.min(jnp.where(later, eids[None, :], n_exp), axis=1)
    block_next = jnp.where(block_next == n_exp, -1, block_next).astype(jnp.int32)
    return dest, block_expert, block_next, block_valid, n_used.reshape(1)


def _moe_layer(x1p, meta_i, gates_b, counts_f, layer, w_gu_all, b_gu, w_dn_all, b_dn, rs):
    _, n_exp, d, f2 = w_gu_all.shape
    n_tok = meta_i.shape[1]
    br = EXPERT_BLOCK_ROWS
    rp = rs // 2
    n_blocks = -(-(n_tok * TOP_K) // br) + n_exp
    dest, block_expert, block_next, block_valid, n_used = _routing_tables(meta_i, counts_f, n_blocks)
    n_rows = n_blocks * br
    xs = _sc_scatter_rows(x1p.reshape(n_tok, rp, LANES), dest.reshape(TOP_K, n_tok // SC_INDEX_ROW, SC_INDEX_ROW),
                          n_rows)
    half = GATE_UP_CHUNK // 2
    b_gu_grouped = b_gu.reshape(n_exp, f2 // GATE_UP_CHUNK, half, 2).transpose(0, 1, 3, 2).reshape(n_exp, 1, f2)
    ys = _expert_call(block_expert, block_next, block_valid, n_used, xs.reshape(n_rows * rp, LANES), layer, w_gu_all,
                      b_gu_grouped, w_dn_all, b_dn.reshape(n_exp, 1, d), rs)
    h = _sc_gather_weighted_sum(ys.reshape(n_rows, rp, LANES),
                                dest.reshape(TOP_K, n_tok // SC_INDEX_ROW, SC_INDEX_ROW), gates_b)
    return h.reshape(n_tok * rs, LANES)


def kernel(x, pool_w, pool_scale, sc_w_in, sc_conv_w, sc_w_out, cf_w_in, cf_b_in, cf_dw_w, cf_dw_b,
           cf_ln_g, cf_ln_b, cf_w_out, cf_b_out, mix_ln_g, mix_ln_b, router_w, router_b,
           moe_w_gu, moe_b_gu, moe_w_dn, moe_b_dn, ffn_ln_g, ffn_ln_b):
    bsz, seq, d = x.shape
    depth = mix_ln_g.shape[0]
    alpha = (2.0 * depth) ** 0.25
    rs = d // LANES
    n_chains = BATCH_CHAINS if bsz % BATCH_CHAINS == 0 else 1
    cb = bsz // n_chains
    sources = [("x", x, c * cb) for c in range(n_chains)]
    ia = ib = ic = 0
    for layer in range(depth):
        kind = layer % 3
        route = (mix_ln_g[layer], mix_ln_b[layer], router_w[layer], router_b[layer])
        if kind == 0:
            mixer, halo = _pool_mix, POOL_HALO
            weights = [pool_w[ia].astype(_BF16), pool_scale[ia].reshape(1, d)]
            ia += 1
        elif kind == 1:
            mixer, halo = _short_conv_mix, SHORT_CONV_HALO
            weights = [sc_w_in[ib].astype(_BF16), sc_conv_w[ib], sc_w_out[ib].astype(_BF16)]
            ib += 1
        else:
            mixer, halo = _conformer_mix, CONFORMER_HALO
            weights = [cf_w_in[ic].astype(_BF16), cf_b_in[ic].reshape(1, 2 * d), cf_dw_w[ic],
                       cf_dw_b[ic].reshape(1, d), cf_ln_g[ic].reshape(1, d), cf_ln_b[ic].reshape(1, d),
                       cf_w_out[ic].astype(_BF16), cf_b_out[ic].reshape(1, d)]
            ic += 1
        routed = [_mixer_call(mixer, src, cb, seq, weights, *route, halo, alpha) for src in sources]
        sources = []
        for x1r, x1p, meta_i, gates_b, counts_f in routed:
            h = _moe_layer(x1p, meta_i, gates_b, counts_f, layer, moe_w_gu, moe_b_gu[layer],
                           moe_w_dn, moe_b_dn[layer], rs)
            sources.append(("moe", x1r, h, ffn_ln_g[layer], ffn_ln_b[layer]))
    out = None
    for c, (_, x1r, h, ln_g, ln_b) in enumerate(sources):
        out = _combine_call(x1r, h, ln_g, ln_b, alpha, rs, bsz * seq, c * cb * seq, out)
    return out.reshape(bsz, seq, d)
```

```python
import functools

import jax
import jax.numpy as jnp
from jax import lax
from jax.experimental import pallas as pl
from jax.experimental.pallas import tpu as pltpu
from jax.experimental.pallas import tpu_sc as plsc

LANES = 128
SUBLANES = 8
TOP_K = 4
POOL_WINDOWS = (2, 4, 8, 16)
POOL_HALO = 16
SHORT_CONV_HALO = 8
CONFORMER_HALO = 32
SWIGLU_LIMIT = 7.0
SWIGLU_ALPHA = 1.702
LN_EPS = 1e-5
TOKEN_BLOCK = 512
EXPERT_BLOCK_ROWS = 2048
EXPERT_SUB_ROWS = 512
COMBINE_BLOCK = 512
BATCH_CHAINS = 2
SC_INDEX_ROW = 128
SC_CHUNK_ROWS = 32
GATE_UP_CHUNK = 2 * LANES
VMEM_LIMIT_BYTES = 56 * 1024 * 1024

_F32 = jnp.float32
_BF16 = jnp.bfloat16


def _layer_norm(z, g, b):
    mu = jnp.mean(z, axis=-1, keepdims=True)
    zc = z - mu
    var = jnp.mean(zc * zc, axis=-1, keepdims=True)
    return zc * lax.rsqrt(var + LN_EPS) * g + b


def _store_rows(row_ref, val, row0=0):
    rows, d = val.shape
    rs = d // LANES
    for j in range(rs):
        row_ref[pl.ds(row0 * rs + j, rows, stride=rs), :] = val[:, j * LANES:(j + 1) * LANES]


def _load_rows(row_ref, rows, rs, row0=0):
    return [row_ref[pl.ds(row0 * rs + j, rows, stride=rs), :] for j in range(rs)]


def _post_norm_and_route(z, g_ref, b_ref, rwt_ref, rb_ref, first,
                         x1r_ref, x1p_ref, mi_ref, mg_ref, cnt_ref, carry_ref):
    n_tok = z.shape[0]
    n_exp = rwt_ref.shape[0]

    @pl.when(first)
    def _():
        carry_ref[...] = jnp.zeros_like(carry_ref)

    x1 = _layer_norm(z, g_ref[...], b_ref[...])
    _store_rows(x1r_ref, x1)
    _store_rows(x1p_ref, _pack_bf16_pairs(x1))

    x_hi = x1.astype(_BF16)
    x_lo = (x1 - x_hi.astype(_F32)).astype(_BF16)
    w = rwt_ref[...]
    w_hi = w.astype(_BF16)
    w_lo = (w - w_hi.astype(_F32)).astype(_BF16)
    nt = (((1,), (1,)), ((), ()))
    logits = (lax.dot_general(w_hi, x_hi, nt, preferred_element_type=_F32)
              + lax.dot_general(w_hi, x_lo, nt, preferred_element_type=_F32)
              + lax.dot_general(w_lo, x_hi, nt, preferred_element_type=_F32)) + rb_ref[...]
    eidx = lax.broadcasted_iota(jnp.int32, logits.shape, 0)
    work = logits
    chosen = jnp.zeros(logits.shape, jnp.bool_)
    vals, idxs = [], []
    for _ in range(TOP_K):
        m = jnp.max(work, axis=0, keepdims=True)
        sel = jnp.min(jnp.where(work == m, eidx, n_exp), axis=0, keepdims=True)
        hit = eidx == sel
        vals.append(m)
        idxs.append(sel)
        chosen = jnp.logical_or(chosen, hit)
        work = jnp.where(hit, -jnp.inf, work)
    exps = [jnp.exp(v - vals[0]) for v in vals]
    denom = functools.reduce(lambda a, b: a + b, exps)
    gate_rows = [e / denom for e in exps] + [jnp.zeros_like(denom)] * (SUBLANES - TOP_K)
    gates_tok = jnp.concatenate(gate_rows, axis=0).T
    lane_group = lax.broadcasted_iota(jnp.int32, (n_tok, LANES), 1) // (LANES // TOP_K)
    spread = jnp.broadcast_to(gates_tok[:, 0:1], (n_tok, LANES))
    for k in range(1, TOP_K):
        spread = jnp.where(lane_group == k, jnp.broadcast_to(gates_tok[:, k:k + 1], (n_tok, LANES)), spread)
    mg_ref[...] = spread

    onehot = chosen.astype(_BF16)
    r = lax.broadcasted_iota(jnp.int32, (n_tok, n_tok), 0)
    c = lax.broadcasted_iota(jnp.int32, (n_tok, n_tok), 1)
    before = (r < c).astype(_BF16)
    cum = jnp.dot(onehot, before, preferred_element_type=_F32) + carry_ref[:, 0:1]
    ranks = [jnp.sum(jnp.where(eidx == s, cum, 0.0), axis=0, keepdims=True) for s in idxs]
    mi_ref[...] = jnp.concatenate(idxs + [rk.astype(jnp.int32) for rk in ranks], axis=0)
    carry_ref[...] = carry_ref[...] + jnp.sum(chosen.astype(_F32), axis=1, keepdims=True)
    cnt_ref[...] = carry_ref[...]


def _pool_mix(x, si, weights, scratch, alpha):
    pw_ref, ps_ref = weights
    (hist_ref,) = scratch
    ts, d = x.shape
    dg = d // len(POOL_WINDOWS)

    @pl.when(si == 0)
    def _():
        hist_ref[0:POOL_HALO, :] = jnp.zeros((POOL_HALO, d), _F32)

    hist_ref[POOL_HALO:POOL_HALO + ts, :] = x
    pos = si * ts + lax.broadcasted_iota(jnp.int32, (ts, 1), 0)
    pieces = []
    for gi, win in enumerate(POOL_WINDOWS):
        c0 = gi * dg
        xg = x[:, c0:c0 + dg]
        assert win & (win - 1) == 0 and win <= POOL_HALO
        ext = hist_ref[:, c0:c0 + dg]
        span = 1
        while span < win:
            ext = ext + pltpu.roll(ext, span, 0)
            span *= 2
        acc = ext[POOL_HALO:, :]
        inv_count = 1.0 / jnp.minimum(pos + 1, win).astype(_F32)
        diff = acc * inv_count - xg
        hg = jnp.dot(diff.astype(_BF16), pw_ref[gi], preferred_element_type=_F32)
        pieces.append(alpha * xg + hg * ps_ref[:, c0:c0 + dg])
    hist_ref[0:POOL_HALO, :] = x[ts - POOL_HALO:, :]
    return jnp.concatenate(pieces, axis=1)


def _short_conv_mix(x, si, weights, scratch, alpha):
    win_ref, cw_ref, wout_ref = weights
    (hist_ref,) = scratch
    ts, d = x.shape
    halo = SHORT_CONV_HALO

    @pl.when(si == 0)
    def _():
        hist_ref[0:halo, :] = jnp.zeros((halo, d), _F32)

    xb = x.astype(_BF16)
    gate_b = jnp.dot(xb, win_ref[:, 0:d], preferred_element_type=_F32)
    gate_c = jnp.dot(xb, win_ref[:, d:2 * d], preferred_element_type=_F32)
    h = jnp.dot(xb, win_ref[:, 2 * d:3 * d], preferred_element_type=_F32)
    v = gate_c * h
    hist_ref[halo:halo + ts, :] = v
    width = cw_ref.shape[0]
    u = cw_ref[width - 1:width, :] * v
    for k in range(width - 1):
        shift = width - 1 - k
        u = u + cw_ref[k:k + 1, :] * hist_ref[halo - shift:halo - shift + ts, :]
    hist_ref[0:halo, :] = v[ts - halo:, :]
    y = jnp.dot((gate_b * u).astype(_BF16), wout_ref[...], preferred_element_type=_F32)
    return alpha * x + y


def _conformer_mix(x, si, weights, scratch, alpha):
    win_ref, bin_ref, dww_ref, dwb_ref, lng_ref, lnb_ref, wout_ref, bout_ref = weights
    (hist_ref,) = scratch
    ts, d = x.shape
    halo = CONFORMER_HALO

    @pl.when(si == 0)
    def _():
        hist_ref[0:halo, :] = jnp.zeros((halo, d), _F32)

    xb = x.astype(_BF16)
    a = jnp.dot(xb, win_ref[:, 0:d], preferred_element_type=_F32) + bin_ref[:, 0:d]
    gate = jnp.dot(xb, win_ref[:, d:2 * d], preferred_element_type=_F32) + bin_ref[:, d:2 * d]
    u = a * jax.nn.sigmoid(gate)
    hist_ref[halo:halo + ts, :] = u
    width = dww_ref.shape[0]
    hist = hist_ref[...]
    acc = dwb_ref[...]
    for r in range(SUBLANES):
        rolled = hist if r == 0 else pltpu.roll(hist, r, 0)
        for q in range(halo // SUBLANES):
            shift = SUBLANES * q + r
            if shift < width:
                k = width - 1 - shift
                start = halo - SUBLANES * q
                acc = acc + dww_ref[k:k + 1, :] * rolled[start:start + ts, :]
    hist_ref[0:halo, :] = u[ts - halo:, :]
    un = _layer_norm(acc, lng_ref[...], lnb_ref[...])
    un = un * jax.nn.sigmoid(un)
    y = jnp.dot(un.astype(_BF16), wout_ref[...], preferred_element_type=_F32) + bout_ref[...]
    return alpha * x + y


def _residual_norm(x1r_ref, h_ref, g_ref, b_ref, rs, alpha):
    tb = x1r_ref.shape[0] // rs
    pieces = [alpha * x1r_ref[pl.ds(j, tb, stride=rs), :] + h_ref[pl.ds(j, tb, stride=rs), :] for j in range(rs)]
    return _layer_norm(jnp.concatenate(pieces, axis=1), g_ref[...], b_ref[...])


def _mixer_kernel(*refs, mix_fn, n_weights, from_moe, alpha, rs):
    bi, si = pl.program_id(0), pl.program_id(1)
    if from_moe:
        xprev_ref, h_ref, cg_ref, cb_ref = refs[:4]
        x = _residual_norm(xprev_ref, h_ref, cg_ref, cb_ref, rs, alpha)
        refs = refs[4:]
    else:
        x = refs[0][0]
        refs = refs[1:]
    weights, refs = refs[:n_weights], refs[n_weights:]
    g_ref, b_ref, rwt_ref, rb_ref = refs[:4]
    x1r_ref, x1p_ref, mi_ref, mg_ref, cnt_ref = refs[4:9]
    scratch, carry_ref = refs[9:-1], refs[-1]
    z = mix_fn(x, si, weights, scratch, alpha)
    _post_norm_and_route(z, g_ref, b_ref, rwt_ref, rb_ref, (bi == 0) & (si == 0),
                         x1r_ref, x1p_ref, mi_ref, mg_ref, cnt_ref, carry_ref)


def _mixer_call(mix_fn, source, bsz, seq, weights, ln_g, ln_b, router_w, router_b, halo, alpha):
    d = router_w.shape[0]
    n_exp = router_w.shape[1]
    ts = min(TOKEN_BLOCK, seq)
    rs = d // LANES
    n_tok = bsz * seq
    nsb = seq // ts

    def full(a):
        nd = a.ndim
        return pl.BlockSpec(a.shape, lambda bi, si, _nd=nd: (0,) * _nd)

    small = [ln_g.reshape(1, d), ln_b.reshape(1, d), router_w.T, router_b.reshape(n_exp, 1)]
    tail = list(weights) + small
    if source[0] == "x":
        _, x, batch_lo = source
        operands = [x] + tail
        in_specs = [pl.BlockSpec((1, ts, d), lambda bi, si: (batch_lo + bi, si, 0))]
    else:
        _, x1r, h, cg, cb = source
        operands = [x1r, h, cg.reshape(1, d), cb.reshape(1, d)] + tail
        in_specs = [
            pl.BlockSpec((ts * rs, LANES), lambda bi, si: (bi * nsb + si, 0)),
            pl.BlockSpec((ts * rs, LANES), lambda bi, si: (bi * nsb + si, 0)),
            pl.BlockSpec((1, d), lambda bi, si: (0, 0)),
            pl.BlockSpec((1, d), lambda bi, si: (0, 0)),
        ]
    in_specs = in_specs + [full(a) for a in tail]
    tok_map = lambda bi, si: (0, bi * nsb + si)
    out_shape = [
        jax.ShapeDtypeStruct((n_tok * rs, LANES), _F32),
        jax.ShapeDtypeStruct((n_tok * rs // 2, LANES), jnp.uint32),
        jax.ShapeDtypeStruct((2 * TOP_K, n_tok), jnp.int32),
        jax.ShapeDtypeStruct((n_tok, LANES), _F32),
        jax.ShapeDtypeStruct((n_exp, LANES), _F32),
    ]
    out_specs = [
        pl.BlockSpec((ts * rs, LANES), lambda bi, si: (bi * nsb + si, 0)),
        pl.BlockSpec((ts * rs // 2, LANES), lambda bi, si: (bi * nsb + si, 0)),
        pl.BlockSpec((2 * TOP_K, ts), tok_map),
        pl.BlockSpec((ts, LANES), lambda bi, si: (bi * nsb + si, 0)),
        pl.BlockSpec((n_exp, LANES), lambda bi, si: (0, 0)),
    ]
    return pl.pallas_call(
        functools.partial(_mixer_kernel, mix_fn=mix_fn, n_weights=len(weights), from_moe=source[0] == "moe",
                          alpha=alpha, rs=rs),
        grid=(bsz, nsb),
        in_specs=in_specs,
        out_specs=out_specs,
        out_shape=out_shape,
        scratch_shapes=[pltpu.VMEM((halo + ts, d), _F32), pltpu.VMEM((n_exp, LANES), _F32)],
        compiler_params=pltpu.CompilerParams(
            dimension_semantics=("arbitrary", "arbitrary"), vmem_limit_bytes=VMEM_LIMIT_BYTES),
        name=mix_fn.__name__.strip("_"),
    )(*operands)


def _sc_workers():
    info = plsc.get_sparse_core_info()
    return info.num_cores, info.num_subcores, info.num_lanes


def _sc_scatter_rows(x3, dest3, n_rows):
    n_cores, n_sub, n_lanes = _sc_workers()
    n_tok, rs, _ = x3.shape
    top_k = dest3.shape[0]
    chunk = SC_CHUNK_ROWS * (LANES * SUBLANES) // (rs * LANES)
    tok_per_w = n_tok // (n_cores * n_sub)
    rows_per_w = tok_per_w // SC_INDEX_ROW
    chunks_per_row = SC_INDEX_ROW // chunk
    assert rows_per_w * SC_INDEX_ROW * n_cores * n_sub == n_tok and chunks_per_row % 2 == 0
    mesh = plsc.VectorSubcoreMesh(core_axis_name="core", subcore_axis_name="subcore")

    @pl.kernel(out_type=jax.ShapeDtypeStruct((n_rows, rs, LANES), x3.dtype), mesh=mesh,
               compiler_params=pltpu.CompilerParams(needs_layout_passes=False),
               scratch_types=[pltpu.VMEM((top_k, rows_per_w, SC_INDEX_ROW), jnp.int32),
                              pltpu.VMEM((2, chunk, rs, LANES), x3.dtype),
                              pltpu.SemaphoreType.DMA((2,)), pltpu.SemaphoreType.DMA((2,))])
    def scatter_kernel(x_hbm, d_hbm, o_hbm, idx_v, buf, rsem, ssem):
        wid = lax.axis_index("subcore") * n_cores + lax.axis_index("core")
        for k in range(top_k):
            pltpu.sync_copy(d_hbm.at[k, pl.ds(wid * rows_per_w, rows_per_w)], idx_v.at[k])
        base = wid * tok_per_w

        def read(j, c, slot):
            return pltpu.make_async_copy(x_hbm.at[pl.ds(base + j * SC_INDEX_ROW + c * chunk, chunk)],
                                         buf.at[slot], rsem.at[slot])

        def scatters(j, c, slot):
            copies = []
            for k in range(top_k):
                for h in range(chunk // n_lanes):
                    rows = idx_v[k, j, pl.ds(c * chunk + h * n_lanes, n_lanes)]
                    copies.append(pltpu.make_async_copy(buf.at[slot, pl.ds(h * n_lanes, n_lanes)],
                                                        o_hbm.at[rows], ssem.at[slot]))
            return copies

        def wait_scatters(j, c, slot):
            for cp in scatters(j, c, slot):
                cp.wait()

        read(0, 0, 0).start()

        def per_index_row(j, carry):
            for c in range(chunks_per_row):
                slot = c % 2
                if c == 0:
                    @pl.when(j > 0)
                    def _():
                        wait_scatters(j - 1, chunks_per_row - 1, 1 - slot)
                    read(j, c + 1, 1 - slot).start()
                elif c < chunks_per_row - 1:
                    wait_scatters(j, c - 1, 1 - slot)
                    read(j, c + 1, 1 - slot).start()
                else:
                    @pl.when(j + 1 < rows_per_w)
                    def _():
                        wait_scatters(j, c - 1, 1 - slot)
                        read(j + 1, 0, 1 - slot).start()
                read(j, c, slot).wait()
                for cp in scatters(j, c, slot):
                    cp.start()
            return carry

        lax.fori_loop(0, rows_per_w, per_index_row, 0)
        wait_scatters(rows_per_w - 1, chunks_per_row - 2, 0)
        wait_scatters(rows_per_w - 1, chunks_per_row - 1, 1)

    return scatter_kernel(x3, dest3)


def _sc_gather_weighted_sum(table3, idx3, gates_b):
    n_cores, n_sub, n_lanes = _sc_workers()
    top_k, n_idx_rows, _ = idx3.shape
    n_tok = n_idx_rows * SC_INDEX_ROW
    rp = table3.shape[1]
    chunk = n_lanes
    tok_per_w = n_tok // (n_cores * n_sub)
    rows_per_w = tok_per_w // SC_INDEX_ROW
    n_chunks = tok_per_w // chunk
    chunks_per_row = SC_INDEX_ROW // chunk
    assert rows_per_w * SC_INDEX_ROW * n_cores * n_sub == n_tok and n_chunks % 2 == 0
    mesh = plsc.VectorSubcoreMesh(core_axis_name="core", subcore_axis_name="subcore")

    @pl.kernel(out_type=jax.ShapeDtypeStruct((n_tok, 2 * rp, LANES), _F32), mesh=mesh,
               compiler_params=pltpu.CompilerParams(needs_layout_passes=False),
               scratch_types=[pltpu.VMEM((top_k, rows_per_w, SC_INDEX_ROW), jnp.int32),
                              pltpu.VMEM((2, top_k, chunk, rp, LANES), jnp.uint32),
                              pltpu.VMEM((2, chunk, LANES), _F32),
                              pltpu.VMEM((2, chunk, 2 * rp, LANES), _F32),
                              pltpu.SemaphoreType.DMA((2,)), pltpu.SemaphoreType.DMA((2,))])
    def gather_sum_kernel(t_hbm, i_hbm, g_hbm, o_hbm, idx_v, ybuf, gbuf, obuf, isem, osem):
        wid = lax.axis_index("subcore") * n_cores + lax.axis_index("core")
        for k in range(top_k):
            pltpu.sync_copy(i_hbm.at[k, pl.ds(wid * rows_per_w, rows_per_w)], idx_v.at[k])
        base = wid * tok_per_w

        def inputs(c, slot):
            j, off = c // chunks_per_row, (c % chunks_per_row) * chunk
            copies = []
            for k in range(top_k):
                rows = idx_v[k, j, pl.ds(off, chunk)]
                copies.append(pltpu.make_async_copy(t_hbm.at[rows], ybuf.at[slot, k], isem.at[slot]))
            copies.append(pltpu.make_async_copy(g_hbm.at[pl.ds(base + c * chunk, chunk)], gbuf.at[slot], isem.at[slot]))
            return copies

        def output(c, slot):
            return pltpu.make_async_copy(obuf.at[slot], o_hbm.at[pl.ds(base + c * chunk, chunk)], osem.at[slot])

        def weighted_sum(slot):
            def token(t, carry):
                gate = [gbuf[slot, t, pl.ds(k * (LANES // top_k), n_lanes)] for k in range(top_k)]
                for r in range(rp):
                    for l in range(LANES // n_lanes):
                        lanes = pl.ds(l * n_lanes, n_lanes)
                        lo = jnp.zeros((n_lanes,), _F32)
                        hi = jnp.zeros((n_lanes,), _F32)
                        for k in range(top_k):
                            w = ybuf[slot, k, t, r, lanes]
                            lo = lo + gate[k] * plsc.bitcast(w << 16, _F32)
                            hi = hi + gate[k] * plsc.bitcast(w & jnp.uint32(0xFFFF0000), _F32)
                        obuf[slot, t, r, lanes] = lo
                        obuf[slot, t, rp + r, lanes] = hi
                return carry

            lax.fori_loop(0, chunk, token, 0)

        for cp in inputs(0, 0):
            cp.start()

        def chunk_pair(g, carry):
            for slot in range(2):
                c = 2 * g + slot

                @pl.when(c + 1 < n_chunks)
                def _():
                    for cp in inputs(c + 1, 1 - slot):
                        cp.start()

                for cp in inputs(c, slot):
                    cp.wait()

                @pl.when(c >= 2)
                def _():
                    output(c - 2, slot).wait()

                weighted_sum(slot)
                output(c, slot).start()
            return carry

        lax.fori_loop(0, n_chunks // 2, chunk_pair, 0)
        output(n_chunks - 2, 0).wait()
        output(n_chunks - 1, 1).wait()

    return gather_sum_kernel(table3, idx3, gates_b)


def _pack_bf16_pairs(v):
    half = v.shape[1] // 2
    lo = lax.bitcast_convert_type(v[:, :half].astype(_BF16).astype(_F32), jnp.uint32)
    hi = lax.bitcast_convert_type(v[:, half:].astype(_BF16).astype(_F32), jnp.uint32)
    return (lo >> 16) | (hi & jnp.uint32(0xFFFF0000))


def _unpack_bf16_pairs(w):
    return (lax.bitcast_convert_type(w << 16, _F32),
            lax.bitcast_convert_type(w & jnp.uint32(0xFFFF0000), _F32))


def _expert_kernel(be_ref, nx_ref, nv_ref, nu_ref, xs_ref, bgu_ref, bdn_ref, wgu_hbm, wdn_hbm, ys_ref,
                   wgu_f, wdn_f, wgu_s, wdn_s, sem, *, rs, layer):
    b = pl.program_id(0)
    n_chunks = wgu_f.shape[1] // GATE_UP_CHUNK
    half = GATE_UP_CHUNK // 2

    def fetch(e):
        return (pltpu.make_async_copy(wgu_hbm.at[layer, e], wgu_f, sem.at[0]),
                pltpu.make_async_copy(wdn_hbm.at[layer, e], wdn_f, sem.at[1]))

    @pl.when(b < nu_ref[0])
    def _():
        @pl.when((b == 0) | (be_ref[b] != be_ref[jnp.maximum(b - 1, 0)]))
        def _():
            @pl.when(b == 0)
            def _():
                for cp in fetch(be_ref[0]):
                    cp.start()

            for cp in fetch(be_ref[b]):
                cp.wait()
            r = lax.broadcasted_iota(jnp.int32, (GATE_UP_CHUNK, GATE_UP_CHUNK), 0)
            c = lax.broadcasted_iota(jnp.int32, (GATE_UP_CHUNK, GATE_UP_CHUNK), 1)
            perm = (r == jnp.where(c < half, 2 * c, 2 * (c - half) + 1)).astype(_BF16)
            for ch in range(n_chunks):
                cols = slice(ch * GATE_UP_CHUNK, (ch + 1) * GATE_UP_CHUNK)
                w = wgu_f[:, cols].astype(_BF16)
                wgu_s[:, cols] = jnp.dot(w, perm, preferred_element_type=_F32).astype(_BF16)
            wdn_s[...] = wdn_f[...].astype(_BF16)

            @pl.when(nx_ref[b] >= 0)
            def _():
                for cp in fetch(nx_ref[b]):
                    cp.start()

        n_valid = nv_ref[b]
        sub = EXPERT_SUB_ROWS

        def sub_block(s, rows=sub):
            row0 = pl.multiple_of(s * sub, sub)
            defined = row0 + lax.broadcasted_iota(jnp.int32, (rows, 1), 0) < n_valid
            words = [jnp.where(defined, w, jnp.uint32(0)) for w in _load_rows(xs_ref, rows, rs // 2, row0)]
            halves = [_unpack_bf16_pairs(w) for w in words]
            x = jnp.concatenate([lo for lo, _ in halves] + [hi for _, hi in halves], axis=1).astype(_BF16)
            h = jnp.dot(x, wgu_s[...], preferred_element_type=_F32) + bgu_ref[0]
            acts = []
            for ch in range(n_chunks):
                g = jnp.minimum(h[:, ch * GATE_UP_CHUNK:ch * GATE_UP_CHUNK + half], SWIGLU_LIMIT)
                up = jnp.clip(h[:, ch * GATE_UP_CHUNK + half:(ch + 1) * GATE_UP_CHUNK], -SWIGLU_LIMIT, SWIGLU_LIMIT)
                acts.append(((up + 1.0) * (g * jax.nn.sigmoid(SWIGLU_ALPHA * g))).astype(_BF16))
            y = jnp.dot(jnp.concatenate(acts, axis=1), wdn_s[...], preferred_element_type=_F32) + bdn_ref[0]
            _store_rows(ys_ref, _pack_bf16_pairs(y), row0)

        n_sub = (n_valid + sub - 1) // sub
        half_tail = (n_valid - (n_sub - 1) * sub) * 2 <= sub
        n_whole = n_sub - half_tail.astype(jnp.int32)

        def pair(i, carry):
            sub_block(2 * i)
            sub_block(2 * i + 1)
            return carry

        lax.fori_loop(0, n_whole // 2, pair, 0)

        @pl.when(n_whole % 2 == 1)
        def _():
            sub_block(n_whole - 1)

        @pl.when(half_tail)
        def _():
            sub_block(n_sub - 1, sub // 2)


def _expert_call(block_expert, block_next, block_valid, n_used, xs, layer, w_gu_all, b_gu_grouped, w_dn_all,
                 b_dn, rs):
    _, n_exp, d, f2 = w_gu_all.shape
    f = f2 // 2
    br = EXPERT_BLOCK_ROWS
    n_blocks = xs.shape[0] // (br * rs // 2)
    assert f2 % GATE_UP_CHUNK == 0 and rs % 2 == 0

    def row_map(b, be, nx, nv, nu):
        return (jnp.minimum(b, nu[0] - 1), 0)

    def w_map(b, be, nx, nv, nu):
        return (be[b], 0, 0)

    return pl.pallas_call(
        functools.partial(_expert_kernel, rs=rs, layer=layer),
        grid_spec=pltpu.PrefetchScalarGridSpec(
            num_scalar_prefetch=4,
            grid=(n_blocks,),
            in_specs=[
                pl.BlockSpec((br * rs // 2, LANES), row_map),
                pl.BlockSpec((1, 1, f2), w_map),
                pl.BlockSpec((1, 1, d), w_map),
                pl.BlockSpec(memory_space=pl.ANY),
                pl.BlockSpec(memory_space=pl.ANY),
            ],
            out_specs=pl.BlockSpec((br * rs // 2, LANES), row_map),
            scratch_shapes=[pltpu.VMEM((d, f2), _F32), pltpu.VMEM((f, d), _F32),
                            pltpu.VMEM((d, f2), _BF16), pltpu.VMEM((f, d), _BF16),
                            pltpu.SemaphoreType.DMA((2,))],
        ),
        out_shape=jax.ShapeDtypeStruct(xs.shape, jnp.uint32),
        compiler_params=pltpu.CompilerParams(
            dimension_semantics=("arbitrary",), vmem_limit_bytes=VMEM_LIMIT_BYTES),
        name="experts",
    )(block_expert, block_next, block_valid, n_used, xs, b_gu_grouped, b_dn, w_gu_all, w_dn_all)


def _combine_kernel(x1r_ref, h_ref, g_ref, b_ref, *rest, rs, alpha):
    rest[-1][...] = _residual_norm(x1r_ref, h_ref, g_ref, b_ref, rs, alpha)


def _combine_call(x1r, h, ln_g, ln_b, alpha, rs, out_tokens, token_lo, out_buf):
    n_tok = x1r.shape[0] // rs
    d = rs * LANES
    tb = min(COMBINE_BLOCK, n_tok)
    steps = n_tok // tb
    block_lo = token_lo // tb
    operands = [x1r, h, ln_g.reshape(1, d), ln_b.reshape(1, d)]
    in_specs = [
        pl.BlockSpec((tb * rs, LANES), lambda i: (i, 0)),
        pl.BlockSpec((tb * rs, LANES), lambda i: (i, 0)),
        pl.BlockSpec((1, d), lambda i: (0, 0)),
        pl.BlockSpec((1, d), lambda i: (0, 0)),
    ]
    aliases = {}
    if out_buf is not None:
        aliases = {len(operands): 0}
        operands.append(out_buf)
        in_specs.append(pl.BlockSpec(memory_space=pl.ANY))
    return pl.pallas_call(
        functools.partial(_combine_kernel, rs=rs, alpha=alpha),
        grid=(steps,),
        in_specs=in_specs,
        out_specs=pl.BlockSpec((tb, d), lambda i: (block_lo + i, 0)),
        out_shape=jax.ShapeDtypeStruct((out_tokens, d), _F32),
        input_output_aliases=aliases,
        compiler_params=pltpu.CompilerParams(
            dimension_semantics=("arbitrary",), vmem_limit_bytes=VMEM_LIMIT_BYTES),
        name="combine",
    )(*operands)


def _routing_tables(meta_i, counts_f, n_blocks):
    n_exp = counts_f.shape[0]
    br = EXPERT_BLOCK_ROWS
    counts = counts_f[:, 0].astype(jnp.int32)
    padded = ((counts + br - 1) // br) * br
    pend = jnp.cumsum(padded)
    pstart = pend - padded
    eids = jnp.arange(n_exp, dtype=jnp.int32)
    idx, rank = meta_i[:TOP_K], meta_i[TOP_K:]
    dest = jnp.sum(jnp.where(idx[..., None] == eids, pstart, 0), axis=-1) + rank
    n_used = (pend[-1] // br).astype(jnp.int32)
    blk = jnp.minimum(jnp.arange(n_blocks, dtype=jnp.int32), n_used - 1)
    block_expert = jnp.minimum(jnp.sum((pend[None, :] <= (blk * br)[:, None]).astype(jnp.int32), axis=1),
                               n_exp - 1)
    group_end = jnp.sum(jnp.where(block_expert[:, None] == eids, pstart + counts, 0), axis=-1)
    block_valid = jnp.clip(group_end - blk * br, 0, br).astype(jnp.int32)
    later = (eids[None, :] > block_expert[:, None]) & (counts[None, :] > 0)
    block_next = jnp.min(jnp.where(later, eids[None, :], n_exp), axis=1)
    block_next = jnp.where(block_next == n_exp, -1, block_next).astype(jnp.int32)
    return dest, block_expert, block_next, block_valid, n_used.reshape(1)


def _moe_layer(x1p, meta_i, gates_b, counts_f, layer, w_gu_all, b_gu, w_dn_all, b_dn, rs):
    _, n_exp, d, f2 = w_gu_all.shape
    n_tok = meta_i.shape[1]
    br = EXPERT_BLOCK_ROWS
    rp = rs // 2
    n_blocks = -(-(n_tok * TOP_K) // br) + n_exp
    dest, block_expert, block_next, block_valid, n_used = _routing_tables(meta_i, counts_f, n_blocks)
    n_rows = n_blocks * br
    xs = _sc_scatter_rows(x1p.reshape(n_tok, rp, LANES), dest.reshape(TOP_K, n_tok // SC_INDEX_ROW, SC_INDEX_ROW),
                          n_rows)
    half = GATE_UP_CHUNK // 2
    b_gu_grouped = b_gu.reshape(n_exp, f2 // GATE_UP_CHUNK, half, 2).transpose(0, 1, 3, 2).reshape(n_exp, 1, f2)
    ys = _expert_call(block_expert, block_next, block_valid, n_used, xs.reshape(n_rows * rp, LANES), layer, w_gu_all,
                      b_gu_grouped, w_dn_all, b_dn.reshape(n_exp, 1, d), rs)
    h = _sc_gather_weighted_sum(ys.reshape(n_rows, rp, LANES),
                                dest.reshape(TOP_K, n_tok // SC_INDEX_ROW, SC_INDEX_ROW), gates_b)
    return h.reshape(n_tok * rs, LANES)


def kernel(x, pool_w, pool_scale, sc_w_in, sc_conv_w, sc_w_out, cf_w_in, cf_b_in, cf_dw_w, cf_dw_b,
           cf_ln_g, cf_ln_b, cf_w_out, cf_b_out, mix_ln_g, mix_ln_b, router_w, router_b,
           moe_w_gu, moe_b_gu, moe_w_dn, moe_b_dn, ffn_ln_g, ffn_ln_b):
    bsz, seq, d = x.shape
    depth = mix_ln_g.shape[0]
    alpha = (2.0 * depth) ** 0.25
    rs = d // LANES
    n_chains = BATCH_CHAINS if bsz % BATCH_CHAINS == 0 else 1
    cb = bsz // n_chains
    sources = [("x", x, c * cb) for c in range(n_chains)]
    ia = ib = ic = 0
    for layer in range(depth):
        kind = layer % 3
        route = (mix_ln_g[layer], mix_ln_b[layer], router_w[layer], router_b[layer])
        if kind == 0:
            mixer, halo = _pool_mix, POOL_HALO
            weights = [pool_w[ia].astype(_BF16), pool_scale[ia].reshape(1, d)]
            ia += 1
        elif kind == 1:
            mixer, halo = _short_conv_mix, SHORT_CONV_HALO
            weights = [sc_w_in[ib].astype(_BF16), sc_conv_w[ib], sc_w_out[ib].astype(_BF16)]
            ib += 1
        else:
            mixer, halo = _conformer_mix, CONFORMER_HALO
            weights = [cf_w_in[ic].astype(_BF16), cf_b_in[ic].reshape(1, 2 * d), cf_dw_w[ic],
                       cf_dw_b[ic].reshape(1, d), cf_ln_g[ic].reshape(1, d), cf_ln_b[ic].reshape(1, d),
                       cf_w_out[ic].astype(_BF16), cf_b_out[ic].reshape(1, d)]
            ic += 1
        routed = [_mixer_call(mixer, src, cb, seq, weights, *route, halo, alpha) for src in sources]
        sources = []
        for x1r, x1p, meta_i, gates_b, counts_f in routed:
            h = _moe_layer(x1p, meta_i, gates_b, counts_f, layer, moe_w_gu, moe_b_gu[layer],
                           moe_w_dn, moe_b_dn[layer], rs)
            sources.append(("moe", x1r, h, ffn_ln_g[layer], ffn_ln_b[layer]))
    out = None
    for c, (_, x1r, h, ln_g, ln_b) in enumerate(sources):
        out = _combine_call(x1r, h, ln_g, ln_b, alpha, rs, bsz * seq, c * cb * seq, out)
    return out.reshape(bsz, seq, d)
```

```python
import functools

import jax
import jax.numpy as jnp
from jax import lax
from jax.experimental import pallas as pl
from jax.experimental.pallas import tpu as pltpu
from jax.experimental.pallas import tpu_sc as plsc

LANES = 128
SUBLANES = 8
TOP_K = 4
POOL_WINDOWS = (2, 4, 8, 16)
POOL_HALO = 16
SHORT_CONV_HALO = 8
CONFORMER_HALO = 32
SWIGLU_LIMIT = 7.0
SWIGLU_ALPHA = 1.702
LN_EPS = 1e-5
TOKEN_BLOCK = 512
EXPERT_BLOCK_ROWS = 2048
EXPERT_SUB_ROWS = 512
COMBINE_BLOCK = 512
BATCH_CHAINS = 2
SC_INDEX_ROW = 128
SC_CHUNK_ROWS = 32
GATE_UP_CHUNK = 2 * LANES
VMEM_LIMIT_BYTES = 56 * 1024 * 1024

_F32 = jnp.float32
_BF16 = jnp.bfloat16


def _layer_norm(z, g, b):
    mu = jnp.mean(z, axis=-1, keepdims=True)
    zc = z - mu
    var = jnp.mean(zc * zc, axis=-1, keepdims=True)
    return zc * lax.rsqrt(var + LN_EPS) * g + b


def _store_rows(row_ref, val, row0=0):
    rows, d = val.shape
    rs = d // LANES
    for j in range(rs):
        row_ref[pl.ds(row0 * rs + j, rows, stride=rs), :] = val[:, j * LANES:(j + 1) * LANES]


def _load_rows(row_ref, rows, rs, row0=0):
    return [row_ref[pl.ds(row0 * rs + j, rows, stride=rs), :] for j in range(rs)]


def _post_norm_and_route(z, g_ref, b_ref, rwt_ref, rb_ref, first,
                         x1r_ref, x1p_ref, mi_ref, mg_ref, cnt_ref, carry_ref):
    n_tok = z.shape[0]
    n_exp = rwt_ref.shape[0]

    @pl.when(first)
    def _():
        carry_ref[...] = jnp.zeros_like(carry_ref)

    x1 = _layer_norm(z, g_ref[...], b_ref[...])
    _store_rows(x1r_ref, x1)
    _store_rows(x1p_ref, _pack_bf16_pairs(x1))

    x_hi = x1.astype(_BF16)
    x_lo = (x1 - x_hi.astype(_F32)).astype(_BF16)
    w = rwt_ref[...]
    w_hi = w.astype(_BF16)
    w_lo = (w - w_hi.astype(_F32)).astype(_BF16)
    nt = (((1,), (1,)), ((), ()))
    logits = (lax.dot_general(w_hi, x_hi, nt, preferred_element_type=_F32)
              + lax.dot_general(w_hi, x_lo, nt, preferred_element_type=_F32)
              + lax.dot_general(w_lo, x_hi, nt, preferred_element_type=_F32)) + rb_ref[...]
    eidx = lax.broadcasted_iota(jnp.int32, logits.shape, 0)
    work = logits
    chosen = jnp.zeros(logits.shape, jnp.bool_)
    vals, idxs = [], []
    for _ in range(TOP_K):
        m = jnp.max(work, axis=0, keepdims=True)
        sel = jnp.min(jnp.where(work == m, eidx, n_exp), axis=0, keepdims=True)
        hit = eidx == sel
        vals.append(m)
        idxs.append(sel)
        chosen = jnp.logical_or(chosen, hit)
        work = jnp.where(hit, -jnp.inf, work)
    exps = [jnp.exp(v - vals[0]) for v in vals]
    denom = functools.reduce(lambda a, b: a + b, exps)
    gate_rows = [e / denom for e in exps] + [jnp.zeros_like(denom)] * (SUBLANES - TOP_K)
    gates_tok = jnp.concatenate(gate_rows, axis=0).T
    lane_group = lax.broadcasted_iota(jnp.int32, (n_tok, LANES), 1) // (LANES // TOP_K)
    spread = jnp.broadcast_to(gates_tok[:, 0:1], (n_tok, LANES))
    for k in range(1, TOP_K):
        spread = jnp.where(lane_group == k, jnp.broadcast_to(gates_tok[:, k:k + 1], (n_tok, LANES)), spread)
    mg_ref[...] = spread

    onehot = chosen.astype(_BF16)
    r = lax.broadcasted_iota(jnp.int32, (n_tok, n_tok), 0)
    c = lax.broadcasted_iota(jnp.int32, (n_tok, n_tok), 1)
    before = (r < c).astype(_BF16)
    cum = jnp.dot(onehot, before, preferred_element_type=_F32) + carry_ref[:, 0:1]
    ranks = [jnp.sum(jnp.where(eidx == s, cum, 0.0), axis=0, keepdims=True) for s in idxs]
    mi_ref[...] = jnp.concatenate(idxs + [rk.astype(jnp.int32) for rk in ranks], axis=0)
    carry_ref[...] = carry_ref[...] + jnp.sum(chosen.astype(_F32), axis=1, keepdims=True)
    cnt_ref[...] = carry_ref[...]


def _pool_mix(x, si, weights, scratch, alpha):
    pw_ref, ps_ref = weights
    (hist_ref,) = scratch
    ts, d = x.shape
    dg = d // len(POOL_WINDOWS)

    @pl.when(si == 0)
    def _():
        hist_ref[0:POOL_HALO, :] = jnp.zeros((POOL_HALO, d), _F32)

    hist_ref[POOL_HALO:POOL_HALO + ts, :] = x
    pos = si * ts + lax.broadcasted_iota(jnp.int32, (ts, 1), 0)
    pieces = []
    for gi, win in enumerate(POOL_WINDOWS):
        c0 = gi * dg
        xg = x[:, c0:c0 + dg]
        assert win & (win - 1) == 0 and win <= POOL_HALO
        ext = hist_ref[:, c0:c0 + dg]
        span = 1
        while span < win:
            ext = ext + pltpu.roll(ext, span, 0)
            span *= 2
        acc = ext[POOL_HALO:, :]
        inv_count = 1.0 / jnp.minimum(pos + 1, win).astype(_F32)
        diff = acc * inv_count - xg
        hg = jnp.dot(diff.astype(_BF16), pw_ref[gi], preferred_element_type=_F32)
        pieces.append(alpha * xg + hg * ps_ref[:, c0:c0 + dg])
    hist_ref[0:POOL_HALO, :] = x[ts - POOL_HALO:, :]
    return jnp.concatenate(pieces, axis=1)


def _short_conv_mix(x, si, weights, scratch, alpha):
    win_ref, cw_ref, wout_ref = weights
    (hist_ref,) = scratch
    ts, d = x.shape
    halo = SHORT_CONV_HALO

    @pl.when(si == 0)
    def _():
        hist_ref[0:halo, :] = jnp.zeros((halo, d), _F32)

    xb = x.astype(_BF16)
    gate_b = jnp.dot(xb, win_ref[:, 0:d], preferred_element_type=_F32)
    gate_c = jnp.dot(xb, win_ref[:, d:2 * d], preferred_element_type=_F32)
    h = jnp.dot(xb, win_ref[:, 2 * d:3 * d], preferred_element_type=_F32)
    v = gate_c * h
    hist_ref[halo:halo + ts, :] = v
    width = cw_ref.shape[0]
    u = cw_ref[width - 1:width, :] * v
    for k in range(width - 1):
        shift = width - 1 - k
        u = u + cw_ref[k:k + 1, :] * hist_ref[halo - shift:halo - shift + ts, :]
    hist_ref[0:halo, :] = v[ts - halo:, :]
    y = jnp.dot((gate_b * u).astype(_BF16), wout_ref[...], preferred_element_type=_F32)
    return alpha * x + y


def _conformer_mix(x, si, weights, scratch, alpha):
    win_ref, bin_ref, dww_ref, dwb_ref, lng_ref, lnb_ref, wout_ref, bout_ref = weights
    (hist_ref,) = scratch
    ts, d = x.shape
    halo = CONFORMER_HALO

    @pl.when(si == 0)
    def _():
        hist_ref[0:halo, :] = jnp.zeros((halo, d), _F32)

    xb = x.astype(_BF16)
    a = jnp.dot(xb, win_ref[:, 0:d], preferred_element_type=_F32) + bin_ref[:, 0:d]
    gate = jnp.dot(xb, win_ref[:, d:2 * d], preferred_element_type=_F32) + bin_ref[:, d:2 * d]
    u = a * jax.nn.sigmoid(gate)
    hist_ref[halo:halo + ts, :] = u
    width = dww_ref.shape[0]
    hist = hist_ref[...]
    acc = dwb_ref[...]
    for r in range(SUBLANES):
        rolled = hist if r == 0 else pltpu.roll(hist, r, 0)
        for q in range(halo // SUBLANES):
            shift = SUBLANES * q + r
            if shift < width:
                k = width - 1 - shift
                start = halo - SUBLANES * q
                acc = acc + dww_ref[k:k + 1, :] * rolled[start:start + ts, :]
    hist_ref[0:halo, :] = u[ts - halo:, :]
    un = _layer_norm(acc, lng_ref[...], lnb_ref[...])
    un = un * jax.nn.sigmoid(un)
    y = jnp.dot(un.astype(_BF16), wout_ref[...], preferred_element_type=_F32) + bout_ref[...]
    return alpha * x + y


def _residual_norm(x1r_ref, h_ref, g_ref, b_ref, rs, alpha):
    tb = x1r_ref.shape[0] // rs
    pieces = [alpha * x1r_ref[pl.ds(j, tb, stride=rs), :] + h_ref[pl.ds(j, tb, stride=rs), :] for j in range(rs)]
    return _layer_norm(jnp.concatenate(pieces, axis=1), g_ref[...], b_ref[...])


def _mixer_kernel(*refs, mix_fn, n_weights, from_moe, alpha, rs):
    bi, si = pl.program_id(0), pl.program_id(1)
    if from_moe:
        xprev_ref, h_ref, cg_ref, cb_ref = refs[:4]
        x = _residual_norm(xprev_ref, h_ref, cg_ref, cb_ref, rs, alpha)
        refs = refs[4:]
    else:
        x = refs[0][0]
        refs = refs[1:]
    weights, refs = refs[:n_weights], refs[n_weights:]
    g_ref, b_ref, rwt_ref, rb_ref = refs[:4]
    x1r_ref, x1p_ref, mi_ref, mg_ref, cnt_ref = refs[4:9]
    scratch, carry_ref = refs[9:-1], refs[-1]
    z = mix_fn(x, si, weights, scratch, alpha)
    _post_norm_and_route(z, g_ref, b_ref, rwt_ref, rb_ref, (bi == 0) & (si == 0),
                         x1r_ref, x1p_ref, mi_ref, mg_ref, cnt_ref, carry_ref)


def _mixer_call(mix_fn, source, bsz, seq, weights, ln_g, ln_b, router_w, router_b, halo, alpha):
    d = router_w.shape[0]
    n_exp = router_w.shape[1]
    ts = min(TOKEN_BLOCK, seq)
    rs = d // LANES
    n_tok = bsz * seq
    nsb = seq // ts

    def full(a):
        nd = a.ndim
        return pl.BlockSpec(a.shape, lambda bi, si, _nd=nd: (0,) * _nd)

    small = [ln_g.reshape(1, d), ln_b.reshape(1, d), router_w.T, router_b.reshape(n_exp, 1)]
    tail = list(weights) + small
    if source[0] == "x":
        _, x, batch_lo = source
        operands = [x] + tail
        in_specs = [pl.BlockSpec((1, ts, d), lambda bi, si: (batch_lo + bi, si, 0))]
    else:
        _, x1r, h, cg, cb = source
        operands = [x1r, h, cg.reshape(1, d), cb.reshape(1, d)] + tail
        in_specs = [
            pl.BlockSpec((ts * rs, LANES), lambda bi, si: (bi * nsb + si, 0)),
            pl.BlockSpec((ts * rs, LANES), lambda bi, si: (bi * nsb + si, 0)),
            pl.BlockSpec((1, d), lambda bi, si: (0, 0)),
            pl.BlockSpec((1, d), lambda bi, si: (0, 0)),
        ]
    in_specs = in_specs + [full(a) for a in tail]
    tok_map = lambda bi, si: (0, bi * nsb + si)
    out_shape = [
        jax.ShapeDtypeStruct((n_tok * rs, LANES), _F32),
        jax.ShapeDtypeStruct((n_tok * rs // 2, LANES), jnp.uint32),
        jax.ShapeDtypeStruct((2 * TOP_K, n_tok), jnp.int32),
        jax.ShapeDtypeStruct((n_tok, LANES), _F32),
        jax.ShapeDtypeStruct((n_exp, LANES), _F32),
    ]
    out_specs = [
        pl.BlockSpec((ts * rs, LANES), lambda bi, si: (bi * nsb + si, 0)),
        pl.BlockSpec((ts * rs // 2, LANES), lambda bi, si: (bi * nsb + si, 0)),
        pl.BlockSpec((2 * TOP_K, ts), tok_map),
        pl.BlockSpec((ts, LANES), lambda bi, si: (bi * nsb + si, 0)),
        pl.BlockSpec((n_exp, LANES), lambda bi, si: (0, 0)),
    ]
    return pl.pallas_call(
        functools.partial(_mixer_kernel, mix_fn=mix_fn, n_weights=len(weights), from_moe=source[0] == "moe",
                          alpha=alpha, rs=rs),
        grid=(bsz, nsb),
        in_specs=in_specs,
        out_specs=out_specs,
        out_shape=out_shape,
        scratch_shapes=[pltpu.VMEM((halo + ts, d), _F32), pltpu.VMEM((n_exp, LANES), _F32)],
        compiler_params=pltpu.CompilerParams(
            dimension_semantics=("arbitrary", "arbitrary"), vmem_limit_bytes=VMEM_LIMIT_BYTES),
        name=mix_fn.__name__.strip("_"),
    )(*operands)


def _sc_workers():
    info = plsc.get_sparse_core_info()
    return info.num_cores, info.num_subcores, info.num_lanes


def _sc_scatter_rows(x3, dest3, n_rows):
    n_cores, n_sub, n_lanes = _sc_workers()
    n_tok, rs, _ = x3.shape
    top_k = dest3.shape[0]
    chunk = SC_CHUNK_ROWS * (LANES * SUBLANES) // (rs * LANES)
    tok_per_w = n_tok // (n_cores * n_sub)
    rows_per_w = tok_per_w // SC_INDEX_ROW
    chunks_per_row = SC_INDEX_ROW // chunk
    assert rows_per_w * SC_INDEX_ROW * n_cores * n_sub == n_tok and chunks_per_row % 2 == 0
    mesh = plsc.VectorSubcoreMesh(core_axis_name="core", subcore_axis_name="subcore")

    @pl.kernel(out_type=jax.ShapeDtypeStruct((n_rows, rs, LANES), x3.dtype), mesh=mesh,
               compiler_params=pltpu.CompilerParams(needs_layout_passes=False),
               scratch_types=[pltpu.VMEM((top_k, rows_per_w, SC_INDEX_ROW), jnp.int32),
                              pltpu.VMEM((2, chunk, rs, LANES), x3.dtype),
                              pltpu.SemaphoreType.DMA((2,)), pltpu.SemaphoreType.DMA((2,))])
    def scatter_kernel(x_hbm, d_hbm, o_hbm, idx_v, buf, rsem, ssem):
        wid = lax.axis_index("subcore") * n_cores + lax.axis_index("core")
        for k in range(top_k):
            pltpu.sync_copy(d_hbm.at[k, pl.ds(wid * rows_per_w, rows_per_w)], idx_v.at[k])
        base = wid * tok_per_w

        def read(j, c, slot):
            return pltpu.make_async_copy(x_hbm.at[pl.ds(base + j * SC_INDEX_ROW + c * chunk, chunk)],
                                         buf.at[slot], rsem.at[slot])

        def scatters(j, c, slot):
            copies = []
            for k in range(top_k):
                for h in range(chunk // n_lanes):
                    rows = idx_v[k, j, pl.ds(c * chunk + h * n_lanes, n_lanes)]
                    copies.append(pltpu.make_async_copy(buf.at[slot, pl.ds(h * n_lanes, n_lanes)],
                                                        o_hbm.at[rows], ssem.at[slot]))
            return copies

        def wait_scatters(j, c, slot):
            for cp in scatters(j, c, slot):
                cp.wait()

        read(0, 0, 0).start()

        def per_index_row(j, carry):
            for c in range(chunks_per_row):
                slot = c % 2
                if c == 0:
                    @pl.when(j > 0)
                    def _():
                        wait_scatters(j - 1, chunks_per_row - 1, 1 - slot)
                    read(j, c + 1, 1 - slot).start()
                elif c < chunks_per_row - 1:
                    wait_scatters(j, c - 1, 1 - slot)
                    read(j, c + 1, 1 - slot).start()
                else:
                    @pl.when(j + 1 < rows_per_w)
                    def _():
                        wait_scatters(j, c - 1, 1 - slot)
                        read(j + 1, 0, 1 - slot).start()
                read(j, c, slot).wait()
                for cp in scatters(j, c, slot):
                    cp.start()
            return carry

        lax.fori_loop(0, rows_per_w, per_index_row, 0)
        wait_scatters(rows_per_w - 1, chunks_per_row - 2, 0)
        wait_scatters(rows_per_w - 1, chunks_per_row - 1, 1)

    return scatter_kernel(x3, dest3)


def _sc_gather_weighted_sum(table3, idx3, gates_b):
    n_cores, n_sub, n_lanes = _sc_workers()
    top_k, n_idx_rows, _ = idx3.shape
    n_tok = n_idx_rows * SC_INDEX_ROW
    rp = table3.shape[1]
    chunk = n_lanes
    tok_per_w = n_tok // (n_cores * n_sub)
    rows_per_w = tok_per_w // SC_INDEX_ROW
    n_chunks = tok_per_w // chunk
    chunks_per_row = SC_INDEX_ROW // chunk
    assert rows_per_w * SC_INDEX_ROW * n_cores * n_sub == n_tok and n_chunks % 2 == 0
    mesh = plsc.VectorSubcoreMesh(core_axis_name="core", subcore_axis_name="subcore")

    @pl.kernel(out_type=jax.ShapeDtypeStruct((n_tok, 2 * rp, LANES), _F32), mesh=mesh,
               compiler_params=pltpu.CompilerParams(needs_layout_passes=False),
               scratch_types=[pltpu.VMEM((top_k, rows_per_w, SC_INDEX_ROW), jnp.int32),
                              pltpu.VMEM((2, top_k, chunk, rp, LANES), jnp.uint32),
                              pltpu.VMEM((2, chunk, LANES), _F32),
                              pltpu.VMEM((2, chunk, 2 * rp, LANES), _F32),
                              pltpu.SemaphoreType.DMA((2,)), pltpu.SemaphoreType.DMA((2,))])
    def gather_sum_kernel(t_hbm, i_hbm, g_hbm, o_hbm, idx_v, ybuf, gbuf, obuf, isem, osem):
        wid = lax.axis_index("subcore") * n_cores + lax.axis_index("core")
        for k in range(top_k):
            pltpu.sync_copy(i_hbm.at[k, pl.ds(wid * rows_per_w, rows_per_w)], idx_v.at[k])
        base = wid * tok_per_w

        def inputs(c, slot):
            j, off = c // chunks_per_row, (c % chunks_per_row) * chunk
            copies = []
            for k in range(top_k):
                rows = idx_v[k, j, pl.ds(off, chunk)]
                copies.append(pltpu.make_async_copy(t_hbm.at[rows], ybuf.at[slot, k], isem.at[slot]))
            copies.append(pltpu.make_async_copy(g_hbm.at[pl.ds(base + c * chunk, chunk)], gbuf.at[slot], isem.at[slot]))
            return copies

        def output(c, slot):
            return pltpu.make_async_copy(obuf.at[slot], o_hbm.at[pl.ds(base + c * chunk, chunk)], osem.at[slot])

        def weighted_sum(slot):
            def token(t, carry):
                gate = [gbuf[slot, t, pl.ds(k * (LANES // top_k), n_lanes)] for k in range(top_k)]
                for r in range(rp):
                    for l in range(LANES // n_lanes):
                        lanes = pl.ds(l * n_lanes, n_lanes)
                        lo = jnp.zeros((n_lanes,), _F32)
                        hi = jnp.zeros((n_lanes,), _F32)
                        for k in range(top_k):
                            w = ybuf[slot, k, t, r, lanes]
                            lo = lo + gate[k] * plsc.bitcast(w << 16, _F32)
                            hi = hi + gate[k] * plsc.bitcast(w & jnp.uint32(0xFFFF0000), _F32)
                        obuf[slot, t, r, lanes] = lo
                        obuf[slot, t, rp + r, lanes] = hi
                return carry

            lax.fori_loop(0, chunk, token, 0)

        for cp in inputs(0, 0):
            cp.start()

        def chunk_pair(g, carry):
            for slot in range(2):
                c = 2 * g + slot

                @pl.when(c + 1 < n_chunks)
                def _():
                    for cp in inputs(c + 1, 1 - slot):
                        cp.start()

                for cp in inputs(c, slot):
                    cp.wait()

                @pl.when(c >= 2)
                def _():
                    output(c - 2, slot).wait()

                weighted_sum(slot)
                output(c, slot).start()
            return carry

        lax.fori_loop(0, n_chunks // 2, chunk_pair, 0)
        output(n_chunks - 2, 0).wait()
        output(n_chunks - 1, 1).wait()

    return gather_sum_kernel(table3, idx3, gates_b)


def _pack_bf16_pairs(v):
    half = v.shape[1] // 2
    lo = lax.bitcast_convert_type(v[:, :half].astype(_BF16).astype(_F32), jnp.uint32)
    hi = lax.bitcast_convert_type(v[:, half:].astype(_BF16).astype(_F32), jnp.uint32)
    return (lo >> 16) | (hi & jnp.uint32(0xFFFF0000))


def _unpack_bf16_pairs(w):
    return (lax.bitcast_convert_type(w << 16, _F32),
            lax.bitcast_convert_type(w & jnp.uint32(0xFFFF0000), _F32))


def _expert_kernel(be_ref, nx_ref, nv_ref, nu_ref, xs_ref, bgu_ref, bdn_ref, wgu_hbm, wdn_hbm, ys_ref,
                   wgu_f, wdn_f, wgu_s, wdn_s, sem, *, rs, layer):
    b = pl.program_id(0)
    n_chunks = wgu_f.shape[1] // GATE_UP_CHUNK
    half = GATE_UP_CHUNK // 2

    def fetch(e):
        return (pltpu.make_async_copy(wgu_hbm.at[layer, e], wgu_f, sem.at[0]),
                pltpu.make_async_copy(wdn_hbm.at[layer, e], wdn_f, sem.at[1]))

    @pl.when(b < nu_ref[0])
    def _():
        @pl.when((b == 0) | (be_ref[b] != be_ref[jnp.maximum(b - 1, 0)]))
        def _():
            @pl.when(b == 0)
            def _():
                for cp in fetch(be_ref[0]):
                    cp.start()

            for cp in fetch(be_ref[b]):
                cp.wait()
            r = lax.broadcasted_iota(jnp.int32, (GATE_UP_CHUNK, GATE_UP_CHUNK), 0)
            c = lax.broadcasted_iota(jnp.int32, (GATE_UP_CHUNK, GATE_UP_CHUNK), 1)
            perm = (r == jnp.where(c < half, 2 * c, 2 * (c - half) + 1)).astype(_BF16)
            for ch in range(n_chunks):
                cols = slice(ch * GATE_UP_CHUNK, (ch + 1) * GATE_UP_CHUNK)
                w = wgu_f[:, cols].astype(_BF16)
                wgu_s[:, cols] = jnp.dot(w, perm, preferred_element_type=_F32).astype(_BF16)
            wdn_s[...] = wdn_f[...].astype(_BF16)

            @pl.when(nx_ref[b] >= 0)
            def _():
                for cp in fetch(nx_ref[b]):
                    cp.start()

        n_valid = nv_ref[b]
        sub = EXPERT_SUB_ROWS

        def sub_block(s, rows=sub):
            row0 = pl.multiple_of(s * sub, sub)
            defined = row0 + lax.broadcasted_iota(jnp.int32, (rows, 1), 0) < n_valid
            words = [jnp.where(defined, w, jnp.uint32(0)) for w in _load_rows(xs_ref, rows, rs // 2, row0)]
            halves = [_unpack_bf16_pairs(w) for w in words]
            x = jnp.concatenate([lo for lo, _ in halves] + [hi for _, hi in halves], axis=1).astype(_BF16)
            h = jnp.dot(x, wgu_s[...], preferred_element_type=_F32) + bgu_ref[0]
            acts = []
            for ch in range(n_chunks):
                g = jnp.minimum(h[:, ch * GATE_UP_CHUNK:ch * GATE_UP_CHUNK + half], SWIGLU_LIMIT)
                up = jnp.clip(h[:, ch * GATE_UP_CHUNK + half:(ch + 1) * GATE_UP_CHUNK], -SWIGLU_LIMIT, SWIGLU_LIMIT)
                acts.append(((up + 1.0) * (g * jax.nn.sigmoid(SWIGLU_ALPHA * g))).astype(_BF16))
            y = jnp.dot(jnp.concatenate(acts, axis=1), wdn_s[...], preferred_element_type=_F32) + bdn_ref[0]
            _store_rows(ys_ref, _pack_bf16_pairs(y), row0)

        n_sub = (n_valid + sub - 1) // sub
        half_tail = (n_valid - (n_sub - 1) * sub) * 2 <= sub
        n_whole = n_sub - half_tail.astype(jnp.int32)

        def pair(i, carry):
            sub_block(2 * i)
            sub_block(2 * i + 1)
            return carry

        lax.fori_loop(0, n_whole // 2, pair, 0)

        @pl.when(n_whole % 2 == 1)
        def _():
            sub_block(n_whole - 1)

        @pl.when(half_tail)
        def _():
            sub_block(n_sub - 1, sub // 2)


def _expert_call(block_expert, block_next, block_valid, n_used, xs, layer, w_gu_all, b_gu_grouped, w_dn_all,
                 b_dn, rs):
    _, n_exp, d, f2 = w_gu_all.shape
    f = f2 // 2
    br = EXPERT_BLOCK_ROWS
    n_blocks = xs.shape[0] // (br * rs // 2)
    assert f2 % GATE_UP_CHUNK == 0 and rs % 2 == 0

    def row_map(b, be, nx, nv, nu):
        return (jnp.minimum(b, nu[0] - 1), 0)

    def w_map(b, be, nx, nv, nu):
        return (be[b], 0, 0)

    return pl.pallas_call(
        functools.partial(_expert_kernel, rs=rs, layer=layer),
        grid_spec=pltpu.PrefetchScalarGridSpec(
            num_scalar_prefetch=4,
            grid=(n_blocks,),
            in_specs=[
                pl.BlockSpec((br * rs // 2, LANES), row_map),
                pl.BlockSpec((1, 1, f2), w_map),
                pl.BlockSpec((1, 1, d), w_map),
                pl.BlockSpec(memory_space=pl.ANY),
                pl.BlockSpec(memory_space=pl.ANY),
            ],
            out_specs=pl.BlockSpec((br * rs // 2, LANES), row_map),
            scratch_shapes=[pltpu.VMEM((d, f2), _F32), pltpu.VMEM((f, d), _F32),
                            pltpu.VMEM((d, f2), _BF16), pltpu.VMEM((f, d), _BF16),
                            pltpu.SemaphoreType.DMA((2,))],
        ),
        out_shape=jax.ShapeDtypeStruct(xs.shape, jnp.uint32),
        compiler_params=pltpu.CompilerParams(
            dimension_semantics=("arbitrary",), vmem_limit_bytes=VMEM_LIMIT_BYTES),
        name="experts",
    )(block_expert, block_next, block_valid, n_used, xs, b_gu_grouped, b_dn, w_gu_all, w_dn_all)


def _combine_kernel(x1r_ref, h_ref, g_ref, b_ref, *rest, rs, alpha):
    rest[-1][...] = _residual_norm(x1r_ref, h_ref, g_ref, b_ref, rs, alpha)


def _combine_call(x1r, h, ln_g, ln_b, alpha, rs, out_tokens, token_lo, out_buf, run_after):
    n_tok = x1r.shape[0] // rs
    d = rs * LANES
    tb = min(COMBINE_BLOCK, n_tok)
    steps = n_tok // tb
    block_lo = token_lo // tb
    operands = [x1r, h, ln_g.reshape(1, d), ln_b.reshape(1, d)]
    in_specs = [
        pl.BlockSpec((tb * rs, LANES), lambda i: (i, 0)),
        pl.BlockSpec((tb * rs, LANES), lambda i: (i, 0)),
        pl.BlockSpec((1, d), lambda i: (0, 0)),
        pl.BlockSpec((1, d), lambda i: (0, 0)),
    ]
    if run_after is not None:
        operands.append(run_after)
        in_specs.append(pl.BlockSpec((SUBLANES, LANES), lambda i: (0, 0)))
    aliases = {}
    if out_buf is not None:
        aliases = {len(operands): 0}
        operands.append(out_buf)
        in_specs.append(pl.BlockSpec(memory_space=pl.ANY))
    return pl.pallas_call(
        functools.partial(_combine_kernel, rs=rs, alpha=alpha),
        grid=(steps,),
        in_specs=in_specs,
        out_specs=pl.BlockSpec((tb, d), lambda i: (block_lo + i, 0)),
        out_shape=jax.ShapeDtypeStruct((out_tokens, d), _F32),
        input_output_aliases=aliases,
        compiler_params=pltpu.CompilerParams(
            dimension_semantics=("arbitrary",), vmem_limit_bytes=VMEM_LIMIT_BYTES),
        name="combine",
    )(*operands)


def _routing_tables(meta_i, counts_f, n_blocks):
    n_exp = counts_f.shape[0]
    br = EXPERT_BLOCK_ROWS
    counts = counts_f[:, 0].astype(jnp.int32)
    padded = ((counts + br - 1) // br) * br
    pend = jnp.cumsum(padded)
    pstart = pend - padded
    eids = jnp.arange(n_exp, dtype=jnp.int32)
    idx, rank = meta_i[:TOP_K], meta_i[TOP_K:]
    dest = jnp.sum(jnp.where(idx[..., None] == eids, pstart, 0), axis=-1) + rank
    n_used = (pend[-1] // br).astype(jnp.int32)
    blk = jnp.minimum(jnp.arange(n_blocks, dtype=jnp.int32), n_used - 1)
    block_expert = jnp.minimum(jnp.sum((pend[None, :] <= (blk * br)[:, None]).astype(jnp.int32), axis=1),
                               n_exp - 1)
    group_end = jnp.sum(jnp.where(block_expert[:, None] == eids, pstart + counts, 0), axis=-1)
    block_valid = jnp.clip(group_end - blk * br, 0, br).astype(jnp.int32)
    later = (eids[None, :] > block_expert[:, None]) & (counts[None, :] > 0)
    block_next = jnp.min(jnp.where(later, eids[None, :], n_exp), axis=1)
    block_next = jnp.where(block_next == n_exp, -1, block_next).astype(jnp.int32)
    return dest, block_expert, block_next, block_valid, n_used.reshape(1)


def _moe_layer(x1p, meta_i, gates_b, counts_f, layer, w_gu_all, b_gu, w_dn_all, b_dn, rs):
    _, n_exp, d, f2 = w_gu_all.shape
    n_tok = meta_i.shape[1]
    br = EXPERT_BLOCK_ROWS
    rp = rs // 2
    n_blocks = -(-(n_tok * TOP_K) // br) + n_exp
    dest, block_expert, block_next, block_valid, n_used = _routing_tables(meta_i, counts_f, n_blocks)
    n_rows = n_blocks * br
    xs = _sc_scatter_rows(x1p.reshape(n_tok, rp, LANES), dest.reshape(TOP_K, n_tok // SC_INDEX_ROW, SC_INDEX_ROW),
                          n_rows)
    half = GATE_UP_CHUNK // 2
    b_gu_grouped = b_gu.reshape(n_exp, f2 // GATE_UP_CHUNK, half, 2).transpose(0, 1, 3, 2).reshape(n_exp, 1, f2)
    ys = _expert_call(block_expert, block_next, block_valid, n_used, xs.reshape(n_rows * rp, LANES), layer, w_gu_all,
                      b_gu_grouped, w_dn_all, b_dn.reshape(n_exp, 1, d), rs)
    h = _sc_gather_weighted_sum(ys.reshape(n_rows, rp, LANES),
                                dest.reshape(TOP_K, n_tok // SC_INDEX_ROW, SC_INDEX_ROW), gates_b)
    return h.reshape(n_tok * rs, LANES), ys


def kernel(x, pool_w, pool_scale, sc_w_in, sc_conv_w, sc_w_out, cf_w_in, cf_b_in, cf_dw_w, cf_dw_b,
           cf_ln_g, cf_ln_b, cf_w_out, cf_b_out, mix_ln_g, mix_ln_b, router_w, router_b,
           moe_w_gu, moe_b_gu, moe_w_dn, moe_b_dn, ffn_ln_g, ffn_ln_b):
    bsz, seq, d = x.shape
    depth = mix_ln_g.shape[0]
    alpha = (2.0 * depth) ** 0.25
    rs = d // LANES
    n_chains = BATCH_CHAINS if bsz % BATCH_CHAINS == 0 else 1
    cb = bsz // n_chains
    sources = [("x", x, c * cb) for c in range(n_chains)]
    ia = ib = ic = 0
    for layer in range(depth):
        kind = layer % 3
        route = (mix_ln_g[layer], mix_ln_b[layer], router_w[layer], router_b[layer])
        if kind == 0:
            mixer, halo = _pool_mix, POOL_HALO
            weights = [pool_w[ia].astype(_BF16), pool_scale[ia].reshape(1, d)]
            ia += 1
        elif kind == 1:
            mixer, halo = _short_conv_mix, SHORT_CONV_HALO
            weights = [sc_w_in[ib].astype(_BF16), sc_conv_w[ib], sc_w_out[ib].astype(_BF16)]
            ib += 1
        else:
            mixer, halo = _conformer_mix, CONFORMER_HALO
            weights = [cf_w_in[ic].astype(_BF16), cf_b_in[ic].reshape(1, 2 * d), cf_dw_w[ic],
                       cf_dw_b[ic].reshape(1, d), cf_ln_g[ic].reshape(1, d), cf_ln_b[ic].reshape(1, d),
                       cf_w_out[ic].astype(_BF16), cf_b_out[ic].reshape(1, d)]
            ic += 1
        routed = [_mixer_call(mixer, src, cb, seq, weights, *route, halo, alpha) for src in sources]
        sources = []
        for x1r, x1p, meta_i, gates_b, counts_f in routed:
            h, ys_last = _moe_layer(x1p, meta_i, gates_b, counts_f, layer, moe_w_gu, moe_b_gu[layer],
                                    moe_w_dn, moe_b_dn[layer], rs)
            sources.append(("moe", x1r, h, ffn_ln_g[layer], ffn_ln_b[layer]))
    out = None
    for c, (_, x1r, h, ln_g, ln_b) in enumerate(sources):
        run_after = ys_last if c + 1 < len(sources) else None
        out = _combine_call(x1r, h, ln_g, ln_b, alpha, rs, bsz * seq, c * cb * seq, out, run_after)
    return out.reshape(bsz, seq, d)
```

```python
import functools

import jax
import jax.numpy as jnp
from jax import lax
from jax.experimental import pallas as pl
from jax.experimental.pallas import tpu as pltpu
from jax.experimental.pallas import tpu_sc as plsc

LANES = 128
SUBLANES = 8
TOP_K = 4
POOL_WINDOWS = (2, 4, 8, 16)
POOL_HALO = 16
SHORT_CONV_HALO = 8
CONFORMER_HALO = 32
SWIGLU_LIMIT = 7.0
SWIGLU_ALPHA = 1.702
LN_EPS = 1e-5
TOKEN_BLOCK = 512
EXPERT_BLOCK_ROWS = 2048
EXPERT_SUB_ROWS = 512
COMBINE_BLOCK = 512
BATCH_CHAINS = 2
SC_INDEX_ROW = 128
SC_CHUNK_ROWS = 32
GATE_UP_CHUNK = 2 * LANES
VMEM_LIMIT_BYTES = 56 * 1024 * 1024

_F32 = jnp.float32
_BF16 = jnp.bfloat16


def _layer_norm(z, g, b):
    mu = jnp.mean(z, axis=-1, keepdims=True)
    zc = z - mu
    var = jnp.mean(zc * zc, axis=-1, keepdims=True)
    return zc * lax.rsqrt(var + LN_EPS) * g + b


def _store_rows(row_ref, val, row0=0):
    rows, d = val.shape
    rs = d // LANES
    for j in range(rs):
        row_ref[pl.ds(row0 * rs + j, rows, stride=rs), :] = val[:, j * LANES:(j + 1) * LANES]


def _load_rows(row_ref, rows, rs, row0=0):
    return [row_ref[pl.ds(row0 * rs + j, rows, stride=rs), :] for j in range(rs)]


def _post_norm_and_route(z, g_ref, b_ref, rwt_ref, rb_ref, first,
                         x1r_ref, x1p_ref, mi_ref, mg_ref, cnt_ref, before_ref, carry_ref):
    n_tok = z.shape[0]
    n_exp = rwt_ref.shape[0]

    @pl.when(first)
    def _():
        carry_ref[...] = jnp.zeros_like(carry_ref)
        r = lax.broadcasted_iota(jnp.int32, (n_tok, n_tok), 0)
        c = lax.broadcasted_iota(jnp.int32, (n_tok, n_tok), 1)
        before_ref[...] = (r < c).astype(_BF16)

    x1 = _layer_norm(z, g_ref[...], b_ref[...])
    _store_rows(x1r_ref, x1)
    x_hi = x1.astype(_BF16)
    x_hi32 = x_hi.astype(_F32)
    _store_rows(x1p_ref, _pack_bf16_pairs(x_hi32))

    x_lo = (x1 - x_hi32).astype(_BF16)
    w = rwt_ref[...]
    w_hi = w.astype(_BF16)
    w_lo = (w - w_hi.astype(_F32)).astype(_BF16)
    nt = (((1,), (1,)), ((), ()))
    logits = (lax.dot_general(w_hi, x_hi, nt, preferred_element_type=_F32)
              + lax.dot_general(w_hi, x_lo, nt, preferred_element_type=_F32)
              + lax.dot_general(w_lo, x_hi, nt, preferred_element_type=_F32)) + rb_ref[...]
    eidx = lax.broadcasted_iota(jnp.int32, logits.shape, 0)
    work = logits
    chosen = jnp.zeros(logits.shape, jnp.bool_)
    vals, idxs = [], []
    for _ in range(TOP_K):
        m = jnp.max(work, axis=0, keepdims=True)
        sel = jnp.min(jnp.where(work == m, eidx, n_exp), axis=0, keepdims=True)
        hit = eidx == sel
        vals.append(m)
        idxs.append(sel)
        chosen = jnp.logical_or(chosen, hit)
        work = jnp.where(hit, -jnp.inf, work)
    exps = [jnp.exp(v - vals[0]) for v in vals]
    denom = functools.reduce(lambda a, b: a + b, exps)
    gate_rows = [e / denom for e in exps] + [jnp.zeros_like(denom)] * (SUBLANES - TOP_K)
    gates_tok = jnp.concatenate(gate_rows, axis=0).T
    lane_group = lax.broadcasted_iota(jnp.int32, (n_tok, LANES), 1) // (LANES // TOP_K)
    spread = jnp.broadcast_to(gates_tok[:, 0:1], (n_tok, LANES))
    for k in range(1, TOP_K):
        spread = jnp.where(lane_group == k, jnp.broadcast_to(gates_tok[:, k:k + 1], (n_tok, LANES)), spread)
    mg_ref[...] = spread

    onehot = chosen.astype(_BF16)
    cum = jnp.dot(onehot, before_ref[...], preferred_element_type=_F32) + carry_ref[:, 0:1]
    ranks = [jnp.sum(jnp.where(eidx == s, cum, 0.0), axis=0, keepdims=True) for s in idxs]
    mi_ref[...] = jnp.concatenate(idxs + [rk.astype(jnp.int32) for rk in ranks], axis=0)
    carry_ref[...] = carry_ref[...] + jnp.sum(chosen.astype(_F32), axis=1, keepdims=True)
    cnt_ref[...] = carry_ref[...]


def _pool_mix(x, si, weights, scratch, alpha):
    pw_ref, ps_ref = weights
    (hist_ref,) = scratch
    ts, d = x.shape
    dg = d // len(POOL_WINDOWS)

    @pl.when(si == 0)
    def _():
        hist_ref[0:POOL_HALO, :] = jnp.zeros((POOL_HALO, d), _F32)

    hist_ref[POOL_HALO:POOL_HALO + ts, :] = x
    pos = si * ts + lax.broadcasted_iota(jnp.int32, (ts, 1), 0)
    pieces = []
    for gi, win in enumerate(POOL_WINDOWS):
        c0 = gi * dg
        xg = x[:, c0:c0 + dg]
        assert win & (win - 1) == 0 and win <= POOL_HALO
        ext = hist_ref[:, c0:c0 + dg]
        span = 1
        while span < win:
            ext = ext + pltpu.roll(ext, span, 0)
            span *= 2
        acc = ext[POOL_HALO:, :]
        inv_count = 1.0 / jnp.minimum(pos + 1, win).astype(_F32)
        diff = acc * inv_count - xg
        hg = jnp.dot(diff.astype(_BF16), pw_ref[gi], preferred_element_type=_F32)
        pieces.append(alpha * xg + hg * ps_ref[:, c0:c0 + dg])
    hist_ref[0:POOL_HALO, :] = x[ts - POOL_HALO:, :]
    return jnp.concatenate(pieces, axis=1)


def _short_conv_mix(x, si, weights, scratch, alpha):
    win_ref, cw_ref, wout_ref = weights
    (hist_ref,) = scratch
    ts, d = x.shape
    halo = SHORT_CONV_HALO

    @pl.when(si == 0)
    def _():
        hist_ref[0:halo, :] = jnp.zeros((halo, d), _F32)

    xb = x.astype(_BF16)
    gate_b = jnp.dot(xb, win_ref[:, 0:d], preferred_element_type=_F32)
    gate_c = jnp.dot(xb, win_ref[:, d:2 * d], preferred_element_type=_F32)
    h = jnp.dot(xb, win_ref[:, 2 * d:3 * d], preferred_element_type=_F32)
    v = gate_c * h
    hist_ref[halo:halo + ts, :] = v
    width = cw_ref.shape[0]
    u = cw_ref[width - 1:width, :] * v
    for k in range(width - 1):
        shift = width - 1 - k
        u = u + cw_ref[k:k + 1, :] * hist_ref[halo - shift:halo - shift + ts, :]
    hist_ref[0:halo, :] = v[ts - halo:, :]
    y = jnp.dot((gate_b * u).astype(_BF16), wout_ref[...], preferred_element_type=_F32)
    return alpha * x + y


def _conformer_mix(x, si, weights, scratch, alpha):
    win_ref, bin_ref, dww_ref, dwb_ref, lng_ref, lnb_ref, wout_ref, bout_ref = weights
    (hist_ref,) = scratch
    ts, d = x.shape
    halo = CONFORMER_HALO

    @pl.when(si == 0)
    def _():
        hist_ref[0:halo, :] = jnp.zeros((halo, d), _F32)

    xb = x.astype(_BF16)
    a = jnp.dot(xb, win_ref[:, 0:d], preferred_element_type=_F32) + bin_ref[:, 0:d]
    gate = jnp.dot(xb, win_ref[:, d:2 * d], preferred_element_type=_F32) + bin_ref[:, d:2 * d]
    u = a * jax.nn.sigmoid(gate)
    hist_ref[halo:halo + ts, :] = u
    width = dww_ref.shape[0]
    hist = hist_ref[...]
    acc = dwb_ref[...]
    for r in range(SUBLANES):
        rolled = hist if r == 0 else pltpu.roll(hist, r, 0)
        for q in range(halo // SUBLANES):
            shift = SUBLANES * q + r
            if shift < width:
                k = width - 1 - shift
                start = halo - SUBLANES * q
                acc = acc + dww_ref[k:k + 1, :] * rolled[start:start + ts, :]
    hist_ref[0:halo, :] = u[ts - halo:, :]
    un = _layer_norm(acc, lng_ref[...], lnb_ref[...])
    un = un * jax.nn.sigmoid(un)
    y = jnp.dot(un.astype(_BF16), wout_ref[...], preferred_element_type=_F32) + bout_ref[...]
    return alpha * x + y


def _residual_norm(x1r_ref, h_ref, g_ref, b_ref, rs, alpha):
    tb = x1r_ref.shape[0] // rs
    pieces = [alpha * x1r_ref[pl.ds(j, tb, stride=rs), :] + h_ref[pl.ds(j, tb, stride=rs), :] for j in range(rs)]
    return _layer_norm(jnp.concatenate(pieces, axis=1), g_ref[...], b_ref[...])


def _mixer_kernel(*refs, mix_fn, n_weights, from_moe, alpha, rs):
    bi, si = pl.program_id(0), pl.program_id(1)
    if from_moe:
        xprev_ref, h_ref, cg_ref, cb_ref = refs[:4]
        x = _residual_norm(xprev_ref, h_ref, cg_ref, cb_ref, rs, alpha)
        refs = refs[4:]
    else:
        x = refs[0][0]
        refs = refs[1:]
    weights, refs = refs[:n_weights], refs[n_weights:]
    g_ref, b_ref, rwt_ref, rb_ref = refs[:4]
    x1r_ref, x1p_ref, mi_ref, mg_ref, cnt_ref = refs[4:9]
    scratch, before_ref, carry_ref = refs[9:-2], refs[-2], refs[-1]
    z = mix_fn(x, si, weights, scratch, alpha)
    _post_norm_and_route(z, g_ref, b_ref, rwt_ref, rb_ref, (bi == 0) & (si == 0),
                         x1r_ref, x1p_ref, mi_ref, mg_ref, cnt_ref, before_ref, carry_ref)


def _mixer_call(mix_fn, source, bsz, seq, weights, ln_g, ln_b, router_w, router_b, halo, alpha):
    d = router_w.shape[0]
    n_exp = router_w.shape[1]
    ts = min(TOKEN_BLOCK, seq)
    rs = d // LANES
    n_tok = bsz * seq
    nsb = seq // ts

    def full(a):
        nd = a.ndim
        return pl.BlockSpec(a.shape, lambda bi, si, _nd=nd: (0,) * _nd)

    small = [ln_g.reshape(1, d), ln_b.reshape(1, d), router_w.T, router_b.reshape(n_exp, 1)]
    tail = list(weights) + small
    if source[0] == "x":
        _, x, batch_lo = source
        operands = [x] + tail
        in_specs = [pl.BlockSpec((1, ts, d), lambda bi, si: (batch_lo + bi, si, 0))]
    else:
        _, x1r, h, cg, cb = source
        operands = [x1r, h, cg.reshape(1, d), cb.reshape(1, d)] + tail
        in_specs = [
            pl.BlockSpec((ts * rs, LANES), lambda bi, si: (bi * nsb + si, 0)),
            pl.BlockSpec((ts * rs, LANES), lambda bi, si: (bi * nsb + si, 0)),
            pl.BlockSpec((1, d), lambda bi, si: (0, 0)),
            pl.BlockSpec((1, d), lambda bi, si: (0, 0)),
        ]
    in_specs = in_specs + [full(a) for a in tail]
    tok_map = lambda bi, si: (0, bi * nsb + si)
    out_shape = [
        jax.ShapeDtypeStruct((n_tok * rs, LANES), _F32),
        jax.ShapeDtypeStruct((n_tok * rs // 2, LANES), jnp.uint32),
        jax.ShapeDtypeStruct((2 * TOP_K, n_tok), jnp.int32),
        jax.ShapeDtypeStruct((n_tok, LANES), _F32),
        jax.ShapeDtypeStruct((n_exp, LANES), _F32),
    ]
    out_specs = [
        pl.BlockSpec((ts * rs, LANES), lambda bi, si: (bi * nsb + si, 0)),
        pl.BlockSpec((ts * rs // 2, LANES), lambda bi, si: (bi * nsb + si, 0)),
        pl.BlockSpec((2 * TOP_K, ts), tok_map),
        pl.BlockSpec((ts, LANES), lambda bi, si: (bi * nsb + si, 0)),
        pl.BlockSpec((n_exp, LANES), lambda bi, si: (0, 0)),
    ]
    return pl.pallas_call(
        functools.partial(_mixer_kernel, mix_fn=mix_fn, n_weights=len(weights), from_moe=source[0] == "moe",
                          alpha=alpha, rs=rs),
        grid=(bsz, nsb),
        in_specs=in_specs,
        out_specs=out_specs,
        out_shape=out_shape,
        scratch_shapes=[pltpu.VMEM((halo + ts, d), _F32), pltpu.VMEM((ts, ts), _BF16),
                        pltpu.VMEM((n_exp, LANES), _F32)],
        compiler_params=pltpu.CompilerParams(
            dimension_semantics=("arbitrary", "arbitrary"), vmem_limit_bytes=VMEM_LIMIT_BYTES),
        name=mix_fn.__name__.strip("_"),
    )(*operands)


def _sc_workers():
    info = plsc.get_sparse_core_info()
    return info.num_cores, info.num_subcores, info.num_lanes


def _sc_scatter_rows(x3, dest3, n_rows):
    n_cores, n_sub, n_lanes = _sc_workers()
    n_tok, rs, _ = x3.shape
    top_k = dest3.shape[0]
    chunk = SC_CHUNK_ROWS * (LANES * SUBLANES) // (rs * LANES)
    tok_per_w = n_tok // (n_cores * n_sub)
    rows_per_w = tok_per_w // SC_INDEX_ROW
    chunks_per_row = SC_INDEX_ROW // chunk
    assert rows_per_w * SC_INDEX_ROW * n_cores * n_sub == n_tok and chunks_per_row % 2 == 0
    mesh = plsc.VectorSubcoreMesh(core_axis_name="core", subcore_axis_name="subcore")

    @pl.kernel(out_type=jax.ShapeDtypeStruct((n_rows, rs, LANES), x3.dtype), mesh=mesh,
               compiler_params=pltpu.CompilerParams(needs_layout_passes=False),
               scratch_types=[pltpu.VMEM((top_k, rows_per_w, SC_INDEX_ROW), jnp.int32),
                              pltpu.VMEM((2, chunk, rs, LANES), x3.dtype),
                              pltpu.SemaphoreType.DMA((2,)), pltpu.SemaphoreType.DMA((2,))])
    def scatter_kernel(x_hbm, d_hbm, o_hbm, idx_v, buf, rsem, ssem):
        wid = lax.axis_index("subcore") * n_cores + lax.axis_index("core")
        for k in range(top_k):
            pltpu.sync_copy(d_hbm.at[k, pl.ds(wid * rows_per_w, rows_per_w)], idx_v.at[k])
        base = wid * tok_per_w

        def read(j, c, slot):
            return pltpu.make_async_copy(x_hbm.at[pl.ds(base + j * SC_INDEX_ROW + c * chunk, chunk)],
                                         buf.at[slot], rsem.at[slot])

        def scatters(j, c, slot):
            copies = []
            for k in range(top_k):
                for h in range(chunk // n_lanes):
                    rows = idx_v[k, j, pl.ds(c * chunk + h * n_lanes, n_lanes)]
                    copies.append(pltpu.make_async_copy(buf.at[slot, pl.ds(h * n_lanes, n_lanes)],
                                                        o_hbm.at[rows], ssem.at[slot]))
            return copies

        def wait_scatters(j, c, slot):
            for cp in scatters(j, c, slot):
                cp.wait()

        read(0, 0, 0).start()

        def per_index_row(j, carry):
            for c in range(chunks_per_row):
                slot = c % 2
                if c == 0:
                    @pl.when(j > 0)
                    def _():
                        wait_scatters(j - 1, chunks_per_row - 1, 1 - slot)
                    read(j, c + 1, 1 - slot).start()
                elif c < chunks_per_row - 1:
                    wait_scatters(j, c - 1, 1 - slot)
                    read(j, c + 1, 1 - slot).start()
                else:
                    @pl.when(j + 1 < rows_per_w)
                    def _():
                        wait_scatters(j, c - 1, 1 - slot)
                        read(j + 1, 0, 1 - slot).start()
                read(j, c, slot).wait()
                for cp in scatters(j, c, slot):
                    cp.start()
            return carry

        lax.fori_loop(0, rows_per_w, per_index_row, 0)
        wait_scatters(rows_per_w - 1, chunks_per_row - 2, 0)
        wait_scatters(rows_per_w - 1, chunks_per_row - 1, 1)

    return scatter_kernel(x3, dest3)


def _sc_gather_weighted_sum(table3, idx3, gates_b):
    n_cores, n_sub, n_lanes = _sc_workers()
    top_k, n_idx_rows, _ = idx3.shape
    n_tok = n_idx_rows * SC_INDEX_ROW
    rp = table3.shape[1]
    chunk = n_lanes
    tok_per_w = n_tok // (n_cores * n_sub)
    rows_per_w = tok_per_w // SC_INDEX_ROW
    n_chunks = tok_per_w // chunk
    chunks_per_row = SC_INDEX_ROW // chunk
    assert rows_per_w * SC_INDEX_ROW * n_cores * n_sub == n_tok and n_chunks % 2 == 0
    mesh = plsc.VectorSubcoreMesh(core_axis_name="core", subcore_axis_name="subcore")

    @pl.kernel(out_type=jax.ShapeDtypeStruct((n_tok, 2 * rp, LANES), _F32), mesh=mesh,
               compiler_params=pltpu.CompilerParams(needs_layout_passes=False),
               scratch_types=[pltpu.VMEM((top_k, rows_per_w, SC_INDEX_ROW), jnp.int32),
                              pltpu.VMEM((2, top_k, chunk, rp, LANES), jnp.uint32),
                              pltpu.VMEM((2, chunk, LANES), _F32),
                              pltpu.VMEM((2, chunk, 2 * rp, LANES), _F32),
                              pltpu.SemaphoreType.DMA((2,)), pltpu.SemaphoreType.DMA((2,))])
    def gather_sum_kernel(t_hbm, i_hbm, g_hbm, o_hbm, idx_v, ybuf, gbuf, obuf, isem, osem):
        wid = lax.axis_index("subcore") * n_cores + lax.axis_index("core")
        for k in range(top_k):
            pltpu.sync_copy(i_hbm.at[k, pl.ds(wid * rows_per_w, rows_per_w)], idx_v.at[k])
        base = wid * tok_per_w

        def inputs(c, slot):
            j, off = c // chunks_per_row, (c % chunks_per_row) * chunk
            copies = []
            for k in range(top_k):
                rows = idx_v[k, j, pl.ds(off, chunk)]
                copies.append(pltpu.make_async_copy(t_hbm.at[rows], ybuf.at[slot, k], isem.at[slot]))
            copies.append(pltpu.make_async_copy(g_hbm.at[pl.ds(base + c * chunk, chunk)], gbuf.at[slot], isem.at[slot]))
            return copies

        def output(c, slot):
            return pltpu.make_async_copy(obuf.at[slot], o_hbm.at[pl.ds(base + c * chunk, chunk)], osem.at[slot])

        def weighted_sum(slot):
            def token(t, carry):
                gate = [gbuf[slot, t, pl.ds(k * (LANES // top_k), n_lanes)] for k in range(top_k)]
                for r in range(rp):
                    for l in range(LANES // n_lanes):
                        lanes = pl.ds(l * n_lanes, n_lanes)
                        lo = jnp.zeros((n_lanes,), _F32)
                        hi = jnp.zeros((n_lanes,), _F32)
                        for k in range(top_k):
                            w = ybuf[slot, k, t, r, lanes]
                            lo = lo + gate[k] * plsc.bitcast(w << 16, _F32)
                            hi = hi + gate[k] * plsc.bitcast(w & jnp.uint32(0xFFFF0000), _F32)
                        obuf[slot, t, r, lanes] = lo
                        obuf[slot, t, rp + r, lanes] = hi
                return carry

            lax.fori_loop(0, chunk, token, 0)

        for cp in inputs(0, 0):
            cp.start()

        def chunk_pair(g, carry):
            for slot in range(2):
                c = 2 * g + slot

                @pl.when(c + 1 < n_chunks)
                def _():
                    for cp in inputs(c + 1, 1 - slot):
                        cp.start()

                for cp in inputs(c, slot):
                    cp.wait()

                @pl.when(c >= 2)
                def _():
                    output(c - 2, slot).wait()

                weighted_sum(slot)
                output(c, slot).start()
            return carry

        lax.fori_loop(0, n_chunks // 2, chunk_pair, 0)
        output(n_chunks - 2, 0).wait()
        output(n_chunks - 1, 1).wait()

    return gather_sum_kernel(table3, idx3, gates_b)


def _pack_bf16_pairs(v):
    half = v.shape[1] // 2
    lo = lax.bitcast_convert_type(v[:, :half], jnp.uint32)
    hi = lax.bitcast_convert_type(v[:, half:], jnp.uint32)
    return (lo >> 16) | (hi & jnp.uint32(0xFFFF0000))


def _unpack_bf16_pairs(w):
    return (lax.bitcast_convert_type(w << 16, _F32),
            lax.bitcast_convert_type(w & jnp.uint32(0xFFFF0000), _F32))


def _expert_kernel(be_ref, nx_ref, nv_ref, nu_ref, xs_ref, bgu_ref, bdn_ref, wgu_hbm, wdn_hbm, ys_ref,
                   wgu_f, wdn_f, wgu_s, wdn_s, sem, *, rs, layer):
    b = pl.program_id(0)
    n_chunks = wgu_f.shape[1] // GATE_UP_CHUNK
    half = GATE_UP_CHUNK // 2

    def fetch(e):
        return (pltpu.make_async_copy(wgu_hbm.at[layer, e], wgu_f, sem.at[0]),
                pltpu.make_async_copy(wdn_hbm.at[layer, e], wdn_f, sem.at[1]))

    @pl.when(b < nu_ref[0])
    def _():
        @pl.when((b == 0) | (be_ref[b] != be_ref[jnp.maximum(b - 1, 0)]))
        def _():
            @pl.when(b == 0)
            def _():
                for cp in fetch(be_ref[0]):
                    cp.start()

            for cp in fetch(be_ref[b]):
                cp.wait()
            r = lax.broadcasted_iota(jnp.int32, (GATE_UP_CHUNK, GATE_UP_CHUNK), 0)
            c = lax.broadcasted_iota(jnp.int32, (GATE_UP_CHUNK, GATE_UP_CHUNK), 1)
            perm = (r == jnp.where(c < half, 2 * c, 2 * (c - half) + 1)).astype(_BF16)
            for ch in range(n_chunks):
                cols = slice(ch * GATE_UP_CHUNK, (ch + 1) * GATE_UP_CHUNK)
                w = wgu_f[:, cols].astype(_BF16)
                wgu_s[:, cols] = jnp.dot(w, perm, preferred_element_type=_F32).astype(_BF16)
            wdn_s[...] = wdn_f[...].astype(_BF16)

            @pl.when(nx_ref[b] >= 0)
            def _():
                for cp in fetch(nx_ref[b]):
                    cp.start()

        n_valid = nv_ref[b]
        sub = EXPERT_SUB_ROWS

        def sub_block(s, rows=sub):
            row0 = pl.multiple_of(s * sub, sub)
            defined = row0 + lax.broadcasted_iota(jnp.int32, (rows, 1), 0) < n_valid
            words = [jnp.where(defined, w, jnp.uint32(0)) for w in _load_rows(xs_ref, rows, rs // 2, row0)]
            halves = [_unpack_bf16_pairs(w) for w in words]
            x = jnp.concatenate([lo for lo, _ in halves] + [hi for _, hi in halves], axis=1).astype(_BF16)
            h = jnp.dot(x, wgu_s[...], preferred_element_type=_F32) + bgu_ref[0]
            acts = []
            for ch in range(n_chunks):
                g = jnp.minimum(h[:, ch * GATE_UP_CHUNK:ch * GATE_UP_CHUNK + half], SWIGLU_LIMIT)
                up = jnp.clip(h[:, ch * GATE_UP_CHUNK + half:(ch + 1) * GATE_UP_CHUNK], -SWIGLU_LIMIT, SWIGLU_LIMIT)
                acts.append(((up + 1.0) * (g * jax.nn.sigmoid(SWIGLU_ALPHA * g))).astype(_BF16))
            y = jnp.dot(jnp.concatenate(acts, axis=1), wdn_s[...], preferred_element_type=_F32) + bdn_ref[0]
            _store_rows(ys_ref, _pack_bf16_pairs(y.astype(_BF16).astype(_F32)), row0)

        n_sub = (n_valid + sub - 1) // sub
        half_tail = (n_valid - (n_sub - 1) * sub) * 2 <= sub
        n_whole = n_sub - half_tail.astype(jnp.int32)

        def pair(i, carry):
            sub_block(2 * i)
            sub_block(2 * i + 1)
            return carry

        lax.fori_loop(0, n_whole // 2, pair, 0)

        @pl.when(n_whole % 2 == 1)
        def _():
            sub_block(n_whole - 1)

        @pl.when(half_tail)
        def _():
            sub_block(n_sub - 1, sub // 2)


def _expert_call(block_expert, block_next, block_valid, n_used, xs, layer, w_gu_all, b_gu_grouped, w_dn_all,
                 b_dn, rs):
    _, n_exp, d, f2 = w_gu_all.shape
    f = f2 // 2
    br = EXPERT_BLOCK_ROWS
    n_blocks = xs.shape[0] // (br * rs // 2)
    assert f2 % GATE_UP_CHUNK == 0 and rs % 2 == 0

    def row_map(b, be, nx, nv, nu):
        return (jnp.minimum(b, nu[0] - 1), 0)

    def w_map(b, be, nx, nv, nu):
        return (be[b], 0, 0)

    return pl.pallas_call(
        functools.partial(_expert_kernel, rs=rs, layer=layer),
        grid_spec=pltpu.PrefetchScalarGridSpec(
            num_scalar_prefetch=4,
            grid=(n_blocks,),
            in_specs=[
                pl.BlockSpec((br * rs // 2, LANES), row_map),
                pl.BlockSpec((1, 1, f2), w_map),
                pl.BlockSpec((1, 1, d), w_map),
                pl.BlockSpec(memory_space=pl.ANY),
                pl.BlockSpec(memory_space=pl.ANY),
            ],
            out_specs=pl.BlockSpec((br * rs // 2, LANES), row_map),
            scratch_shapes=[pltpu.VMEM((d, f2), _F32), pltpu.VMEM((f, d), _F32),
                            pltpu.VMEM((d, f2), _BF16), pltpu.VMEM((f, d), _BF16),
                            pltpu.SemaphoreType.DMA((2,))],
        ),
        out_shape=jax.ShapeDtypeStruct(xs.shape, jnp.uint32),
        compiler_params=pltpu.CompilerParams(
            dimension_semantics=("arbitrary",), vmem_limit_bytes=VMEM_LIMIT_BYTES),
        name="experts",
    )(block_expert, block_next, block_valid, n_used, xs, b_gu_grouped, b_dn, w_gu_all, w_dn_all)


def _combine_kernel(x1r_ref, h_ref, g_ref, b_ref, *rest, rs, alpha):
    rest[-1][...] = _residual_norm(x1r_ref, h_ref, g_ref, b_ref, rs, alpha)


def _combine_call(x1r, h, ln_g, ln_b, alpha, rs, out_tokens, token_lo, out_buf):
    n_tok = x1r.shape[0] // rs
    d = rs * LANES
    tb = min(COMBINE_BLOCK, n_tok)
    steps = n_tok // tb
    block_lo = token_lo // tb
    operands = [x1r, h, ln_g.reshape(1, d), ln_b.reshape(1, d)]
    in_specs = [
        pl.BlockSpec((tb * rs, LANES), lambda i: (i, 0)),
        pl.BlockSpec((tb * rs, LANES), lambda i: (i, 0)),
        pl.BlockSpec((1, d), lambda i: (0, 0)),
        pl.BlockSpec((1, d), lambda i: (0, 0)),
    ]
    aliases = {}
    if out_buf is not None:
        aliases = {len(operands): 0}
        operands.append(out_buf)
        in_specs.append(pl.BlockSpec(memory_space=pl.ANY))
    return pl.pallas_call(
        functools.partial(_combine_kernel, rs=rs, alpha=alpha),
        grid=(steps,),
        in_specs=in_specs,
        out_specs=pl.BlockSpec((tb, d), lambda i: (block_lo + i, 0)),
        out_shape=jax.ShapeDtypeStruct((out_tokens, d), _F32),
        input_output_aliases=aliases,
        compiler_params=pltpu.CompilerParams(
            dimension_semantics=("arbitrary",), vmem_limit_bytes=VMEM_LIMIT_BYTES),
        name="combine",
    )(*operands)


def _routing_tables(meta_i, counts_f, n_blocks):
    n_exp = counts_f.shape[0]
    br = EXPERT_BLOCK_ROWS
    counts = counts_f[:, 0].astype(jnp.int32)
    padded = ((counts + br - 1) // br) * br
    pend = jnp.cumsum(padded)
    pstart = pend - padded
    eids = jnp.arange(n_exp, dtype=jnp.int32)
    idx, rank = meta_i[:TOP_K], meta_i[TOP_K:]
    dest = jnp.sum(jnp.where(idx[..., None] == eids, pstart, 0), axis=-1) + rank
    n_used = (pend[-1] // br).astype(jnp.int32)
    blk = jnp.minimum(jnp.arange(n_blocks, dtype=jnp.int32), n_used - 1)
    block_expert = jnp.minimum(jnp.sum((pend[None, :] <= (blk * br)[:, None]).astype(jnp.int32), axis=1),
                               n_exp - 1)
    group_end = jnp.sum(jnp.where(block_expert[:, None] == eids, pstart + counts, 0), axis=-1)
    block_valid = jnp.clip(group_end - blk * br, 0, br).astype(jnp.int32)
    later = (eids[None, :] > block_expert[:, None]) & (counts[None, :] > 0)
    block_next = jnp.min(jnp.where(later, eids[None, :], n_exp), axis=1)
    block_next = jnp.where(block_next == n_exp, -1, block_next).astype(jnp.int32)
    return dest, block_expert, block_next, block_valid, n_used.reshape(1)


def _moe_layer(x1p, meta_i, gates_b, counts_f, layer, w_gu_all, b_gu, w_dn_all, b_dn, rs):
    _, n_exp, d, f2 = w_gu_all.shape
    n_tok = meta_i.shape[1]
    br = EXPERT_BLOCK_ROWS
    rp = rs // 2
    n_blocks = -(-(n_tok * TOP_K) // br) + n_exp
    dest, block_expert, block_next, block_valid, n_used = _routing_tables(meta_i, counts_f, n_blocks)
    n_rows = n_blocks * br
    xs = _sc_scatter_rows(x1p.reshape(n_tok, rp, LANES), dest.reshape(TOP_K, n_tok // SC_INDEX_ROW, SC_INDEX_ROW),
                          n_rows)
    half = GATE_UP_CHUNK // 2
    b_gu_grouped = b_gu.reshape(n_exp, f2 // GATE_UP_CHUNK, half, 2).transpose(0, 1, 3, 2).reshape(n_exp, 1, f2)
    ys = _expert_call(block_expert, block_next, block_valid, n_used, xs.reshape(n_rows * rp, LANES), layer, w_gu_all,
                      b_gu_grouped, w_dn_all, b_dn.reshape(n_exp, 1, d), rs)
    h = _sc_gather_weighted_sum(ys.reshape(n_rows, rp, LANES),
                                dest.reshape(TOP_K, n_tok // SC_INDEX_ROW, SC_INDEX_ROW), gates_b)
    return h.reshape(n_tok * rs, LANES)


def kernel(x, pool_w, pool_scale, sc_w_in, sc_conv_w, sc_w_out, cf_w_in, cf_b_in, cf_dw_w, cf_dw_b,
           cf_ln_g, cf_ln_b, cf_w_out, cf_b_out, mix_ln_g, mix_ln_b, router_w, router_b,
           moe_w_gu, moe_b_gu, moe_w_dn, moe_b_dn, ffn_ln_g, ffn_ln_b):
    bsz, seq, d = x.shape
    depth = mix_ln_g.shape[0]
    alpha = (2.0 * depth) ** 0.25
    rs = d // LANES
    n_chains = BATCH_CHAINS if bsz % BATCH_CHAINS == 0 else 1
    cb = bsz // n_chains
    sources = [("x", x, c * cb) for c in range(n_chains)]
    ia = ib = ic = 0
    for layer in range(depth):
        kind = layer % 3
        route = (mix_ln_g[layer], mix_ln_b[layer], router_w[layer], router_b[layer])
        if kind == 0:
            mixer, halo = _pool_mix, POOL_HALO
            weights = [pool_w[ia].astype(_BF16), pool_scale[ia].reshape(1, d)]
            ia += 1
        elif kind == 1:
            mixer, halo = _short_conv_mix, SHORT_CONV_HALO
            weights = [sc_w_in[ib].astype(_BF16), sc_conv_w[ib], sc_w_out[ib].astype(_BF16)]
            ib += 1
        else:
            mixer, halo = _conformer_mix, CONFORMER_HALO
            weights = [cf_w_in[ic].astype(_BF16), cf_b_in[ic].reshape(1, 2 * d), cf_dw_w[ic],
                       cf_dw_b[ic].reshape(1, d), cf_ln_g[ic].reshape(1, d), cf_ln_b[ic].reshape(1, d),
                       cf_w_out[ic].astype(_BF16), cf_b_out[ic].reshape(1, d)]
            ic += 1
        routed = [_mixer_call(mixer, src, cb, seq, weights, *route, halo, alpha) for src in sources]
        sources = []
        for x1r, x1p, meta_i, gates_b, counts_f in routed:
            h = _moe_layer(x1p, meta_i, gates_b, counts_f, layer, moe_w_gu, moe_b_gu[layer],
                           moe_w_dn, moe_b_dn[layer], rs)
            sources.append(("moe", x1r, h, ffn_ln_g[layer], ffn_ln_b[layer]))
    out = None
    for c, (_, x1r, h, ln_g, ln_b) in enumerate(sources):
        out = _combine_call(x1r, h, ln_g, ln_b, alpha, rs, bsz * seq, c * cb * seq, out)
    return out.reshape(bsz, seq, d)
```

```python
import functools

import jax
import jax.numpy as jnp
from jax import lax
from jax.experimental import pallas as pl
from jax.experimental.pallas import tpu as pltpu
from jax.experimental.pallas import tpu_sc as plsc

LANES = 128
SUBLANES = 8
TOP_K = 4
POOL_WINDOWS = (2, 4, 8, 16)
POOL_HALO = 16
SHORT_CONV_HALO = 8
CONFORMER_HALO = 32
SWIGLU_LIMIT = 7.0
SWIGLU_ALPHA = 1.702
LN_EPS = 1e-5
TOKEN_BLOCK = 512
MATMUL_MIXER_TOKEN_BLOCK = 1024
EXPERT_BLOCK_ROWS = 2048
EXPERT_SUB_ROWS = 512
COMBINE_BLOCK = 512
BATCH_CHAINS = 2
SC_INDEX_ROW = 128
SC_CHUNK_ROWS = 32
GATE_UP_CHUNK = 2 * LANES
VMEM_LIMIT_BYTES = 56 * 1024 * 1024

_F32 = jnp.float32
_BF16 = jnp.bfloat16


def _layer_norm(z, g, b):
    mu = jnp.mean(z, axis=-1, keepdims=True)
    zc = z - mu
    var = jnp.mean(zc * zc, axis=-1, keepdims=True)
    return zc * lax.rsqrt(var + LN_EPS) * g + b


def _store_rows(row_ref, val, row0=0):
    rows, d = val.shape
    rs = d // LANES
    for j in range(rs):
        row_ref[pl.ds(row0 * rs + j, rows, stride=rs), :] = val[:, j * LANES:(j + 1) * LANES]


def _load_rows(row_ref, rows, rs, row0=0):
    return [row_ref[pl.ds(row0 * rs + j, rows, stride=rs), :] for j in range(rs)]


def _post_norm_and_route(z, g_ref, b_ref, rwt_ref, rb_ref, first,
                         x1r_ref, x1p_ref, mi_ref, mg_ref, cnt_ref, carry_ref):
    n_tok = z.shape[0]
    n_exp = rwt_ref.shape[0]

    @pl.when(first)
    def _():
        carry_ref[...] = jnp.zeros_like(carry_ref)

    x1 = _layer_norm(z, g_ref[...], b_ref[...])
    _store_rows(x1r_ref, x1)
    _store_rows(x1p_ref, _pack_bf16_pairs(x1))

    x_hi = x1.astype(_BF16)
    x_lo = (x1 - x_hi.astype(_F32)).astype(_BF16)
    w = rwt_ref[...]
    w_hi = w.astype(_BF16)
    w_lo = (w - w_hi.astype(_F32)).astype(_BF16)
    nt = (((1,), (1,)), ((), ()))
    logits = (lax.dot_general(w_hi, x_hi, nt, preferred_element_type=_F32)
              + lax.dot_general(w_hi, x_lo, nt, preferred_element_type=_F32)
              + lax.dot_general(w_lo, x_hi, nt, preferred_element_type=_F32)) + rb_ref[...]
    eidx = lax.broadcasted_iota(jnp.int32, logits.shape, 0)
    work = logits
    chosen = jnp.zeros(logits.shape, jnp.bool_)
    vals, idxs = [], []
    for _ in range(TOP_K):
        m = jnp.max(work, axis=0, keepdims=True)
        sel = jnp.min(jnp.where(work == m, eidx, n_exp), axis=0, keepdims=True)
        hit = eidx == sel
        vals.append(m)
        idxs.append(sel)
        chosen = jnp.logical_or(chosen, hit)
        work = jnp.where(hit, -jnp.inf, work)
    exps = [jnp.exp(v - vals[0]) for v in vals]
    denom = functools.reduce(lambda a, b: a + b, exps)
    gate_rows = [e / denom for e in exps] + [jnp.zeros_like(denom)] * (SUBLANES - TOP_K)
    gates_tok = jnp.concatenate(gate_rows, axis=0).T
    lane_group = lax.broadcasted_iota(jnp.int32, (n_tok, LANES), 1) // (LANES // TOP_K)
    spread = jnp.broadcast_to(gates_tok[:, 0:1], (n_tok, LANES))
    for k in range(1, TOP_K):
        spread = jnp.where(lane_group == k, jnp.broadcast_to(gates_tok[:, k:k + 1], (n_tok, LANES)), spread)
    mg_ref[...] = spread

    onehot = chosen.astype(_BF16)
    r = lax.broadcasted_iota(jnp.int32, (n_tok, n_tok), 0)
    c = lax.broadcasted_iota(jnp.int32, (n_tok, n_tok), 1)
    before = (r < c).astype(_BF16)
    cum = jnp.dot(onehot, before, preferred_element_type=_F32) + carry_ref[:, 0:1]
    ranks = [jnp.sum(jnp.where(eidx == s, cum, 0.0), axis=0, keepdims=True) for s in idxs]
    mi_ref[...] = jnp.concatenate(idxs + [rk.astype(jnp.int32) for rk in ranks], axis=0)
    carry_ref[...] = carry_ref[...] + jnp.sum(chosen.astype(_F32), axis=1, keepdims=True)
    cnt_ref[...] = carry_ref[...]


def _pool_mix(x, si, weights, scratch, alpha):
    pw_ref, ps_ref = weights
    (hist_ref,) = scratch
    ts, d = x.shape
    dg = d // len(POOL_WINDOWS)

    @pl.when(si == 0)
    def _():
        hist_ref[0:POOL_HALO, :] = jnp.zeros((POOL_HALO, d), _F32)

    hist_ref[POOL_HALO:POOL_HALO + ts, :] = x
    pos = si * ts + lax.broadcasted_iota(jnp.int32, (ts, 1), 0)
    pieces = []
    for gi, win in enumerate(POOL_WINDOWS):
        c0 = gi * dg
        xg = x[:, c0:c0 + dg]
        assert win & (win - 1) == 0 and win <= POOL_HALO
        ext = hist_ref[:, c0:c0 + dg]
        span = 1
        while span < win:
            ext = ext + pltpu.roll(ext, span, 0)
            span *= 2
        acc = ext[POOL_HALO:, :]
        inv_count = 1.0 / jnp.minimum(pos + 1, win).astype(_F32)
        diff = acc * inv_count - xg
        hg = jnp.dot(diff.astype(_BF16), pw_ref[gi], preferred_element_type=_F32)
        pieces.append(alpha * xg + hg * ps_ref[:, c0:c0 + dg])
    hist_ref[0:POOL_HALO, :] = x[ts - POOL_HALO:, :]
    return jnp.concatenate(pieces, axis=1)


def _short_conv_mix(x, si, weights, scratch, alpha):
    win_ref, cw_ref, wout_ref = weights
    (hist_ref,) = scratch
    ts, d = x.shape
    halo = SHORT_CONV_HALO

    @pl.when(si == 0)
    def _():
        hist_ref[0:halo, :] = jnp.zeros((halo, d), _F32)

    xb = x.astype(_BF16)
    gate_b = jnp.dot(xb, win_ref[:, 0:d], preferred_element_type=_F32)
    gate_c = jnp.dot(xb, win_ref[:, d:2 * d], preferred_element_type=_F32)
    h = jnp.dot(xb, win_ref[:, 2 * d:3 * d], preferred_element_type=_F32)
    v = gate_c * h
    hist_ref[halo:halo + ts, :] = v
    width = cw_ref.shape[0]
    u = cw_ref[width - 1:width, :] * v
    for k in range(width - 1):
        shift = width - 1 - k
        u = u + cw_ref[k:k + 1, :] * hist_ref[halo - shift:halo - shift + ts, :]
    hist_ref[0:halo, :] = v[ts - halo:, :]
    y = jnp.dot((gate_b * u).astype(_BF16), wout_ref[...], preferred_element_type=_F32)
    return alpha * x + y


def _conformer_mix(x, si, weights, scratch, alpha):
    win_ref, bin_ref, dww_ref, dwb_ref, lng_ref, lnb_ref, wout_ref, bout_ref = weights
    (hist_ref,) = scratch
    ts, d = x.shape
    halo = CONFORMER_HALO

    @pl.when(si == 0)
    def _():
        hist_ref[0:halo, :] = jnp.zeros((halo, d), _F32)

    xb = x.astype(_BF16)
    a = jnp.dot(xb, win_ref[:, 0:d], preferred_element_type=_F32) + bin_ref[:, 0:d]
    gate = jnp.dot(xb, win_ref[:, d:2 * d], preferred_element_type=_F32) + bin_ref[:, d:2 * d]
    u = a * jax.nn.sigmoid(gate)
    hist_ref[halo:halo + ts, :] = u
    width = dww_ref.shape[0]
    hist = hist_ref[...]
    acc = dwb_ref[...]
    for r in range(SUBLANES):
        rolled = hist if r == 0 else pltpu.roll(hist, r, 0)
        for q in range(halo // SUBLANES):
            shift = SUBLANES * q + r
            if shift < width:
                k = width - 1 - shift
                start = halo - SUBLANES * q
                acc = acc + dww_ref[k:k + 1, :] * rolled[start:start + ts, :]
    hist_ref[0:halo, :] = u[ts - halo:, :]
    un = _layer_norm(acc, lng_ref[...], lnb_ref[...])
    un = un * jax.nn.sigmoid(un)
    y = jnp.dot(un.astype(_BF16), wout_ref[...], preferred_element_type=_F32) + bout_ref[...]
    return alpha * x + y


def _residual_norm(x1r_ref, h_ref, g_ref, b_ref, rs, alpha):
    tb = x1r_ref.shape[0] // rs
    pieces = [alpha * x1r_ref[pl.ds(j, tb, stride=rs), :] + h_ref[pl.ds(j, tb, stride=rs), :] for j in range(rs)]
    return _layer_norm(jnp.concatenate(pieces, axis=1), g_ref[...], b_ref[...])


def _mixer_kernel(*refs, mix_fn, n_weights, from_moe, alpha, rs):
    bi, si = pl.program_id(0), pl.program_id(1)
    if from_moe:
        xprev_ref, h_ref, cg_ref, cb_ref = refs[:4]
        x = _residual_norm(xprev_ref, h_ref, cg_ref, cb_ref, rs, alpha)
        refs = refs[4:]
    else:
        x = refs[0][0]
        refs = refs[1:]
    weights, refs = refs[:n_weights], refs[n_weights:]
    g_ref, b_ref, rwt_ref, rb_ref = refs[:4]
    x1r_ref, x1p_ref, mi_ref, mg_ref, cnt_ref = refs[4:9]
    scratch, carry_ref = refs[9:-1], refs[-1]
    z = mix_fn(x, si, weights, scratch, alpha)
    _post_norm_and_route(z, g_ref, b_ref, rwt_ref, rb_ref, (bi == 0) & (si == 0),
                         x1r_ref, x1p_ref, mi_ref, mg_ref, cnt_ref, carry_ref)


def _mixer_call(mix_fn, source, bsz, seq, weights, ln_g, ln_b, router_w, router_b, halo, block, alpha):
    d = router_w.shape[0]
    n_exp = router_w.shape[1]
    ts = min(block, seq)
    rs = d // LANES
    n_tok = bsz * seq
    nsb = seq // ts

    def full(a):
        nd = a.ndim
        return pl.BlockSpec(a.shape, lambda bi, si, _nd=nd: (0,) * _nd)

    small = [ln_g.reshape(1, d), ln_b.reshape(1, d), router_w.T, router_b.reshape(n_exp, 1)]
    tail = list(weights) + small
    if source[0] == "x":
        _, x, batch_lo = source
        operands = [x] + tail
        in_specs = [pl.BlockSpec((1, ts, d), lambda bi, si: (batch_lo + bi, si, 0))]
    else:
        _, x1r, h, cg, cb = source
        operands = [x1r, h, cg.reshape(1, d), cb.reshape(1, d)] + tail
        in_specs = [
            pl.BlockSpec((ts * rs, LANES), lambda bi, si: (bi * nsb + si, 0)),
            pl.BlockSpec((ts * rs, LANES), lambda bi, si: (bi * nsb + si, 0)),
            pl.BlockSpec((1, d), lambda bi, si: (0, 0)),
            pl.BlockSpec((1, d), lambda bi, si: (0, 0)),
        ]
    in_specs = in_specs + [full(a) for a in tail]
    tok_map = lambda bi, si: (0, bi * nsb + si)
    out_shape = [
        jax.ShapeDtypeStruct((n_tok * rs, LANES), _F32),
        jax.ShapeDtypeStruct((n_tok * rs // 2, LANES), jnp.uint32),
        jax.ShapeDtypeStruct((2 * TOP_K, n_tok), jnp.int32),
        jax.ShapeDtypeStruct((n_tok, LANES), _F32),
        jax.ShapeDtypeStruct((n_exp, LANES), _F32),
    ]
    out_specs = [
        pl.BlockSpec((ts * rs, LANES), lambda bi, si: (bi * nsb + si, 0)),
        pl.BlockSpec((ts * rs // 2, LANES), lambda bi, si: (bi * nsb + si, 0)),
        pl.BlockSpec((2 * TOP_K, ts), tok_map),
        pl.BlockSpec((ts, LANES), lambda bi, si: (bi * nsb + si, 0)),
        pl.BlockSpec((n_exp, LANES), lambda bi, si: (0, 0)),
    ]
    return pl.pallas_call(
        functools.partial(_mixer_kernel, mix_fn=mix_fn, n_weights=len(weights), from_moe=source[0] == "moe",
                          alpha=alpha, rs=rs),
        grid=(bsz, nsb),
        in_specs=in_specs,
        out_specs=out_specs,
        out_shape=out_shape,
        scratch_shapes=[pltpu.VMEM((halo + ts, d), _F32), pltpu.VMEM((n_exp, LANES), _F32)],
        compiler_params=pltpu.CompilerParams(
            dimension_semantics=("arbitrary", "arbitrary"), vmem_limit_bytes=VMEM_LIMIT_BYTES),
        name=mix_fn.__name__.strip("_"),
    )(*operands)


def _sc_workers():
    info = plsc.get_sparse_core_info()
    return info.num_cores, info.num_subcores, info.num_lanes


def _sc_scatter_rows(x3, dest3, n_rows):
    n_cores, n_sub, n_lanes = _sc_workers()
    n_tok, rs, _ = x3.shape
    top_k = dest3.shape[0]
    chunk = SC_CHUNK_ROWS * (LANES * SUBLANES) // (rs * LANES)
    tok_per_w = n_tok // (n_cores * n_sub)
    rows_per_w = tok_per_w // SC_INDEX_ROW
    chunks_per_row = SC_INDEX_ROW // chunk
    assert rows_per_w * SC_INDEX_ROW * n_cores * n_sub == n_tok and chunks_per_row % 2 == 0
    mesh = plsc.VectorSubcoreMesh(core_axis_name="core", subcore_axis_name="subcore")

    @pl.kernel(out_type=jax.ShapeDtypeStruct((n_rows, rs, LANES), x3.dtype), mesh=mesh,
               compiler_params=pltpu.CompilerParams(needs_layout_passes=False),
               scratch_types=[pltpu.VMEM((top_k, rows_per_w, SC_INDEX_ROW), jnp.int32),
                              pltpu.VMEM((2, chunk, rs, LANES), x3.dtype),
                              pltpu.SemaphoreType.DMA((2,)), pltpu.SemaphoreType.DMA((2,))])
    def scatter_kernel(x_hbm, d_hbm, o_hbm, idx_v, buf, rsem, ssem):
        wid = lax.axis_index("subcore") * n_cores + lax.axis_index("core")
        for k in range(top_k):
            pltpu.sync_copy(d_hbm.at[k, pl.ds(wid * rows_per_w, rows_per_w)], idx_v.at[k])
        base = wid * tok_per_w

        def read(j, c, slot):
            return pltpu.make_async_copy(x_hbm.at[pl.ds(base + j * SC_INDEX_ROW + c * chunk, chunk)],
                                         buf.at[slot], rsem.at[slot])

        def scatters(j, c, slot):
            copies = []
            for k in range(top_k):
                for h in range(chunk // n_lanes):
                    rows = idx_v[k, j, pl.ds(c * chunk + h * n_lanes, n_lanes)]
                    copies.append(pltpu.make_async_copy(buf.at[slot, pl.ds(h * n_lanes, n_lanes)],
                                                        o_hbm.at[rows], ssem.at[slot]))
            return copies

        def wait_scatters(j, c, slot):
            for cp in scatters(j, c, slot):
                cp.wait()

        read(0, 0, 0).start()

        def per_index_row(j, carry):
            for c in range(chunks_per_row):
                slot = c % 2
                if c == 0:
                    @pl.when(j > 0)
                    def _():
                        wait_scatters(j - 1, chunks_per_row - 1, 1 - slot)
                    read(j, c + 1, 1 - slot).start()
                elif c < chunks_per_row - 1:
                    wait_scatters(j, c - 1, 1 - slot)
                    read(j, c + 1, 1 - slot).start()
                else:
                    @pl.when(j + 1 < rows_per_w)
                    def _():
                        wait_scatters(j, c - 1, 1 - slot)
                        read(j + 1, 0, 1 - slot).start()
                read(j, c, slot).wait()
                for cp in scatters(j, c, slot):
                    cp.start()
            return carry

        lax.fori_loop(0, rows_per_w, per_index_row, 0)
        wait_scatters(rows_per_w - 1, chunks_per_row - 2, 0)
        wait_scatters(rows_per_w - 1, chunks_per_row - 1, 1)

    return scatter_kernel(x3, dest3)


def _sc_gather_weighted_sum(table3, idx3, gates_b):
    n_cores, n_sub, n_lanes = _sc_workers()
    top_k, n_idx_rows, _ = idx3.shape
    n_tok = n_idx_rows * SC_INDEX_ROW
    rp = table3.shape[1]
    chunk = n_lanes
    tok_per_w = n_tok // (n_cores * n_sub)
    rows_per_w = tok_per_w // SC_INDEX_ROW
    n_chunks = tok_per_w // chunk
    chunks_per_row = SC_INDEX_ROW // chunk
    assert rows_per_w * SC_INDEX_ROW * n_cores * n_sub == n_tok and n_chunks % 2 == 0
    mesh = plsc.VectorSubcoreMesh(core_axis_name="core", subcore_axis_name="subcore")

    @pl.kernel(out_type=jax.ShapeDtypeStruct((n_tok, 2 * rp, LANES), _F32), mesh=mesh,
               compiler_params=pltpu.CompilerParams(needs_layout_passes=False),
               scratch_types=[pltpu.VMEM((top_k, rows_per_w, SC_INDEX_ROW), jnp.int32),
                              pltpu.VMEM((2, top_k, chunk, rp, LANES), jnp.uint32),
                              pltpu.VMEM((2, chunk, LANES), _F32),
                              pltpu.VMEM((2, chunk, 2 * rp, LANES), _F32),
                              pltpu.SemaphoreType.DMA((2,)), pltpu.SemaphoreType.DMA((2,))])
    def gather_sum_kernel(t_hbm, i_hbm, g_hbm, o_hbm, idx_v, ybuf, gbuf, obuf, isem, osem):
        wid = lax.axis_index("subcore") * n_cores + lax.axis_index("core")
        for k in range(top_k):
            pltpu.sync_copy(i_hbm.at[k, pl.ds(wid * rows_per_w, rows_per_w)], idx_v.at[k])
        base = wid * tok_per_w

        def inputs(c, slot):
            j, off = c // chunks_per_row, (c % chunks_per_row) * chunk
            copies = []
            for k in range(top_k):
                rows = idx_v[k, j, pl.ds(off, chunk)]
                copies.append(pltpu.make_async_copy(t_hbm.at[rows], ybuf.at[slot, k], isem.at[slot]))
            copies.append(pltpu.make_async_copy(g_hbm.at[pl.ds(base + c * chunk, chunk)], gbuf.at[slot], isem.at[slot]))
            return copies

        def output(c, slot):
            return pltpu.make_async_copy(obuf.at[slot], o_hbm.at[pl.ds(base + c * chunk, chunk)], osem.at[slot])

        def weighted_sum(slot):
            def token(t, carry):
                gate = [gbuf[slot, t, pl.ds(k * (LANES // top_k), n_lanes)] for k in range(top_k)]
                for r in range(rp):
                    for l in range(LANES // n_lanes):
                        lanes = pl.ds(l * n_lanes, n_lanes)
                        lo = jnp.zeros((n_lanes,), _F32)
                        hi = jnp.zeros((n_lanes,), _F32)
                        for k in range(top_k):
                            w = ybuf[slot, k, t, r, lanes]
                            lo = lo + gate[k] * plsc.bitcast(w << 16, _F32)
                            hi = hi + gate[k] * plsc.bitcast(w & jnp.uint32(0xFFFF0000), _F32)
                        obuf[slot, t, r, lanes] = lo
                        obuf[slot, t, rp + r, lanes] = hi
                return carry

            lax.fori_loop(0, chunk, token, 0)

        for cp in inputs(0, 0):
            cp.start()

        def chunk_pair(g, carry):
            for slot in range(2):
                c = 2 * g + slot

                @pl.when(c + 1 < n_chunks)
                def _():
                    for cp in inputs(c + 1, 1 - slot):
                        cp.start()

                for cp in inputs(c, slot):
                    cp.wait()

                @pl.when(c >= 2)
                def _():
                    output(c - 2, slot).wait()

                weighted_sum(slot)
                output(c, slot).start()
            return carry

        lax.fori_loop(0, n_chunks // 2, chunk_pair, 0)
        output(n_chunks - 2, 0).wait()
        output(n_chunks - 1, 1).wait()

    return gather_sum_kernel(table3, idx3, gates_b)


def _pack_bf16_pairs(v):
    half = v.shape[1] // 2
    lo = lax.bitcast_convert_type(v[:, :half].astype(_BF16).astype(_F32), jnp.uint32)
    hi = lax.bitcast_convert_type(v[:, half:].astype(_BF16).astype(_F32), jnp.uint32)
    return (lo >> 16) | (hi & jnp.uint32(0xFFFF0000))


def _unpack_bf16_pairs(w):
    return (lax.bitcast_convert_type(w << 16, _F32),
            lax.bitcast_convert_type(w & jnp.uint32(0xFFFF0000), _F32))


def _expert_kernel(be_ref, nx_ref, nv_ref, nu_ref, xs_ref, bgu_ref, bdn_ref, wgu_hbm, wdn_hbm, ys_ref,
                   wgu_f, wdn_f, wgu_s, wdn_s, sem, *, rs, layer):
    b = pl.program_id(0)
    n_chunks = wgu_f.shape[1] // GATE_UP_CHUNK
    half = GATE_UP_CHUNK // 2

    def fetch(e):
        return (pltpu.make_async_copy(wgu_hbm.at[layer, e], wgu_f, sem.at[0]),
                pltpu.make_async_copy(wdn_hbm.at[layer, e], wdn_f, sem.at[1]))

    @pl.when(b < nu_ref[0])
    def _():
        @pl.when((b == 0) | (be_ref[b] != be_ref[jnp.maximum(b - 1, 0)]))
        def _():
            @pl.when(b == 0)
            def _():
                for cp in fetch(be_ref[0]):
                    cp.start()

            for cp in fetch(be_ref[b]):
                cp.wait()
            r = lax.broadcasted_iota(jnp.int32, (GATE_UP_CHUNK, GATE_UP_CHUNK), 0)
            c = lax.broadcasted_iota(jnp.int32, (GATE_UP_CHUNK, GATE_UP_CHUNK), 1)
            perm = (r == jnp.where(c < half, 2 * c, 2 * (c - half) + 1)).astype(_BF16)
            for ch in range(n_chunks):
                cols = slice(ch * GATE_UP_CHUNK, (ch + 1) * GATE_UP_CHUNK)
                w = wgu_f[:, cols].astype(_BF16)
                wgu_s[:, cols] = jnp.dot(w, perm, preferred_element_type=_F32).astype(_BF16)
            wdn_s[...] = wdn_f[...].astype(_BF16)

            @pl.when(nx_ref[b] >= 0)
            def _():
                for cp in fetch(nx_ref[b]):
                    cp.start()

        n_valid = nv_ref[b]
        sub = EXPERT_SUB_ROWS

        def sub_block(s, rows=sub):
            row0 = pl.multiple_of(s * sub, sub)
            defined = row0 + lax.broadcasted_iota(jnp.int32, (rows, 1), 0) < n_valid
            words = [jnp.where(defined, w, jnp.uint32(0)) for w in _load_rows(xs_ref, rows, rs // 2, row0)]
            halves = [_unpack_bf16_pairs(w) for w in words]
            x = jnp.concatenate([lo for lo, _ in halves] + [hi for _, hi in halves], axis=1).astype(_BF16)
            h = jnp.dot(x, wgu_s[...], preferred_element_type=_F32) + bgu_ref[0]
            acts = []
            for ch in range(n_chunks):
                g = jnp.minimum(h[:, ch * GATE_UP_CHUNK:ch * GATE_UP_CHUNK + half], SWIGLU_LIMIT)
                up = jnp.clip(h[:, ch * GATE_UP_CHUNK + half:(ch + 1) * GATE_UP_CHUNK], -SWIGLU_LIMIT, SWIGLU_LIMIT)
                acts.append(((up + 1.0) * (g * jax.nn.sigmoid(SWIGLU_ALPHA * g))).astype(_BF16))
            y = jnp.dot(jnp.concatenate(acts, axis=1), wdn_s[...], preferred_element_type=_F32) + bdn_ref[0]
            _store_rows(ys_ref, _pack_bf16_pairs(y), row0)

        n_sub = (n_valid + sub - 1) // sub
        half_tail = (n_valid - (n_sub - 1) * sub) * 2 <= sub
        n_whole = n_sub - half_tail.astype(jnp.int32)

        def pair(i, carry):
            sub_block(2 * i)
            sub_block(2 * i + 1)
            return carry

        lax.fori_loop(0, n_whole // 2, pair, 0)

        @pl.when(n_whole % 2 == 1)
        def _():
            sub_block(n_whole - 1)

        @pl.when(half_tail)
        def _():
            sub_block(n_sub - 1, sub // 2)


def _expert_call(block_expert, block_next, block_valid, n_used, xs, layer, w_gu_all, b_gu_grouped, w_dn_all,
                 b_dn, rs):
    _, n_exp, d, f2 = w_gu_all.shape
    f = f2 // 2
    br = EXPERT_BLOCK_ROWS
    n_blocks = xs.shape[0] // (br * rs // 2)
    assert f2 % GATE_UP_CHUNK == 0 and rs % 2 == 0

    def row_map(b, be, nx, nv, nu):
        return (jnp.minimum(b, nu[0] - 1), 0)

    def w_map(b, be, nx, nv, nu):
        return (be[b], 0, 0)

    return pl.pallas_call(
        functools.partial(_expert_kernel, rs=rs, layer=layer),
        grid_spec=pltpu.PrefetchScalarGridSpec(
            num_scalar_prefetch=4,
            grid=(n_blocks,),
            in_specs=[
                pl.BlockSpec((br * rs // 2, LANES), row_map),
                pl.BlockSpec((1, 1, f2), w_map),
                pl.BlockSpec((1, 1, d), w_map),
                pl.BlockSpec(memory_space=pl.ANY),
                pl.BlockSpec(memory_space=pl.ANY),
            ],
            out_specs=pl.BlockSpec((br * rs // 2, LANES), row_map),
            scratch_shapes=[pltpu.VMEM((d, f2), _F32), pltpu.VMEM((f, d), _F32),
                            pltpu.VMEM((d, f2), _BF16), pltpu.VMEM((f, d), _BF16),
                            pltpu.SemaphoreType.DMA((2,))],
        ),
        out_shape=jax.ShapeDtypeStruct(xs.shape, jnp.uint32),
        compiler_params=pltpu.CompilerParams(
            dimension_semantics=("arbitrary",), vmem_limit_bytes=VMEM_LIMIT_BYTES),
        name="experts",
    )(block_expert, block_next, block_valid, n_used, xs, b_gu_grouped, b_dn, w_gu_all, w_dn_all)


def _combine_kernel(x1r_ref, h_ref, g_ref, b_ref, *rest, rs, alpha):
    rest[-1][...] = _residual_norm(x1r_ref, h_ref, g_ref, b_ref, rs, alpha)


def _combine_call(x1r, h, ln_g, ln_b, alpha, rs, out_tokens, token_lo, out_buf):
    n_tok = x1r.shape[0] // rs
    d = rs * LANES
    tb = min(COMBINE_BLOCK, n_tok)
    steps = n_tok // tb
    block_lo = token_lo // tb
    operands = [x1r, h, ln_g.reshape(1, d), ln_b.reshape(1, d)]
    in_specs = [
        pl.BlockSpec((tb * rs, LANES), lambda i: (i, 0)),
        pl.BlockSpec((tb * rs, LANES), lambda i: (i, 0)),
        pl.BlockSpec((1, d), lambda i: (0, 0)),
        pl.BlockSpec((1, d), lambda i: (0, 0)),
    ]
    aliases = {}
    if out_buf is not None:
        aliases = {len(operands): 0}
        operands.append(out_buf)
        in_specs.append(pl.BlockSpec(memory_space=pl.ANY))
    return pl.pallas_call(
        functools.partial(_combine_kernel, rs=rs, alpha=alpha),
        grid=(steps,),
        in_specs=in_specs,
        out_specs=pl.BlockSpec((tb, d), lambda i: (block_lo + i, 0)),
        out_shape=jax.ShapeDtypeStruct((out_tokens, d), _F32),
        input_output_aliases=aliases,
        compiler_params=pltpu.CompilerParams(
            dimension_semantics=("arbitrary",), vmem_limit_bytes=VMEM_LIMIT_BYTES),
        name="combine",
    )(*operands)


def _routing_tables(meta_i, counts_f, n_blocks):
    n_exp = counts_f.shape[0]
    br = EXPERT_BLOCK_ROWS
    counts = counts_f[:, 0].astype(jnp.int32)
    padded = ((counts + br - 1) // br) * br
    pend = jnp.cumsum(padded)
    pstart = pend - padded
    eids = jnp.arange(n_exp, dtype=jnp.int32)
    idx, rank = meta_i[:TOP_K], meta_i[TOP_K:]
    dest = jnp.sum(jnp.where(idx[..., None] == eids, pstart, 0), axis=-1) + rank
    n_used = (pend[-1] // br).astype(jnp.int32)
    blk = jnp.minimum(jnp.arange(n_blocks, dtype=jnp.int32), n_used - 1)
    block_expert = jnp.minimum(jnp.sum((pend[None, :] <= (blk * br)[:, None]).astype(jnp.int32), axis=1),
                               n_exp - 1)
    group_end = jnp.sum(jnp.where(block_expert[:, None] == eids, pstart + counts, 0), axis=-1)
    block_valid = jnp.clip(group_end - blk * br, 0, br).astype(jnp.int32)
    later = (eids[None, :] > block_expert[:, None]) & (counts[None, :] > 0)
    block_next = jnp.min(jnp.where(later, eids[None, :], n_exp), axis=1)
    block_next = jnp.where(block_next == n_exp, -1, block_next).astype(jnp.int32)
    return dest, block_expert, block_next, block_valid, n_used.reshape(1)


def _moe_layer(x1p, meta_i, gates_b, counts_f, layer, w_gu_all, b_gu, w_dn_all, b_dn, rs):
    _, n_exp, d, f2 = w_gu_all.shape
    n_tok = meta_i.shape[1]
    br = EXPERT_BLOCK_ROWS
    rp = rs // 2
    n_blocks = -(-(n_tok * TOP_K) // br) + n_exp
    dest, block_expert, block_next, block_valid, n_used = _routing_tables(meta_i, counts_f, n_blocks)
    n_rows = n_blocks * br
    xs = _sc_scatter_rows(x1p.reshape(n_tok, rp, LANES), dest.reshape(TOP_K, n_tok // SC_INDEX_ROW, SC_INDEX_ROW),
                          n_rows)
    half = GATE_UP_CHUNK // 2
    b_gu_grouped = b_gu.reshape(n_exp, f2 // GATE_UP_CHUNK, half, 2).transpose(0, 1, 3, 2).reshape(n_exp, 1, f2)
    ys = _expert_call(block_expert, block_next, block_valid, n_used, xs.reshape(n_rows * rp, LANES), layer, w_gu_all,
                      b_gu_grouped, w_dn_all, b_dn.reshape(n_exp, 1, d), rs)
    h = _sc_gather_weighted_sum(ys.reshape(n_rows, rp, LANES),
                                dest.reshape(TOP_K, n_tok // SC_INDEX_ROW, SC_INDEX_ROW), gates_b)
    return h.reshape(n_tok * rs, LANES)


def kernel(x, pool_w, pool_scale, sc_w_in, sc_conv_w, sc_w_out, cf_w_in, cf_b_in, cf_dw_w, cf_dw_b,
           cf_ln_g, cf_ln_b, cf_w_out, cf_b_out, mix_ln_g, mix_ln_b, router_w, router_b,
           moe_w_gu, moe_b_gu, moe_w_dn, moe_b_dn, ffn_ln_g, ffn_ln_b):
    bsz, seq, d = x.shape
    depth = mix_ln_g.shape[0]
    alpha = (2.0 * depth) ** 0.25
    rs = d // LANES
    n_chains = BATCH_CHAINS if bsz % BATCH_CHAINS == 0 else 1
    cb = bsz // n_chains
    sources = [("x", x, c * cb) for c in range(n_chains)]
    ia = ib = ic = 0
    for layer in range(depth):
        kind = layer % 3
        route = (mix_ln_g[layer], mix_ln_b[layer], router_w[layer], router_b[layer])
        if kind == 0:
            mixer, halo, block = _pool_mix, POOL_HALO, TOKEN_BLOCK
            weights = [pool_w[ia].astype(_BF16), pool_scale[ia].reshape(1, d)]
            ia += 1
        elif kind == 1:
            mixer, halo, block = _short_conv_mix, SHORT_CONV_HALO, MATMUL_MIXER_TOKEN_BLOCK
            weights = [sc_w_in[ib].astype(_BF16), sc_conv_w[ib], sc_w_out[ib].astype(_BF16)]
            ib += 1
        else:
            mixer, halo, block = _conformer_mix, CONFORMER_HALO, TOKEN_BLOCK
            weights = [cf_w_in[ic].astype(_BF16), cf_b_in[ic].reshape(1, 2 * d), cf_dw_w[ic],
                       cf_dw_b[ic].reshape(1, d), cf_ln_g[ic].reshape(1, d), cf_ln_b[ic].reshape(1, d),
                       cf_w_out[ic].astype(_BF16), cf_b_out[ic].reshape(1, d)]
            ic += 1
        routed = [_mixer_call(mixer, src, cb, seq, weights, *route, halo, block, alpha) for src in sources]
        sources = []
        for x1r, x1p, meta_i, gates_b, counts_f in routed:
            h = _moe_layer(x1p, meta_i, gates_b, counts_f, layer, moe_w_gu, moe_b_gu[layer],
                           moe_w_dn, moe_b_dn[layer], rs)
            sources.append(("moe", x1r, h, ffn_ln_g[layer], ffn_ln_b[layer]))
    out = None
    for c, (_, x1r, h, ln_g, ln_b) in enumerate(sources):
        out = _combine_call(x1r, h, ln_g, ln_b, alpha, rs, bsz * seq, c * cb * seq, out)
    return out.reshape(bsz, seq, d)
```

```python
import functools

import jax
import jax.numpy as jnp
from jax import lax
from jax.experimental import pallas as pl
from jax.experimental.pallas import tpu as pltpu
from jax.experimental.pallas import tpu_sc as plsc

LANES = 128
SUBLANES = 8
TOP_K = 4
POOL_WINDOWS = (2, 4, 8, 16)
POOL_HALO = 16
SHORT_CONV_HALO = 8
CONFORMER_HALO = 32
SWIGLU_LIMIT = 7.0
SWIGLU_ALPHA = 1.702
LN_EPS = 1e-5
TOKEN_BLOCK = 512
MATMUL_MIXER_TOKEN_BLOCK = 1024
EXPERT_BLOCK_ROWS = 2048
EXPERT_SUB_ROWS = 512
COMBINE_BLOCK = 512
BATCH_CHAINS = 2
SC_INDEX_ROW = 128
SC_CHUNK_ROWS = 32
GATE_UP_CHUNK = 2 * LANES
VMEM_LIMIT_BYTES = 56 * 1024 * 1024

_F32 = jnp.float32
_BF16 = jnp.bfloat16


def _layer_norm(z, g, b):
    mu = jnp.mean(z, axis=-1, keepdims=True)
    zc = z - mu
    var = jnp.mean(zc * zc, axis=-1, keepdims=True)
    return zc * lax.rsqrt(var + LN_EPS) * g + b


def _store_rows(row_ref, val, row0=0):
    rows, d = val.shape
    rs = d // LANES
    for j in range(rs):
        row_ref[pl.ds(row0 * rs + j, rows, stride=rs), :] = val[:, j * LANES:(j + 1) * LANES]


def _load_rows(row_ref, rows, rs, row0=0):
    return [row_ref[pl.ds(row0 * rs + j, rows, stride=rs), :] for j in range(rs)]


def _post_norm_and_route(z, g_ref, b_ref, rwt_ref, rb_ref, first,
                         x1r_ref, x1p_ref, mi_ref, mg_ref, cnt_ref, carry_ref):
    n_tok = z.shape[0]
    n_exp = rwt_ref.shape[0]

    @pl.when(first)
    def _():
        carry_ref[...] = jnp.zeros_like(carry_ref)

    x1 = _layer_norm(z, g_ref[...], b_ref[...])
    _store_rows(x1r_ref, x1)
    _store_rows(x1p_ref, _pack_bf16_pairs(x1))

    x_hi = x1.astype(_BF16)
    x_lo = (x1 - x_hi.astype(_F32)).astype(_BF16)
    w = rwt_ref[...]
    w_hi = w.astype(_BF16)
    w_lo = (w - w_hi.astype(_F32)).astype(_BF16)
    nt = (((1,), (1,)), ((), ()))
    logits = (lax.dot_general(w_hi, x_hi, nt, preferred_element_type=_F32)
              + lax.dot_general(w_hi, x_lo, nt, preferred_element_type=_F32)
              + lax.dot_general(w_lo, x_hi, nt, preferred_element_type=_F32)) + rb_ref[...]
    eidx = lax.broadcasted_iota(jnp.int32, logits.shape, 0)
    work = logits
    chosen = jnp.zeros(logits.shape, jnp.bool_)
    vals, idxs = [], []
    for _ in range(TOP_K):
        m = jnp.max(work, axis=0, keepdims=True)
        sel = jnp.min(jnp.where(work == m, eidx, n_exp), axis=0, keepdims=True)
        hit = eidx == sel
        vals.append(m)
        idxs.append(sel)
        chosen = jnp.logical_or(chosen, hit)
        work = jnp.where(hit, -jnp.inf, work)
    exps = [jnp.exp(v - vals[0]) for v in vals]
    denom = functools.reduce(lambda a, b: a + b, exps)
    gate_rows = [e / denom for e in exps] + [jnp.zeros_like(denom)] * (SUBLANES - TOP_K)
    gates_tok = jnp.concatenate(gate_rows, axis=0).T
    lane_group = lax.broadcasted_iota(jnp.int32, (n_tok, LANES), 1) // (LANES // TOP_K)
    spread = jnp.broadcast_to(gates_tok[:, 0:1], (n_tok, LANES))
    for k in range(1, TOP_K):
        spread = jnp.where(lane_group == k, jnp.broadcast_to(gates_tok[:, k:k + 1], (n_tok, LANES)), spread)
    mg_ref[...] = spread

    onehot = chosen.astype(_BF16)
    r = lax.broadcasted_iota(jnp.int32, (n_tok, n_tok), 0)
    c = lax.broadcasted_iota(jnp.int32, (n_tok, n_tok), 1)
    before = (r < c).astype(_BF16)
    cum = jnp.dot(onehot, before, preferred_element_type=_F32) + carry_ref[:, 0:1]
    ranks = [jnp.sum(jnp.where(eidx == s, cum, 0.0), axis=0, keepdims=True) for s in idxs]
    mi_ref[...] = jnp.concatenate(idxs + [rk.astype(jnp.int32) for rk in ranks], axis=0)
    carry_ref[...] = carry_ref[...] + jnp.sum(chosen.astype(_F32), axis=1, keepdims=True)
    cnt_ref[...] = carry_ref[...]


def _pool_mix(x, si, weights, scratch, alpha):
    pw_ref, ps_ref = weights
    (hist_ref,) = scratch
    ts, d = x.shape
    dg = d // len(POOL_WINDOWS)

    @pl.when(si == 0)
    def _():
        hist_ref[0:POOL_HALO, :] = jnp.zeros((POOL_HALO, d), _F32)

    hist_ref[POOL_HALO:POOL_HALO + ts, :] = x
    pos = si * ts + lax.broadcasted_iota(jnp.int32, (ts, 1), 0)
    pieces = []
    for gi, win in enumerate(POOL_WINDOWS):
        c0 = gi * dg
        xg = x[:, c0:c0 + dg]
        assert win & (win - 1) == 0 and win <= POOL_HALO
        ext = hist_ref[:, c0:c0 + dg]
        span = 1
        while span < win:
            ext = ext + pltpu.roll(ext, span, 0)
            span *= 2
        acc = ext[POOL_HALO:, :]
        inv_count = 1.0 / jnp.minimum(pos + 1, win).astype(_F32)
        diff = acc * inv_count - xg
        hg = jnp.dot(diff.astype(_BF16), pw_ref[gi], preferred_element_type=_F32)
        pieces.append(alpha * xg + hg * ps_ref[:, c0:c0 + dg])
    hist_ref[0:POOL_HALO, :] = x[ts - POOL_HALO:, :]
    return jnp.concatenate(pieces, axis=1)


def _short_conv_mix(x, si, weights, scratch, alpha):
    win_ref, cw_ref, wout_ref = weights
    (hist_ref,) = scratch
    ts, d = x.shape
    halo = SHORT_CONV_HALO

    @pl.when(si == 0)
    def _():
        hist_ref[0:halo, :] = jnp.zeros((halo, d), _F32)

    xb = x.astype(_BF16)
    gate_b = jnp.dot(xb, win_ref[:, 0:d], preferred_element_type=_F32)
    gate_c = jnp.dot(xb, win_ref[:, d:2 * d], preferred_element_type=_F32)
    h = jnp.dot(xb, win_ref[:, 2 * d:3 * d], preferred_element_type=_F32)
    v = gate_c * h
    hist_ref[halo:halo + ts, :] = v
    width = cw_ref.shape[0]
    u = cw_ref[width - 1:width, :] * v
    for k in range(width - 1):
        shift = width - 1 - k
        u = u + cw_ref[k:k + 1, :] * hist_ref[halo - shift:halo - shift + ts, :]
    hist_ref[0:halo, :] = v[ts - halo:, :]
    y = jnp.dot((gate_b * u).astype(_BF16), wout_ref[...], preferred_element_type=_F32)
    return alpha * x + y


def _conformer_mix(x, si, weights, scratch, alpha):
    win_ref, bin_ref, dww_ref, dwb_ref, lng_ref, lnb_ref, wout_ref, bout_ref = weights
    (hist_ref,) = scratch
    ts, d = x.shape
    halo = CONFORMER_HALO

    @pl.when(si == 0)
    def _():
        hist_ref[0:halo, :] = jnp.zeros((halo, d), _F32)

    xb = x.astype(_BF16)
    a = jnp.dot(xb, win_ref[:, 0:d], preferred_element_type=_F32) + bin_ref[:, 0:d]
    gate = jnp.dot(xb, win_ref[:, d:2 * d], preferred_element_type=_F32) + bin_ref[:, d:2 * d]
    u = a * jax.nn.sigmoid(gate)
    hist_ref[halo:halo + ts, :] = u
    width = dww_ref.shape[0]
    hist = hist_ref[...]
    acc = dwb_ref[...]
    for r in range(SUBLANES):
        rolled = hist if r == 0 else pltpu.roll(hist, r, 0)
        for q in range(halo // SUBLANES):
            shift = SUBLANES * q + r
            if shift < width:
                k = width - 1 - shift
                start = halo - SUBLANES * q
                acc = acc + dww_ref[k:k + 1, :] * rolled[start:start + ts, :]
    hist_ref[0:halo, :] = u[ts - halo:, :]
    un = _layer_norm(acc, lng_ref[...], lnb_ref[...])
    un = un * jax.nn.sigmoid(un)
    y = jnp.dot(un.astype(_BF16), wout_ref[...], preferred_element_type=_F32) + bout_ref[...]
    return alpha * x + y


def _residual_norm(x1r_ref, h_ref, g_ref, b_ref, rs, alpha):
    tb = x1r_ref.shape[0] // rs
    pieces = [alpha * x1r_ref[pl.ds(j, tb, stride=rs), :] + h_ref[pl.ds(j, tb, stride=rs), :] for j in range(rs)]
    return _layer_norm(jnp.concatenate(pieces, axis=1), g_ref[...], b_ref[...])


def _mixer_kernel(*refs, mix_fn, n_weights, from_moe, alpha, rs):
    bi, si = pl.program_id(0), pl.program_id(1)
    if from_moe:
        xprev_ref, h_ref, cg_ref, cb_ref = refs[:4]
        x = _residual_norm(xprev_ref, h_ref, cg_ref, cb_ref, rs, alpha)
        refs = refs[4:]
    else:
        x = refs[0][0]
        refs = refs[1:]
    weights, refs = refs[:n_weights], refs[n_weights:]
    g_ref, b_ref, rwt_ref, rb_ref = refs[:4]
    x1r_ref, x1p_ref, mi_ref, mg_ref, cnt_ref = refs[4:9]
    scratch, carry_ref = refs[9:-1], refs[-1]
    z = mix_fn(x, si, weights, scratch, alpha)
    _post_norm_and_route(z, g_ref, b_ref, rwt_ref, rb_ref, (bi == 0) & (si == 0),
                         x1r_ref, x1p_ref, mi_ref, mg_ref, cnt_ref, carry_ref)


def _mixer_call(mix_fn, source, bsz, seq, weights, ln_g, ln_b, router_w, router_b, halo, block, alpha):
    d = router_w.shape[0]
    n_exp = router_w.shape[1]
    ts = min(block, seq)
    rs = d // LANES
    n_tok = bsz * seq
    nsb = seq // ts

    def full(a):
        nd = a.ndim
        return pl.BlockSpec(a.shape, lambda bi, si, _nd=nd: (0,) * _nd)

    small = [ln_g.reshape(1, d), ln_b.reshape(1, d), router_w.T, router_b.reshape(n_exp, 1)]
    tail = list(weights) + small
    if source[0] == "x":
        _, x, batch_lo = source
        operands = [x] + tail
        in_specs = [pl.BlockSpec((1, ts, d), lambda bi, si: (batch_lo + bi, si, 0))]
    else:
        _, x1r, h, cg, cb = source
        operands = [x1r, h, cg.reshape(1, d), cb.reshape(1, d)] + tail
        in_specs = [
            pl.BlockSpec((ts * rs, LANES), lambda bi, si: (bi * nsb + si, 0)),
            pl.BlockSpec((ts * rs, LANES), lambda bi, si: (bi * nsb + si, 0)),
            pl.BlockSpec((1, d), lambda bi, si: (0, 0)),
            pl.BlockSpec((1, d), lambda bi, si: (0, 0)),
        ]
    in_specs = in_specs + [full(a) for a in tail]
    tok_map = lambda bi, si: (0, bi * nsb + si)
    out_shape = [
        jax.ShapeDtypeStruct((n_tok * rs, LANES), _F32),
        jax.ShapeDtypeStruct((n_tok * rs // 2, LANES), jnp.uint32),
        jax.ShapeDtypeStruct((2 * TOP_K, n_tok), jnp.int32),
        jax.ShapeDtypeStruct((n_tok, LANES), _F32),
        jax.ShapeDtypeStruct((n_exp, LANES), _F32),
    ]
    out_specs = [
        pl.BlockSpec((ts * rs, LANES), lambda bi, si: (bi * nsb + si, 0)),
        pl.BlockSpec((ts * rs // 2, LANES), lambda bi, si: (bi * nsb + si, 0)),
        pl.BlockSpec((2 * TOP_K, ts), tok_map),
        pl.BlockSpec((ts, LANES), lambda bi, si: (bi * nsb + si, 0)),
        pl.BlockSpec((n_exp, LANES), lambda bi, si: (0, 0)),
    ]
    return pl.pallas_call(
        functools.partial(_mixer_kernel, mix_fn=mix_fn, n_weights=len(weights), from_moe=source[0] == "moe",
                          alpha=alpha, rs=rs),
        grid=(bsz, nsb),
        in_specs=in_specs,
        out_specs=out_specs,
        out_shape=out_shape,
        scratch_shapes=[pltpu.VMEM((halo + ts, d), _F32), pltpu.VMEM((n_exp, LANES), _F32)],
        compiler_params=pltpu.CompilerParams(
            dimension_semantics=("arbitrary", "arbitrary"), vmem_limit_bytes=VMEM_LIMIT_BYTES),
        name=mix_fn.__name__.strip("_"),
    )(*operands)


def _sc_workers():
    info = plsc.get_sparse_core_info()
    return info.num_cores, info.num_subcores, info.num_lanes


def _sc_scatter_rows(x3, dest3, n_rows):
    n_cores, n_sub, n_lanes = _sc_workers()
    n_tok, rs, _ = x3.shape
    top_k = dest3.shape[0]
    chunk = SC_CHUNK_ROWS * (LANES * SUBLANES) // (rs * LANES)
    tok_per_w = n_tok // (n_cores * n_sub)
    rows_per_w = tok_per_w // SC_INDEX_ROW
    chunks_per_row = SC_INDEX_ROW // chunk
    assert rows_per_w * SC_INDEX_ROW * n_cores * n_sub == n_tok and chunks_per_row % 2 == 0
    mesh = plsc.VectorSubcoreMesh(core_axis_name="core", subcore_axis_name="subcore")

    @pl.kernel(out_type=jax.ShapeDtypeStruct((n_rows, rs, LANES), x3.dtype), mesh=mesh,
               compiler_params=pltpu.CompilerParams(needs_layout_passes=False),
               scratch_types=[pltpu.VMEM((top_k, rows_per_w, SC_INDEX_ROW), jnp.int32),
                              pltpu.VMEM((2, chunk, rs, LANES), x3.dtype),
                              pltpu.SemaphoreType.DMA((2,)), pltpu.SemaphoreType.DMA((2,))])
    def scatter_kernel(x_hbm, d_hbm, o_hbm, idx_v, buf, rsem, ssem):
        wid = lax.axis_index("subcore") * n_cores + lax.axis_index("core")
        for k in range(top_k):
            pltpu.sync_copy(d_hbm.at[k, pl.ds(wid * rows_per_w, rows_per_w)], idx_v.at[k])
        base = wid * tok_per_w

        def read(j, c, slot):
            return pltpu.make_async_copy(x_hbm.at[pl.ds(base + j * SC_INDEX_ROW + c * chunk, chunk)],
                                         buf.at[slot], rsem.at[slot])

        def scatters(j, c, slot):
            copies = []
            for k in range(top_k):
                for h in range(chunk // n_lanes):
                    rows = idx_v[k, j, pl.ds(c * chunk + h * n_lanes, n_lanes)]
                    copies.append(pltpu.make_async_copy(buf.at[slot, pl.ds(h * n_lanes, n_lanes)],
                                                        o_hbm.at[rows], ssem.at[slot]))
            return copies

        def wait_scatters(j, c, slot):
            for cp in scatters(j, c, slot):
                cp.wait()

        read(0, 0, 0).start()

        def per_index_row(j, carry):
            for c in range(chunks_per_row):
                slot = c % 2
                if c == 0:
                    @pl.when(j > 0)
                    def _():
                        wait_scatters(j - 1, chunks_per_row - 1, 1 - slot)
                    read(j, c + 1, 1 - slot).start()
                elif c < chunks_per_row - 1:
                    wait_scatters(j, c - 1, 1 - slot)
                    read(j, c + 1, 1 - slot).start()
                else:
                    @pl.when(j + 1 < rows_per_w)
                    def _():
                        wait_scatters(j, c - 1, 1 - slot)
                        read(j + 1, 0, 1 - slot).start()
                read(j, c, slot).wait()
                for cp in scatters(j, c, slot):
                    cp.start()
            return carry

        lax.fori_loop(0, rows_per_w, per_index_row, 0)
        wait_scatters(rows_per_w - 1, chunks_per_row - 2, 0)
        wait_scatters(rows_per_w - 1, chunks_per_row - 1, 1)

    return scatter_kernel(x3, dest3)


def _sc_gather_weighted_sum(table3, idx3, gates_b):
    n_cores, n_sub, n_lanes = _sc_workers()
    top_k, n_idx_rows, _ = idx3.shape
    n_tok = n_idx_rows * SC_INDEX_ROW
    rp = table3.shape[1]
    chunk = n_lanes
    tok_per_w = n_tok // (n_cores * n_sub)
    rows_per_w = tok_per_w // SC_INDEX_ROW
    n_chunks = tok_per_w // chunk
    chunks_per_row = SC_INDEX_ROW // chunk
    assert rows_per_w * SC_INDEX_ROW * n_cores * n_sub == n_tok and n_chunks % 2 == 0
    mesh = plsc.VectorSubcoreMesh(core_axis_name="core", subcore_axis_name="subcore")

    @pl.kernel(out_type=jax.ShapeDtypeStruct((n_tok, 2 * rp, LANES), _F32), mesh=mesh,
               compiler_params=pltpu.CompilerParams(needs_layout_passes=False),
               scratch_types=[pltpu.VMEM((top_k, rows_per_w, SC_INDEX_ROW), jnp.int32),
                              pltpu.VMEM((2, top_k, chunk, rp, LANES), jnp.uint32),
                              pltpu.VMEM((2, chunk, LANES), _F32),
                              pltpu.VMEM((2, chunk, 2 * rp, LANES), _F32),
                              pltpu.SemaphoreType.DMA((2,)), pltpu.SemaphoreType.DMA((2,))])
    def gather_sum_kernel(t_hbm, i_hbm, g_hbm, o_hbm, idx_v, ybuf, gbuf, obuf, isem, osem):
        wid = lax.axis_index("subcore") * n_cores + lax.axis_index("core")
        for k in range(top_k):
            pltpu.sync_copy(i_hbm.at[k, pl.ds(wid * rows_per_w, rows_per_w)], idx_v.at[k])
        base = wid * tok_per_w

        def inputs(c, slot):
            j, off = c // chunks_per_row, (c % chunks_per_row) * chunk
            copies = []
            for k in range(top_k):
                rows = idx_v[k, j, pl.ds(off, chunk)]
                copies.append(pltpu.make_async_copy(t_hbm.at[rows], ybuf.at[slot, k], isem.at[slot]))
            copies.append(pltpu.make_async_copy(g_hbm.at[pl.ds(base + c * chunk, chunk)], gbuf.at[slot], isem.at[slot]))
            return copies

        def output(c, slot):
            return pltpu.make_async_copy(obuf.at[slot], o_hbm.at[pl.ds(base + c * chunk, chunk)], osem.at[slot])

        def weighted_sum(slot):
            def token(t, carry):
                gate = [gbuf[slot, t, pl.ds(k * (LANES // top_k), n_lanes)] for k in range(top_k)]
                for r in range(rp):
                    for l in range(LANES // n_lanes):
                        lanes = pl.ds(l * n_lanes, n_lanes)
                        lo = jnp.zeros((n_lanes,), _F32)
                        hi = jnp.zeros((n_lanes,), _F32)
                        for k in range(top_k):
                            w = ybuf[slot, k, t, r, lanes]
                            lo = lo + gate[k] * plsc.bitcast(w << 16, _F32)
                            hi = hi + gate[k] * plsc.bitcast(w & jnp.uint32(0xFFFF0000), _F32)
                        obuf[slot, t, r, lanes] = lo
                        obuf[slot, t, rp + r, lanes] = hi
                return carry

            lax.fori_loop(0, chunk, token, 0)

        for cp in inputs(0, 0):
            cp.start()

        def chunk_pair(g, carry):
            for slot in range(2):
                c = 2 * g + slot

                @pl.when(c + 1 < n_chunks)
                def _():
                    for cp in inputs(c + 1, 1 - slot):
                        cp.start()

                for cp in inputs(c, slot):
                    cp.wait()

                @pl.when(c >= 2)
                def _():
                    output(c - 2, slot).wait()

                weighted_sum(slot)
                output(c, slot).start()
            return carry

        lax.fori_loop(0, n_chunks // 2, chunk_pair, 0)
        output(n_chunks - 2, 0).wait()
        output(n_chunks - 1, 1).wait()

    return gather_sum_kernel(table3, idx3, gates_b)


def _pack_bf16_pairs(v):
    half = v.shape[1] // 2
    lo = lax.bitcast_convert_type(v[:, :half].astype(_BF16).astype(_F32), jnp.uint32)
    hi = lax.bitcast_convert_type(v[:, half:].astype(_BF16).astype(_F32), jnp.uint32)
    return (lo >> 16) | (hi & jnp.uint32(0xFFFF0000))


def _unpack_bf16_pairs(w):
    return (lax.bitcast_convert_type(w << 16, _F32),
            lax.bitcast_convert_type(w & jnp.uint32(0xFFFF0000), _F32))


def _expert_kernel(be_ref, nx_ref, nv_ref, nu_ref, xs_ref, bgu_ref, bdn_ref, wgu_hbm, wdn_hbm, ys_ref,
                   wgu_f, wdn_f, wgu_s, wdn_s, sem, *, rs, layer):
    b = pl.program_id(0)
    n_chunks = wgu_f.shape[1] // GATE_UP_CHUNK
    half = GATE_UP_CHUNK // 2

    def fetch(e):
        return (pltpu.make_async_copy(wgu_hbm.at[layer, e], wgu_f, sem.at[0]),
                pltpu.make_async_copy(wdn_hbm.at[layer, e], wdn_f, sem.at[1]))

    @pl.when(b < nu_ref[0])
    def _():
        @pl.when((b == 0) | (be_ref[b] != be_ref[jnp.maximum(b - 1, 0)]))
        def _():
            @pl.when(b == 0)
            def _():
                for cp in fetch(be_ref[0]):
                    cp.start()

            for cp in fetch(be_ref[b]):
                cp.wait()
            r = lax.broadcasted_iota(jnp.int32, (GATE_UP_CHUNK, GATE_UP_CHUNK), 0)
            c = lax.broadcasted_iota(jnp.int32, (GATE_UP_CHUNK, GATE_UP_CHUNK), 1)
            perm = (r == jnp.where(c < half, 2 * c, 2 * (c - half) + 1)).astype(_BF16)
            for ch in range(n_chunks):
                cols = slice(ch * GATE_UP_CHUNK, (ch + 1) * GATE_UP_CHUNK)
                w = wgu_f[:, cols].astype(_BF16)
                wgu_s[:, cols] = jnp.dot(w, perm, preferred_element_type=_F32).astype(_BF16)
            wdn_s[...] = wdn_f[...].astype(_BF16)

            @pl.when(nx_ref[b] >= 0)
            def _():
                for cp in fetch(nx_ref[b]):
                    cp.start()

        n_valid = nv_ref[b]
        sub = EXPERT_SUB_ROWS

        def sub_block(s, rows=sub):
            row0 = pl.multiple_of(s * sub, sub)
            defined = row0 + lax.broadcasted_iota(jnp.int32, (rows, 1), 0) < n_valid
            words = [jnp.where(defined, w, jnp.uint32(0)) for w in _load_rows(xs_ref, rows, rs // 2, row0)]
            halves = [_unpack_bf16_pairs(w) for w in words]
            x = jnp.concatenate([lo for lo, _ in halves] + [hi for _, hi in halves], axis=1).astype(_BF16)
            h = jnp.dot(x, wgu_s[...], preferred_element_type=_F32) + bgu_ref[0]
            acts = []
            for ch in range(n_chunks):
                g = jnp.minimum(h[:, ch * GATE_UP_CHUNK:ch * GATE_UP_CHUNK + half], SWIGLU_LIMIT)
                up = jnp.clip(h[:, ch * GATE_UP_CHUNK + half:(ch + 1) * GATE_UP_CHUNK], -SWIGLU_LIMIT, SWIGLU_LIMIT)
                acts.append(((up + 1.0) * (g * jax.nn.sigmoid(SWIGLU_ALPHA * g))).astype(_BF16))
            y = jnp.dot(jnp.concatenate(acts, axis=1), wdn_s[...], preferred_element_type=_F32) + bdn_ref[0]
            _store_rows(ys_ref, _pack_bf16_pairs(y), row0)

        n_sub = (n_valid + sub - 1) // sub
        half_tail = (n_valid - (n_sub - 1) * sub) * 2 <= sub
        n_whole = n_sub - half_tail.astype(jnp.int32)

        def pair(i, carry):
            sub_block(2 * i)
            sub_block(2 * i + 1)
            return carry

        lax.fori_loop(0, n_whole // 2, pair, 0)

        @pl.when(n_whole % 2 == 1)
        def _():
            sub_block(n_whole - 1)

        @pl.when(half_tail)
        def _():
            sub_block(n_sub - 1, sub // 2)


def _expert_call(block_expert, block_next, block_valid, n_used, xs, layer, w_gu_all, b_gu_grouped, w_dn_all,
                 b_dn, rs):
    _, n_exp, d, f2 = w_gu_all.shape
    f = f2 // 2
    br = EXPERT_BLOCK_ROWS
    n_blocks = xs.shape[0] // (br * rs // 2)
    assert f2 % GATE_UP_CHUNK == 0 and rs % 2 == 0

    def row_map(b, be, nx, nv, nu):
        return (jnp.minimum(b, nu[0] - 1), 0)

    def w_map(b, be, nx, nv, nu):
        return (be[b], 0, 0)

    return pl.pallas_call(
        functools.partial(_expert_kernel, rs=rs, layer=layer),
        grid_spec=pltpu.PrefetchScalarGridSpec(
            num_scalar_prefetch=4,
            grid=(n_blocks,),
            in_specs=[
                pl.BlockSpec((br * rs // 2, LANES), row_map),
                pl.BlockSpec((1, 1, f2), w_map),
                pl.BlockSpec((1, 1, d), w_map),
                pl.BlockSpec(memory_space=pl.ANY),
                pl.BlockSpec(memory_space=pl.ANY),
            ],
            out_specs=pl.BlockSpec((br * rs // 2, LANES), row_map),
            scratch_shapes=[pltpu.VMEM((d, f2), _F32), pltpu.VMEM((f, d), _F32),
                            pltpu.VMEM((d, f2), _BF16), pltpu.VMEM((f, d), _BF16),
                            pltpu.SemaphoreType.DMA((2,))],
        ),
        out_shape=jax.ShapeDtypeStruct(xs.shape, jnp.uint32),
        compiler_params=pltpu.CompilerParams(
            dimension_semantics=("arbitrary",), vmem_limit_bytes=VMEM_LIMIT_BYTES),
        name="experts",
    )(block_expert, block_next, block_valid, n_used, xs, b_gu_grouped, b_dn, w_gu_all, w_dn_all)


def _combine_kernel(x1r_ref, h_ref, g_ref, b_ref, *rest, rs, alpha):
    rest[-1][...] = _residual_norm(x1r_ref, h_ref, g_ref, b_ref, rs, alpha)


def _combine_call(x1r, h, ln_g, ln_b, alpha, rs, out_tokens, token_lo, out_buf):
    n_tok = x1r.shape[0] // rs
    d = rs * LANES
    tb = min(COMBINE_BLOCK, n_tok)
    steps = n_tok // tb
    block_lo = token_lo // tb
    operands = [x1r, h, ln_g.reshape(1, d), ln_b.reshape(1, d)]
    in_specs = [
        pl.BlockSpec((tb * rs, LANES), lambda i: (i, 0)),
        pl.BlockSpec((tb * rs, LANES), lambda i: (i, 0)),
        pl.BlockSpec((1, d), lambda i: (0, 0)),
        pl.BlockSpec((1, d), lambda i: (0, 0)),
    ]
    aliases = {}
    if out_buf is not None:
        aliases = {len(operands): 0}
        operands.append(out_buf)
        in_specs.append(pl.BlockSpec(memory_space=pl.ANY))
    return pl.pallas_call(
        functools.partial(_combine_kernel, rs=rs, alpha=alpha),
        grid=(steps,),
        in_specs=in_specs,
        out_specs=pl.BlockSpec((tb, d), lambda i: (block_lo + i, 0)),
        out_shape=jax.ShapeDtypeStruct((out_tokens, d), _F32),
        input_output_aliases=aliases,
        compiler_params=pltpu.CompilerParams(
            dimension_semantics=("arbitrary",), vmem_limit_bytes=VMEM_LIMIT_BYTES),
        name="combine",
    )(*operands)


def _routing_tables(meta_i, counts_f, n_blocks):
    n_exp = counts_f.shape[0]
    br = EXPERT_BLOCK_ROWS
    counts = counts_f[:, 0].astype(jnp.int32)
    padded = ((counts + br - 1) // br) * br
    pend = jnp.cumsum(padded)
    pstart = pend - padded
    eids = jnp.arange(n_exp, dtype=jnp.int32)
    idx, rank = meta_i[:TOP_K], meta_i[TOP_K:]
    dest = jnp.sum(jnp.where(idx[..., None] == eids, pstart, 0), axis=-1) + rank
    n_used = (pend[-1] // br).astype(jnp.int32)
    blk = jnp.minimum(jnp.arange(n_blocks, dtype=jnp.int32), n_used - 1)
    block_expert = jnp.minimum(jnp.sum((pend[None, :] <= (blk * br)[:, None]).astype(jnp.int32), axis=1),
                               n_exp - 1)
    group_end = jnp.sum(jnp.where(block_expert[:, None] == eids, pstart + counts, 0), axis=-1)
    block_valid = jnp.clip(group_end - blk * br, 0, br).astype(jnp.int32)
    later = (eids[None, :] > block_expert[:, None]) & (counts[None, :] > 0)
    block_next = jnp.min(jnp.where(later, eids[None, :], n_exp), axis=1)
    block_next = jnp.where(block_next == n_exp, -1, block_next).astype(jnp.int32)
    return dest, block_expert, block_next, block_valid, n_used.reshape(1)


def _moe_layer(x1p, meta_i, gates_b, counts_f, layer, w_gu_all, b_gu, w_dn_all, b_dn, rs):
    _, n_exp, d, f2 = w_gu_all.shape
    n_tok = meta_i.shape[1]
    br = EXPERT_BLOCK_ROWS
    rp = rs // 2
    n_blocks = -(-(n_tok * TOP_K) // br) + n_exp
    dest, block_expert, block_next, block_valid, n_used = _routing_tables(meta_i, counts_f, n_blocks)
    n_rows = n_blocks * br
    xs = _sc_scatter_rows(x1p.reshape(n_tok, rp, LANES), dest.reshape(TOP_K, n_tok // SC_INDEX_ROW, SC_INDEX_ROW),
                          n_rows)
    half = GATE_UP_CHUNK // 2
    b_gu_grouped = b_gu.reshape(n_exp, f2 // GATE_UP_CHUNK, half, 2).transpose(0, 1, 3, 2).reshape(n_exp, 1, f2)
    ys = _expert_call(block_expert, block_next, block_valid, n_used, xs.reshape(n_rows * rp, LANES), layer, w_gu_all,
                      b_gu_grouped, w_dn_all, b_dn.reshape(n_exp, 1, d), rs)
    h = _sc_gather_weighted_sum(ys.reshape(n_rows, rp, LANES),
                                dest.reshape(TOP_K, n_tok // SC_INDEX_ROW, SC_INDEX_ROW), gates_b)
    return h.reshape(n_tok * rs, LANES)


def kernel(x, pool_w, pool_scale, sc_w_in, sc_conv_w, sc_w_out, cf_w_in, cf_b_in, cf_dw_w, cf_dw_b,
           cf_ln_g, cf_ln_b, cf_w_out, cf_b_out, mix_ln_g, mix_ln_b, router_w, router_b,
           moe_w_gu, moe_b_gu, moe_w_dn, moe_b_dn, ffn_ln_g, ffn_ln_b):
    bsz, seq, d = x.shape
    depth = mix_ln_g.shape[0]
    alpha = (2.0 * depth) ** 0.25
    rs = d // LANES
    n_chains = BATCH_CHAINS if bsz % BATCH_CHAINS == 0 else 1
    cb = bsz // n_chains
    sources = [("x", x, c * cb) for c in range(n_chains)]
    ia = ib = ic = 0
    for layer in range(depth):
        kind = layer % 3
        route = (mix_ln_g[layer], mix_ln_b[layer], router_w[layer], router_b[layer])
        if kind == 0:
            mixer, halo, block = _pool_mix, POOL_HALO, TOKEN_BLOCK
            weights = [pool_w[ia].astype(_BF16), pool_scale[ia].reshape(1, d)]
            ia += 1
        elif kind == 1:
            mixer, halo, block = _short_conv_mix, SHORT_CONV_HALO, MATMUL_MIXER_TOKEN_BLOCK
            weights = [sc_w_in[ib].astype(_BF16), sc_conv_w[ib], sc_w_out[ib].astype(_BF16)]
            ib += 1
        else:
            mixer, halo, block = _conformer_mix, CONFORMER_HALO, MATMUL_MIXER_TOKEN_BLOCK
            weights = [cf_w_in[ic].astype(_BF16), cf_b_in[ic].reshape(1, 2 * d), cf_dw_w[ic],
                       cf_dw_b[ic].reshape(1, d), cf_ln_g[ic].reshape(1, d), cf_ln_b[ic].reshape(1, d),
                       cf_w_out[ic].astype(_BF16), cf_b_out[ic].reshape(1, d)]
            ic += 1
        routed = [_mixer_call(mixer, src, cb, seq, weights, *route, halo, block, alpha) for src in sources]
        sources = []
        for x1r, x1p, meta_i, gates_b, counts_f in routed:
            h = _moe_layer(x1p, meta_i, gates_b, counts_f, layer, moe_w_gu, moe_b_gu[layer],
                           moe_w_dn, moe_b_dn[layer], rs)
            sources.append(("moe", x1r, h, ffn_ln_g[layer], ffn_ln_b[layer]))
    out = None
    for c, (_, x1r, h, ln_g, ln_b) in enumerate(sources):
        out = _combine_call(x1r, h, ln_g, ln_b, alpha, rs, bsz * seq, c * cb * seq, out)
    return out.reshape(bsz, seq, d)
```
